```python
import jax, jax.numpy as jnp
from jax import lax
import numpy as np

D_MODEL = 2048
BATCH = 4
SEQ = 4096
DEPTH = 4
DEC_BATCH = 16
DEC_SEQ = 32
PAST_LEN = 4096

CHUNK = 64
N_A_LAYERS = DEPTH // 2
N_B_LAYERS = DEPTH - N_A_LAYERS
POOL_WINDOWS = (2, 4, 8, 16)
N_POOL_GROUPS = len(POOL_WINDOWS)
POOL_GROUP = D_MODEL // N_POOL_GROUPS
POOL_STATE = max(POOL_WINDOWS) - 1
HEAD_DIM = 64
N_HEADS = D_MODEL // HEAD_DIM
N_KV_HEADS = N_HEADS // 8
GQA_GROUP = N_HEADS // N_KV_HEADS
WINDOW = 128
WINDOW_CHUNKS = WINDOW // CHUNK
D_FF = 4 * D_MODEL
ROPE_THETA = 10000.0
EPS = 1e-6
ATTN_SCALE = HEAD_DIM ** -0.5
NEG_INF = -1e30

kernel_name = 'pool_swa_sink_yoco_stream'


def rmsnorm(x, g):
    xf = x.astype(jnp.float32)
    y = xf * lax.rsqrt(jnp.mean(xf * xf, axis=-1, keepdims=True) + EPS)
    return (y * g.astype(jnp.float32)).astype(x.dtype)


def modulate(x, shift, scale):
    return x * (1 + scale[:, None, :]) + shift[:, None, :]


def rope(x, pos):
    half = HEAD_DIM // 2
    inv = ROPE_THETA ** (-jnp.arange(half, dtype=jnp.float32) / half)
    ang = pos.astype(jnp.float32)[:, None] * inv[None, :]
    cos = jnp.cos(ang)[None, :, None, :]
    sin = jnp.sin(ang)[None, :, None, :]
    x1 = x[..., :half].astype(jnp.float32)
    x2 = x[..., half:].astype(jnp.float32)
    return jnp.concatenate([x1 * cos - x2 * sin, x2 * cos + x1 * sin], axis=-1).astype(x.dtype)


def pool_mix(h_ext, pos0, w_pool, pool_scale):
    B, L, D = h_ext.shape
    T = L - POOL_STATE
    hf = h_ext.astype(jnp.float32)
    cs = jnp.concatenate([jnp.zeros_like(hf[:, :1]), jnp.cumsum(hf, axis=1)], axis=1)
    end = cs[:, POOL_STATE + 1:]
    x_new = hf[:, POOL_STATE:]
    pos = pos0 + jnp.arange(T)
    outs = []
    for g, w in enumerate(POOL_WINDOWS):
        sl = slice(g * POOL_GROUP, (g + 1) * POOL_GROUP)
        start = cs[:, POOL_STATE + 1 - w:POOL_STATE + 1 - w + T, sl]
        cnt = jnp.minimum(w, pos + 1).astype(jnp.float32)[None, :, None]
        outs.append((end[..., sl] - start) / cnt - x_new[..., sl])
    pooled = jnp.stack(outs, axis=2).astype(h_ext.dtype)
    y = jnp.einsum('btgc,gcd->btgd', pooled, w_pool).reshape(B, T, D)
    return y * pool_scale


def sink_softmax(s, sink):
    sink = sink.astype(jnp.float32)
    m = jnp.maximum(jnp.max(s, axis=-1, keepdims=True), sink)
    e = jnp.exp(s - m)
    return e / (jnp.sum(e, axis=-1, keepdims=True) + jnp.exp(sink - m))


def attn_prompt(q, k, v, sink):
    B, S = q.shape[0], q.shape[1]
    n = S // CHUNK
    span = (WINDOW_CHUNKS + 1) * CHUNK
    pad = ((0, 0), (WINDOW_CHUNKS * CHUNK, 0), (0, 0), (0, 0))
    kc = jnp.pad(k, pad).reshape(B, n + WINDOW_CHUNKS, CHUNK, N_KV_HEADS, HEAD_DIM)
    vc = jnp.pad(v, pad).reshape(B, n + WINDOW_CHUNKS, CHUNK, N_KV_HEADS, HEAD_DIM)
    kb = jnp.concatenate([kc[:, i:i + n] for i in range(WINDOW_CHUNKS + 1)], axis=2)
    vb = jnp.concatenate([vc[:, i:i + n] for i in range(WINDOW_CHUNKS + 1)], axis=2)
    kpos = (jnp.arange(n)[:, None] - WINDOW_CHUNKS) * CHUNK + jnp.arange(span)[None, :]
    valid = kpos >= 0
    qb = q.reshape(B, n, CHUNK, N_KV_HEADS, GQA_GROUP, HEAD_DIM)
    s = jnp.einsum('bnqkgd,bnskd->bnkgqs', qb, kb, preferred_element_type=jnp.float32) * ATTN_SCALE
    s = jnp.where(valid[None, :, None, None, None, :], s, NEG_INF)
    p = sink_softmax(s, sink.reshape(N_KV_HEADS, GQA_GROUP)[None, None, :, :, None, None])
    o = jnp.einsum('bnkgqs,bnskd->bnqkgd', p.astype(v.dtype), vb)
    return o.reshape(B, S, N_HEADS * HEAD_DIM)


def attn_sample(q, k_new, v_new, k_past, v_past, sink):
    B, T = q.shape[0], q.shape[1]
    k = jnp.concatenate([k_past, k_new], axis=1)
    v = jnp.concatenate([v_past, v_new], axis=1)
    qg = q.reshape(B, T, N_KV_HEADS, GQA_GROUP, HEAD_DIM)
    s = jnp.einsum('btkgd,bskd->bkgts', qg, k, preferred_element_type=jnp.float32) * ATTN_SCALE
    p = sink_softmax(s, sink.reshape(N_KV_HEADS, GQA_GROUP)[None, :, :, None, None])
    o = jnp.einsum('bkgts,bskd->btkgd', p.astype(v.dtype), v)
    return o.reshape(B, T, N_HEADS * HEAD_DIM)


def shared_kv(x, c, pos, w_kv_mod, b_kv_mod, g_kv, w_kv):
    B, T, _ = x.shape
    mod = jax.nn.silu(c) @ w_kv_mod + b_kv_mod
    shift, scale = jnp.split(mod, 2, axis=-1)
    h = modulate(rmsnorm(x, g_kv), shift, scale)
    kv = h @ w_kv
    k, v = jnp.split(kv, 2, axis=-1)
    k = rope(k.reshape(B, T, N_KV_HEADS, HEAD_DIM), pos)
    v = v.reshape(B, T, N_KV_HEADS, HEAD_DIM)
    return k, v


def run_trunk(x, c, pos0, pool_prefix, kv_past, w_mod, b_mod, g_norm, w_pool, pool_scale,
              w_kv_mod, b_kv_mod, g_kv, w_kv, w_q, sinks, w_o, w_up, w_down):
    B, T, _ = x.shape
    pos = pos0 + jnp.arange(T)
    sc = jax.nn.silu(c)
    new_pool = []
    k = v = None
    for l in range(DEPTH):
        mod = sc @ w_mod[l] + b_mod[l]
        sh1, sc1, gt1, sh2, sc2, gt2 = jnp.split(mod, 6, axis=-1)
        h = modulate(rmsnorm(x, g_norm[l, 0]), sh1, sc1)
        if l < N_A_LAYERS:
            h_ext = jnp.concatenate([pool_prefix[l], h], axis=1)
            new_pool.append(h_ext[:, -POOL_STATE:])
            mix = pool_mix(h_ext, pos0, w_pool[l], pool_scale[l])
        else:
            j = l - N_A_LAYERS
            q = rope((h @ w_q[j]).reshape(B, T, N_HEADS, HEAD_DIM), pos)
            if kv_past is None:
                o = attn_prompt(q, k, v, sinks[j])
            else:
                o = attn_sample(q, k, v, kv_past[0], kv_past[1], sinks[j])
            mix = o @ w_o[j]
        x = x + gt1[:, None, :] * rmsnorm(mix, g_norm[l, 1])
        h = modulate(rmsnorm(x, g_norm[l, 2]), sh2, sc2)
        f = jnp.square(jax.nn.relu(h @ w_up[l])) @ w_down[l]
        x = x + gt2[:, None, :] * rmsnorm(f, g_norm[l, 3])
        if l == N_A_LAYERS - 1:
            k, v = shared_kv(x, c, pos, w_kv_mod, b_kv_mod, g_kv, w_kv)
    return x, jnp.stack(new_pool, axis=0), k, v


def setup_inputs(seed: int = 0) -> dict:
    key = jax.random.key(seed)
    ks = jax.random.split(key, 24)
    f32 = jnp.float32
    nrm = lambda k, shape, s: jax.random.normal(k, shape, f32) * s
    D = D_MODEL
    kv_rows = min(WINDOW, PAST_LEN)
    return {
        'x_prompt': nrm(ks[0], (BATCH, SEQ, D), 1.0),
        'x_sample': nrm(ks[1], (DEC_BATCH, DEC_SEQ, D), 1.0),
        'c_prompt': nrm(ks[2], (BATCH, D), 1.0),
        'c_sample': nrm(ks[3], (DEC_BATCH, D), 1.0),
        'state_pool': nrm(ks[4], (N_A_LAYERS, DEC_BATCH, POOL_STATE, D), 1.0),
        'cache_k': nrm(ks[5], (DEC_BATCH, kv_rows, N_KV_HEADS, HEAD_DIM), 1.0),
        'cache_v': nrm(ks[6], (DEC_BATCH, kv_rows, N_KV_HEADS, HEAD_DIM), 1.0),
        'w_mod': nrm(ks[7], (DEPTH, D, 6 * D), D ** -0.5),
        'b_mod': nrm(ks[8], (DEPTH, 6 * D), 0.02),
        'g_norm': 1.0 + nrm(ks[9], (DEPTH, 4, D), 0.02),
        'w_pool': nrm(ks[10], (N_A_LAYERS, N_POOL_GROUPS, POOL_GROUP, POOL_GROUP), POOL_GROUP ** -0.5),
        'pool_scale': 1.0 + nrm(ks[11], (N_A_LAYERS, D), 0.1),
        'w_kv_mod': nrm(ks[12], (D, 2 * D), D ** -0.5),
        'b_kv_mod': nrm(ks[13], (2 * D,), 0.02),
        'g_kv': 1.0 + nrm(ks[14], (D,), 0.02),
        'w_kv': nrm(ks[15], (D, 2 * N_KV_HEADS * HEAD_DIM), D ** -0.5),
        'w_q': nrm(ks[16], (N_B_LAYERS, D, N_HEADS * HEAD_DIM), D ** -0.5),
        'sinks': nrm(ks[17], (N_B_LAYERS, N_HEADS), 0.5),
        'w_o': nrm(ks[18], (N_B_LAYERS, N_HEADS * HEAD_DIM, D), (N_HEADS * HEAD_DIM) ** -0.5),
        'w_up': nrm(ks[19], (DEPTH, D, D_FF), D ** -0.5),
        'w_down': nrm(ks[20], (DEPTH, D_FF, D), D_FF ** -0.5),
    }


def reference(x_prompt, x_sample, c_prompt, c_sample, state_pool, cache_k, cache_v,
              w_mod, b_mod, g_norm, w_pool, pool_scale, w_kv_mod, b_kv_mod, g_kv, w_kv,
              w_q, sinks, w_o, w_up, w_down):
    prompt_prefix = jnp.zeros((N_A_LAYERS, x_prompt.shape[0], POOL_STATE, D_MODEL), x_prompt.dtype)
    y_prompt, pool_p, k_p, v_p = run_trunk(
        x_prompt, c_prompt, 0, prompt_prefix, None, w_mod, b_mod, g_norm, w_pool, pool_scale,
        w_kv_mod, b_kv_mod, g_kv, w_kv, w_q, sinks, w_o, w_up, w_down)
    y_sample, pool_s, k_s, v_s = run_trunk(
        x_sample, c_sample, PAST_LEN, state_pool, (cache_k, cache_v), w_mod, b_mod, g_norm, w_pool,
        pool_scale, w_kv_mod, b_kv_mod, g_kv, w_kv, w_q, sinks, w_o, w_up, w_down)
    kv_keep = min(WINDOW, x_prompt.shape[1])
    new_k_prompt = k_p[:, -kv_keep:]
    new_v_prompt = v_p[:, -kv_keep:]
    return (y_prompt, y_sample, pool_p, pool_s, new_k_prompt, new_v_prompt, k_s, v_s)
```

```python
import functools

import jax
import jax.numpy as jnp
import numpy as np
from jax import lax
from jax.experimental import pallas as pl
from jax.experimental.pallas import tpu as pltpu

F32 = jnp.float32
BF16 = jnp.bfloat16

D_MODEL = 2048
DEPTH = 4
PAST_LEN = 4096
CHUNK = 64
N_A_LAYERS = DEPTH // 2
POOL_WINDOWS = (2, 4, 8, 16)
POOL_GROUP = D_MODEL // len(POOL_WINDOWS)
POOL_STATE = max(POOL_WINDOWS) - 1
POOL_HALO = POOL_STATE + 1
HEAD_DIM = 64
N_HEADS = D_MODEL // HEAD_DIM
N_KV_HEADS = N_HEADS // 8
WINDOW = 128
D_FF = 4 * D_MODEL
ROPE_THETA = 10000.0
EPS = 1e-6
ATTN_SCALE = HEAD_DIM ** -0.5
NEG_INF = -1e30

LANES = 128
KEY_SPAN = 2 * LANES
MOD_ROWS = 32
VMEM_LIMIT = 56 * 1024 * 1024


def _params(*sem):
    return pltpu.CompilerParams(dimension_semantics=sem, vmem_limit_bytes=VMEM_LIMIT)


def _rms(x, g):
    ms = jnp.mean(x * x, axis=-1, keepdims=True)
    return x * lax.rsqrt(ms + EPS) * g


def _modulate(x, shift, scale):
    return x * (1.0 + scale) + shift


def _mod_kernel(c_ref, w_ref, b_ref, o_ref):
    c = c_ref[...]
    sc = (c * jax.nn.sigmoid(c)).astype(BF16)
    o_ref[0] = jnp.dot(sc, w_ref[0].astype(BF16), preferred_element_type=F32) + b_ref[0]


def _modulation(c_all, w, b, tn=1024):
    L, D, N = w.shape
    return pl.pallas_call(
        _mod_kernel,
        out_shape=jax.ShapeDtypeStruct((L, MOD_ROWS, N), F32),
        grid=(L, N // tn),
        in_specs=[
            pl.BlockSpec((MOD_ROWS, D), lambda l, n: (0, 0)),
            pl.BlockSpec((1, D, tn), lambda l, n: (l, 0, n)),
            pl.BlockSpec((1, 1, tn), lambda l, n: (l, 0, n)),
        ],
        out_specs=pl.BlockSpec((1, MOD_ROWS, tn), lambda l, n: (l, 0, n)),
        compiler_params=_params("parallel", "parallel"),
        name="modulation",
    )(c_all, w, b)


def _pool_kernel(x_ref, mod_ref, g_ref, pre_ref, wp_ref, ps_ref, o_ref, np_ref, hbuf,
                 *, bb, tt, pos0):
    t = pl.program_id(1)

    @pl.when(t == 0)
    def _():
        hbuf[:, 1:POOL_HALO, :] = pre_ref[0]

    @pl.when(t > 0)
    def _():
        hbuf[:, 0:POOL_HALO, :] = hbuf[:, tt:tt + POOL_HALO, :]

    x = x_ref[...]
    m = mod_ref[...]
    h = _modulate(_rms(x, g_ref[:, 0:1, :]), m[:, 0:1, :], m[:, 1:2, :])
    hbuf[:, POOL_HALO:POOL_HALO + tt, :] = h
    np_ref[0] = hbuf[:, tt + 1:tt + POOL_HALO, :]

    pos = pos0 + t * tt + lax.broadcasted_iota(jnp.int32, (1, tt, POOL_GROUP), 1)
    ys = []
    for g, w in enumerate(POOL_WINDOWS):
        cs = slice(g * POOL_GROUP, (g + 1) * POOL_GROUP)
        acc = hbuf[:, POOL_HALO:POOL_HALO + tt, cs]
        hg = acc
        for j in range(1, w):
            acc = acc + hbuf[:, POOL_HALO - j:POOL_HALO - j + tt, cs]
        cnt = jnp.minimum(w, pos + 1).astype(F32)
        pooled = acc / cnt - hg
        ys.append(jnp.dot(pooled.reshape(bb * tt, POOL_GROUP).astype(BF16), wp_ref[0, g],
                          preferred_element_type=F32))
    y = jnp.concatenate(ys, axis=-1).reshape(bb, tt, D_MODEL) * ps_ref[...]
    o_ref[...] = x + m[:, 2:3, :] * _rms(y, g_ref[:, 1:2, :])


def _pool_layer(x, mod, g_norm, prefix, w_pool, pool_scale, l, *, bb, tt, pos0):
    B, T, D = x.shape
    kern = functools.partial(_pool_kernel, bb=bb, tt=tt, pos0=pos0)
    return pl.pallas_call(
        kern,
        out_shape=(jax.ShapeDtypeStruct((B, T, D), F32),
                   jax.ShapeDtypeStruct((1, B, POOL_STATE, D), F32)),
        grid=(B // bb, T // tt),
        in_specs=[
            pl.BlockSpec((bb, tt, D), lambda b, t: (b, t, 0)),
            pl.BlockSpec((bb, 6, D), lambda b, t: (b, 0, 0)),
            pl.BlockSpec((1, 4, D), lambda b, t: (l, 0, 0)),
            pl.BlockSpec((1, bb, POOL_STATE, D), lambda b, t: (l, b, 0, 0)),
            pl.BlockSpec((1, len(POOL_WINDOWS), POOL_GROUP, POOL_GROUP),
                         lambda b, t: (l, 0, 0, 0)),
            pl.BlockSpec((1, 1, D), lambda b, t: (l, 0, 0)),
        ],
        out_specs=(pl.BlockSpec((bb, tt, D), lambda b, t: (b, t, 0)),
                   pl.BlockSpec((1, bb, POOL_STATE, D), lambda b, t: (0, b, 0, 0))),
        scratch_shapes=[pltpu.VMEM((bb, POOL_HALO + tt, D), F32)],
        compiler_params=_params("parallel", "arbitrary"),
        name=f"pool_mixer_{l}",
    )(x, mod, g_norm, prefix, w_pool, pool_scale)


def _rope_tables(pos):
    half = HEAD_DIM // 2
    inv = ROPE_THETA ** (-jnp.arange(half, dtype=F32) / half)
    ang = pos.astype(F32)[:, None] * inv[None, :]
    cos, sin = jnp.cos(ang), jnp.sin(ang)
    zero = jnp.zeros_like(sin)
    c = jnp.tile(cos, (1, 4))
    s_lo = jnp.tile(jnp.concatenate([-sin, zero], axis=1), (1, 2))
    s_hi = jnp.tile(jnp.concatenate([zero, sin], axis=1), (1, 2))
    return c, s_lo, s_hi


def _rope_block(blk, c, s_lo, s_hi):
    return (blk * c + pltpu.roll(blk, LANES - HEAD_DIM // 2, 1) * s_lo
            + pltpu.roll(blk, HEAD_DIM // 2, 1) * s_hi)


def _qproj_kernel(x_ref, mod_ref, g_ref, c_ref, slo_ref, shi_ref, w_ref, q_ref, *, bb, tt):
    x = x_ref[...]
    m = mod_ref[...]
    h = _modulate(_rms(x, g_ref[:, 0:1, :]), m[:, 0:1, :], m[:, 1:2, :])
    q = jnp.dot(h.reshape(bb * tt, D_MODEL).astype(BF16), w_ref[0], preferred_element_type=F32)
    c, s_lo, s_hi = c_ref[...], slo_ref[...], shi_ref[...]
    for j in range(D_MODEL // LANES):
        cs = slice(j * LANES, (j + 1) * LANES)
        q_ref[:, cs] = (_rope_block(q[:, cs], c, s_lo, s_hi) * ATTN_SCALE).astype(BF16)


def _qproj_layer(x, mod, g_norm, tables, w_q, l, j, *, bb, tt):
    B, T, D = x.shape
    nt = T // tt
    M = bb * tt
    kern = functools.partial(_qproj_kernel, bb=bb, tt=tt)
    tab_spec = pl.BlockSpec((M, LANES), lambda b, t: (t, 0))
    return pl.pallas_call(
        kern,
        out_shape=jax.ShapeDtypeStruct((B * T, D), BF16),
        grid=(B // bb, nt),
        in_specs=[
            pl.BlockSpec((bb, tt, D), lambda b, t: (b, t, 0)),
            pl.BlockSpec((bb, 6, D), lambda b, t: (b, 0, 0)),
            pl.BlockSpec((1, 4, D), lambda b, t: (l, 0, 0)),
            tab_spec, tab_spec, tab_spec,
            pl.BlockSpec((1, D, D), lambda b, t: (j, 0, 0)),
        ],
        out_specs=pl.BlockSpec((M, D), lambda b, t: (b * nt + t, 0)),
        compiler_params=_params("parallel", "parallel"),
        name=f"q_proj_{l}",
    )(x, mod, g_norm, *tables, w_q)


def _kv_kernel(x_ref, mod_ref, g_ref, c_ref, slo_ref, shi_ref, w_ref,
               k_ref, v_ref, kd_ref, vd_ref, *, bb, tt):
    x = x_ref[...]
    m = mod_ref[...]
    h = _modulate(_rms(x, g_ref[...]), m[:, 0:1, :], m[:, 1:2, :])
    kv = jnp.dot(h.reshape(bb * tt, D_MODEL).astype(BF16), w_ref[...],
                 preferred_element_type=F32)
    c, s_lo, s_hi = c_ref[...], slo_ref[...], shi_ref[...]
    lo = lax.broadcasted_iota(jnp.int32, (bb * tt, LANES), 1) < HEAD_DIM
    zero = jnp.zeros((bb * tt, LANES), F32)
    ks, vs = [], []
    for j in range(N_KV_HEADS):
        ks.append(_rope_block(kv[:, j * LANES:(j + 1) * LANES], c, s_lo, s_hi))
        vs.append(kv[:, (N_KV_HEADS + j) * LANES:(N_KV_HEADS + j + 1) * LANES])
    for j in range(N_KV_HEADS):
        for src, dst in ((ks, kd_ref), (vs, vd_ref)):
            dst[:, (2 * j) * LANES:(2 * j + 1) * LANES] = jnp.where(lo, src[j], zero).astype(BF16)
            dst[:, (2 * j + 1) * LANES:(2 * j + 2) * LANES] = jnp.where(lo, zero, src[j]).astype(BF16)
    for p in range(N_KV_HEADS // 2):
        cs = slice(p * LANES, (p + 1) * LANES)
        k_ref[:, cs] = jnp.where(lo, ks[2 * p], ks[2 * p + 1])
        v_ref[:, cs] = jnp.where(lo, vs[2 * p], vs[2 * p + 1])


def _kv_layer(x, mod, g_kv, tables, w_kv_dup, *, bb, tt):
    B, T, D = x.shape
    nt = T // tt
    M = bb * tt
    KV = N_KV_HEADS * HEAD_DIM
    kern = functools.partial(_kv_kernel, bb=bb, tt=tt)
    tab_spec = pl.BlockSpec((M, LANES), lambda b, t: (t, 0))
    row = lambda b, t: (b * nt + t, 0)
    return pl.pallas_call(
        kern,
        out_shape=(jax.ShapeDtypeStruct((B * T, KV), F32),
                   jax.ShapeDtypeStruct((B * T, KV), F32),
                   jax.ShapeDtypeStruct((B * T, 4 * KV), BF16),
                   jax.ShapeDtypeStruct((B * T, 4 * KV), BF16)),
        grid=(B // bb, nt),
        in_specs=[
            pl.BlockSpec((bb, tt, D), lambda b, t: (b, t, 0)),
            pl.BlockSpec((bb, 2, D), lambda b, t: (b, 0, 0)),
            pl.BlockSpec((1, 1, D), lambda b, t: (0, 0, 0)),
            tab_spec, tab_spec, tab_spec,
            pl.BlockSpec((D, 4 * KV), lambda b, t: (0, 0)),
        ],
        out_specs=(pl.BlockSpec((M, KV), row), pl.BlockSpec((M, KV), row),
                   pl.BlockSpec((M, 4 * KV), row), pl.BlockSpec((M, 4 * KV), row)),
        compiler_params=_params("parallel", "parallel"),
        name="shared_kv",
    )(x, mod, g_kv, *tables, w_kv_dup)


def _sink_softmax(s, sink):
    m = jnp.maximum(jnp.max(s, axis=-1, keepdims=True), sink)
    e = jnp.exp(s - m)
    den = jnp.sum(e, axis=-1, keepdims=True) + jnp.exp(sink - m)
    return e * (1.0 / den)


def _attn_kernel(sink_ref, q_ref, kd_ref, vd_ref, o_ref, *, tq, chunked, n_valid):
    i = pl.program_id(1)
    if chunked:
        start = pl.multiple_of(i * tq, tq)
        kw = kd_ref[0, pl.ds(start, KEY_SPAN), :]
        vw = vd_ref[0, pl.ds(start, KEY_SPAN), :]
        row = lax.broadcasted_iota(jnp.int32, (tq, KEY_SPAN), 0)
        kp = lax.broadcasted_iota(jnp.int32, (tq, KEY_SPAN), 1)
        cq = row // CHUNK
        kb = kp // CHUNK
        first_real = jnp.where(i > 0, 0, WINDOW)
        valid = (kb >= cq) & (kb <= cq + WINDOW // CHUNK) & (kp >= first_real)
    else:
        kw = kd_ref[0]
        vw = vd_ref[0]
        valid = lax.broadcasted_iota(jnp.int32, (tq, KEY_SPAN), 1) < n_valid
    nt = (((1,), (1,)), ((), ()))
    for p in range(N_HEADS // 2):
        j = p // (N_HEADS // N_KV_HEADS // 2)
        qp = q_ref[:, p * LANES:(p + 1) * LANES]
        k_lo = kw[:, (2 * j) * LANES:(2 * j + 1) * LANES]
        k_hi = kw[:, (2 * j + 1) * LANES:(2 * j + 2) * LANES]
        v_lo = vw[:, (2 * j) * LANES:(2 * j + 1) * LANES]
        v_hi = vw[:, (2 * j + 1) * LANES:(2 * j + 2) * LANES]
        s0 = lax.dot_general(qp, k_lo, nt, preferred_element_type=F32)
        s1 = lax.dot_general(qp, k_hi, nt, preferred_element_type=F32)
        p0 = _sink_softmax(jnp.where(valid, s0, NEG_INF), sink_ref[2 * p])
        p1 = _sink_softmax(jnp.where(valid, s1, NEG_INF), sink_ref[2 * p + 1])
        o = (jnp.dot(p0.astype(BF16), v_lo, preferred_element_type=F32)
             + jnp.dot(p1.astype(BF16), v_hi, preferred_element_type=F32))
        o_ref[:, p * LANES:(p + 1) * LANES] = o.astype(BF16)


def _attention(q, kd, vd, sinks, *, B, T, tq, chunked, n_valid):
    nq = T // tq
    Tk = kd.shape[1]
    W = kd.shape[2]
    kern = functools.partial(_attn_kernel, tq=tq, chunked=chunked, n_valid=n_valid)
    return pl.pallas_call(
        kern,
        out_shape=jax.ShapeDtypeStruct((B * T, D_MODEL), BF16),
        grid=(B, nq),
        in_specs=[
            pl.BlockSpec(memory_space=pltpu.SMEM),
            pl.BlockSpec((tq, D_MODEL), lambda b, i: (b * nq + i, 0)),
            pl.BlockSpec((1, Tk, W), lambda b, i: (b, 0, 0)),
            pl.BlockSpec((1, Tk, W), lambda b, i: (b, 0, 0)),
        ],
        out_specs=pl.BlockSpec((tq, D_MODEL), lambda b, i: (b * nq + i, 0)),
        compiler_params=_params("parallel", "parallel"),
        name="swa_attention",
    )(sinks, q, kd, vd)


def _oproj_kernel(o_ref, x_ref, mod_ref, g_ref, w_ref, y_ref, *, bb, tt):
    mix = jnp.dot(o_ref[...], w_ref[0], preferred_element_type=F32).reshape(bb, tt, D_MODEL)
    m = mod_ref[...]
    y_ref[...] = x_ref[...] + m[:, 2:3, :] * _rms(mix, g_ref[:, 1:2, :])


def _oproj_layer(o, x, mod, g_norm, w_o, l, j, *, bb, tt):
    B, T, D = x.shape
    nt = T // tt
    kern = functools.partial(_oproj_kernel, bb=bb, tt=tt)
    return pl.pallas_call(
        kern,
        out_shape=jax.ShapeDtypeStruct((B, T, D), F32),
        grid=(B // bb, nt),
        in_specs=[
            pl.BlockSpec((bb * tt, D), lambda b, t: (b * nt + t, 0)),
            pl.BlockSpec((bb, tt, D), lambda b, t: (b, t, 0)),
            pl.BlockSpec((bb, 6, D), lambda b, t: (b, 0, 0)),
            pl.BlockSpec((1, 4, D), lambda b, t: (l, 0, 0)),
            pl.BlockSpec((1, D, D), lambda b, t: (j, 0, 0)),
        ],
        out_specs=pl.BlockSpec((bb, tt, D), lambda b, t: (b, t, 0)),
        compiler_params=_params("parallel", "parallel"),
        name=f"o_proj_{l}",
    )(o, x, mod, g_norm, w_o)


def _mlp_kernel(x_ref, mod_ref, g_ref, wu_ref, wd_ref, y_ref, h_ref, *, bb, tt):
    f = pl.program_id(2)
    m = mod_ref[...]

    @pl.when(f == 0)
    def _():
        h = _modulate(_rms(x_ref[...], g_ref[:, 2:3, :]), m[:, 3:4, :], m[:, 4:5, :])
        h_ref[...] = h.reshape(bb * tt, D_MODEL).astype(BF16)

    u = jnp.dot(h_ref[...], wu_ref[0], preferred_element_type=F32)
    a = jnp.square(jnp.maximum(u, 0.0)).astype(BF16)
    d = jnp.dot(a, wd_ref[0], preferred_element_type=F32).reshape(bb, tt, D_MODEL)

    @pl.when(f == 0)
    def _():
        y_ref[...] = d

    @pl.when(f > 0)
    def _():
        y_ref[...] += d

    @pl.when(f == pl.num_programs(2) - 1)
    def _():
        y_ref[...] = x_ref[...] + m[:, 5:6, :] * _rms(y_ref[...], g_ref[:, 3:4, :])


def _mlp_layer(x, mod, g_norm, w_up, w_down, l, *, bb, tt, tff):
    B, T, D = x.shape
    kern = functools.partial(_mlp_kernel, bb=bb, tt=tt)
    return pl.pallas_call(
        kern,
        out_shape=jax.ShapeDtypeStruct((B, T, D), F32),
        grid=(B // bb, T // tt, D_FF // tff),
        in_specs=[
            pl.BlockSpec((bb, tt, D), lambda b, t, f: (b, t, 0)),
            pl.BlockSpec((bb, 6, D), lambda b, t, f: (b, 0, 0)),
            pl.BlockSpec((1, 4, D), lambda b, t, f: (l, 0, 0)),
            pl.BlockSpec((1, D, tff), lambda b, t, f: (l, 0, f)),
            pl.BlockSpec((1, tff, D), lambda b, t, f: (l, f, 0)),
        ],
        out_specs=pl.BlockSpec((bb, tt, D), lambda b, t, f: (b, t, 0)),
        scratch_shapes=[pltpu.VMEM((bb * tt, D), BF16)],
        compiler_params=_params("parallel", "parallel", "arbitrary"),
        name=f"mlp_{l}",
    )(x, mod, g_norm, w_up, w_down)


def _split_dup(a):
    z = jnp.zeros_like(a)
    lo = jnp.concatenate([a, z], axis=-1)
    hi = jnp.concatenate([z, a], axis=-1)
    out = jnp.stack([lo, hi], axis=3)
    return out.reshape(a.shape[0], a.shape[1], -1).astype(BF16)


def _trunk(x, mods, kvmod, pos0, prefix, kv_past, wts, *, bb, tt, tff, tq):
    B, T, D = x.shape
    pos = pos0 + jnp.arange(T)
    tables = tuple(jnp.tile(tb, (bb, 1)) if bb > 1 else tb for tb in _rope_tables(pos))
    new_pool = []
    k = v = kd = vd = None
    for l in range(DEPTH):
        mod = mods[l]
        if l < N_A_LAYERS:
            x, npool = _pool_layer(x, mod, wts["g_norm"], prefix, wts["w_pool"],
                                   wts["pool_scale"], l, bb=bb, tt=tt, pos0=pos0)
            new_pool.append(npool)
        else:
            j = l - N_A_LAYERS
            q = _qproj_layer(x, mod, wts["g_norm"], tables, wts["w_q"], l, j, bb=bb, tt=tt)
            o = _attention(q, kd, vd, wts["sinks"][j], B=B, T=T, tq=tq,
                           chunked=kv_past is None,
                           n_valid=None if kv_past is None else kv_past[0].shape[1] + T)
            x = _oproj_layer(o, x, mod, wts["g_norm"], wts["w_o"], l, j, bb=bb, tt=tt)
        x = _mlp_layer(x, mod, wts["g_norm"], wts["w_up"], wts["w_down"], l,
                       bb=bb, tt=tt, tff=tff)
        if l == N_A_LAYERS - 1:
            k, v, kd, vd = _kv_layer(x, kvmod, wts["g_kv"], tables, wts["w_kv_dup"],
                                     bb=bb, tt=tt)
            W = kd.shape[-1]
            kd = kd.reshape(B, T, W)
            vd = vd.reshape(B, T, W)
            if kv_past is None:
                front = ((0, 0), (WINDOW, 0), (0, 0))
                kd, vd = jnp.pad(kd, front), jnp.pad(vd, front)
            else:
                back = ((0, 0), (0, KEY_SPAN - kv_past[0].shape[1] - T), (0, 0))
                kd = jnp.pad(jnp.concatenate([_split_dup(kv_past[0]), kd], axis=1), back)
                vd = jnp.pad(jnp.concatenate([_split_dup(kv_past[1]), vd], axis=1), back)
    k = k.reshape(B, T, N_KV_HEADS, HEAD_DIM)
    v = v.reshape(B, T, N_KV_HEADS, HEAD_DIM)
    return x, jnp.concatenate(new_pool, axis=0), k, v


def kernel(x_prompt, x_sample, c_prompt, c_sample, state_pool, cache_k, cache_v, w_mod, b_mod,
           g_norm, w_pool, pool_scale, w_kv_mod, b_kv_mod, g_kv, w_kv, w_q, sinks, w_o, w_up,
           w_down):
    Bp, Bs = x_prompt.shape[0], x_sample.shape[0]
    D = D_MODEL
    KV = N_KV_HEADS * HEAD_DIM

    c_all = jnp.concatenate(
        [c_prompt, c_sample, jnp.zeros((MOD_ROWS - Bp - Bs, D), F32)], axis=0)
    mod_all = _modulation(c_all, w_mod, b_mod.reshape(DEPTH, 1, 6 * D)).reshape(
        DEPTH, MOD_ROWS, 6, D)
    kvmod_all = _modulation(c_all, w_kv_mod.reshape(1, D, 2 * D),
                            b_kv_mod.reshape(1, 1, 2 * D)).reshape(MOD_ROWS, 2, D)

    wk = w_kv[:, :KV].reshape(D, N_KV_HEADS, 1, HEAD_DIM)
    wv = w_kv[:, KV:].reshape(D, N_KV_HEADS, 1, HEAD_DIM)
    w_kv_dup = jnp.concatenate([jnp.tile(wk, (1, 1, 2, 1)).reshape(D, 2 * KV),
                                jnp.tile(wv, (1, 1, 2, 1)).reshape(D, 2 * KV)], axis=1)
    wts = {
        "g_norm": g_norm,
        "w_pool": w_pool.astype(BF16),
        "pool_scale": pool_scale.reshape(N_A_LAYERS, 1, D),
        "g_kv": g_kv.reshape(1, 1, D),
        "w_kv_dup": w_kv_dup.astype(BF16),
        "w_q": w_q.astype(BF16),
        "sinks": sinks,
        "w_o": w_o.astype(BF16),
        "w_up": w_up.astype(BF16),
        "w_down": w_down.astype(BF16),
    }

    prompt_prefix = jnp.zeros((N_A_LAYERS, Bp, POOL_STATE, D), F32)
    y_p, pool_p, k_p, v_p = _trunk(
        x_prompt, [mod_all[l, :Bp] for l in range(DEPTH)], kvmod_all[:Bp], 0, prompt_prefix,
        None, wts, bb=1, tt=512, tff=1024, tq=LANES)
    T_s = x_sample.shape[1]
    y_s, pool_s, k_s, v_s = _trunk(
        x_sample, [mod_all[l, Bp:Bp + Bs] for l in range(DEPTH)], kvmod_all[Bp:Bp + Bs],
        PAST_LEN, state_pool, (cache_k, cache_v), wts, bb=Bs, tt=T_s, tff=1024, tq=T_s)
    keep = min(WINDOW, x_prompt.shape[1])
    return (y_p, y_s, pool_p, pool_s, k_p[:, -keep:], v_p[:, -keep:], k_s, v_s)
```

```python
import functools

import jax
import jax.numpy as jnp
import numpy as np
from jax import lax
from jax.experimental import pallas as pl
from jax.experimental.pallas import tpu as pltpu

F32 = jnp.float32
BF16 = jnp.bfloat16

D_MODEL = 2048
DEPTH = 4
PAST_LEN = 4096
CHUNK = 64
N_A_LAYERS = DEPTH // 2
POOL_WINDOWS = (2, 4, 8, 16)
POOL_GROUP = D_MODEL // len(POOL_WINDOWS)
POOL_STATE = max(POOL_WINDOWS) - 1
POOL_HALO = POOL_STATE + 1
HEAD_DIM = 64
N_HEADS = D_MODEL // HEAD_DIM
N_KV_HEADS = N_HEADS // 8
WINDOW = 128
D_FF = 4 * D_MODEL
ROPE_THETA = 10000.0
EPS = 1e-6
ATTN_SCALE = HEAD_DIM ** -0.5
NEG_INF = -1e30

LANES = 128
KEY_SPAN = 2 * LANES
N_PAIRS = N_HEADS // 2
PAIRS_PER_KV = N_PAIRS // N_KV_HEADS
MLP_OUT_CHUNK = 512
MOD_ROWS = 32
VMEM_LIMIT = 56 * 1024 * 1024


def _params(*sem):
    return pltpu.CompilerParams(dimension_semantics=sem, vmem_limit_bytes=VMEM_LIMIT)


def _rms(x, g):
    ms = jnp.mean(x * x, axis=-1, keepdims=True)
    return x * lax.rsqrt(ms + EPS) * g


def _modulate(x, shift, scale):
    return x * (1.0 + scale) + shift


def _mod_kernel(c_ref, w_ref, b_ref, o_ref):
    c = c_ref[...]
    sc = (c * jax.nn.sigmoid(c)).astype(BF16)
    o_ref[0] = jnp.dot(sc, w_ref[0].astype(BF16), preferred_element_type=F32) + b_ref[0]


def _modulation(c_all, w, b, tn=1024):
    L, D, N = w.shape
    return pl.pallas_call(
        _mod_kernel,
        out_shape=jax.ShapeDtypeStruct((L, MOD_ROWS, N), F32),
        grid=(L, N // tn),
        in_specs=[
            pl.BlockSpec((MOD_ROWS, D), lambda l, n: (0, 0)),
            pl.BlockSpec((1, D, tn), lambda l, n: (l, 0, n)),
            pl.BlockSpec((1, 1, tn), lambda l, n: (l, 0, n)),
        ],
        out_specs=pl.BlockSpec((1, MOD_ROWS, tn), lambda l, n: (l, 0, n)),
        compiler_params=_params("parallel", "parallel"),
        name="modulation",
    )(c_all, w, b)


def _pool_kernel(x_ref, mod_ref, g_ref, pre_ref, wp_ref, ps_ref, o_ref, np_ref, hbuf,
                 *, bb, tt, pos0):
    t = pl.program_id(1)

    @pl.when(t == 0)
    def _():
        hbuf[:, 1:POOL_HALO, :] = pre_ref[0]

    @pl.when(t > 0)
    def _():
        hbuf[:, 0:POOL_HALO, :] = hbuf[:, tt:tt + POOL_HALO, :]

    x = x_ref[...]
    m = mod_ref[...]
    h = _modulate(_rms(x, g_ref[:, 0:1, :]), m[:, 0:1, :], m[:, 1:2, :])
    hbuf[:, POOL_HALO:POOL_HALO + tt, :] = h
    np_ref[0] = hbuf[:, tt + 1:tt + POOL_HALO, :]

    pos = pos0 + t * tt + lax.broadcasted_iota(jnp.int32, (1, tt, POOL_GROUP), 1)
    ys = []
    for g, w in enumerate(POOL_WINDOWS):
        cs = slice(g * POOL_GROUP, (g + 1) * POOL_GROUP)
        acc = hbuf[:, POOL_HALO:POOL_HALO + tt, cs]
        hg = acc
        for j in range(1, w):
            acc = acc + hbuf[:, POOL_HALO - j:POOL_HALO - j + tt, cs]
        cnt = jnp.minimum(w, pos + 1).astype(F32)
        pooled = acc / cnt - hg
        ys.append(jnp.dot(pooled.reshape(bb * tt, POOL_GROUP).astype(BF16), wp_ref[0, g],
                          preferred_element_type=F32))
    y = jnp.concatenate(ys, axis=-1).reshape(bb, tt, D_MODEL) * ps_ref[...]
    o_ref[...] = x + m[:, 2:3, :] * _rms(y, g_ref[:, 1:2, :])


def _pool_layer(x, mod, g_norm, prefix, w_pool, pool_scale, l, *, bb, tt, pos0):
    B, T, D = x.shape
    kern = functools.partial(_pool_kernel, bb=bb, tt=tt, pos0=pos0)
    return pl.pallas_call(
        kern,
        out_shape=(jax.ShapeDtypeStruct((B, T, D), F32),
                   jax.ShapeDtypeStruct((1, B, POOL_STATE, D), F32)),
        grid=(B // bb, T // tt),
        in_specs=[
            pl.BlockSpec((bb, tt, D), lambda b, t: (b, t, 0)),
            pl.BlockSpec((bb, 6, D), lambda b, t: (b, 0, 0)),
            pl.BlockSpec((1, 4, D), lambda b, t: (l, 0, 0)),
            pl.BlockSpec((1, bb, POOL_STATE, D), lambda b, t: (l, b, 0, 0)),
            pl.BlockSpec((1, len(POOL_WINDOWS), POOL_GROUP, POOL_GROUP),
                         lambda b, t: (l, 0, 0, 0)),
            pl.BlockSpec((1, 1, D), lambda b, t: (l, 0, 0)),
        ],
        out_specs=(pl.BlockSpec((bb, tt, D), lambda b, t: (b, t, 0)),
                   pl.BlockSpec((1, bb, POOL_STATE, D), lambda b, t: (0, b, 0, 0))),
        scratch_shapes=[pltpu.VMEM((bb, POOL_HALO + tt, D), F32)],
        compiler_params=_params("parallel", "arbitrary"),
        name=f"pool_mixer_{l}",
    )(x, mod, g_norm, prefix, w_pool, pool_scale)


def _rope_tables(pos):
    half = HEAD_DIM // 2
    inv = ROPE_THETA ** (-jnp.arange(half, dtype=F32) / half)
    ang = pos.astype(F32)[:, None] * inv[None, :]
    cos, sin = jnp.cos(ang), jnp.sin(ang)
    zero = jnp.zeros_like(sin)
    c = jnp.tile(cos, (1, 4))
    s_lo = jnp.tile(jnp.concatenate([-sin, zero], axis=1), (1, 2))
    s_hi = jnp.tile(jnp.concatenate([zero, sin], axis=1), (1, 2))
    return c, s_lo, s_hi


def _rope_block(blk, c, s_lo, s_hi):
    return (blk * c + pltpu.roll(blk, LANES - HEAD_DIM // 2, 1) * s_lo
            + pltpu.roll(blk, HEAD_DIM // 2, 1) * s_hi)


def _qproj_kernel(x_ref, mod_ref, g_ref, c_ref, slo_ref, shi_ref, w_ref, q_ref, *, bb, tt):
    x = x_ref[...]
    m = mod_ref[...]
    h = _modulate(_rms(x, g_ref[:, 0:1, :]), m[:, 0:1, :], m[:, 1:2, :])
    q = jnp.dot(h.reshape(bb * tt, D_MODEL).astype(BF16), w_ref[0], preferred_element_type=F32)
    c, s_lo, s_hi = c_ref[...], slo_ref[...], shi_ref[...]
    for p in range(N_PAIRS):
        cs = slice(p * LANES, (p + 1) * LANES)
        q_ref[p] = (_rope_block(q[:, cs], c, s_lo, s_hi) * ATTN_SCALE).astype(BF16)


def _qproj_layer(x, mod, g_norm, tables, w_q, l, j, *, bb, tt):
    B, T, D = x.shape
    nt = T // tt
    M = bb * tt
    kern = functools.partial(_qproj_kernel, bb=bb, tt=tt)
    tab_spec = pl.BlockSpec((M, LANES), lambda b, t: (t, 0))
    return pl.pallas_call(
        kern,
        out_shape=jax.ShapeDtypeStruct((N_PAIRS, B * T, LANES), BF16),
        grid=(B // bb, nt),
        in_specs=[
            pl.BlockSpec((bb, tt, D), lambda b, t: (b, t, 0)),
            pl.BlockSpec((bb, 6, D), lambda b, t: (b, 0, 0)),
            pl.BlockSpec((1, 4, D), lambda b, t: (l, 0, 0)),
            tab_spec, tab_spec, tab_spec,
            pl.BlockSpec((1, D, D), lambda b, t: (j, 0, 0)),
        ],
        out_specs=pl.BlockSpec((N_PAIRS, M, LANES), lambda b, t: (0, b * nt + t, 0)),
        compiler_params=_params("parallel", "parallel"),
        name=f"q_proj_{l}",
    )(x, mod, g_norm, *tables, w_q)


def _kv_kernel(x_ref, mod_ref, g_ref, c_ref, slo_ref, shi_ref, w_ref,
               k_ref, v_ref, kd_ref, vd_ref, *, bb, tt):
    x = x_ref[...]
    m = mod_ref[...]
    h = _modulate(_rms(x, g_ref[...]), m[:, 0:1, :], m[:, 1:2, :])
    kv = jnp.dot(h.reshape(bb * tt, D_MODEL).astype(BF16), w_ref[...],
                 preferred_element_type=F32)
    c, s_lo, s_hi = c_ref[...], slo_ref[...], shi_ref[...]
    lo = lax.broadcasted_iota(jnp.int32, (bb * tt, LANES), 1) < HEAD_DIM
    zero = jnp.zeros((bb * tt, LANES), F32)
    ks, vs = [], []
    for j in range(N_KV_HEADS):
        ks.append(_rope_block(kv[:, j * LANES:(j + 1) * LANES], c, s_lo, s_hi))
        vs.append(kv[:, (N_KV_HEADS + j) * LANES:(N_KV_HEADS + j + 1) * LANES])
    for j in range(N_KV_HEADS):
        for src, dst in ((ks, kd_ref), (vs, vd_ref)):
            dst[:, (2 * j) * LANES:(2 * j + 1) * LANES] = jnp.where(lo, src[j], zero).astype(BF16)
            dst[:, (2 * j + 1) * LANES:(2 * j + 2) * LANES] = jnp.where(lo, zero, src[j]).astype(BF16)
    for p in range(N_KV_HEADS // 2):
        cs = slice(p * LANES, (p + 1) * LANES)
        k_ref[:, cs] = jnp.where(lo, ks[2 * p], ks[2 * p + 1])
        v_ref[:, cs] = jnp.where(lo, vs[2 * p], vs[2 * p + 1])


def _kv_layer(x, mod, g_kv, tables, w_kv_dup, *, bb, tt):
    B, T, D = x.shape
    nt = T // tt
    M = bb * tt
    KV = N_KV_HEADS * HEAD_DIM
    kern = functools.partial(_kv_kernel, bb=bb, tt=tt)
    tab_spec = pl.BlockSpec((M, LANES), lambda b, t: (t, 0))
    row = lambda b, t: (b * nt + t, 0)
    return pl.pallas_call(
        kern,
        out_shape=(jax.ShapeDtypeStruct((B * T, KV), F32),
                   jax.ShapeDtypeStruct((B * T, KV), F32),
                   jax.ShapeDtypeStruct((B * T, 4 * KV), BF16),
                   jax.ShapeDtypeStruct((B * T, 4 * KV), BF16)),
        grid=(B // bb, nt),
        in_specs=[
            pl.BlockSpec((bb, tt, D), lambda b, t: (b, t, 0)),
            pl.BlockSpec((bb, 2, D), lambda b, t: (b, 0, 0)),
            pl.BlockSpec((1, 1, D), lambda b, t: (0, 0, 0)),
            tab_spec, tab_spec, tab_spec,
            pl.BlockSpec((D, 4 * KV), lambda b, t: (0, 0)),
        ],
        out_specs=(pl.BlockSpec((M, KV), row), pl.BlockSpec((M, KV), row),
                   pl.BlockSpec((M, 4 * KV), row), pl.BlockSpec((M, 4 * KV), row)),
        compiler_params=_params("parallel", "parallel"),
        name="shared_kv",
    )(x, mod, g_kv, *tables, w_kv_dup)


def _sink_softmax(s, sink):
    m = jnp.maximum(jnp.max(s, axis=-1, keepdims=True), sink)
    e = jnp.exp(s - m)
    den = jnp.sum(e, axis=-1, keepdims=True) + jnp.exp(sink - m)
    return e * (1.0 / den)


def _attn_kernel(sink_ref, q_ref, kd_ref, vd_ref, o_ref, *, tq, chunked, n_valid):
    i = pl.program_id(1)
    G = PAIRS_PER_KV
    R = G * tq
    kp = lax.broadcasted_iota(jnp.int32, (R, KEY_SPAN), 1)
    if chunked:
        start = pl.multiple_of(i * tq, tq)
        kw = kd_ref[0, pl.ds(start, KEY_SPAN), :]
        vw = vd_ref[0, pl.ds(start, KEY_SPAN), :]
        row = lax.broadcasted_iota(jnp.int32, (R, KEY_SPAN), 0)
        cq = (row % tq) // CHUNK
        kb = kp // CHUNK
        first_real = jnp.where(i > 0, 0, WINDOW)
        valid = (kb >= cq) & (kb <= cq + WINDOW // CHUNK) & (kp >= first_real)
    else:
        kw = kd_ref[0]
        vw = vd_ref[0]
        valid = kp < n_valid
    bias = jnp.where(valid, 0.0, NEG_INF)
    rblk = lax.broadcasted_iota(jnp.int32, (R, 1), 0) // tq
    nt = (((1,), (1,)), ((), ()))
    for j in range(N_KV_HEADS):
        sink_cols = []
        for parity in range(2):
            col = jnp.full((R, 1), sink_ref[2 * G * j + parity], F32)
            for r in range(1, G):
                col = jnp.where(rblk == r, sink_ref[2 * (G * j + r) + parity], col)
            sink_cols.append(col)
        qg = q_ref[G * j:G * (j + 1)].reshape(R, LANES)
        k_lo = kw[:, (2 * j) * LANES:(2 * j + 1) * LANES]
        k_hi = kw[:, (2 * j + 1) * LANES:(2 * j + 2) * LANES]
        v_lo = vw[:, (2 * j) * LANES:(2 * j + 1) * LANES]
        v_hi = vw[:, (2 * j + 1) * LANES:(2 * j + 2) * LANES]
        s0 = lax.dot_general(qg, k_lo, nt, preferred_element_type=F32) + bias
        s1 = lax.dot_general(qg, k_hi, nt, preferred_element_type=F32) + bias
        p0 = _sink_softmax(s0, sink_cols[0])
        p1 = _sink_softmax(s1, sink_cols[1])
        o = (jnp.dot(p0.astype(BF16), v_lo, preferred_element_type=F32)
             + jnp.dot(p1.astype(BF16), v_hi, preferred_element_type=F32))
        o_ref[G * j:G * (j + 1)] = o.reshape(G, tq, LANES).astype(BF16)


def _attention(q, kd, vd, sinks, *, B, T, tq, chunked, n_valid):
    nq = T // tq
    Tk = kd.shape[1]
    W = kd.shape[2]
    kern = functools.partial(_attn_kernel, tq=tq, chunked=chunked, n_valid=n_valid)
    return pl.pallas_call(
        kern,
        out_shape=jax.ShapeDtypeStruct((N_PAIRS, B * T, LANES), BF16),
        grid=(B, nq),
        in_specs=[
            pl.BlockSpec(memory_space=pltpu.SMEM),
            pl.BlockSpec((N_PAIRS, tq, LANES), lambda b, i: (0, b * nq + i, 0)),
            pl.BlockSpec((1, Tk, W), lambda b, i: (b, 0, 0)),
            pl.BlockSpec((1, Tk, W), lambda b, i: (b, 0, 0)),
        ],
        out_specs=pl.BlockSpec((N_PAIRS, tq, LANES), lambda b, i: (0, b * nq + i, 0)),
        compiler_params=_params("parallel", "parallel"),
        name="swa_attention",
    )(sinks, q, kd, vd)


def _oproj_kernel(o_ref, x_ref, mod_ref, g_ref, w_ref, y_ref, *, bb, tt):
    o = jnp.concatenate([o_ref[p] for p in range(N_PAIRS)], axis=1)
    mix = jnp.dot(o, w_ref[0], preferred_element_type=F32).reshape(bb, tt, D_MODEL)
    m = mod_ref[...]
    y_ref[...] = x_ref[...] + m[:, 2:3, :] * _rms(mix, g_ref[:, 1:2, :])


def _oproj_layer(o, x, mod, g_norm, w_o, l, j, *, bb, tt):
    B, T, D = x.shape
    nt = T // tt
    kern = functools.partial(_oproj_kernel, bb=bb, tt=tt)
    return pl.pallas_call(
        kern,
        out_shape=jax.ShapeDtypeStruct((B, T, D), F32),
        grid=(B // bb, nt),
        in_specs=[
            pl.BlockSpec((N_PAIRS, bb * tt, LANES), lambda b, t: (0, b * nt + t, 0)),
            pl.BlockSpec((bb, tt, D), lambda b, t: (b, t, 0)),
            pl.BlockSpec((bb, 6, D), lambda b, t: (b, 0, 0)),
            pl.BlockSpec((1, 4, D), lambda b, t: (l, 0, 0)),
            pl.BlockSpec((1, D, D), lambda b, t: (j, 0, 0)),
        ],
        out_specs=pl.BlockSpec((bb, tt, D), lambda b, t: (b, t, 0)),
        compiler_params=_params("parallel", "parallel"),
        name=f"o_proj_{l}",
    )(o, x, mod, g_norm, w_o)


def _mlp_kernel(x_ref, mod_ref, g_ref, wu_ref, wd_ref, y_ref, h_ref, *, bb, tt):
    f = pl.program_id(2)
    m = mod_ref[...]

    @pl.when(f == 0)
    def _():
        h = _modulate(_rms(x_ref[...], g_ref[:, 2:3, :]), m[:, 3:4, :], m[:, 4:5, :])
        h_ref[...] = h.reshape(bb * tt, D_MODEL).astype(BF16)
        y_ref[...] = jnp.zeros_like(y_ref)

    u = jnp.dot(h_ref[...], wu_ref[0], preferred_element_type=F32)
    a = jnp.square(jnp.maximum(u, 0.0)).astype(BF16)
    for n in range(D_MODEL // MLP_OUT_CHUNK):
        cs = slice(n * MLP_OUT_CHUNK, (n + 1) * MLP_OUT_CHUNK)
        y_ref[:, :, cs] += jnp.dot(a, wd_ref[0, :, cs], preferred_element_type=F32).reshape(
            bb, tt, MLP_OUT_CHUNK)

    @pl.when(f == pl.num_programs(2) - 1)
    def _():
        y_ref[...] = x_ref[...] + m[:, 5:6, :] * _rms(y_ref[...], g_ref[:, 3:4, :])


def _mlp_layer(x, mod, g_norm, w_up, w_down, l, *, bb, tt, tff):
    B, T, D = x.shape
    kern = functools.partial(_mlp_kernel, bb=bb, tt=tt)
    return pl.pallas_call(
        kern,
        out_shape=jax.ShapeDtypeStruct((B, T, D), F32),
        grid=(B // bb, T // tt, D_FF // tff),
        in_specs=[
            pl.BlockSpec((bb, tt, D), lambda b, t, f: (b, t, 0)),
            pl.BlockSpec((bb, 6, D), lambda b, t, f: (b, 0, 0)),
            pl.BlockSpec((1, 4, D), lambda b, t, f: (l, 0, 0)),
            pl.BlockSpec((1, D, tff), lambda b, t, f: (l, 0, f)),
            pl.BlockSpec((1, tff, D), lambda b, t, f: (l, f, 0)),
        ],
        out_specs=pl.BlockSpec((bb, tt, D), lambda b, t, f: (b, t, 0)),
        scratch_shapes=[pltpu.VMEM((bb * tt, D), BF16)],
        compiler_params=_params("parallel", "parallel", "arbitrary"),
        name=f"mlp_{l}",
    )(x, mod, g_norm, w_up, w_down)


def _split_dup(a):
    z = jnp.zeros_like(a)
    lo = jnp.concatenate([a, z], axis=-1)
    hi = jnp.concatenate([z, a], axis=-1)
    out = jnp.stack([lo, hi], axis=3)
    return out.reshape(a.shape[0], a.shape[1], -1).astype(BF16)


def _trunk(x, mods, kvmod, pos0, prefix, kv_past, wts, *, bb, tt, tff, tq):
    B, T, D = x.shape
    pos = pos0 + jnp.arange(T)
    tables = tuple(jnp.tile(tb, (bb, 1)) if bb > 1 else tb for tb in _rope_tables(pos))
    new_pool = []
    k = v = kd = vd = None
    for l in range(DEPTH):
        mod = mods[l]
        if l < N_A_LAYERS:
            x, npool = _pool_layer(x, mod, wts["g_norm"], prefix, wts["w_pool"],
                                   wts["pool_scale"], l, bb=bb, tt=tt, pos0=pos0)
            new_pool.append(npool)
        else:
            j = l - N_A_LAYERS
            q = _qproj_layer(x, mod, wts["g_norm"], tables, wts["w_q"], l, j, bb=bb, tt=tt)
            o = _attention(q, kd, vd, wts["sinks"][j], B=B, T=T, tq=tq,
                           chunked=kv_past is None,
                           n_valid=None if kv_past is None else kv_past[0].shape[1] + T)
            x = _oproj_layer(o, x, mod, wts["g_norm"], wts["w_o"], l, j, bb=bb, tt=tt)
        x = _mlp_layer(x, mod, wts["g_norm"], wts["w_up"], wts["w_down"], l,
                       bb=bb, tt=tt, tff=tff)
        if l == N_A_LAYERS - 1:
            k, v, kd, vd = _kv_layer(x, kvmod, wts["g_kv"], tables, wts["w_kv_dup"],
                                     bb=bb, tt=tt)
            W = kd.shape[-1]
            kd = kd.reshape(B, T, W)
            vd = vd.reshape(B, T, W)
            if kv_past is None:
                front = ((0, 0), (WINDOW, 0), (0, 0))
                kd, vd = jnp.pad(kd, front), jnp.pad(vd, front)
            else:
                back = ((0, 0), (0, KEY_SPAN - kv_past[0].shape[1] - T), (0, 0))
                kd = jnp.pad(jnp.concatenate([_split_dup(kv_past[0]), kd], axis=1), back)
                vd = jnp.pad(jnp.concatenate([_split_dup(kv_past[1]), vd], axis=1), back)
    k = k.reshape(B, T, N_KV_HEADS, HEAD_DIM)
    v = v.reshape(B, T, N_KV_HEADS, HEAD_DIM)
    return x, jnp.concatenate(new_pool, axis=0), k, v


def _prep_weights(g_norm, w_pool, pool_scale, g_kv, w_kv, w_q, sinks, w_o, w_up, w_down):
    D = D_MODEL
    KV = N_KV_HEADS * HEAD_DIM
    wk = w_kv[:, :KV].reshape(D, N_KV_HEADS, 1, HEAD_DIM)
    wv = w_kv[:, KV:].reshape(D, N_KV_HEADS, 1, HEAD_DIM)
    w_kv_dup = jnp.concatenate([jnp.tile(wk, (1, 1, 2, 1)).reshape(D, 2 * KV),
                                jnp.tile(wv, (1, 1, 2, 1)).reshape(D, 2 * KV)], axis=1)
    return {
        "g_norm": g_norm,
        "w_pool": w_pool.astype(BF16),
        "pool_scale": pool_scale.reshape(N_A_LAYERS, 1, D),
        "g_kv": g_kv.reshape(1, 1, D),
        "w_kv_dup": w_kv_dup.astype(BF16),
        "w_q": w_q.astype(BF16),
        "sinks": sinks,
        "w_o": w_o.astype(BF16),
        "w_up": w_up.astype(BF16),
        "w_down": w_down.astype(BF16),
    }


def kernel(x_prompt, x_sample, c_prompt, c_sample, state_pool, cache_k, cache_v, w_mod, b_mod,
           g_norm, w_pool, pool_scale, w_kv_mod, b_kv_mod, g_kv, w_kv, w_q, sinks, w_o, w_up,
           w_down):
    Bp, Bs = x_prompt.shape[0], x_sample.shape[0]
    D = D_MODEL
    KV = N_KV_HEADS * HEAD_DIM

    c_all = jnp.concatenate(
        [c_prompt, c_sample, jnp.zeros((MOD_ROWS - Bp - Bs, D), F32)], axis=0)
    mod_all = _modulation(c_all, w_mod, b_mod.reshape(DEPTH, 1, 6 * D)).reshape(
        DEPTH, MOD_ROWS, 6, D)
    kvmod_all = _modulation(c_all, w_kv_mod.reshape(1, D, 2 * D),
                            b_kv_mod.reshape(1, 1, 2 * D)).reshape(MOD_ROWS, 2, D)

    wts = _prep_weights(g_norm, w_pool, pool_scale, g_kv, w_kv, w_q, sinks, w_o, w_up, w_down)

    prompt_prefix = jnp.zeros((N_A_LAYERS, Bp, POOL_STATE, D), F32)
    y_p, pool_p, k_p, v_p = _trunk(
        x_prompt, [mod_all[l, :Bp] for l in range(DEPTH)], kvmod_all[:Bp], 0, prompt_prefix,
        None, wts, bb=1, tt=512, tff=1024, tq=LANES)
    T_s = x_sample.shape[1]
    y_s, pool_s, k_s, v_s = _trunk(
        x_sample, [mod_all[l, Bp:Bp + Bs] for l in range(DEPTH)], kvmod_all[Bp:Bp + Bs],
        PAST_LEN, state_pool, (cache_k, cache_v), wts, bb=Bs, tt=T_s, tff=1024, tq=T_s)
    keep = min(WINDOW, x_prompt.shape[1])
    return (y_p, y_s, pool_p, pool_s, k_p[:, -keep:], v_p[:, -keep:], k_s, v_s)
```

```python
import functools

import jax
import jax.numpy as jnp
import numpy as np
from jax import lax
from jax.experimental import pallas as pl
from jax.experimental.pallas import tpu as pltpu

F32 = jnp.float32
BF16 = jnp.bfloat16

D_MODEL = 2048
DEPTH = 4
PAST_LEN = 4096
CHUNK = 64
N_A_LAYERS = DEPTH // 2
POOL_WINDOWS = (2, 4, 8, 16)
POOL_GROUP = D_MODEL // len(POOL_WINDOWS)
POOL_STATE = max(POOL_WINDOWS) - 1
SUBLANES = 8
POOL_LEAD = SUBLANES
POOL_BASE = POOL_LEAD + POOL_STATE + 1
HEAD_DIM = 64
N_HEADS = D_MODEL // HEAD_DIM
N_KV_HEADS = N_HEADS // 8
WINDOW = 128
D_FF = 4 * D_MODEL
ROPE_THETA = 10000.0
EPS = 1e-6
ATTN_SCALE = HEAD_DIM ** -0.5
NEG_INF = -1e30

LANES = 128
KEY_SPAN = 2 * LANES
N_PAIRS = N_HEADS // 2
PAIRS_PER_KV = N_PAIRS // N_KV_HEADS
MLP_OUT_CHUNK = 512
MOD_ROWS = 32
VMEM_LIMIT = 56 * 1024 * 1024


def _params(*sem):
    return pltpu.CompilerParams(dimension_semantics=sem, vmem_limit_bytes=VMEM_LIMIT)


def _rms(x, g):
    ms = jnp.mean(x * x, axis=-1, keepdims=True)
    return x * lax.rsqrt(ms + EPS) * g


def _modulate(x, shift, scale):
    return x * (1.0 + scale) + shift


def _mod_kernel(c_ref, w_ref, b_ref, o_ref):
    c = c_ref[...]
    sc = (c * jax.nn.sigmoid(c)).astype(BF16)
    o_ref[0] = jnp.dot(sc, w_ref[0].astype(BF16), preferred_element_type=F32) + b_ref[0]


def _modulation(c_all, w, b, tn=1024):
    L, D, N = w.shape
    return pl.pallas_call(
        _mod_kernel,
        out_shape=jax.ShapeDtypeStruct((L, MOD_ROWS, N), F32),
        grid=(L, N // tn),
        in_specs=[
            pl.BlockSpec((MOD_ROWS, D), lambda l, n: (0, 0)),
            pl.BlockSpec((1, D, tn), lambda l, n: (l, 0, n)),
            pl.BlockSpec((1, 1, tn), lambda l, n: (l, 0, n)),
        ],
        out_specs=pl.BlockSpec((1, MOD_ROWS, tn), lambda l, n: (l, 0, n)),
        compiler_params=_params("parallel", "parallel"),
        name="modulation",
    )(c_all, w, b)


def _pool_kernel(x_ref, mod_ref, g_ref, pre_ref, wp_ref, ps_ref, o_ref, np_ref, hbuf, s1, s2,
                 *, bb, tt, pos0):
    t = pl.program_id(1)
    G = POOL_GROUP
    LEAD, BASE = POOL_LEAD, POOL_BASE
    L = BASE + tt

    @pl.when(t == 0)
    def _():
        hbuf[:, 0:LEAD + 1, :] = jnp.zeros((bb, LEAD + 1, D_MODEL), F32)
        hbuf[:, LEAD + 1:BASE, :] = pre_ref[0]
        s1[:, 0:LEAD, :] = jnp.zeros((bb, LEAD, 3 * G), F32)
        s2[:, 0:LEAD, :] = jnp.zeros((bb, LEAD, 3 * G), F32)

    @pl.when(t > 0)
    def _():
        hbuf[:, LEAD:BASE, :] = hbuf[:, tt + LEAD:tt + BASE, :]

    x = x_ref[...]
    m = mod_ref[...]
    h = _modulate(_rms(x, g_ref[:, 0:1, :]), m[:, 0:1, :], m[:, 1:2, :])
    hbuf[:, BASE:L, :] = h
    np_ref[0] = hbuf[:, L - POOL_STATE:L, :]

    s1[:, LEAD:L, :] = hbuf[:, LEAD:L, G:] + hbuf[:, LEAD - 1:L - 1, G:]
    s2[:, LEAD:L, :] = s1[:, LEAD:L, :] + s1[:, LEAD - 2:L - 2, :]
    s1[:, 2 * LEAD:L, 0:2 * G] = (s2[:, 2 * LEAD:L, G:] +
                                  s2[:, 2 * LEAD - 4:L - 4, G:])
    sums = [
        hbuf[:, BASE:L, 0:G] + hbuf[:, BASE - 1:L - 1, 0:G],
        s2[:, BASE:L, 0:G],
        s1[:, BASE:L, 0:G],
        s1[:, BASE:L, G:2 * G] + s1[:, BASE - 8:L - 8, G:2 * G],
    ]

    pos = pos0 + t * tt + lax.broadcasted_iota(jnp.int32, (1, tt, POOL_GROUP), 1)
    ys = []
    for g, w in enumerate(POOL_WINDOWS):
        cs = slice(g * POOL_GROUP, (g + 1) * POOL_GROUP)
        cnt = jnp.minimum(w, pos + 1).astype(F32)
        pooled = sums[g] / cnt - hbuf[:, BASE:L, cs]
        ys.append(jnp.dot(pooled.reshape(bb * tt, POOL_GROUP).astype(BF16), wp_ref[0, g],
                          preferred_element_type=F32))
    y = jnp.concatenate(ys, axis=-1).reshape(bb, tt, D_MODEL) * ps_ref[...]
    o_ref[...] = x + m[:, 2:3, :] * _rms(y, g_ref[:, 1:2, :])


def _pool_layer(x, mod, g_norm, prefix, w_pool, pool_scale, l, *, bb, tt, pos0):
    B, T, D = x.shape
    kern = functools.partial(_pool_kernel, bb=bb, tt=tt, pos0=pos0)
    return pl.pallas_call(
        kern,
        out_shape=(jax.ShapeDtypeStruct((B, T, D), F32),
                   jax.ShapeDtypeStruct((1, B, POOL_STATE, D), F32)),
        grid=(B // bb, T // tt),
        in_specs=[
            pl.BlockSpec((bb, tt, D), lambda b, t: (b, t, 0)),
            pl.BlockSpec((bb, 6, D), lambda b, t: (b, 0, 0)),
            pl.BlockSpec((1, 4, D), lambda b, t: (l, 0, 0)),
            pl.BlockSpec((1, bb, POOL_STATE, D), lambda b, t: (l, b, 0, 0)),
            pl.BlockSpec((1, len(POOL_WINDOWS), POOL_GROUP, POOL_GROUP),
                         lambda b, t: (l, 0, 0, 0)),
            pl.BlockSpec((1, 1, D), lambda b, t: (l, 0, 0)),
        ],
        out_specs=(pl.BlockSpec((bb, tt, D), lambda b, t: (b, t, 0)),
                   pl.BlockSpec((1, bb, POOL_STATE, D), lambda b, t: (0, b, 0, 0))),
        scratch_shapes=[pltpu.VMEM((bb, POOL_BASE + tt, D), F32),
                        pltpu.VMEM((bb, POOL_BASE + tt, 3 * POOL_GROUP), F32),
                        pltpu.VMEM((bb, POOL_BASE + tt, 3 * POOL_GROUP), F32)],
        compiler_params=_params("parallel", "arbitrary"),
        name=f"pool_mixer_{l}",
    )(x, mod, g_norm, prefix, w_pool, pool_scale)


def _rope_tables(pos):
    half = HEAD_DIM // 2
    inv = ROPE_THETA ** (-jnp.arange(half, dtype=F32) / half)
    ang = pos.astype(F32)[:, None] * inv[None, :]
    cos, sin = jnp.cos(ang), jnp.sin(ang)
    zero = jnp.zeros_like(sin)
    c = jnp.tile(cos, (1, 4))
    s_lo = jnp.tile(jnp.concatenate([-sin, zero], axis=1), (1, 2))
    s_hi = jnp.tile(jnp.concatenate([zero, sin], axis=1), (1, 2))
    return c, s_lo, s_hi


def _rope_block(blk, c, s_lo, s_hi):
    return (blk * c + pltpu.roll(blk, LANES - HEAD_DIM // 2, 1) * s_lo
            + pltpu.roll(blk, HEAD_DIM // 2, 1) * s_hi)


def _qproj_kernel(x_ref, mod_ref, g_ref, c_ref, slo_ref, shi_ref, w_ref, q_ref, *, bb, tt):
    x = x_ref[...]
    m = mod_ref[...]
    h = _modulate(_rms(x, g_ref[:, 0:1, :]), m[:, 0:1, :], m[:, 1:2, :])
    q = jnp.dot(h.reshape(bb * tt, D_MODEL).astype(BF16), w_ref[0], preferred_element_type=F32)
    c, s_lo, s_hi = c_ref[...], slo_ref[...], shi_ref[...]
    for p in range(N_PAIRS):
        cs = slice(p * LANES, (p + 1) * LANES)
        q_ref[p] = (_rope_block(q[:, cs], c, s_lo, s_hi) * ATTN_SCALE).astype(BF16)


def _qproj_layer(x, mod, g_norm, tables, w_q, l, j, *, bb, tt):
    B, T, D = x.shape
    nt = T // tt
    M = bb * tt
    kern = functools.partial(_qproj_kernel, bb=bb, tt=tt)
    tab_spec = pl.BlockSpec((M, LANES), lambda b, t: (t, 0))
    return pl.pallas_call(
        kern,
        out_shape=jax.ShapeDtypeStruct((N_PAIRS, B * T, LANES), BF16),
        grid=(B // bb, nt),
        in_specs=[
            pl.BlockSpec((bb, tt, D), lambda b, t: (b, t, 0)),
            pl.BlockSpec((bb, 6, D), lambda b, t: (b, 0, 0)),
            pl.BlockSpec((1, 4, D), lambda b, t: (l, 0, 0)),
            tab_spec, tab_spec, tab_spec,
            pl.BlockSpec((1, D, D), lambda b, t: (j, 0, 0)),
        ],
        out_specs=pl.BlockSpec((N_PAIRS, M, LANES), lambda b, t: (0, b * nt + t, 0)),
        compiler_params=_params("parallel", "parallel"),
        name=f"q_proj_{l}",
    )(x, mod, g_norm, *tables, w_q)


def _kv_kernel(x_ref, mod_ref, g_ref, c_ref, slo_ref, shi_ref, w_ref,
               k_ref, v_ref, kd_ref, vd_ref, *, bb, tt):
    x = x_ref[...]
    m = mod_ref[...]
    h = _modulate(_rms(x, g_ref[...]), m[:, 0:1, :], m[:, 1:2, :])
    kv = jnp.dot(h.reshape(bb * tt, D_MODEL).astype(BF16), w_ref[...],
                 preferred_element_type=F32)
    c, s_lo, s_hi = c_ref[...], slo_ref[...], shi_ref[...]
    lo = lax.broadcasted_iota(jnp.int32, (bb * tt, LANES), 1) < HEAD_DIM
    zero = jnp.zeros((bb * tt, LANES), F32)
    ks, vs = [], []
    for j in range(N_KV_HEADS):
        ks.append(_rope_block(kv[:, j * LANES:(j + 1) * LANES], c, s_lo, s_hi))
        vs.append(kv[:, (N_KV_HEADS + j) * LANES:(N_KV_HEADS + j + 1) * LANES])
    for j in range(N_KV_HEADS):
        for src, dst in ((ks, kd_ref), (vs, vd_ref)):
            dst[:, (2 * j) * LANES:(2 * j + 1) * LANES] = jnp.where(lo, src[j], zero).astype(BF16)
            dst[:, (2 * j + 1) * LANES:(2 * j + 2) * LANES] = jnp.where(lo, zero, src[j]).astype(BF16)
    for p in range(N_KV_HEADS // 2):
        cs = slice(p * LANES, (p + 1) * LANES)
        k_ref[:, cs] = jnp.where(lo, ks[2 * p], ks[2 * p + 1])
        v_ref[:, cs] = jnp.where(lo, vs[2 * p], vs[2 * p + 1])


def _kv_layer(x, mod, g_kv, tables, w_kv_dup, *, bb, tt):
    B, T, D = x.shape
    nt = T // tt
    M = bb * tt
    KV = N_KV_HEADS * HEAD_DIM
    kern = functools.partial(_kv_kernel, bb=bb, tt=tt)
    tab_spec = pl.BlockSpec((M, LANES), lambda b, t: (t, 0))
    row = lambda b, t: (b * nt + t, 0)
    return pl.pallas_call(
        kern,
        out_shape=(jax.ShapeDtypeStruct((B * T, KV), F32),
                   jax.ShapeDtypeStruct((B * T, KV), F32),
                   jax.ShapeDtypeStruct((B * T, 4 * KV), BF16),
                   jax.ShapeDtypeStruct((B * T, 4 * KV), BF16)),
        grid=(B // bb, nt),
        in_specs=[
            pl.BlockSpec((bb, tt, D), lambda b, t: (b, t, 0)),
            pl.BlockSpec((bb, 2, D), lambda b, t: (b, 0, 0)),
            pl.BlockSpec((1, 1, D), lambda b, t: (0, 0, 0)),
            tab_spec, tab_spec, tab_spec,
            pl.BlockSpec((D, 4 * KV), lambda b, t: (0, 0)),
        ],
        out_specs=(pl.BlockSpec((M, KV), row), pl.BlockSpec((M, KV), row),
                   pl.BlockSpec((M, 4 * KV), row), pl.BlockSpec((M, 4 * KV), row)),
        compiler_params=_params("parallel", "parallel"),
        name="shared_kv",
    )(x, mod, g_kv, *tables, w_kv_dup)


def _sink_softmax(s, sink):
    m = jnp.maximum(jnp.max(s, axis=-1, keepdims=True), sink)
    e = jnp.exp(s - m)
    den = jnp.sum(e, axis=-1, keepdims=True) + jnp.exp(sink - m)
    return e * (1.0 / den)


def _attn_kernel(sink_ref, q_ref, *refs, tq, chunked, n_valid):
    i = pl.program_id(1)
    G = PAIRS_PER_KV
    R = G * tq
    kp = lax.broadcasted_iota(jnp.int32, (R, KEY_SPAN), 1)
    if chunked:
        k_prev, k_cur, v_prev, v_cur, o_ref = refs
        kw = jnp.concatenate([k_prev[0], k_cur[0]], axis=0)
        vw = jnp.concatenate([v_prev[0], v_cur[0]], axis=0)
        row = lax.broadcasted_iota(jnp.int32, (R, KEY_SPAN), 0)
        cq = (row % tq) // CHUNK
        kb = kp // CHUNK
        first_real = jnp.where(i > 0, 0, WINDOW)
        valid = (kb >= cq) & (kb <= cq + WINDOW // CHUNK) & (kp >= first_real)
    else:
        kd_ref, vd_ref, o_ref = refs
        kw = kd_ref[0]
        vw = vd_ref[0]
        valid = kp < n_valid
    bias = jnp.where(valid, 0.0, NEG_INF)
    rblk = lax.broadcasted_iota(jnp.int32, (R, 1), 0) // tq
    nt = (((1,), (1,)), ((), ()))
    for j in range(N_KV_HEADS):
        sink_cols = []
        for parity in range(2):
            col = jnp.full((R, 1), sink_ref[2 * G * j + parity], F32)
            for r in range(1, G):
                col = jnp.where(rblk == r, sink_ref[2 * (G * j + r) + parity], col)
            sink_cols.append(col)
        qg = q_ref[G * j:G * (j + 1)].reshape(R, LANES)
        k_lo = kw[:, (2 * j) * LANES:(2 * j + 1) * LANES]
        k_hi = kw[:, (2 * j + 1) * LANES:(2 * j + 2) * LANES]
        v_lo = vw[:, (2 * j) * LANES:(2 * j + 1) * LANES]
        v_hi = vw[:, (2 * j + 1) * LANES:(2 * j + 2) * LANES]
        s0 = lax.dot_general(qg, k_lo, nt, preferred_element_type=F32) + bias
        s1 = lax.dot_general(qg, k_hi, nt, preferred_element_type=F32) + bias
        p0 = _sink_softmax(s0, sink_cols[0])
        p1 = _sink_softmax(s1, sink_cols[1])
        o = (jnp.dot(p0.astype(BF16), v_lo, preferred_element_type=F32)
             + jnp.dot(p1.astype(BF16), v_hi, preferred_element_type=F32))
        o_ref[G * j:G * (j + 1)] = o.reshape(G, tq, LANES).astype(BF16)


def _attention(q, kd, vd, sinks, *, B, T, tq, chunked, n_valid):
    nq = T // tq
    Tk = kd.shape[1]
    W = kd.shape[2]
    kern = functools.partial(_attn_kernel, tq=tq, chunked=chunked, n_valid=n_valid)
    if chunked:
        assert tq == WINDOW and Tk == T
        prev = pl.BlockSpec((1, tq, W), lambda b, i: (b, jnp.maximum(i - 1, 0), 0))
        cur = pl.BlockSpec((1, tq, W), lambda b, i: (b, i, 0))
        kv_specs, kv_args = [prev, cur, prev, cur], (kd, kd, vd, vd)
    else:
        assert Tk == KEY_SPAN and nq == 1
        whole = pl.BlockSpec((1, Tk, W), lambda b, i: (b, 0, 0))
        kv_specs, kv_args = [whole, whole], (kd, vd)
    return pl.pallas_call(
        kern,
        out_shape=jax.ShapeDtypeStruct((N_PAIRS, B * T, LANES), BF16),
        grid=(B, nq),
        in_specs=[
            pl.BlockSpec(memory_space=pltpu.SMEM),
            pl.BlockSpec((N_PAIRS, tq, LANES), lambda b, i: (0, b * nq + i, 0)),
            *kv_specs,
        ],
        out_specs=pl.BlockSpec((N_PAIRS, tq, LANES), lambda b, i: (0, b * nq + i, 0)),
        compiler_params=_params("parallel", "parallel"),
        name="swa_attention",
    )(sinks, q, *kv_args)


def _oproj_kernel(o_ref, x_ref, mod_ref, g_ref, w_ref, y_ref, *, bb, tt):
    o = jnp.concatenate([o_ref[p] for p in range(N_PAIRS)], axis=1)
    mix = jnp.dot(o, w_ref[0], preferred_element_type=F32).reshape(bb, tt, D_MODEL)
    m = mod_ref[...]
    y_ref[...] = x_ref[...] + m[:, 2:3, :] * _rms(mix, g_ref[:, 1:2, :])


def _oproj_layer(o, x, mod, g_norm, w_o, l, j, *, bb, tt):
    B, T, D = x.shape
    nt = T // tt
    kern = functools.partial(_oproj_kernel, bb=bb, tt=tt)
    return pl.pallas_call(
        kern,
        out_shape=jax.ShapeDtypeStruct((B, T, D), F32),
        grid=(B // bb, nt),
        in_specs=[
            pl.BlockSpec((N_PAIRS, bb * tt, LANES), lambda b, t: (0, b * nt + t, 0)),
            pl.BlockSpec((bb, tt, D), lambda b, t: (b, t, 0)),
            pl.BlockSpec((bb, 6, D), lambda b, t: (b, 0, 0)),
            pl.BlockSpec((1, 4, D), lambda b, t: (l, 0, 0)),
            pl.BlockSpec((1, D, D), lambda b, t: (j, 0, 0)),
        ],
        out_specs=pl.BlockSpec((bb, tt, D), lambda b, t: (b, t, 0)),
        compiler_params=_params("parallel", "parallel"),
        name=f"o_proj_{l}",
    )(o, x, mod, g_norm, w_o)


def _mlp_kernel(x_ref, mod_ref, g_ref, wu_ref, wd_ref, y_ref, *rest, bb, tt, emit_bf16):
    f = pl.program_id(2)
    m = mod_ref[...]
    if emit_bf16:
        wu_out, wd_out, h_ref = rest
        wu = wu_ref[0].astype(BF16)
        wd = wd_ref[0].astype(BF16)
        wu_out[0] = wu
        wd_out[0] = wd
    else:
        h_ref, = rest
        wu = wu_ref[0]

    @pl.when(f == 0)
    def _():
        h = _modulate(_rms(x_ref[...], g_ref[:, 2:3, :]), m[:, 3:4, :], m[:, 4:5, :])
        h_ref[...] = h.reshape(bb * tt, D_MODEL).astype(BF16)
        y_ref[...] = jnp.zeros_like(y_ref)

    u = jnp.dot(h_ref[...], wu, preferred_element_type=F32)
    a = jnp.square(jnp.maximum(u, 0.0)).astype(BF16)
    for n in range(D_MODEL // MLP_OUT_CHUNK):
        cs = slice(n * MLP_OUT_CHUNK, (n + 1) * MLP_OUT_CHUNK)
        wd_n = wd[:, cs] if emit_bf16 else wd_ref[0, :, cs]
        y_ref[:, :, cs] += jnp.dot(a, wd_n, preferred_element_type=F32).reshape(
            bb, tt, MLP_OUT_CHUNK)

    @pl.when(f == pl.num_programs(2) - 1)
    def _():
        y_ref[...] = x_ref[...] + m[:, 5:6, :] * _rms(y_ref[...], g_ref[:, 3:4, :])


def _mlp_layer(x, mod, g_norm, w_up, w_down, l, *, bb, tt, tff, emit_bf16):
    B, T, D = x.shape
    grid = (B // bb, T // tt, D_FF // tff)
    kern = functools.partial(_mlp_kernel, bb=bb, tt=tt, emit_bf16=emit_bf16)
    wl = l if emit_bf16 else 0
    y_shape = jax.ShapeDtypeStruct((B, T, D), F32)
    y_spec = pl.BlockSpec((bb, tt, D), lambda b, t, f: (b, t, 0))
    if emit_bf16:
        assert grid[0] * grid[1] == 1
        out_shape = (y_shape, jax.ShapeDtypeStruct((1, D, D_FF), BF16),
                     jax.ShapeDtypeStruct((1, D_FF, D), BF16))
        out_specs = (y_spec, pl.BlockSpec((1, D, tff), lambda b, t, f: (0, 0, f)),
                     pl.BlockSpec((1, tff, D), lambda b, t, f: (0, f, 0)))
    else:
        out_shape, out_specs = y_shape, y_spec
    return pl.pallas_call(
        kern,
        out_shape=out_shape,
        grid=grid,
        in_specs=[
            pl.BlockSpec((bb, tt, D), lambda b, t, f: (b, t, 0)),
            pl.BlockSpec((bb, 6, D), lambda b, t, f: (b, 0, 0)),
            pl.BlockSpec((1, 4, D), lambda b, t, f: (l, 0, 0)),
            pl.BlockSpec((1, D, tff), lambda b, t, f: (wl, 0, f)),
            pl.BlockSpec((1, tff, D), lambda b, t, f: (wl, f, 0)),
        ],
        out_specs=out_specs,
        scratch_shapes=[pltpu.VMEM((bb * tt, D), BF16)],
        compiler_params=_params("parallel", "parallel", "arbitrary"),
        name=f"mlp_{l}",
    )(x, mod, g_norm, w_up, w_down)


def _split_dup(a):
    z = jnp.zeros_like(a)
    lo = jnp.concatenate([a, z], axis=-1)
    hi = jnp.concatenate([z, a], axis=-1)
    out = jnp.stack([lo, hi], axis=3)
    return out.reshape(a.shape[0], a.shape[1], -1).astype(BF16)


def _trunk(x, mods, kvmod, pos0, prefix, kv_past, wts, mlp_bf16, *, bb, tt, tff, tq):
    B, T, D = x.shape
    pos = pos0 + jnp.arange(T)
    tables = tuple(jnp.tile(tb, (bb, 1)) if bb > 1 else tb for tb in _rope_tables(pos))
    new_pool = []
    mlp_out = []
    k = v = kd = vd = None
    for l in range(DEPTH):
        mod = mods[l]
        if l < N_A_LAYERS:
            x, npool = _pool_layer(x, mod, wts["g_norm"], prefix, wts["w_pool"],
                                   wts["pool_scale"], l, bb=bb, tt=tt, pos0=pos0)
            new_pool.append(npool)
        else:
            j = l - N_A_LAYERS
            q = _qproj_layer(x, mod, wts["g_norm"], tables, wts["w_q"], l, j, bb=bb, tt=tt)
            o = _attention(q, kd, vd, wts["sinks"][j], B=B, T=T, tq=tq,
                           chunked=kv_past is None,
                           n_valid=None if kv_past is None else kv_past[0].shape[1] + T)
            x = _oproj_layer(o, x, mod, wts["g_norm"], wts["w_o"], l, j, bb=bb, tt=tt)
        if mlp_bf16 is None:
            x, wu_l, wd_l = _mlp_layer(x, mod, wts["g_norm"], wts["w_up"], wts["w_down"], l,
                                       bb=bb, tt=tt, tff=tff, emit_bf16=True)
            mlp_out.append((wu_l, wd_l))
        else:
            x = _mlp_layer(x, mod, wts["g_norm"], *mlp_bf16[l], l, bb=bb, tt=tt, tff=tff,
                           emit_bf16=False)
        if l == N_A_LAYERS - 1:
            k, v, kd, vd = _kv_layer(x, kvmod, wts["g_kv"], tables, wts["w_kv_dup"],
                                     bb=bb, tt=tt)
            W = kd.shape[-1]
            kd = kd.reshape(B, T, W)
            vd = vd.reshape(B, T, W)
            if kv_past is not None:
                back = ((0, 0), (0, KEY_SPAN - kv_past[0].shape[1] - T), (0, 0))
                kd = jnp.pad(jnp.concatenate([_split_dup(kv_past[0]), kd], axis=1), back)
                vd = jnp.pad(jnp.concatenate([_split_dup(kv_past[1]), vd], axis=1), back)
    KV = N_KV_HEADS * HEAD_DIM
    return x, jnp.concatenate(new_pool, axis=0), k.reshape(B, T, KV), v.reshape(B, T, KV), mlp_out


def _prep_weights(g_norm, w_pool, pool_scale, g_kv, w_kv, w_q, sinks, w_o, w_up, w_down):
    D = D_MODEL
    KV = N_KV_HEADS * HEAD_DIM
    wk = w_kv[:, :KV].reshape(D, N_KV_HEADS, 1, HEAD_DIM)
    wv = w_kv[:, KV:].reshape(D, N_KV_HEADS, 1, HEAD_DIM)
    w_kv_dup = jnp.concatenate([jnp.tile(wk, (1, 1, 2, 1)).reshape(D, 2 * KV),
                                jnp.tile(wv, (1, 1, 2, 1)).reshape(D, 2 * KV)], axis=1)
    return {
        "g_norm": g_norm,
        "w_pool": w_pool.astype(BF16),
        "pool_scale": pool_scale.reshape(N_A_LAYERS, 1, D),
        "g_kv": g_kv.reshape(1, 1, D),
        "w_kv_dup": w_kv_dup.astype(BF16),
        "w_q": w_q.astype(BF16),
        "sinks": sinks,
        "w_o": w_o.astype(BF16),
        "w_up": w_up,
        "w_down": w_down,
    }


def kernel(x_prompt, x_sample, c_prompt, c_sample, state_pool, cache_k, cache_v, w_mod, b_mod,
           g_norm, w_pool, pool_scale, w_kv_mod, b_kv_mod, g_kv, w_kv, w_q, sinks, w_o, w_up,
           w_down):
    Bp, Bs = x_prompt.shape[0], x_sample.shape[0]
    D = D_MODEL
    KV = N_KV_HEADS * HEAD_DIM

    c_all = jnp.concatenate(
        [c_prompt, c_sample, jnp.zeros((MOD_ROWS - Bp - Bs, D), F32)], axis=0)
    mod_all = _modulation(c_all, w_mod, b_mod.reshape(DEPTH, 1, 6 * D)).reshape(
        DEPTH, MOD_ROWS, 6, D)
    kvmod_all = _modulation(c_all, w_kv_mod.reshape(1, D, 2 * D),
                            b_kv_mod.reshape(1, 1, 2 * D)).reshape(MOD_ROWS, 2, D)

    wts = _prep_weights(g_norm, w_pool, pool_scale, g_kv, w_kv, w_q, sinks, w_o, w_up, w_down)

    T_s = x_sample.shape[1]
    y_s, pool_s, k_s, v_s, mlp_bf16 = _trunk(
        x_sample, [mod_all[l, Bp:Bp + Bs] for l in range(DEPTH)], kvmod_all[Bp:Bp + Bs],
        PAST_LEN, state_pool, (cache_k, cache_v), wts, None, bb=Bs, tt=T_s, tff=512, tq=T_s)
    prompt_prefix = jnp.zeros((N_A_LAYERS, Bp, POOL_STATE, D), F32)
    y_p, pool_p, k_p, v_p, _ = _trunk(
        x_prompt, [mod_all[l, :Bp] for l in range(DEPTH)], kvmod_all[:Bp], 0, prompt_prefix,
        None, wts, mlp_bf16, bb=1, tt=512, tff=1024, tq=WINDOW)
    keep = min(WINDOW, x_prompt.shape[1])
    heads = (N_KV_HEADS, HEAD_DIM)
    return (y_p, y_s, pool_p, pool_s,
            k_p[:, -keep:].reshape(Bp, keep, *heads), v_p[:, -keep:].reshape(Bp, keep, *heads),
            k_s.reshape(Bs, T_s, *heads), v_s.reshape(Bs, T_s, *heads))
```

```python
import functools

import jax
import jax.numpy as jnp
import numpy as np
from jax import lax
from jax.experimental import pallas as pl
from jax.experimental.pallas import tpu as pltpu

F32 = jnp.float32
BF16 = jnp.bfloat16

D_MODEL = 2048
DEPTH = 4
PAST_LEN = 4096
CHUNK = 64
N_A_LAYERS = DEPTH // 2
POOL_WINDOWS = (2, 4, 8, 16)
POOL_GROUP = D_MODEL // len(POOL_WINDOWS)
POOL_STATE = max(POOL_WINDOWS) - 1
SUBLANES = 8
POOL_LEAD = SUBLANES
POOL_BASE = POOL_LEAD + POOL_STATE + 1
HEAD_DIM = 64
N_HEADS = D_MODEL // HEAD_DIM
N_KV_HEADS = N_HEADS // 8
WINDOW = 128
D_FF = 4 * D_MODEL
ROPE_THETA = 10000.0
EPS = 1e-6
ATTN_SCALE = HEAD_DIM ** -0.5
NEG_INF = -1e30

LANES = 128
KEY_SPAN = 2 * LANES
N_PAIRS = N_HEADS // 2
PAIRS_PER_KV = N_PAIRS // N_KV_HEADS
MLP_OUT_CHUNK = 512
MOD_ROWS = 32
VMEM_LIMIT = 56 * 1024 * 1024


def _params(*sem):
    return pltpu.CompilerParams(dimension_semantics=sem, vmem_limit_bytes=VMEM_LIMIT)


def _rms(x, g):
    ms = jnp.mean(x * x, axis=-1, keepdims=True)
    return x * lax.rsqrt(ms + EPS) * g


def _modulate(x, shift, scale):
    return x * (1.0 + scale) + shift


def _mod_kernel(c_ref, w_ref, b_ref, o_ref):
    c = c_ref[...]
    sc = (c * jax.nn.sigmoid(c)).astype(BF16)
    o_ref[0] = jnp.dot(sc, w_ref[0].astype(BF16), preferred_element_type=F32) + b_ref[0]


def _modulation(c_all, w, b, tn=1024):
    L, D, N = w.shape
    return pl.pallas_call(
        _mod_kernel,
        out_shape=jax.ShapeDtypeStruct((L, MOD_ROWS, N), F32),
        grid=(L, N // tn),
        in_specs=[
            pl.BlockSpec((MOD_ROWS, D), lambda l, n: (0, 0)),
            pl.BlockSpec((1, D, tn), lambda l, n: (l, 0, n)),
            pl.BlockSpec((1, 1, tn), lambda l, n: (l, 0, n)),
        ],
        out_specs=pl.BlockSpec((1, MOD_ROWS, tn), lambda l, n: (l, 0, n)),
        compiler_params=_params("parallel", "parallel"),
        name="modulation",
    )(c_all, w, b)


def _pool_kernel(x_ref, mod_ref, g_ref, pre_ref, wp_ref, ps_ref, o_ref, np_ref, hbuf, s1, s2,
                 *, bb, tt, pos0):
    t = pl.program_id(1)
    G = POOL_GROUP
    LEAD, BASE = POOL_LEAD, POOL_BASE
    L = BASE + tt

    @pl.when(t == 0)
    def _():
        hbuf[:, 0:LEAD + 1, :] = jnp.zeros((bb, LEAD + 1, D_MODEL), F32)
        hbuf[:, LEAD + 1:BASE, :] = pre_ref[0]
        s1[:, 0:LEAD, :] = jnp.zeros((bb, LEAD, 3 * G), F32)
        s2[:, 0:LEAD, :] = jnp.zeros((bb, LEAD, 3 * G), F32)

    @pl.when(t > 0)
    def _():
        hbuf[:, LEAD:BASE, :] = hbuf[:, tt + LEAD:tt + BASE, :]

    x = x_ref[...]
    m = mod_ref[...]
    h = _modulate(_rms(x, g_ref[:, 0:1, :]), m[:, 0:1, :], m[:, 1:2, :])
    hbuf[:, BASE:L, :] = h
    np_ref[0] = hbuf[:, L - POOL_STATE:L, :]

    s1[:, LEAD:L, :] = hbuf[:, LEAD:L, G:] + hbuf[:, LEAD - 1:L - 1, G:]
    s2[:, LEAD:L, :] = s1[:, LEAD:L, :] + s1[:, LEAD - 2:L - 2, :]
    s1[:, 2 * LEAD:L, 0:2 * G] = (s2[:, 2 * LEAD:L, G:] +
                                  s2[:, 2 * LEAD - 4:L - 4, G:])
    sums = [
        hbuf[:, BASE:L, 0:G] + hbuf[:, BASE - 1:L - 1, 0:G],
        s2[:, BASE:L, 0:G],
        s1[:, BASE:L, 0:G],
        s1[:, BASE:L, G:2 * G] + s1[:, BASE - 8:L - 8, G:2 * G],
    ]

    pos = pos0 + t * tt + lax.broadcasted_iota(jnp.int32, (1, tt, POOL_GROUP), 1)
    ys = []
    for g, w in enumerate(POOL_WINDOWS):
        cs = slice(g * POOL_GROUP, (g + 1) * POOL_GROUP)
        cnt = jnp.minimum(w, pos + 1).astype(F32)
        pooled = sums[g] / cnt - hbuf[:, BASE:L, cs]
        ys.append(jnp.dot(pooled.reshape(bb * tt, POOL_GROUP).astype(BF16), wp_ref[0, g],
                          preferred_element_type=F32))
    y = jnp.concatenate(ys, axis=-1).reshape(bb, tt, D_MODEL) * ps_ref[...]
    o_ref[...] = x + m[:, 2:3, :] * _rms(y, g_ref[:, 1:2, :])


def _pool_layer(x, mod, g_norm, prefix, w_pool, pool_scale, l, *, bb, tt, pos0):
    B, T, D = x.shape
    kern = functools.partial(_pool_kernel, bb=bb, tt=tt, pos0=pos0)
    return pl.pallas_call(
        kern,
        out_shape=(jax.ShapeDtypeStruct((B, T, D), F32),
                   jax.ShapeDtypeStruct((1, B, POOL_STATE, D), F32)),
        grid=(B // bb, T // tt),
        in_specs=[
            pl.BlockSpec((bb, tt, D), lambda b, t: (b, t, 0)),
            pl.BlockSpec((bb, 6, D), lambda b, t: (b, 0, 0)),
            pl.BlockSpec((1, 4, D), lambda b, t: (l, 0, 0)),
            pl.BlockSpec((1, bb, POOL_STATE, D), lambda b, t: (l, b, 0, 0)),
            pl.BlockSpec((1, len(POOL_WINDOWS), POOL_GROUP, POOL_GROUP),
                         lambda b, t: (l, 0, 0, 0)),
            pl.BlockSpec((1, 1, D), lambda b, t: (l, 0, 0)),
        ],
        out_specs=(pl.BlockSpec((bb, tt, D), lambda b, t: (b, t, 0)),
                   pl.BlockSpec((1, bb, POOL_STATE, D), lambda b, t: (0, b, 0, 0))),
        scratch_shapes=[pltpu.VMEM((bb, POOL_BASE + tt, D), F32),
                        pltpu.VMEM((bb, POOL_BASE + tt, 3 * POOL_GROUP), F32),
                        pltpu.VMEM((bb, POOL_BASE + tt, 3 * POOL_GROUP), F32)],
        compiler_params=_params("parallel", "arbitrary"),
        name=f"pool_mixer_{l}",
    )(x, mod, g_norm, prefix, w_pool, pool_scale)


def _rope_tables(pos):
    half = HEAD_DIM // 2
    inv = ROPE_THETA ** (-jnp.arange(half, dtype=F32) / half)
    ang = pos.astype(F32)[:, None] * inv[None, :]
    cos, sin = jnp.cos(ang), jnp.sin(ang)
    zero = jnp.zeros_like(sin)
    c = jnp.tile(cos, (1, 4))
    s_lo = jnp.tile(jnp.concatenate([-sin, zero], axis=1), (1, 2))
    s_hi = jnp.tile(jnp.concatenate([zero, sin], axis=1), (1, 2))
    return c, s_lo, s_hi


def _rope_block(blk, c, s_lo, s_hi):
    return (blk * c + pltpu.roll(blk, LANES - HEAD_DIM // 2, 1) * s_lo
            + pltpu.roll(blk, HEAD_DIM // 2, 1) * s_hi)


def _qproj_kernel(x_ref, mod_ref, g_ref, c_ref, slo_ref, shi_ref, w_ref, q_ref, *, bb, tt):
    x = x_ref[...]
    m = mod_ref[...]
    h = _modulate(_rms(x, g_ref[:, 0:1, :]), m[:, 0:1, :], m[:, 1:2, :])
    q = jnp.dot(h.reshape(bb * tt, D_MODEL).astype(BF16), w_ref[0], preferred_element_type=F32)
    c, s_lo, s_hi = c_ref[...], slo_ref[...], shi_ref[...]
    for p in range(N_PAIRS):
        cs = slice(p * LANES, (p + 1) * LANES)
        q_ref[p] = (_rope_block(q[:, cs], c, s_lo, s_hi) * ATTN_SCALE).astype(BF16)


def _qproj_layer(x, mod, g_norm, tables, w_q, l, j, *, bb, tt):
    B, T, D = x.shape
    nt = T // tt
    M = bb * tt
    kern = functools.partial(_qproj_kernel, bb=bb, tt=tt)
    tab_spec = pl.BlockSpec((M, LANES), lambda b, t: (t, 0))
    return pl.pallas_call(
        kern,
        out_shape=jax.ShapeDtypeStruct((N_PAIRS, B * T, LANES), BF16),
        grid=(B // bb, nt),
        in_specs=[
            pl.BlockSpec((bb, tt, D), lambda b, t: (b, t, 0)),
            pl.BlockSpec((bb, 6, D), lambda b, t: (b, 0, 0)),
            pl.BlockSpec((1, 4, D), lambda b, t: (l, 0, 0)),
            tab_spec, tab_spec, tab_spec,
            pl.BlockSpec((1, D, D), lambda b, t: (j, 0, 0)),
        ],
        out_specs=pl.BlockSpec((N_PAIRS, M, LANES), lambda b, t: (0, b * nt + t, 0)),
        compiler_params=_params("parallel", "parallel"),
        name=f"q_proj_{l}",
    )(x, mod, g_norm, *tables, w_q)


def _kv_kernel(x_ref, mod_ref, g_ref, c_ref, slo_ref, shi_ref, w_ref,
               k_ref, v_ref, kd_ref, vd_ref, *, bb, tt):
    x = x_ref[...]
    m = mod_ref[...]
    h = _modulate(_rms(x, g_ref[...]), m[:, 0:1, :], m[:, 1:2, :])
    kv = jnp.dot(h.reshape(bb * tt, D_MODEL).astype(BF16), w_ref[...],
                 preferred_element_type=F32)
    c, s_lo, s_hi = c_ref[...], slo_ref[...], shi_ref[...]
    lo = lax.broadcasted_iota(jnp.int32, (bb * tt, LANES), 1) < HEAD_DIM
    zero = jnp.zeros((bb * tt, LANES), F32)
    ks, vs = [], []
    for j in range(N_KV_HEADS):
        ks.append(_rope_block(kv[:, j * LANES:(j + 1) * LANES], c, s_lo, s_hi))
        vs.append(kv[:, (N_KV_HEADS + j) * LANES:(N_KV_HEADS + j + 1) * LANES])
    for j in range(N_KV_HEADS):
        for src, dst in ((ks, kd_ref), (vs, vd_ref)):
            dst[:, (2 * j) * LANES:(2 * j + 1) * LANES] = jnp.where(lo, src[j], zero).astype(BF16)
            dst[:, (2 * j + 1) * LANES:(2 * j + 2) * LANES] = jnp.where(lo, zero, src[j]).astype(BF16)
    for p in range(N_KV_HEADS // 2):
        cs = slice(p * LANES, (p + 1) * LANES)
        k_ref[:, cs] = jnp.where(lo, ks[2 * p], ks[2 * p + 1])
        v_ref[:, cs] = jnp.where(lo, vs[2 * p], vs[2 * p + 1])


def _kv_layer(x, mod, g_kv, tables, w_kv_dup, *, bb, tt):
    B, T, D = x.shape
    nt = T // tt
    M = bb * tt
    KV = N_KV_HEADS * HEAD_DIM
    kern = functools.partial(_kv_kernel, bb=bb, tt=tt)
    tab_spec = pl.BlockSpec((M, LANES), lambda b, t: (t, 0))
    row = lambda b, t: (b * nt + t, 0)
    return pl.pallas_call(
        kern,
        out_shape=(jax.ShapeDtypeStruct((B * T, KV), F32),
                   jax.ShapeDtypeStruct((B * T, KV), F32),
                   jax.ShapeDtypeStruct((B * T, 4 * KV), BF16),
                   jax.ShapeDtypeStruct((B * T, 4 * KV), BF16)),
        grid=(B // bb, nt),
        in_specs=[
            pl.BlockSpec((bb, tt, D), lambda b, t: (b, t, 0)),
            pl.BlockSpec((bb, 2, D), lambda b, t: (b, 0, 0)),
            pl.BlockSpec((1, 1, D), lambda b, t: (0, 0, 0)),
            tab_spec, tab_spec, tab_spec,
            pl.BlockSpec((D, 4 * KV), lambda b, t: (0, 0)),
        ],
        out_specs=(pl.BlockSpec((M, KV), row), pl.BlockSpec((M, KV), row),
                   pl.BlockSpec((M, 4 * KV), row), pl.BlockSpec((M, 4 * KV), row)),
        compiler_params=_params("parallel", "parallel"),
        name="shared_kv",
    )(x, mod, g_kv, *tables, w_kv_dup)


def _sink_softmax(s, sink):
    m = jnp.maximum(jnp.max(s, axis=-1, keepdims=True), sink)
    e = jnp.exp(s - m)
    den = jnp.sum(e, axis=-1, keepdims=True) + jnp.exp(sink - m)
    return e * (1.0 / den)


def _attn_kernel(sink_ref, q_ref, *refs, tq, chunked, n_valid):
    i = pl.program_id(1)
    G = PAIRS_PER_KV
    R = G * tq
    kp = lax.broadcasted_iota(jnp.int32, (R, KEY_SPAN), 1)
    if chunked:
        k_prev, k_cur, v_prev, v_cur, o_ref = refs
        kw = jnp.concatenate([k_prev[0], k_cur[0]], axis=0)
        vw = jnp.concatenate([v_prev[0], v_cur[0]], axis=0)
        row = lax.broadcasted_iota(jnp.int32, (R, KEY_SPAN), 0)
        cq = (row % tq) // CHUNK
        kb = kp // CHUNK
        first_real = jnp.where(i > 0, 0, WINDOW)
        valid = (kb >= cq) & (kb <= cq + WINDOW // CHUNK) & (kp >= first_real)
    else:
        kd_ref, vd_ref, o_ref = refs
        kw = kd_ref[0]
        vw = vd_ref[0]
        valid = kp < n_valid
    bias = jnp.where(valid, 0.0, NEG_INF)
    rblk = lax.broadcasted_iota(jnp.int32, (R, 1), 0) // tq
    nt = (((1,), (1,)), ((), ()))
    for j in range(N_KV_HEADS):
        sink_cols = []
        for parity in range(2):
            col = jnp.full((R, 1), sink_ref[2 * G * j + parity], F32)
            for r in range(1, G):
                col = jnp.where(rblk == r, sink_ref[2 * (G * j + r) + parity], col)
            sink_cols.append(col)
        qg = q_ref[G * j:G * (j + 1)].reshape(R, LANES)
        k_lo = kw[:, (2 * j) * LANES:(2 * j + 1) * LANES]
        k_hi = kw[:, (2 * j + 1) * LANES:(2 * j + 2) * LANES]
        v_lo = vw[:, (2 * j) * LANES:(2 * j + 1) * LANES]
        v_hi = vw[:, (2 * j + 1) * LANES:(2 * j + 2) * LANES]
        s0 = lax.dot_general(qg, k_lo, nt, preferred_element_type=F32) + bias
        s1 = lax.dot_general(qg, k_hi, nt, preferred_element_type=F32) + bias
        p0 = _sink_softmax(s0, sink_cols[0])
        p1 = _sink_softmax(s1, sink_cols[1])
        o = (jnp.dot(p0.astype(BF16), v_lo, preferred_element_type=F32)
             + jnp.dot(p1.astype(BF16), v_hi, preferred_element_type=F32))
        o_ref[G * j:G * (j + 1)] = o.reshape(G, tq, LANES).astype(BF16)


def _attention(q, kd, vd, sinks, *, B, T, tq, chunked, n_valid):
    nq = T // tq
    Tk = kd.shape[1]
    W = kd.shape[2]
    kern = functools.partial(_attn_kernel, tq=tq, chunked=chunked, n_valid=n_valid)
    if chunked:
        assert tq == WINDOW and Tk == T
        prev = pl.BlockSpec((1, tq, W), lambda b, i: (b, jnp.maximum(i - 1, 0), 0))
        cur = pl.BlockSpec((1, tq, W), lambda b, i: (b, i, 0))
        kv_specs, kv_args = [prev, cur, prev, cur], (kd, kd, vd, vd)
    else:
        assert Tk == KEY_SPAN and nq == 1
        whole = pl.BlockSpec((1, Tk, W), lambda b, i: (b, 0, 0))
        kv_specs, kv_args = [whole, whole], (kd, vd)
    return pl.pallas_call(
        kern,
        out_shape=jax.ShapeDtypeStruct((N_PAIRS, B * T, LANES), BF16),
        grid=(B, nq),
        in_specs=[
            pl.BlockSpec(memory_space=pltpu.SMEM),
            pl.BlockSpec((N_PAIRS, tq, LANES), lambda b, i: (0, b * nq + i, 0)),
            *kv_specs,
        ],
        out_specs=pl.BlockSpec((N_PAIRS, tq, LANES), lambda b, i: (0, b * nq + i, 0)),
        compiler_params=_params("parallel", "parallel"),
        name="swa_attention",
    )(sinks, q, *kv_args)


def _oproj_kernel(o_ref, x_ref, mod_ref, g_ref, w_ref, y_ref, *, bb, tt):
    o = jnp.concatenate([o_ref[p] for p in range(N_PAIRS)], axis=1)
    mix = jnp.dot(o, w_ref[0], preferred_element_type=F32).reshape(bb, tt, D_MODEL)
    m = mod_ref[...]
    y_ref[...] = x_ref[...] + m[:, 2:3, :] * _rms(mix, g_ref[:, 1:2, :])


def _oproj_layer(o, x, mod, g_norm, w_o, l, j, *, bb, tt):
    B, T, D = x.shape
    nt = T // tt
    kern = functools.partial(_oproj_kernel, bb=bb, tt=tt)
    return pl.pallas_call(
        kern,
        out_shape=jax.ShapeDtypeStruct((B, T, D), F32),
        grid=(B // bb, nt),
        in_specs=[
            pl.BlockSpec((N_PAIRS, bb * tt, LANES), lambda b, t: (0, b * nt + t, 0)),
            pl.BlockSpec((bb, tt, D), lambda b, t: (b, t, 0)),
            pl.BlockSpec((bb, 6, D), lambda b, t: (b, 0, 0)),
            pl.BlockSpec((1, 4, D), lambda b, t: (l, 0, 0)),
            pl.BlockSpec((1, D, D), lambda b, t: (j, 0, 0)),
        ],
        out_specs=pl.BlockSpec((bb, tt, D), lambda b, t: (b, t, 0)),
        compiler_params=_params("parallel", "parallel"),
        name=f"o_proj_{l}",
    )(o, x, mod, g_norm, w_o)


def _mlp_kernel(x_ref, mod_ref, g_ref, wu_ref, wd_ref, y_ref, *rest, bb, tt, emit_bf16):
    f = pl.program_id(2)
    m = mod_ref[...]
    if emit_bf16:
        wu_out, wd_out, h_ref = rest
        wu = wu_ref[0].astype(BF16)
        wd = wd_ref[0].astype(BF16)
        wu_out[0] = wu
        wd_out[0] = wd
    else:
        h_ref, = rest

    @pl.when(f == 0)
    def _():
        h = _modulate(_rms(x_ref[...], g_ref[:, 2:3, :]), m[:, 3:4, :], m[:, 4:5, :])
        h_ref[...] = h.reshape(bb * tt, D_MODEL).astype(BF16)
        y_ref[...] = jnp.zeros_like(y_ref)

    u = jnp.dot(h_ref[...], wu if emit_bf16 else wu_ref[0], preferred_element_type=F32)
    a = jnp.square(jnp.maximum(u, 0.0)).astype(BF16)
    for n in range(D_MODEL // MLP_OUT_CHUNK):
        cs = slice(n * MLP_OUT_CHUNK, (n + 1) * MLP_OUT_CHUNK)
        wd_n = wd[:, cs] if emit_bf16 else wd_ref[0, :, cs]
        y_ref[:, :, cs] += jnp.dot(a, wd_n, preferred_element_type=F32).reshape(
            bb, tt, MLP_OUT_CHUNK)

    @pl.when(f == pl.num_programs(2) - 1)
    def _():
        y_ref[...] = x_ref[...] + m[:, 5:6, :] * _rms(y_ref[...], g_ref[:, 3:4, :])


def _mlp_layer(x, mod, g_norm, w_up, w_down, l, *, bb, tt, tff, emit_bf16):
    B, T, D = x.shape
    grid = (B // bb, T // tt, D_FF // tff)
    kern = functools.partial(_mlp_kernel, bb=bb, tt=tt, emit_bf16=emit_bf16)
    wl = l if emit_bf16 else 0
    y_shape = jax.ShapeDtypeStruct((B, T, D), F32)
    y_spec = pl.BlockSpec((bb, tt, D), lambda b, t, f: (b, t, 0))
    if emit_bf16:
        assert grid[0] * grid[1] == 1
        out_shape = (y_shape, jax.ShapeDtypeStruct((1, D, D_FF), BF16),
                     jax.ShapeDtypeStruct((1, D_FF, D), BF16))
        out_specs = (y_spec, pl.BlockSpec((1, D, tff), lambda b, t, f: (0, 0, f)),
                     pl.BlockSpec((1, tff, D), lambda b, t, f: (0, f, 0)))
    else:
        out_shape, out_specs = y_shape, y_spec
    return pl.pallas_call(
        kern,
        out_shape=out_shape,
        grid=grid,
        in_specs=[
            pl.BlockSpec((bb, tt, D), lambda b, t, f: (b, t, 0)),
            pl.BlockSpec((bb, 6, D), lambda b, t, f: (b, 0, 0)),
            pl.BlockSpec((1, 4, D), lambda b, t, f: (l, 0, 0)),
            pl.BlockSpec((1, D, tff), lambda b, t, f: (wl, 0, f)),
            pl.BlockSpec((1, tff, D), lambda b, t, f: (wl, f, 0)),
        ],
        out_specs=out_specs,
        scratch_shapes=[pltpu.VMEM((bb * tt, D), BF16)],
        compiler_params=_params("parallel", "parallel", "arbitrary"),
        name=f"mlp_{l}",
    )(x, mod, g_norm, w_up, w_down)


def _mlp_skew_kernel(x0_ref, xn_ref, xp_ref, modn_ref, modp_ref, g_ref, wu_ref, wd_ref, y_ref,
                     h_ref, acc_ref, *, tm, rs):
    s = pl.program_id(0)
    f = pl.program_id(1)
    nt = pl.num_programs(0) - 1
    cur = s % 2
    rows = pl.ds(pl.multiple_of(f * rs, rs), rs)
    g_pre, g_post = g_ref[0, 2:3, :], g_ref[0, 3:4, :]

    def prologue(x, m):
        return _modulate(_rms(x, g_pre), m[3:4, :], m[4:5, :]).astype(BF16)

    def epilogue():
        mp = modp_ref[0]
        y_ref[...] = xp_ref[...] + mp[5:6, :] * _rms(acc_ref[1 - cur, rows, :], g_post)
        acc_ref[1 - cur, rows, :] = jnp.zeros((rs, D_MODEL), F32)

    @pl.when((s == 0) & (f == 0))
    def _():
        h_ref[0] = prologue(x0_ref[...], modn_ref[0])
        acc_ref[...] = jnp.zeros_like(acc_ref)

    def matmuls_and_prologue():
        u = jnp.dot(h_ref[cur], wu_ref[0], preferred_element_type=F32)
        a = jnp.square(jnp.maximum(u, 0.0)).astype(BF16)
        for n in range(D_MODEL // MLP_OUT_CHUNK):
            cs = slice(n * MLP_OUT_CHUNK, (n + 1) * MLP_OUT_CHUNK)
            acc_ref[cur, :, cs] += jnp.dot(a, wd_ref[0, :, cs], preferred_element_type=F32)
        h_ref[1 - cur, rows, :] = prologue(xn_ref[...], modn_ref[0])

    @pl.when(s == 0)
    def _():
        matmuls_and_prologue()

    @pl.when((s > 0) & (s < nt))
    def _():
        epilogue()
        matmuls_and_prologue()

    @pl.when(s == nt)
    def _():
        epilogue()


def _mlp_layer_skewed(x, mod, g_norm, w_up, w_down, l, *, tm, tff):
    B, T, D = x.shape
    nf = D_FF // tff
    rs = tm // nf
    tpb = T // tm
    nt = B * tpb
    assert tpb >= 2
    x2 = x.reshape(B * T, D)
    nxt = lambda s: jnp.minimum(s + 1, nt - 1)
    prv = lambda s: jnp.maximum(s - 1, 0)
    wf = lambda s, f: jnp.where(s == nt, nf - 1, f)
    kern = functools.partial(_mlp_skew_kernel, tm=tm, rs=rs)
    y = pl.pallas_call(
        kern,
        out_shape=jax.ShapeDtypeStruct((B * T, D), F32),
        grid=(nt + 1, nf),
        in_specs=[
            pl.BlockSpec((tm, D), lambda s, f: (0, 0)),
            pl.BlockSpec((rs, D), lambda s, f: (nxt(s) * nf + f, 0)),
            pl.BlockSpec((rs, D), lambda s, f: (prv(s) * nf + f, 0)),
            pl.BlockSpec((1, 6, D), lambda s, f: (nxt(s) // tpb, 0, 0)),
            pl.BlockSpec((1, 6, D), lambda s, f: (prv(s) // tpb, 0, 0)),
            pl.BlockSpec((1, 4, D), lambda s, f: (l, 0, 0)),
            pl.BlockSpec((1, D, tff), lambda s, f: (0, 0, wf(s, f))),
            pl.BlockSpec((1, tff, D), lambda s, f: (0, wf(s, f), 0)),
        ],
        out_specs=pl.BlockSpec((rs, D), lambda s, f: (jnp.where(s == 0, 0, (s - 1) * nf + f), 0)),
        scratch_shapes=[pltpu.VMEM((2, tm, D), BF16), pltpu.VMEM((2, tm, D), F32)],
        compiler_params=_params("arbitrary", "arbitrary"),
        name=f"mlp_skewed_{l}",
    )(x2, x2, x2, mod, mod, g_norm, w_up, w_down)
    return y.reshape(B, T, D)


def _split_dup(a):
    z = jnp.zeros_like(a)
    lo = jnp.concatenate([a, z], axis=-1)
    hi = jnp.concatenate([z, a], axis=-1)
    out = jnp.stack([lo, hi], axis=3)
    return out.reshape(a.shape[0], a.shape[1], -1).astype(BF16)


def _trunk(x, mods, kvmod, pos0, prefix, kv_past, wts, mlp_bf16, *, bb, tt, tff, tq):
    B, T, D = x.shape
    pos = pos0 + jnp.arange(T)
    tables = tuple(jnp.tile(tb, (bb, 1)) if bb > 1 else tb for tb in _rope_tables(pos))
    new_pool = []
    mlp_out = []
    k = v = kd = vd = None
    for l in range(DEPTH):
        mod = mods[l]
        if l < N_A_LAYERS:
            x, npool = _pool_layer(x, mod, wts["g_norm"], prefix, wts["w_pool"],
                                   wts["pool_scale"], l, bb=bb, tt=tt, pos0=pos0)
            new_pool.append(npool)
        else:
            j = l - N_A_LAYERS
            q = _qproj_layer(x, mod, wts["g_norm"], tables, wts["w_q"], l, j, bb=bb, tt=tt)
            o = _attention(q, kd, vd, wts["sinks"][j], B=B, T=T, tq=tq,
                           chunked=kv_past is None,
                           n_valid=None if kv_past is None else kv_past[0].shape[1] + T)
            x = _oproj_layer(o, x, mod, wts["g_norm"], wts["w_o"], l, j, bb=bb, tt=tt)
        if mlp_bf16 is None:
            x, wu_l, wd_l = _mlp_layer(x, mod, wts["g_norm"], wts["w_up"], wts["w_down"], l,
                                       bb=bb, tt=tt, tff=tff, emit_bf16=True)
            mlp_out.append((wu_l, wd_l))
        else:
            x = _mlp_layer_skewed(x, mod, wts["g_norm"], *mlp_bf16[l], l, tm=tt, tff=tff)
        if l == N_A_LAYERS - 1:
            k, v, kd, vd = _kv_layer(x, kvmod, wts["g_kv"], tables, wts["w_kv_dup"],
                                     bb=bb, tt=tt)
            W = kd.shape[-1]
            kd = kd.reshape(B, T, W)
            vd = vd.reshape(B, T, W)
            if kv_past is not None:
                back = ((0, 0), (0, KEY_SPAN - kv_past[0].shape[1] - T), (0, 0))
                kd = jnp.pad(jnp.concatenate([_split_dup(kv_past[0]), kd], axis=1), back)
                vd = jnp.pad(jnp.concatenate([_split_dup(kv_past[1]), vd], axis=1), back)
    KV = N_KV_HEADS * HEAD_DIM
    return x, jnp.concatenate(new_pool, axis=0), k.reshape(B, T, KV), v.reshape(B, T, KV), mlp_out


def _prep_weights(g_norm, w_pool, pool_scale, g_kv, w_kv, w_q, sinks, w_o, w_up, w_down):
    D = D_MODEL
    KV = N_KV_HEADS * HEAD_DIM
    wk = w_kv[:, :KV].reshape(D, N_KV_HEADS, 1, HEAD_DIM)
    wv = w_kv[:, KV:].reshape(D, N_KV_HEADS, 1, HEAD_DIM)
    w_kv_dup = jnp.concatenate([jnp.tile(wk, (1, 1, 2, 1)).reshape(D, 2 * KV),
                                jnp.tile(wv, (1, 1, 2, 1)).reshape(D, 2 * KV)], axis=1)
    return {
        "g_norm": g_norm,
        "w_pool": w_pool.astype(BF16),
        "pool_scale": pool_scale.reshape(N_A_LAYERS, 1, D),
        "g_kv": g_kv.reshape(1, 1, D),
        "w_kv_dup": w_kv_dup.astype(BF16),
        "w_q": w_q.astype(BF16),
        "sinks": sinks,
        "w_o": w_o.astype(BF16),
        "w_up": w_up,
        "w_down": w_down,
    }


def kernel(x_prompt, x_sample, c_prompt, c_sample, state_pool, cache_k, cache_v, w_mod, b_mod,
           g_norm, w_pool, pool_scale, w_kv_mod, b_kv_mod, g_kv, w_kv, w_q, sinks, w_o, w_up,
           w_down):
    Bp, Bs = x_prompt.shape[0], x_sample.shape[0]
    D = D_MODEL

    c_all = jnp.concatenate(
        [c_prompt, c_sample, jnp.zeros((MOD_ROWS - Bp - Bs, D), F32)], axis=0)
    mod_all = _modulation(c_all, w_mod, b_mod.reshape(DEPTH, 1, 6 * D)).reshape(
        DEPTH, MOD_ROWS, 6, D)
    kvmod_all = _modulation(c_all, w_kv_mod.reshape(1, D, 2 * D),
                            b_kv_mod.reshape(1, 1, 2 * D)).reshape(MOD_ROWS, 2, D)

    wts = _prep_weights(g_norm, w_pool, pool_scale, g_kv, w_kv, w_q, sinks, w_o, w_up, w_down)

    T_s = x_sample.shape[1]
    y_s, pool_s, k_s, v_s, mlp_bf16 = _trunk(
        x_sample, [mod_all[l, Bp:Bp + Bs] for l in range(DEPTH)], kvmod_all[Bp:Bp + Bs],
        PAST_LEN, state_pool, (cache_k, cache_v), wts, None, bb=Bs, tt=T_s, tff=512, tq=T_s)
    prompt_prefix = jnp.zeros((N_A_LAYERS, Bp, POOL_STATE, D), F32)
    y_p, pool_p, k_p, v_p, _ = _trunk(
        x_prompt, [mod_all[l, :Bp] for l in range(DEPTH)], kvmod_all[:Bp], 0, prompt_prefix,
        None, wts, mlp_bf16, bb=1, tt=512, tff=1024, tq=WINDOW)
    keep = min(WINDOW, x_prompt.shape[1])
    heads = (N_KV_HEADS, HEAD_DIM)
    return (y_p, y_s, pool_p, pool_s,
            k_p[:, -keep:].reshape(Bp, keep, *heads), v_p[:, -keep:].reshape(Bp, keep, *heads),
            k_s.reshape(Bs, T_s, *heads), v_s.reshape(Bs, T_s, *heads))
```

```python
import functools

import jax
import jax.numpy as jnp
import numpy as np
from jax import lax
from jax.experimental import pallas as pl
from jax.experimental.pallas import tpu as pltpu

F32 = jnp.float32
BF16 = jnp.bfloat16

D_MODEL = 2048
DEPTH = 4
PAST_LEN = 4096
CHUNK = 64
N_A_LAYERS = DEPTH // 2
POOL_WINDOWS = (2, 4, 8, 16)
POOL_GROUP = D_MODEL // len(POOL_WINDOWS)
POOL_STATE = max(POOL_WINDOWS) - 1
SUBLANES = 8
POOL_LEAD = SUBLANES
POOL_BASE = POOL_LEAD + POOL_STATE + 1
HEAD_DIM = 64
N_HEADS = D_MODEL // HEAD_DIM
N_KV_HEADS = N_HEADS // 8
WINDOW = 128
D_FF = 4 * D_MODEL
ROPE_THETA = 10000.0
EPS = 1e-6
ATTN_SCALE = HEAD_DIM ** -0.5
NEG_INF = -1e30

LANES = 128
KEY_SPAN = 2 * LANES
N_PAIRS = N_HEADS // 2
PAIRS_PER_KV = N_PAIRS // N_KV_HEADS
MLP_OUT_CHUNK = 512
MOD_ROWS = 32
VMEM_LIMIT = 56 * 1024 * 1024


def _params(*sem):
    return pltpu.CompilerParams(dimension_semantics=sem, vmem_limit_bytes=VMEM_LIMIT)


def _rms(x, g):
    ms = jnp.mean(x * x, axis=-1, keepdims=True)
    return x * lax.rsqrt(ms + EPS) * g


def _modulate(x, shift, scale):
    return x * (1.0 + scale) + shift


def _mod_kernel(c_ref, w_ref, b_ref, o_ref):
    c = c_ref[...]
    sc = (c * jax.nn.sigmoid(c)).astype(BF16)
    o_ref[0] = jnp.dot(sc, w_ref[0].astype(BF16), preferred_element_type=F32) + b_ref[0]


def _modulation(c_all, w, b, tn=1024):
    L, D, N = w.shape
    return pl.pallas_call(
        _mod_kernel,
        out_shape=jax.ShapeDtypeStruct((L, MOD_ROWS, N), F32),
        grid=(L, N // tn),
        in_specs=[
            pl.BlockSpec((MOD_ROWS, D), lambda l, n: (0, 0)),
            pl.BlockSpec((1, D, tn), lambda l, n: (l, 0, n)),
            pl.BlockSpec((1, 1, tn), lambda l, n: (l, 0, n)),
        ],
        out_specs=pl.BlockSpec((1, MOD_ROWS, tn), lambda l, n: (l, 0, n)),
        compiler_params=_params("parallel", "parallel"),
        name="modulation",
    )(c_all, w, b)


def _pool_kernel(x_ref, mod_ref, g_ref, pre_ref, wp_ref, ps_ref, o_ref, np_ref, hbuf, s1, s2,
                 *, bb, tt, pos0):
    t = pl.program_id(1)
    G = POOL_GROUP
    LEAD, BASE = POOL_LEAD, POOL_BASE
    L = BASE + tt

    @pl.when(t == 0)
    def _():
        hbuf[:, 0:LEAD + 1, :] = jnp.zeros((bb, LEAD + 1, D_MODEL), F32)
        hbuf[:, LEAD + 1:BASE, :] = pre_ref[0]
        s1[:, 0:LEAD, :] = jnp.zeros((bb, LEAD, 3 * G), F32)
        s2[:, 0:LEAD, :] = jnp.zeros((bb, LEAD, 3 * G), F32)

    @pl.when(t > 0)
    def _():
        hbuf[:, LEAD:BASE, :] = hbuf[:, tt + LEAD:tt + BASE, :]

    x = x_ref[...]
    m = mod_ref[...]
    h = _modulate(_rms(x, g_ref[:, 0:1, :]), m[:, 0:1, :], m[:, 1:2, :])
    hbuf[:, BASE:L, :] = h
    np_ref[0] = hbuf[:, L - POOL_STATE:L, :]

    s1[:, LEAD:L, :] = hbuf[:, LEAD:L, G:] + hbuf[:, LEAD - 1:L - 1, G:]
    s2[:, LEAD:L, :] = s1[:, LEAD:L, :] + s1[:, LEAD - 2:L - 2, :]
    s1[:, 2 * LEAD:L, 0:2 * G] = (s2[:, 2 * LEAD:L, G:] +
                                  s2[:, 2 * LEAD - 4:L - 4, G:])
    sums = [
        hbuf[:, BASE:L, 0:G] + hbuf[:, BASE - 1:L - 1, 0:G],
        s2[:, BASE:L, 0:G],
        s1[:, BASE:L, 0:G],
        s1[:, BASE:L, G:2 * G] + s1[:, BASE - 8:L - 8, G:2 * G],
    ]

    pos = pos0 + t * tt + lax.broadcasted_iota(jnp.int32, (1, tt, POOL_GROUP), 1)
    ys = []
    for g, w in enumerate(POOL_WINDOWS):
        cs = slice(g * POOL_GROUP, (g + 1) * POOL_GROUP)
        cnt = jnp.minimum(w, pos + 1).astype(F32)
        pooled = sums[g] / cnt - hbuf[:, BASE:L, cs]
        ys.append(jnp.dot(pooled.reshape(bb * tt, POOL_GROUP).astype(BF16), wp_ref[0, g],
                          preferred_element_type=F32))
    y = jnp.concatenate(ys, axis=-1).reshape(bb, tt, D_MODEL) * ps_ref[...]
    o_ref[...] = x + m[:, 2:3, :] * _rms(y, g_ref[:, 1:2, :])


def _pool_layer(x, mod, g_norm, prefix, w_pool, pool_scale, l, *, bb, tt, pos0):
    B, T, D = x.shape
    kern = functools.partial(_pool_kernel, bb=bb, tt=tt, pos0=pos0)
    return pl.pallas_call(
        kern,
        out_shape=(jax.ShapeDtypeStruct((B, T, D), F32),
                   jax.ShapeDtypeStruct((1, B, POOL_STATE, D), F32)),
        grid=(B // bb, T // tt),
        in_specs=[
            pl.BlockSpec((bb, tt, D), lambda b, t: (b, t, 0)),
            pl.BlockSpec((bb, 6, D), lambda b, t: (b, 0, 0)),
            pl.BlockSpec((1, 4, D), lambda b, t: (l, 0, 0)),
            pl.BlockSpec((1, bb, POOL_STATE, D), lambda b, t: (l, b, 0, 0)),
            pl.BlockSpec((1, len(POOL_WINDOWS), POOL_GROUP, POOL_GROUP),
                         lambda b, t: (l, 0, 0, 0)),
            pl.BlockSpec((1, 1, D), lambda b, t: (l, 0, 0)),
        ],
        out_specs=(pl.BlockSpec((bb, tt, D), lambda b, t: (b, t, 0)),
                   pl.BlockSpec((1, bb, POOL_STATE, D), lambda b, t: (0, b, 0, 0))),
        scratch_shapes=[pltpu.VMEM((bb, POOL_BASE + tt, D), F32),
                        pltpu.VMEM((bb, POOL_BASE + tt, 3 * POOL_GROUP), F32),
                        pltpu.VMEM((bb, POOL_BASE + tt, 3 * POOL_GROUP), F32)],
        compiler_params=_params("parallel", "arbitrary"),
        name=f"pool_mixer_{l}",
    )(x, mod, g_norm, prefix, w_pool, pool_scale)


def _rope_tables(pos):
    half = HEAD_DIM // 2
    inv = ROPE_THETA ** (-jnp.arange(half, dtype=F32) / half)
    ang = pos.astype(F32)[:, None] * inv[None, :]
    cos, sin = jnp.cos(ang), jnp.sin(ang)
    zero = jnp.zeros_like(sin)
    c = jnp.tile(cos, (1, 4))
    s_lo = jnp.tile(jnp.concatenate([-sin, zero], axis=1), (1, 2))
    s_hi = jnp.tile(jnp.concatenate([zero, sin], axis=1), (1, 2))
    return c, s_lo, s_hi


def _rope_block(blk, c, s_lo, s_hi):
    return (blk * c + pltpu.roll(blk, LANES - HEAD_DIM // 2, 1) * s_lo
            + pltpu.roll(blk, HEAD_DIM // 2, 1) * s_hi)


def _qproj_kernel(x_ref, mod_ref, g_ref, c_ref, slo_ref, shi_ref, w_ref, q_ref, *, bb, tt):
    x = x_ref[...]
    m = mod_ref[...]
    h = _modulate(_rms(x, g_ref[:, 0:1, :]), m[:, 0:1, :], m[:, 1:2, :])
    q = jnp.dot(h.reshape(bb * tt, D_MODEL).astype(BF16), w_ref[0], preferred_element_type=F32)
    c, s_lo, s_hi = c_ref[...], slo_ref[...], shi_ref[...]
    for p in range(N_PAIRS):
        cs = slice(p * LANES, (p + 1) * LANES)
        q_ref[p] = (_rope_block(q[:, cs], c, s_lo, s_hi) * ATTN_SCALE).astype(BF16)


def _qproj_layer(x, mod, g_norm, tables, w_q, l, j, *, bb, tt):
    B, T, D = x.shape
    nt = T // tt
    M = bb * tt
    kern = functools.partial(_qproj_kernel, bb=bb, tt=tt)
    tab_spec = pl.BlockSpec((M, LANES), lambda b, t: (t, 0))
    return pl.pallas_call(
        kern,
        out_shape=jax.ShapeDtypeStruct((N_PAIRS, B * T, LANES), BF16),
        grid=(B // bb, nt),
        in_specs=[
            pl.BlockSpec((bb, tt, D), lambda b, t: (b, t, 0)),
            pl.BlockSpec((bb, 6, D), lambda b, t: (b, 0, 0)),
            pl.BlockSpec((1, 4, D), lambda b, t: (l, 0, 0)),
            tab_spec, tab_spec, tab_spec,
            pl.BlockSpec((1, D, D), lambda b, t: (j, 0, 0)),
        ],
        out_specs=pl.BlockSpec((N_PAIRS, M, LANES), lambda b, t: (0, b * nt + t, 0)),
        compiler_params=_params("parallel", "parallel"),
        name=f"q_proj_{l}",
    )(x, mod, g_norm, *tables, w_q)


def _kv_kernel(x_ref, mod_ref, g_ref, c_ref, slo_ref, shi_ref, w_ref,
               k_ref, v_ref, kd_ref, vd_ref, *, bb, tt):
    x = x_ref[...]
    m = mod_ref[...]
    h = _modulate(_rms(x, g_ref[...]), m[:, 0:1, :], m[:, 1:2, :])
    kv = jnp.dot(h.reshape(bb * tt, D_MODEL).astype(BF16), w_ref[...],
                 preferred_element_type=F32)
    c, s_lo, s_hi = c_ref[...], slo_ref[...], shi_ref[...]
    lo = lax.broadcasted_iota(jnp.int32, (bb * tt, LANES), 1) < HEAD_DIM
    zero = jnp.zeros((bb * tt, LANES), F32)
    n_blk = N_KV_HEADS // 2
    for p in range(n_blk):
        cs = slice(p * LANES, (p + 1) * LANES)
        k_blk = _rope_block(kv[:, cs], c, s_lo, s_hi)
        v_blk = kv[:, (n_blk + p) * LANES:(n_blk + p + 1) * LANES]
        k_ref[:, cs] = k_blk
        v_ref[:, cs] = v_blk
        for blk, dst in ((k_blk, kd_ref), (v_blk, vd_ref)):
            swapped = pltpu.roll(blk, HEAD_DIM, 1)
            parts = (jnp.where(lo, blk, zero), jnp.where(lo, zero, swapped),
                     jnp.where(lo, swapped, zero), jnp.where(lo, zero, blk))
            for i, part in enumerate(parts):
                dst[:, (4 * p + i) * LANES:(4 * p + i + 1) * LANES] = part.astype(BF16)


def _kv_layer(x, mod, g_kv, tables, w_kv, *, bb, tt):
    B, T, D = x.shape
    nt = T // tt
    M = bb * tt
    KV = N_KV_HEADS * HEAD_DIM
    kern = functools.partial(_kv_kernel, bb=bb, tt=tt)
    tab_spec = pl.BlockSpec((M, LANES), lambda b, t: (t, 0))
    row = lambda b, t: (b * nt + t, 0)
    return pl.pallas_call(
        kern,
        out_shape=(jax.ShapeDtypeStruct((B * T, KV), F32),
                   jax.ShapeDtypeStruct((B * T, KV), F32),
                   jax.ShapeDtypeStruct((B * T, 4 * KV), BF16),
                   jax.ShapeDtypeStruct((B * T, 4 * KV), BF16)),
        grid=(B // bb, nt),
        in_specs=[
            pl.BlockSpec((bb, tt, D), lambda b, t: (b, t, 0)),
            pl.BlockSpec((bb, 2, D), lambda b, t: (b, 0, 0)),
            pl.BlockSpec((1, 1, D), lambda b, t: (0, 0, 0)),
            tab_spec, tab_spec, tab_spec,
            pl.BlockSpec((D, 2 * KV), lambda b, t: (0, 0)),
        ],
        out_specs=(pl.BlockSpec((M, KV), row), pl.BlockSpec((M, KV), row),
                   pl.BlockSpec((M, 4 * KV), row), pl.BlockSpec((M, 4 * KV), row)),
        compiler_params=_params("parallel", "parallel"),
        name="shared_kv",
    )(x, mod, g_kv, *tables, w_kv)


def _sink_softmax(s, sink):
    m = jnp.max(s, axis=-1, keepdims=True)
    e = jnp.exp(s - m)
    den = jnp.sum(e, axis=-1, keepdims=True) + jnp.exp(sink - m)
    return e * (1.0 / den)


def _attn_kernel(sink_ref, q_ref, *refs, tq, chunked, n_valid):
    i = pl.program_id(1)
    G = PAIRS_PER_KV
    ts = WINDOW if chunked else tq
    R = G * ts
    kp = lax.broadcasted_iota(jnp.int32, (R, KEY_SPAN), 1)
    rblk = lax.broadcasted_iota(jnp.int32, (R, 1), 0) // ts
    nt = (((1,), (1,)), ((), ()))
    sink_cols = []
    for j in range(N_KV_HEADS):
        for parity in range(2):
            col = jnp.full((R, 1), sink_ref[2 * G * j + parity], F32)
            for r in range(1, G):
                col = jnp.where(rblk == r, sink_ref[2 * (G * j + r) + parity], col)
            sink_cols.append(col)
    if chunked:
        k_prev, k_cur, v_prev, v_cur, o_ref = refs
        row = lax.broadcasted_iota(jnp.int32, (R, KEY_SPAN), 0)
        cq = (row % ts) // CHUNK
        kb = kp // CHUNK
        in_window = (kb >= cq) & (kb <= cq + WINDOW // CHUNK)
    else:
        kd_ref, vd_ref, o_ref = refs

    for sub in range(tq // ts):
        rs = slice(sub * ts, (sub + 1) * ts)
        if not chunked:
            kw, vw = kd_ref[0], vd_ref[0]
            valid = kp < n_valid
        elif sub == 0:
            kw = jnp.concatenate([k_prev[0], k_cur[0, rs, :]], axis=0)
            vw = jnp.concatenate([v_prev[0], v_cur[0, rs, :]], axis=0)
            valid = in_window & (kp >= jnp.where(i > 0, 0, WINDOW))
        else:
            span = slice(sub * ts - WINDOW, (sub + 1) * ts)
            kw, vw = k_cur[0, span, :], v_cur[0, span, :]
            valid = in_window
        bias = jnp.where(valid, 0.0, NEG_INF)
        for j in range(N_KV_HEADS):
            qg = q_ref[G * j:G * (j + 1), rs, :].reshape(R, LANES)
            k_lo = kw[:, (2 * j) * LANES:(2 * j + 1) * LANES]
            k_hi = kw[:, (2 * j + 1) * LANES:(2 * j + 2) * LANES]
            v_lo = vw[:, (2 * j) * LANES:(2 * j + 1) * LANES]
            v_hi = vw[:, (2 * j + 1) * LANES:(2 * j + 2) * LANES]
            s0 = lax.dot_general(qg, k_lo, nt, preferred_element_type=F32) + bias
            s1 = lax.dot_general(qg, k_hi, nt, preferred_element_type=F32) + bias
            p0 = _sink_softmax(s0, sink_cols[2 * j])
            p1 = _sink_softmax(s1, sink_cols[2 * j + 1])
            o = (jnp.dot(p0.astype(BF16), v_lo, preferred_element_type=F32)
                 + jnp.dot(p1.astype(BF16), v_hi, preferred_element_type=F32))
            o_ref[G * j:G * (j + 1), rs, :] = o.reshape(G, ts, LANES).astype(BF16)


def _attention(q, kd, vd, sinks, *, B, T, tq, chunked, n_valid):
    nq = T // tq
    Tk = kd.shape[1]
    W = kd.shape[2]
    kern = functools.partial(_attn_kernel, tq=tq, chunked=chunked, n_valid=n_valid)
    if chunked:
        assert tq % WINDOW == 0 and Tk == T
        nsub = tq // WINDOW
        prev = pl.BlockSpec((1, WINDOW, W), lambda b, i: (b, jnp.maximum(i * nsub - 1, 0), 0))
        cur = pl.BlockSpec((1, tq, W), lambda b, i: (b, i, 0))
        kv_specs, kv_args = [prev, cur, prev, cur], (kd, kd, vd, vd)
    else:
        assert Tk == KEY_SPAN and nq == 1
        whole = pl.BlockSpec((1, Tk, W), lambda b, i: (b, 0, 0))
        kv_specs, kv_args = [whole, whole], (kd, vd)
    return pl.pallas_call(
        kern,
        out_shape=jax.ShapeDtypeStruct((N_PAIRS, B * T, LANES), BF16),
        grid=(B, nq),
        in_specs=[
            pl.BlockSpec(memory_space=pltpu.SMEM),
            pl.BlockSpec((N_PAIRS, tq, LANES), lambda b, i: (0, b * nq + i, 0)),
            *kv_specs,
        ],
        out_specs=pl.BlockSpec((N_PAIRS, tq, LANES), lambda b, i: (0, b * nq + i, 0)),
        compiler_params=_params("parallel", "parallel"),
        name="swa_attention",
    )(sinks, q, *kv_args)


def _oproj_kernel(o_ref, x_ref, mod_ref, g_ref, w_ref, y_ref, *, bb, tt):
    o = jnp.concatenate([o_ref[p] for p in range(N_PAIRS)], axis=1)
    mix = jnp.dot(o, w_ref[0], preferred_element_type=F32).reshape(bb, tt, D_MODEL)
    m = mod_ref[...]
    y_ref[...] = x_ref[...] + m[:, 2:3, :] * _rms(mix, g_ref[:, 1:2, :])


def _oproj_layer(o, x, mod, g_norm, w_o, l, j, *, bb, tt):
    B, T, D = x.shape
    nt = T // tt
    kern = functools.partial(_oproj_kernel, bb=bb, tt=tt)
    return pl.pallas_call(
        kern,
        out_shape=jax.ShapeDtypeStruct((B, T, D), F32),
        grid=(B // bb, nt),
        in_specs=[
            pl.BlockSpec((N_PAIRS, bb * tt, LANES), lambda b, t: (0, b * nt + t, 0)),
            pl.BlockSpec((bb, tt, D), lambda b, t: (b, t, 0)),
            pl.BlockSpec((bb, 6, D), lambda b, t: (b, 0, 0)),
            pl.BlockSpec((1, 4, D), lambda b, t: (l, 0, 0)),
            pl.BlockSpec((1, D, D), lambda b, t: (j, 0, 0)),
        ],
        out_specs=pl.BlockSpec((bb, tt, D), lambda b, t: (b, t, 0)),
        compiler_params=_params("parallel", "parallel"),
        name=f"o_proj_{l}",
    )(o, x, mod, g_norm, w_o)


def _mlp_kernel(x_ref, mod_ref, g_ref, wu_ref, wd_ref, y_ref, *rest, bb, tt, emit_bf16):
    f = pl.program_id(2)
    m = mod_ref[...]
    if emit_bf16:
        wu_out, wd_out, h_ref = rest
        wu = wu_ref[0].astype(BF16)
        wd = wd_ref[0].astype(BF16)
        wu_out[0] = wu
        wd_out[0] = wd
    else:
        h_ref, = rest

    @pl.when(f == 0)
    def _():
        h = _modulate(_rms(x_ref[...], g_ref[:, 2:3, :]), m[:, 3:4, :], m[:, 4:5, :])
        h_ref[...] = h.reshape(bb * tt, D_MODEL).astype(BF16)
        y_ref[...] = jnp.zeros_like(y_ref)

    u = jnp.dot(h_ref[...], wu if emit_bf16 else wu_ref[0], preferred_element_type=F32)
    a = jnp.square(jnp.maximum(u, 0.0)).astype(BF16)
    for n in range(D_MODEL // MLP_OUT_CHUNK):
        cs = slice(n * MLP_OUT_CHUNK, (n + 1) * MLP_OUT_CHUNK)
        wd_n = wd[:, cs] if emit_bf16 else wd_ref[0, :, cs]
        y_ref[:, :, cs] += jnp.dot(a, wd_n, preferred_element_type=F32).reshape(
            bb, tt, MLP_OUT_CHUNK)

    @pl.when(f == pl.num_programs(2) - 1)
    def _():
        y_ref[...] = x_ref[...] + m[:, 5:6, :] * _rms(y_ref[...], g_ref[:, 3:4, :])


def _mlp_layer(x, mod, g_norm, w_up, w_down, l, *, bb, tt, tff, emit_bf16):
    B, T, D = x.shape
    grid = (B // bb, T // tt, D_FF // tff)
    kern = functools.partial(_mlp_kernel, bb=bb, tt=tt, emit_bf16=emit_bf16)
    wl = l if emit_bf16 else 0
    y_shape = jax.ShapeDtypeStruct((B, T, D), F32)
    y_spec = pl.BlockSpec((bb, tt, D), lambda b, t, f: (b, t, 0))
    if emit_bf16:
        assert grid[0] * grid[1] == 1
        out_shape = (y_shape, jax.ShapeDtypeStruct((1, D, D_FF), BF16),
                     jax.ShapeDtypeStruct((1, D_FF, D), BF16))
        out_specs = (y_spec, pl.BlockSpec((1, D, tff), lambda b, t, f: (0, 0, f)),
                     pl.BlockSpec((1, tff, D), lambda b, t, f: (0, f, 0)))
    else:
        out_shape, out_specs = y_shape, y_spec
    return pl.pallas_call(
        kern,
        out_shape=out_shape,
        grid=grid,
        in_specs=[
            pl.BlockSpec((bb, tt, D), lambda b, t, f: (b, t, 0)),
            pl.BlockSpec((bb, 6, D), lambda b, t, f: (b, 0, 0)),
            pl.BlockSpec((1, 4, D), lambda b, t, f: (l, 0, 0)),
            pl.BlockSpec((1, D, tff), lambda b, t, f: (wl, 0, f)),
            pl.BlockSpec((1, tff, D), lambda b, t, f: (wl, f, 0)),
        ],
        out_specs=out_specs,
        scratch_shapes=[pltpu.VMEM((bb * tt, D), BF16)],
        compiler_params=_params("parallel", "parallel", "arbitrary"),
        name=f"mlp_{l}",
    )(x, mod, g_norm, w_up, w_down)


def _zero_after(x):
    bits = pltpu.bitcast(x, jnp.uint32)
    r = bits[:, 0:LANES]
    for k in range(1, x.shape[1] // LANES):
        r = r | bits[:, k * LANES:(k + 1) * LANES]
    r8 = r[0:SUBLANES]
    for k in range(1, x.shape[0] // SUBLANES):
        r8 = r8 | r[k * SUBLANES:(k + 1) * SUBLANES]
    z = pltpu.bitcast((r8 >> 16) >> 16, F32)
    return jnp.max(z, axis=(0, 1), keepdims=True)


def _mlp_skew_kernel(x0_ref, xn_ref, xp_ref, modn_ref, modp_ref, g_ref, wu_ref, wd_ref, y_ref,
                     h0_ref, h1_ref, acc0_ref, acc1_ref, *, nt, rs):
    s = pl.program_id(0)
    f = pl.program_id(1)
    rows = pl.ds(pl.multiple_of(f * rs, rs), rs)
    g_pre, g_post = g_ref[0, 2:3, :], g_ref[0, 3:4, :]
    h_refs = (h0_ref, h1_ref)
    acc_refs = (acc0_ref, acc1_ref)

    def prologue(x, m):
        return _modulate(_rms(x, g_pre), m[3:4, :], m[4:5, :])

    def epilogue(acc_ref):
        mp = modp_ref[0]
        y = xp_ref[...] + mp[5:6, :] * _rms(acc_ref[rows, :], g_post)
        y_ref[...] = y
        acc_ref[rows, :] = jnp.zeros((rs, D_MODEL), F32)
        return y

    def matmuls(h_ref, acc_ref, anchors):
        u = jnp.dot(h_ref[...], wu_ref[0], preferred_element_type=F32)
        a = jnp.square(jnp.maximum(u, 0.0)).astype(BF16)
        for n in range(D_MODEL // MLP_OUT_CHUNK):
            cs = slice(n * MLP_OUT_CHUNK, (n + 1) * MLP_OUT_CHUNK)
            d = jnp.dot(a, wd_ref[0, :, cs], preferred_element_type=F32)
            if anchors.get(n) is not None:
                d = d + anchors[n]
            acc_ref[:, cs] += d

    @pl.when((s == 0) & (f == 0))
    def _():
        h0_ref[...] = prologue(x0_ref[...], modn_ref[0]).astype(BF16)
        acc0_ref[...] = jnp.zeros_like(acc0_ref)
        acc1_ref[...] = jnp.zeros_like(acc1_ref)

    for slot in range(2):
        other = 1 - slot
        parity = (s % 2) == slot

        if slot == 0:
            @pl.when(s == 0)
            def _():
                h = prologue(xn_ref[...], modn_ref[0])
                h_refs[other][rows, :] = h.astype(BF16)
                matmuls(h_refs[slot], acc_refs[slot], {2: _zero_after(h)})

        @pl.when(parity & (s > 0) & (s < nt))
        def _():
            y = epilogue(acc_refs[other])
            h = prologue(xn_ref[...], modn_ref[0])
            h_refs[other][rows, :] = h.astype(BF16)
            matmuls(h_refs[slot], acc_refs[slot], {0: _zero_after(y), 2: _zero_after(h)})

        if nt % 2 == slot:
            @pl.when(s == nt)
            def _():
                epilogue(acc_refs[other])


def _mlp_layer_skewed(x, mod, g_norm, w_up, w_down, l, *, tm, tff):
    B, T, D = x.shape
    nf = D_FF // tff
    rs = tm // nf
    tpb = T // tm
    nt = B * tpb
    assert tpb >= 2
    x2 = x.reshape(B * T, D)
    nxt = lambda s: jnp.minimum(s + 1, nt - 1)
    prv = lambda s: jnp.maximum(s - 1, 0)
    wf = lambda s, f: jnp.where(s == nt, nf - 1, f)
    kern = functools.partial(_mlp_skew_kernel, nt=nt, rs=rs)
    y = pl.pallas_call(
        kern,
        out_shape=jax.ShapeDtypeStruct((B * T, D), F32),
        grid=(nt + 1, nf),
        in_specs=[
            pl.BlockSpec((tm, D), lambda s, f: (0, 0)),
            pl.BlockSpec((rs, D), lambda s, f: (nxt(s) * nf + f, 0)),
            pl.BlockSpec((rs, D), lambda s, f: (prv(s) * nf + f, 0)),
            pl.BlockSpec((1, 6, D), lambda s, f: (nxt(s) // tpb, 0, 0)),
            pl.BlockSpec((1, 6, D), lambda s, f: (prv(s) // tpb, 0, 0)),
            pl.BlockSpec((1, 4, D), lambda s, f: (l, 0, 0)),
            pl.BlockSpec((1, D, tff), lambda s, f: (0, 0, wf(s, f))),
            pl.BlockSpec((1, tff, D), lambda s, f: (0, wf(s, f), 0)),
        ],
        out_specs=pl.BlockSpec((rs, D), lambda s, f: (jnp.where(s == 0, 0, (s - 1) * nf + f), 0)),
        scratch_shapes=[pltpu.VMEM((tm, D), BF16), pltpu.VMEM((tm, D), BF16),
                        pltpu.VMEM((tm, D), F32), pltpu.VMEM((tm, D), F32)],
        compiler_params=_params("arbitrary", "arbitrary"),
        name=f"mlp_skewed_{l}",
    )(x2, x2, x2, mod, mod, g_norm, w_up, w_down)
    return y.reshape(B, T, D)


def _split_dup(a):
    z = jnp.zeros_like(a)
    lo = jnp.concatenate([a, z], axis=-1)
    hi = jnp.concatenate([z, a], axis=-1)
    out = jnp.stack([lo, hi], axis=3)
    return out.reshape(a.shape[0], a.shape[1], -1).astype(BF16)


def _trunk(x, mods, kvmod, pos0, prefix, kv_past, wts, mlp_bf16, *, bb, tt, tff, tq):
    B, T, D = x.shape
    pos = pos0 + jnp.arange(T)
    tables = tuple(jnp.tile(tb, (bb, 1)) if bb > 1 else tb for tb in _rope_tables(pos))
    new_pool = []
    mlp_out = []
    k = v = kd = vd = None
    for l in range(DEPTH):
        mod = mods[l]
        if l < N_A_LAYERS:
            x, npool = _pool_layer(x, mod, wts["g_norm"], prefix, wts["w_pool"],
                                   wts["pool_scale"], l, bb=bb, tt=tt, pos0=pos0)
            new_pool.append(npool)
        else:
            j = l - N_A_LAYERS
            q = _qproj_layer(x, mod, wts["g_norm"], tables, wts["w_q"], l, j, bb=bb, tt=tt)
            o = _attention(q, kd, vd, wts["sinks"][j], B=B, T=T, tq=tq,
                           chunked=kv_past is None,
                           n_valid=None if kv_past is None else kv_past[0].shape[1] + T)
            x = _oproj_layer(o, x, mod, wts["g_norm"], wts["w_o"], l, j, bb=bb, tt=tt)
        if mlp_bf16 is None:
            x, wu_l, wd_l = _mlp_layer(x, mod, wts["g_norm"], wts["w_up"], wts["w_down"], l,
                                       bb=bb, tt=tt, tff=tff, emit_bf16=True)
            mlp_out.append((wu_l, wd_l))
        else:
            x = _mlp_layer_skewed(x, mod, wts["g_norm"], *mlp_bf16[l], l, tm=tt, tff=tff)
        if l == N_A_LAYERS - 1:
            k, v, kd, vd = _kv_layer(x, kvmod, wts["g_kv"], tables, wts["w_kv"],
                                     bb=bb, tt=tt)
            W = kd.shape[-1]
            kd = kd.reshape(B, T, W)
            vd = vd.reshape(B, T, W)
            if kv_past is not None:
                back = ((0, 0), (0, KEY_SPAN - kv_past[0].shape[1] - T), (0, 0))
                kd = jnp.pad(jnp.concatenate([_split_dup(kv_past[0]), kd], axis=1), back)
                vd = jnp.pad(jnp.concatenate([_split_dup(kv_past[1]), vd], axis=1), back)
    KV = N_KV_HEADS * HEAD_DIM
    return x, jnp.concatenate(new_pool, axis=0), k.reshape(B, T, KV), v.reshape(B, T, KV), mlp_out


def _prep_weights(g_norm, w_pool, pool_scale, g_kv, w_kv, w_q, sinks, w_o, w_up, w_down):
    D = D_MODEL
    return {
        "g_norm": g_norm,
        "w_pool": w_pool.astype(BF16),
        "pool_scale": pool_scale.reshape(N_A_LAYERS, 1, D),
        "g_kv": g_kv.reshape(1, 1, D),
        "w_kv": w_kv.astype(BF16),
        "w_q": w_q.astype(BF16),
        "sinks": sinks,
        "w_o": w_o.astype(BF16),
        "w_up": w_up,
        "w_down": w_down,
    }


def kernel(x_prompt, x_sample, c_prompt, c_sample, state_pool, cache_k, cache_v, w_mod, b_mod,
           g_norm, w_pool, pool_scale, w_kv_mod, b_kv_mod, g_kv, w_kv, w_q, sinks, w_o, w_up,
           w_down):
    Bp, Bs = x_prompt.shape[0], x_sample.shape[0]
    D = D_MODEL

    c_all = jnp.concatenate(
        [c_prompt, c_sample, jnp.zeros((MOD_ROWS - Bp - Bs, D), F32)], axis=0)
    mod_all = _modulation(c_all, w_mod, b_mod.reshape(DEPTH, 1, 6 * D)).reshape(
        DEPTH, MOD_ROWS, 6, D)
    kvmod_all = _modulation(c_all, w_kv_mod.reshape(1, D, 2 * D),
                            b_kv_mod.reshape(1, 1, 2 * D)).reshape(MOD_ROWS, 2, D)

    wts = _prep_weights(g_norm, w_pool, pool_scale, g_kv, w_kv, w_q, sinks, w_o, w_up, w_down)

    T_s = x_sample.shape[1]
    y_s, pool_s, k_s, v_s, mlp_bf16 = _trunk(
        x_sample, [mod_all[l, Bp:Bp + Bs] for l in range(DEPTH)], kvmod_all[Bp:Bp + Bs],
        PAST_LEN, state_pool, (cache_k, cache_v), wts, None, bb=Bs, tt=T_s, tff=512, tq=T_s)
    prompt_prefix = jnp.zeros((N_A_LAYERS, Bp, POOL_STATE, D), F32)
    y_p, pool_p, k_p, v_p, _ = _trunk(
        x_prompt, [mod_all[l, :Bp] for l in range(DEPTH)], kvmod_all[:Bp], 0, prompt_prefix,
        None, wts, mlp_bf16, bb=1, tt=512, tff=1024, tq=4 * WINDOW)
    keep = min(WINDOW, x_prompt.shape[1])
    heads = (N_KV_HEADS, HEAD_DIM)
    return (y_p, y_s, pool_p, pool_s,
            k_p[:, -keep:].reshape(Bp, keep, *heads), v_p[:, -keep:].reshape(Bp, keep, *heads),
            k_s.reshape(Bs, T_s, *heads), v_s.reshape(Bs, T_s, *heads))
```

```python
import functools

import jax
import jax.numpy as jnp
from jax import lax
from jax.experimental import pallas as pl
from jax.experimental.pallas import tpu as pltpu

F32 = jnp.float32
BF16 = jnp.bfloat16

D_MODEL = 2048
DEPTH = 4
PAST_LEN = 4096
CHUNK = 64
N_A_LAYERS = DEPTH // 2
POOL_WINDOWS = (2, 4, 8, 16)
POOL_GROUP = D_MODEL // len(POOL_WINDOWS)
POOL_STATE = max(POOL_WINDOWS) - 1
SUBLANES = 8
POOL_LEAD = SUBLANES
POOL_BASE = POOL_LEAD + POOL_STATE + 1
HEAD_DIM = 64
N_HEADS = D_MODEL // HEAD_DIM
N_KV_HEADS = N_HEADS // 8
WINDOW = 128
D_FF = 4 * D_MODEL
ROPE_THETA = 10000.0
EPS = 1e-6
ATTN_SCALE = HEAD_DIM ** -0.5
NEG_INF = -1e30

LANES = 128
KEY_SPAN = 2 * LANES
N_PAIRS = N_HEADS // 2
PAIRS_PER_KV = N_PAIRS // N_KV_HEADS
MLP_OUT_CHUNK = 512
MOD_ROWS = 32
VMEM_LIMIT = 56 * 1024 * 1024


def _params(*sem):
    return pltpu.CompilerParams(dimension_semantics=sem, vmem_limit_bytes=VMEM_LIMIT)


def _rms(x, g):
    ms = jnp.mean(x * x, axis=-1, keepdims=True)
    return x * lax.rsqrt(ms + EPS) * g


def _modulate(x, shift, scale):
    return x * (1.0 + scale) + shift


def _mod_kernel(c_ref, w_ref, b_ref, o_ref):
    c = c_ref[...]
    sc = (c * jax.nn.sigmoid(c)).astype(BF16)
    o_ref[0] = jnp.dot(sc, w_ref[0].astype(BF16), preferred_element_type=F32) + b_ref[0]


def _modulation(c_all, w, b, tn=1024):
    L, D, N = w.shape
    return pl.pallas_call(
        _mod_kernel,
        out_shape=jax.ShapeDtypeStruct((L, MOD_ROWS, N), F32),
        grid=(L, N // tn),
        in_specs=[
            pl.BlockSpec((MOD_ROWS, D), lambda l, n: (0, 0)),
            pl.BlockSpec((1, D, tn), lambda l, n: (l, 0, n)),
            pl.BlockSpec((1, 1, tn), lambda l, n: (l, 0, n)),
        ],
        out_specs=pl.BlockSpec((1, MOD_ROWS, tn), lambda l, n: (l, 0, n)),
        compiler_params=_params("parallel", "parallel"),
        name="modulation",
    )(c_all, w, b)


def _pool_kernel(x_ref, mod_ref, g_ref, pre_ref, wp_ref, ps_ref, o_ref, np_ref, hbuf, s1, s2,
                 *, bb, tt, pos0):
    t = pl.program_id(1)
    G = POOL_GROUP
    LEAD, BASE = POOL_LEAD, POOL_BASE
    L = BASE + tt

    @pl.when(t == 0)
    def _():
        hbuf[:, 0:LEAD + 1, :] = jnp.zeros((bb, LEAD + 1, D_MODEL), F32)
        hbuf[:, LEAD + 1:BASE, :] = pre_ref[0]
        s1[:, 0:LEAD, :] = jnp.zeros((bb, LEAD, 3 * G), F32)
        s2[:, 0:LEAD, :] = jnp.zeros((bb, LEAD, 3 * G), F32)

    @pl.when(t > 0)
    def _():
        hbuf[:, LEAD:BASE, :] = hbuf[:, tt + LEAD:tt + BASE, :]

    x = x_ref[...]
    m = mod_ref[...]
    h = _modulate(_rms(x, g_ref[:, 0:1, :]), m[:, 0:1, :], m[:, 1:2, :])
    hbuf[:, BASE:L, :] = h
    np_ref[0] = hbuf[:, L - POOL_STATE:L, :]

    s1[:, LEAD:L, :] = hbuf[:, LEAD:L, G:] + hbuf[:, LEAD - 1:L - 1, G:]
    s2[:, LEAD:L, :] = s1[:, LEAD:L, :] + s1[:, LEAD - 2:L - 2, :]
    s1[:, 2 * LEAD:L, 0:2 * G] = (s2[:, 2 * LEAD:L, G:] +
                                  s2[:, 2 * LEAD - 4:L - 4, G:])
    sums = [
        hbuf[:, BASE:L, 0:G] + hbuf[:, BASE - 1:L - 1, 0:G],
        s2[:, BASE:L, 0:G],
        s1[:, BASE:L, 0:G],
        s1[:, BASE:L, G:2 * G] + s1[:, BASE - 8:L - 8, G:2 * G],
    ]

    pos = pos0 + t * tt + lax.broadcasted_iota(jnp.int32, (1, tt, POOL_GROUP), 1)
    ys = []
    for g, w in enumerate(POOL_WINDOWS):
        cs = slice(g * POOL_GROUP, (g + 1) * POOL_GROUP)
        cnt = jnp.minimum(w, pos + 1).astype(F32)
        pooled = sums[g] / cnt - hbuf[:, BASE:L, cs]
        ys.append(jnp.dot(pooled.reshape(bb * tt, POOL_GROUP).astype(BF16), wp_ref[0, g],
                          preferred_element_type=F32))
    y = jnp.concatenate(ys, axis=-1).reshape(bb, tt, D_MODEL) * ps_ref[...]
    o_ref[...] = x + m[:, 2:3, :] * _rms(y, g_ref[:, 1:2, :])


def _pool_layer(x, mod, g_norm, prefix, w_pool, pool_scale, l, *, bb, tt, pos0):
    B, T, D = x.shape
    kern = functools.partial(_pool_kernel, bb=bb, tt=tt, pos0=pos0)
    return pl.pallas_call(
        kern,
        out_shape=(jax.ShapeDtypeStruct((B, T, D), F32),
                   jax.ShapeDtypeStruct((1, B, POOL_STATE, D), F32)),
        grid=(B // bb, T // tt),
        in_specs=[
            pl.BlockSpec((bb, tt, D), lambda b, t: (b, t, 0)),
            pl.BlockSpec((bb, 6, D), lambda b, t: (b, 0, 0)),
            pl.BlockSpec((1, 4, D), lambda b, t: (l, 0, 0)),
            pl.BlockSpec((1, bb, POOL_STATE, D), lambda b, t: (l, b, 0, 0)),
            pl.BlockSpec((1, len(POOL_WINDOWS), POOL_GROUP, POOL_GROUP),
                         lambda b, t: (l, 0, 0, 0)),
            pl.BlockSpec((1, 1, D), lambda b, t: (l, 0, 0)),
        ],
        out_specs=(pl.BlockSpec((bb, tt, D), lambda b, t: (b, t, 0)),
                   pl.BlockSpec((1, bb, POOL_STATE, D), lambda b, t: (0, b, 0, 0))),
        scratch_shapes=[pltpu.VMEM((bb, POOL_BASE + tt, D), F32),
                        pltpu.VMEM((bb, POOL_BASE + tt, 3 * POOL_GROUP), F32),
                        pltpu.VMEM((bb, POOL_BASE + tt, 3 * POOL_GROUP), F32)],
        compiler_params=_params("parallel", "arbitrary"),
        name=f"pool_mixer_{l}",
    )(x, mod, g_norm, prefix, w_pool, pool_scale)


def _rope_tables(pos):
    half = HEAD_DIM // 2
    inv = ROPE_THETA ** (-jnp.arange(half, dtype=F32) / half)
    ang = pos.astype(F32)[:, None] * inv[None, :]
    cos, sin = jnp.cos(ang), jnp.sin(ang)
    zero = jnp.zeros_like(sin)
    c = jnp.tile(cos, (1, 4))
    s_lo = jnp.tile(jnp.concatenate([-sin, zero], axis=1), (1, 2))
    s_hi = jnp.tile(jnp.concatenate([zero, sin], axis=1), (1, 2))
    return c, s_lo, s_hi


def _rope_block(blk, c, s_lo, s_hi):
    return (blk * c + pltpu.roll(blk, LANES - HEAD_DIM // 2, 1) * s_lo
            + pltpu.roll(blk, HEAD_DIM // 2, 1) * s_hi)


def _qproj_kernel(x_ref, mod_ref, g_ref, c_ref, slo_ref, shi_ref, w_ref, q_ref, *, bb, tt):
    x = x_ref[...]
    m = mod_ref[...]
    h = _modulate(_rms(x, g_ref[:, 0:1, :]), m[:, 0:1, :], m[:, 1:2, :])
    q = jnp.dot(h.reshape(bb * tt, D_MODEL).astype(BF16), w_ref[0], preferred_element_type=F32)
    c, s_lo, s_hi = c_ref[...], slo_ref[...], shi_ref[...]
    for p in range(N_PAIRS):
        cs = slice(p * LANES, (p + 1) * LANES)
        q_ref[p] = (_rope_block(q[:, cs], c, s_lo, s_hi) * ATTN_SCALE).astype(BF16)


def _qproj_layer(x, mod, g_norm, tables, w_q, l, j, *, bb, tt):
    B, T, D = x.shape
    nt = T // tt
    M = bb * tt
    kern = functools.partial(_qproj_kernel, bb=bb, tt=tt)
    tab_spec = pl.BlockSpec((M, LANES), lambda b, t: (t, 0))
    return pl.pallas_call(
        kern,
        out_shape=jax.ShapeDtypeStruct((N_PAIRS, B * T, LANES), BF16),
        grid=(B // bb, nt),
        in_specs=[
            pl.BlockSpec((bb, tt, D), lambda b, t: (b, t, 0)),
            pl.BlockSpec((bb, 6, D), lambda b, t: (b, 0, 0)),
            pl.BlockSpec((1, 4, D), lambda b, t: (l, 0, 0)),
            tab_spec, tab_spec, tab_spec,
            pl.BlockSpec((1, D, D), lambda b, t: (j, 0, 0)),
        ],
        out_specs=pl.BlockSpec((N_PAIRS, M, LANES), lambda b, t: (0, b * nt + t, 0)),
        compiler_params=_params("parallel", "parallel"),
        name=f"q_proj_{l}",
    )(x, mod, g_norm, *tables, w_q)


def _kv_kernel(x_ref, mod_ref, g_ref, c_ref, slo_ref, shi_ref, w_ref,
               k_ref, v_ref, kd_ref, vd_ref, *, bb, tt):
    x = x_ref[...]
    m = mod_ref[...]
    h = _modulate(_rms(x, g_ref[...]), m[:, 0:1, :], m[:, 1:2, :])
    kv = jnp.dot(h.reshape(bb * tt, D_MODEL).astype(BF16), w_ref[...],
                 preferred_element_type=F32)
    c, s_lo, s_hi = c_ref[...], slo_ref[...], shi_ref[...]
    lo = lax.broadcasted_iota(jnp.int32, (bb * tt, LANES), 1) < HEAD_DIM
    zero = jnp.zeros((bb * tt, LANES), F32)
    n_blk = N_KV_HEADS // 2
    for p in range(n_blk):
        cs = slice(p * LANES, (p + 1) * LANES)
        k_blk = _rope_block(kv[:, cs], c, s_lo, s_hi)
        v_blk = kv[:, (n_blk + p) * LANES:(n_blk + p + 1) * LANES]
        k_ref[:, cs] = k_blk
        v_ref[:, cs] = v_blk
        for blk, dst in ((k_blk, kd_ref), (v_blk, vd_ref)):
            swapped = pltpu.roll(blk, HEAD_DIM, 1)
            parts = (jnp.where(lo, blk, zero), jnp.where(lo, zero, swapped),
                     jnp.where(lo, swapped, zero), jnp.where(lo, zero, blk))
            for i, part in enumerate(parts):
                dst[:, (4 * p + i) * LANES:(4 * p + i + 1) * LANES] = part.astype(BF16)


def _kv_layer(x, mod, g_kv, tables, w_kv, *, bb, tt):
    B, T, D = x.shape
    nt = T // tt
    M = bb * tt
    KV = N_KV_HEADS * HEAD_DIM
    kern = functools.partial(_kv_kernel, bb=bb, tt=tt)
    tab_spec = pl.BlockSpec((M, LANES), lambda b, t: (t, 0))
    row = lambda b, t: (b * nt + t, 0)
    return pl.pallas_call(
        kern,
        out_shape=(jax.ShapeDtypeStruct((B * T, KV), F32),
                   jax.ShapeDtypeStruct((B * T, KV), F32),
                   jax.ShapeDtypeStruct((B * T, 4 * KV), BF16),
                   jax.ShapeDtypeStruct((B * T, 4 * KV), BF16)),
        grid=(B // bb, nt),
        in_specs=[
            pl.BlockSpec((bb, tt, D), lambda b, t: (b, t, 0)),
            pl.BlockSpec((bb, 2, D), lambda b, t: (b, 0, 0)),
            pl.BlockSpec((1, 1, D), lambda b, t: (0, 0, 0)),
            tab_spec, tab_spec, tab_spec,
            pl.BlockSpec((D, 2 * KV), lambda b, t: (0, 0)),
        ],
        out_specs=(pl.BlockSpec((M, KV), row), pl.BlockSpec((M, KV), row),
                   pl.BlockSpec((M, 4 * KV), row), pl.BlockSpec((M, 4 * KV), row)),
        compiler_params=_params("parallel", "parallel"),
        name="shared_kv",
    )(x, mod, g_kv, *tables, w_kv)


def _sink_softmax(s, sink):
    m = jnp.max(s, axis=-1, keepdims=True)
    e = jnp.exp(s - m)
    den = jnp.sum(e, axis=-1, keepdims=True) + jnp.exp(sink - m)
    return e * (1.0 / den)


def _attn_kernel(sink_ref, q_ref, *refs, tq, chunked, n_valid, fuse_out):
    i = pl.program_id(1)
    G = PAIRS_PER_KV
    ts = WINDOW if chunked else tq
    R = G * ts
    kp = lax.broadcasted_iota(jnp.int32, (R, KEY_SPAN), 1)
    rblk = lax.broadcasted_iota(jnp.int32, (R, 1), 0) // ts
    nt = (((1,), (1,)), ((), ()))
    sink_cols = []
    for j in range(N_KV_HEADS):
        for parity in range(2):
            col = jnp.full((R, 1), sink_ref[2 * G * j + parity], F32)
            for r in range(1, G):
                col = jnp.where(rblk == r, sink_ref[2 * (G * j + r) + parity], col)
            sink_cols.append(col)
    if fuse_out:
        x_ref, mod_ref, g_ref, wo_ref = refs[-5:-1]
        refs = refs[:-5] + refs[-1:]
    if chunked:
        k_prev, k_cur, v_prev, v_cur, o_ref = refs
        row = lax.broadcasted_iota(jnp.int32, (R, KEY_SPAN), 0)
        cq = (row % ts) // CHUNK
        kb = kp // CHUNK
        in_window = (kb >= cq) & (kb <= cq + WINDOW // CHUNK)
    else:
        kd_ref, vd_ref, o_ref = refs

    def finish(rows, mix):
        m = mod_ref[0]
        o_ref[rows, :] = x_ref[rows, :] + m[2:3, :] * _rms(mix, g_ref[0, 1:2, :])

    pending = None
    for sub in range(tq // ts):
        rs = slice(sub * ts, (sub + 1) * ts)
        if not chunked:
            kw, vw = kd_ref[0], vd_ref[0]
            valid = kp < n_valid
        elif sub == 0:
            kw = jnp.concatenate([k_prev[0], k_cur[0, rs, :]], axis=0)
            vw = jnp.concatenate([v_prev[0], v_cur[0, rs, :]], axis=0)
            valid = in_window & (kp >= jnp.where(i > 0, 0, WINDOW))
        else:
            span = slice(sub * ts - WINDOW, (sub + 1) * ts)
            kw, vw = k_cur[0, span, :], v_cur[0, span, :]
            valid = in_window
        bias = jnp.where(valid, 0.0, NEG_INF)
        heads = []
        mix_parts = []
        for j in range(N_KV_HEADS):
            qg = q_ref[G * j:G * (j + 1), rs, :].reshape(R, LANES)
            k_lo = kw[:, (2 * j) * LANES:(2 * j + 1) * LANES]
            k_hi = kw[:, (2 * j + 1) * LANES:(2 * j + 2) * LANES]
            v_lo = vw[:, (2 * j) * LANES:(2 * j + 1) * LANES]
            v_hi = vw[:, (2 * j + 1) * LANES:(2 * j + 2) * LANES]
            s0 = lax.dot_general(qg, k_lo, nt, preferred_element_type=F32) + bias
            s1 = lax.dot_general(qg, k_hi, nt, preferred_element_type=F32) + bias
            if pending is not None:
                cs = slice(j * (D_MODEL // N_KV_HEADS), (j + 1) * (D_MODEL // N_KV_HEADS))
                mix_parts.append(jnp.dot(pending[1], wo_ref[0, :, cs],
                                         preferred_element_type=F32))
            p0 = _sink_softmax(s0, sink_cols[2 * j])
            p1 = _sink_softmax(s1, sink_cols[2 * j + 1])
            o = (jnp.dot(p0.astype(BF16), v_lo, preferred_element_type=F32)
                 + jnp.dot(p1.astype(BF16), v_hi, preferred_element_type=F32)).astype(BF16)
            if fuse_out:
                heads.extend(o[r * ts:(r + 1) * ts] for r in range(G))
            else:
                o_ref[G * j:G * (j + 1), rs, :] = o.reshape(G, ts, LANES)
        if fuse_out:
            if pending is not None:
                finish(pending[0], jnp.concatenate(mix_parts, axis=1))
            pending = (rs, jnp.concatenate(heads, axis=1))
    if fuse_out:
        finish(pending[0], jnp.dot(pending[1], wo_ref[0], preferred_element_type=F32))


def _attention(q, kd, vd, sinks, *, B, T, tq, chunked, n_valid, fuse_out=None):
    nq = T // tq
    Tk = kd.shape[1]
    W = kd.shape[2]
    kern = functools.partial(_attn_kernel, tq=tq, chunked=chunked, n_valid=n_valid,
                             fuse_out=fuse_out is not None)
    if chunked:
        assert tq % WINDOW == 0 and Tk == T
        nsub = tq // WINDOW
        prev = pl.BlockSpec((1, WINDOW, W), lambda b, i: (b, jnp.maximum(i * nsub - 1, 0), 0))
        cur = pl.BlockSpec((1, tq, W), lambda b, i: (b, i, 0))
        kv_specs, kv_args = [prev, cur, prev, cur], (kd, kd, vd, vd)
    else:
        assert Tk == KEY_SPAN and nq == 1
        whole = pl.BlockSpec((1, Tk, W), lambda b, i: (b, 0, 0))
        kv_specs, kv_args = [whole, whole], (kd, vd)
    rows = lambda b, i: (b * nq + i, 0)
    if fuse_out is None:
        out_shape = jax.ShapeDtypeStruct((N_PAIRS, B * T, LANES), BF16)
        out_spec = pl.BlockSpec((N_PAIRS, tq, LANES), lambda b, i: (0, b * nq + i, 0))
        extra_specs, extra_args = [], ()
    else:
        x, mod, g_norm, w_o, l, j = fuse_out
        D = D_MODEL
        out_shape = jax.ShapeDtypeStruct((B * T, D), F32)
        out_spec = pl.BlockSpec((tq, D), rows)
        extra_specs = [pl.BlockSpec((tq, D), rows),
                       pl.BlockSpec((1, 6, D), lambda b, i: (b, 0, 0)),
                       pl.BlockSpec((1, 4, D), lambda b, i: (l, 0, 0)),
                       pl.BlockSpec((1, D, D), lambda b, i: (j, 0, 0))]
        extra_args = (x.reshape(B * T, D), mod, g_norm, w_o)
    out = pl.pallas_call(
        kern,
        out_shape=out_shape,
        grid=(B, nq),
        in_specs=[
            pl.BlockSpec(memory_space=pltpu.SMEM),
            pl.BlockSpec((N_PAIRS, tq, LANES), lambda b, i: (0, b * nq + i, 0)),
            *kv_specs,
            *extra_specs,
        ],
        out_specs=out_spec,
        compiler_params=_params("parallel", "parallel"),
        name="swa_attention",
    )(sinks, q, *kv_args, *extra_args)
    return out if fuse_out is None else out.reshape(B, T, D_MODEL)


def _oproj_kernel(o_ref, x_ref, mod_ref, g_ref, w_ref, y_ref, *, bb, tt):
    o = jnp.concatenate([o_ref[p] for p in range(N_PAIRS)], axis=1)
    mix = jnp.dot(o, w_ref[0], preferred_element_type=F32).reshape(bb, tt, D_MODEL)
    m = mod_ref[...]
    y_ref[...] = x_ref[...] + m[:, 2:3, :] * _rms(mix, g_ref[:, 1:2, :])


def _oproj_layer(o, x, mod, g_norm, w_o, l, j, *, bb, tt):
    B, T, D = x.shape
    nt = T // tt
    kern = functools.partial(_oproj_kernel, bb=bb, tt=tt)
    return pl.pallas_call(
        kern,
        out_shape=jax.ShapeDtypeStruct((B, T, D), F32),
        grid=(B // bb, nt),
        in_specs=[
            pl.BlockSpec((N_PAIRS, bb * tt, LANES), lambda b, t: (0, b * nt + t, 0)),
            pl.BlockSpec((bb, tt, D), lambda b, t: (b, t, 0)),
            pl.BlockSpec((bb, 6, D), lambda b, t: (b, 0, 0)),
            pl.BlockSpec((1, 4, D), lambda b, t: (l, 0, 0)),
            pl.BlockSpec((1, D, D), lambda b, t: (j, 0, 0)),
        ],
        out_specs=pl.BlockSpec((bb, tt, D), lambda b, t: (b, t, 0)),
        compiler_params=_params("parallel", "parallel"),
        name=f"o_proj_{l}",
    )(o, x, mod, g_norm, w_o)


def _mlp_kernel(x_ref, mod_ref, g_ref, wu_ref, wd_ref, y_ref, *rest, bb, tt, emit_bf16):
    f = pl.program_id(2)
    m = mod_ref[...]
    if emit_bf16:
        wu_out, wd_out, h_ref = rest
        wu = wu_ref[0].astype(BF16)
        wd = wd_ref[0].astype(BF16)
        wu_out[0] = wu
        wd_out[0] = wd
    else:
        h_ref, = rest

    @pl.when(f == 0)
    def _():
        h = _modulate(_rms(x_ref[...], g_ref[:, 2:3, :]), m[:, 3:4, :], m[:, 4:5, :])
        h_ref[...] = h.reshape(bb * tt, D_MODEL).astype(BF16)
        y_ref[...] = jnp.zeros_like(y_ref)

    u = jnp.dot(h_ref[...], wu if emit_bf16 else wu_ref[0], preferred_element_type=F32)
    a = jnp.square(jnp.maximum(u, 0.0)).astype(BF16)
    for n in range(D_MODEL // MLP_OUT_CHUNK):
        cs = slice(n * MLP_OUT_CHUNK, (n + 1) * MLP_OUT_CHUNK)
        wd_n = wd[:, cs] if emit_bf16 else wd_ref[0, :, cs]
        y_ref[:, :, cs] += jnp.dot(a, wd_n, preferred_element_type=F32).reshape(
            bb, tt, MLP_OUT_CHUNK)

    @pl.when(f == pl.num_programs(2) - 1)
    def _():
        y_ref[...] = x_ref[...] + m[:, 5:6, :] * _rms(y_ref[...], g_ref[:, 3:4, :])


def _mlp_layer(x, mod, g_norm, w_up, w_down, l, *, bb, tt, tff, emit_bf16):
    B, T, D = x.shape
    grid = (B // bb, T // tt, D_FF // tff)
    kern = functools.partial(_mlp_kernel, bb=bb, tt=tt, emit_bf16=emit_bf16)
    wl = l if emit_bf16 else 0
    y_shape = jax.ShapeDtypeStruct((B, T, D), F32)
    y_spec = pl.BlockSpec((bb, tt, D), lambda b, t, f: (b, t, 0))
    if emit_bf16:
        assert grid[0] * grid[1] == 1
        out_shape = (y_shape, jax.ShapeDtypeStruct((1, D, D_FF), BF16),
                     jax.ShapeDtypeStruct((1, D_FF, D), BF16))
        out_specs = (y_spec, pl.BlockSpec((1, D, tff), lambda b, t, f: (0, 0, f)),
                     pl.BlockSpec((1, tff, D), lambda b, t, f: (0, f, 0)))
    else:
        out_shape, out_specs = y_shape, y_spec
    return pl.pallas_call(
        kern,
        out_shape=out_shape,
        grid=grid,
        in_specs=[
            pl.BlockSpec((bb, tt, D), lambda b, t, f: (b, t, 0)),
            pl.BlockSpec((bb, 6, D), lambda b, t, f: (b, 0, 0)),
            pl.BlockSpec((1, 4, D), lambda b, t, f: (l, 0, 0)),
            pl.BlockSpec((1, D, tff), lambda b, t, f: (wl, 0, f)),
            pl.BlockSpec((1, tff, D), lambda b, t, f: (wl, f, 0)),
        ],
        out_specs=out_specs,
        scratch_shapes=[pltpu.VMEM((bb * tt, D), BF16)],
        compiler_params=_params("parallel", "parallel", "arbitrary"),
        name=f"mlp_{l}",
    )(x, mod, g_norm, w_up, w_down)


def _zero_after(x):
    bits = pltpu.bitcast(x, jnp.uint32)
    r = bits[:, 0:LANES]
    for k in range(1, x.shape[1] // LANES):
        r = r | bits[:, k * LANES:(k + 1) * LANES]
    r8 = r[0:SUBLANES]
    for k in range(1, x.shape[0] // SUBLANES):
        r8 = r8 | r[k * SUBLANES:(k + 1) * SUBLANES]
    z = pltpu.bitcast((r8 >> 16) >> 16, F32)
    return jnp.max(z, axis=(0, 1), keepdims=True)


def _mlp_skew_kernel(x0_ref, xn_ref, xp_ref, modn_ref, modp_ref, g_ref, wu_ref, wd_ref, y_ref,
                     h0_ref, h1_ref, acc0_ref, acc1_ref, *, nt, rs):
    s = pl.program_id(0)
    f = pl.program_id(1)
    rows = pl.ds(pl.multiple_of(f * rs, rs), rs)
    g_pre, g_post = g_ref[0, 2:3, :], g_ref[0, 3:4, :]
    h_refs = (h0_ref, h1_ref)
    acc_refs = (acc0_ref, acc1_ref)

    def prologue(x, m):
        return _modulate(_rms(x, g_pre), m[3:4, :], m[4:5, :])

    def epilogue(acc_ref):
        mp = modp_ref[0]
        y = xp_ref[...] + mp[5:6, :] * _rms(acc_ref[rows, :], g_post)
        y_ref[...] = y
        acc_ref[rows, :] = jnp.zeros((rs, D_MODEL), F32)
        return y

    def matmuls(h_ref, acc_ref, anchors):
        u = jnp.dot(h_ref[...], wu_ref[0], preferred_element_type=F32)
        a = jnp.square(jnp.maximum(u, 0.0)).astype(BF16)
        for n in range(D_MODEL // MLP_OUT_CHUNK):
            cs = slice(n * MLP_OUT_CHUNK, (n + 1) * MLP_OUT_CHUNK)
            d = jnp.dot(a, wd_ref[0, :, cs], preferred_element_type=F32)
            if anchors.get(n) is not None:
                d = d + anchors[n]
            acc_ref[:, cs] += d

    @pl.when((s == 0) & (f == 0))
    def _():
        h0_ref[...] = prologue(x0_ref[...], modn_ref[0]).astype(BF16)
        acc0_ref[...] = jnp.zeros_like(acc0_ref)
        acc1_ref[...] = jnp.zeros_like(acc1_ref)

    for slot in range(2):
        other = 1 - slot
        parity = (s % 2) == slot

        if slot == 0:
            @pl.when(s == 0)
            def _():
                h = prologue(xn_ref[...], modn_ref[0])
                h_refs[other][rows, :] = h.astype(BF16)
                matmuls(h_refs[slot], acc_refs[slot], {2: _zero_after(h)})

        @pl.when(parity & (s > 0) & (s < nt))
        def _():
            y = epilogue(acc_refs[other])
            h = prologue(xn_ref[...], modn_ref[0])
            h_refs[other][rows, :] = h.astype(BF16)
            matmuls(h_refs[slot], acc_refs[slot], {0: _zero_after(y), 2: _zero_after(h)})

        if nt % 2 == slot:
            @pl.when(s == nt)
            def _():
                epilogue(acc_refs[other])


def _mlp_layer_skewed(x, mod, g_norm, w_up, w_down, l, *, tm, tff):
    B, T, D = x.shape
    nf = D_FF // tff
    rs = tm // nf
    tpb = T // tm
    nt = B * tpb
    assert tpb >= 2
    x2 = x.reshape(B * T, D)
    nxt = lambda s: jnp.minimum(s + 1, nt - 1)
    prv = lambda s: jnp.maximum(s - 1, 0)
    wf = lambda s, f: jnp.where(s == nt, nf - 1, f)
    kern = functools.partial(_mlp_skew_kernel, nt=nt, rs=rs)
    y = pl.pallas_call(
        kern,
        out_shape=jax.ShapeDtypeStruct((B * T, D), F32),
        grid=(nt + 1, nf),
        in_specs=[
            pl.BlockSpec((tm, D), lambda s, f: (0, 0)),
            pl.BlockSpec((rs, D), lambda s, f: (nxt(s) * nf + f, 0)),
            pl.BlockSpec((rs, D), lambda s, f: (prv(s) * nf + f, 0)),
            pl.BlockSpec((1, 6, D), lambda s, f: (nxt(s) // tpb, 0, 0)),
            pl.BlockSpec((1, 6, D), lambda s, f: (prv(s) // tpb, 0, 0)),
            pl.BlockSpec((1, 4, D), lambda s, f: (l, 0, 0)),
            pl.BlockSpec((1, D, tff), lambda s, f: (0, 0, wf(s, f))),
            pl.BlockSpec((1, tff, D), lambda s, f: (0, wf(s, f), 0)),
        ],
        out_specs=pl.BlockSpec((rs, D), lambda s, f: (jnp.where(s == 0, 0, (s - 1) * nf + f), 0)),
        scratch_shapes=[pltpu.VMEM((tm, D), BF16), pltpu.VMEM((tm, D), BF16),
                        pltpu.VMEM((tm, D), F32), pltpu.VMEM((tm, D), F32)],
        compiler_params=_params("arbitrary", "arbitrary"),
        name=f"mlp_skewed_{l}",
    )(x2, x2, x2, mod, mod, g_norm, w_up, w_down)
    return y.reshape(B, T, D)


def _split_dup(a):
    z = jnp.zeros_like(a)
    lo = jnp.concatenate([a, z], axis=-1)
    hi = jnp.concatenate([z, a], axis=-1)
    out = jnp.stack([lo, hi], axis=3)
    return out.reshape(a.shape[0], a.shape[1], -1).astype(BF16)


def _trunk(x, mods, kvmod, pos0, prefix, kv_past, wts, mlp_bf16, *, bb, tt, tff, tq):
    B, T, D = x.shape
    pos = pos0 + jnp.arange(T)
    tables = tuple(jnp.tile(tb, (bb, 1)) if bb > 1 else tb for tb in _rope_tables(pos))
    new_pool = []
    mlp_out = []
    k = v = kd = vd = None
    for l in range(DEPTH):
        mod = mods[l]
        if l < N_A_LAYERS:
            x, npool = _pool_layer(x, mod, wts["g_norm"], prefix, wts["w_pool"],
                                   wts["pool_scale"], l, bb=bb, tt=tt, pos0=pos0)
            new_pool.append(npool)
        else:
            j = l - N_A_LAYERS
            q = _qproj_layer(x, mod, wts["g_norm"], tables, wts["w_q"], l, j, bb=bb, tt=tt)
            if kv_past is None:
                x = _attention(q, kd, vd, wts["sinks"][j], B=B, T=T, tq=tq, chunked=True,
                               n_valid=None,
                               fuse_out=(x, mod, wts["g_norm"], wts["w_o"], l, j))
            else:
                o = _attention(q, kd, vd, wts["sinks"][j], B=B, T=T, tq=tq, chunked=False,
                               n_valid=kv_past[0].shape[1] + T)
                x = _oproj_layer(o, x, mod, wts["g_norm"], wts["w_o"], l, j, bb=bb, tt=tt)
        if mlp_bf16 is None:
            x, wu_l, wd_l = _mlp_layer(x, mod, wts["g_norm"], wts["w_up"], wts["w_down"], l,
                                       bb=bb, tt=tt, tff=tff, emit_bf16=True)
            mlp_out.append((wu_l, wd_l))
        else:
            x = _mlp_layer_skewed(x, mod, wts["g_norm"], *mlp_bf16[l], l, tm=tt, tff=tff)
        if l == N_A_LAYERS - 1:
            k, v, kd, vd = _kv_layer(x, kvmod, wts["g_kv"], tables, wts["w_kv"],
                                     bb=bb, tt=tt)
            W = kd.shape[-1]
            kd = kd.reshape(B, T, W)
            vd = vd.reshape(B, T, W)
            if kv_past is not None:
                back = ((0, 0), (0, KEY_SPAN - kv_past[0].shape[1] - T), (0, 0))
                kd = jnp.pad(jnp.concatenate([_split_dup(kv_past[0]), kd], axis=1), back)
                vd = jnp.pad(jnp.concatenate([_split_dup(kv_past[1]), vd], axis=1), back)
    KV = N_KV_HEADS * HEAD_DIM
    return x, jnp.concatenate(new_pool, axis=0), k.reshape(B, T, KV), v.reshape(B, T, KV), mlp_out


def _prep_weights(g_norm, w_pool, pool_scale, g_kv, w_kv, w_q, sinks, w_o, w_up, w_down):
    D = D_MODEL
    return {
        "g_norm": g_norm,
        "w_pool": w_pool.astype(BF16),
        "pool_scale": pool_scale.reshape(N_A_LAYERS, 1, D),
        "g_kv": g_kv.reshape(1, 1, D),
        "w_kv": w_kv.astype(BF16),
        "w_q": w_q.astype(BF16),
        "sinks": sinks,
        "w_o": w_o.astype(BF16),
        "w_up": w_up,
        "w_down": w_down,
    }


def kernel(x_prompt, x_sample, c_prompt, c_sample, state_pool, cache_k, cache_v, w_mod, b_mod,
           g_norm, w_pool, pool_scale, w_kv_mod, b_kv_mod, g_kv, w_kv, w_q, sinks, w_o, w_up,
           w_down):
    Bp, Bs = x_prompt.shape[0], x_sample.shape[0]
    D = D_MODEL

    c_all = jnp.concatenate(
        [c_prompt, c_sample, jnp.zeros((MOD_ROWS - Bp - Bs, D), F32)], axis=0)
    mod_all = _modulation(c_all, w_mod, b_mod.reshape(DEPTH, 1, 6 * D)).reshape(
        DEPTH, MOD_ROWS, 6, D)
    kvmod_all = _modulation(c_all, w_kv_mod.reshape(1, D, 2 * D),
                            b_kv_mod.reshape(1, 1, 2 * D)).reshape(MOD_ROWS, 2, D)

    wts = _prep_weights(g_norm, w_pool, pool_scale, g_kv, w_kv, w_q, sinks, w_o, w_up, w_down)

    T_s = x_sample.shape[1]
    y_s, pool_s, k_s, v_s, mlp_bf16 = _trunk(
        x_sample, [mod_all[l, Bp:Bp + Bs] for l in range(DEPTH)], kvmod_all[Bp:Bp + Bs],
        PAST_LEN, state_pool, (cache_k, cache_v), wts, None, bb=Bs, tt=T_s, tff=512, tq=T_s)
    prompt_prefix = jnp.zeros((N_A_LAYERS, Bp, POOL_STATE, D), F32)
    y_p, pool_p, k_p, v_p, _ = _trunk(
        x_prompt, [mod_all[l, :Bp] for l in range(DEPTH)], kvmod_all[:Bp], 0, prompt_prefix,
        None, wts, mlp_bf16, bb=1, tt=512, tff=1024, tq=4 * WINDOW)
    keep = min(WINDOW, x_prompt.shape[1])
    heads = (N_KV_HEADS, HEAD_DIM)
    return (y_p, y_s, pool_p, pool_s,
            k_p[:, -keep:].reshape(Bp, keep, *heads), v_p[:, -keep:].reshape(Bp, keep, *heads),
            k_s.reshape(Bs, T_s, *heads), v_s.reshape(Bs, T_s, *heads))
```

```python
import functools

import jax
import jax.numpy as jnp
from jax import lax
from jax.experimental import pallas as pl
from jax.experimental.pallas import tpu as pltpu

F32 = jnp.float32
BF16 = jnp.bfloat16

D_MODEL = 2048
DEPTH = 4
PAST_LEN = 4096
CHUNK = 64
N_A_LAYERS = DEPTH // 2
POOL_WINDOWS = (2, 4, 8, 16)
POOL_GROUP = D_MODEL // len(POOL_WINDOWS)
POOL_STATE = max(POOL_WINDOWS) - 1
SUBLANES = 8
POOL_LEAD = SUBLANES
POOL_BASE = POOL_LEAD + POOL_STATE + 1
HEAD_DIM = 64
N_HEADS = D_MODEL // HEAD_DIM
N_KV_HEADS = N_HEADS // 8
WINDOW = 128
D_FF = 4 * D_MODEL
ROPE_THETA = 10000.0
EPS = 1e-6
ATTN_SCALE = HEAD_DIM ** -0.5
NEG_INF = -1e30

LANES = 128
KEY_SPAN = 2 * LANES
N_PAIRS = N_HEADS // 2
PAIRS_PER_KV = N_PAIRS // N_KV_HEADS
MLP_OUT_CHUNK = 512
MOD_ROWS = 32
VMEM_LIMIT = 56 * 1024 * 1024


def _params(*sem):
    return pltpu.CompilerParams(dimension_semantics=sem, vmem_limit_bytes=VMEM_LIMIT)


def _norm_mod(x, g, shift, scale):
    ms = jnp.mean(x * x, axis=-1, keepdims=True)
    return (x * lax.rsqrt(ms + EPS)) * (g * (1.0 + scale)) + shift


def _gated_norm(y, g, gate):
    ms = jnp.mean(y * y, axis=-1, keepdims=True)
    return (y * lax.rsqrt(ms + EPS)) * (g * gate)


def _mod_kernel(c_ref, w_ref, b_ref, o_ref):
    c = c_ref[...]
    sc = (c * jax.nn.sigmoid(c)).astype(BF16)
    o_ref[0] = jnp.dot(sc, w_ref[0].astype(BF16), preferred_element_type=F32) + b_ref[0]


def _modulation(c_all, w, b, tn=1024):
    L, D, N = w.shape
    return pl.pallas_call(
        _mod_kernel,
        out_shape=jax.ShapeDtypeStruct((L, MOD_ROWS, N), F32),
        grid=(L, N // tn),
        in_specs=[
            pl.BlockSpec((MOD_ROWS, D), lambda l, n: (0, 0)),
            pl.BlockSpec((1, D, tn), lambda l, n: (l, 0, n)),
            pl.BlockSpec((1, 1, tn), lambda l, n: (l, 0, n)),
        ],
        out_specs=pl.BlockSpec((1, MOD_ROWS, tn), lambda l, n: (l, 0, n)),
        compiler_params=_params("parallel", "parallel"),
        name="modulation",
    )(c_all, w, b)


def _pool_kernel(x_ref, mod_ref, g_ref, pre_ref, wp_ref, ps_ref, o_ref, np_ref, hbuf, s1, s2,
                 *, bb, tt, pos0):
    t = pl.program_id(1)
    G = POOL_GROUP
    LEAD, BASE = POOL_LEAD, POOL_BASE
    L = BASE + tt

    @pl.when(t == 0)
    def _():
        hbuf[:, 0:LEAD + 1, :] = jnp.zeros((bb, LEAD + 1, D_MODEL), F32)
        hbuf[:, LEAD + 1:BASE, :] = pre_ref[0]
        s1[:, 0:LEAD, :] = jnp.zeros((bb, LEAD, 3 * G), F32)
        s2[:, 0:LEAD, :] = jnp.zeros((bb, LEAD, 3 * G), F32)

    @pl.when(t > 0)
    def _():
        hbuf[:, LEAD:BASE, :] = hbuf[:, tt + LEAD:tt + BASE, :]

    x = x_ref[...]
    m = mod_ref[...]
    h = _norm_mod(x, g_ref[:, 0:1, :], m[:, 0:1, :], m[:, 1:2, :])
    hbuf[:, BASE:L, :] = h
    np_ref[0] = hbuf[:, L - POOL_STATE:L, :]

    s1[:, LEAD:L, :] = hbuf[:, LEAD:L, G:] + hbuf[:, LEAD - 1:L - 1, G:]
    s2[:, LEAD:L, :] = s1[:, LEAD:L, :] + s1[:, LEAD - 2:L - 2, :]
    s1[:, 2 * LEAD:L, 0:2 * G] = (s2[:, 2 * LEAD:L, G:] +
                                  s2[:, 2 * LEAD - 4:L - 4, G:])
    sums = [
        hbuf[:, BASE:L, 0:G] + hbuf[:, BASE - 1:L - 1, 0:G],
        s2[:, BASE:L, 0:G],
        s1[:, BASE:L, 0:G],
        s1[:, BASE:L, G:2 * G] + s1[:, BASE - 8:L - 8, G:2 * G],
    ]

    pos = pos0 + t * tt + lax.broadcasted_iota(jnp.int32, (1, tt, LANES), 1)
    ys = []
    for g, w in enumerate(POOL_WINDOWS):
        cs = slice(g * POOL_GROUP, (g + 1) * POOL_GROUP)
        inv_cnt = 1.0 / jnp.minimum(w, pos + 1).astype(F32)
        inv_cnt = jnp.concatenate([inv_cnt] * (POOL_GROUP // LANES), axis=-1)
        pooled = sums[g] * inv_cnt - hbuf[:, BASE:L, cs]
        ys.append(jnp.dot(pooled.reshape(bb * tt, POOL_GROUP).astype(BF16), wp_ref[0, g],
                          preferred_element_type=F32))
    y = jnp.concatenate(ys, axis=-1).reshape(bb, tt, D_MODEL) * ps_ref[...]
    o_ref[...] = x + _gated_norm(y, g_ref[:, 1:2, :], m[:, 2:3, :])


def _pool_layer(x, mod, g_norm, prefix, w_pool, pool_scale, l, *, bb, tt, pos0):
    B, T, D = x.shape
    kern = functools.partial(_pool_kernel, bb=bb, tt=tt, pos0=pos0)
    return pl.pallas_call(
        kern,
        out_shape=(jax.ShapeDtypeStruct((B, T, D), F32),
                   jax.ShapeDtypeStruct((1, B, POOL_STATE, D), F32)),
        grid=(B // bb, T // tt),
        in_specs=[
            pl.BlockSpec((bb, tt, D), lambda b, t: (b, t, 0)),
            pl.BlockSpec((bb, 6, D), lambda b, t: (b, 0, 0)),
            pl.BlockSpec((1, 4, D), lambda b, t: (l, 0, 0)),
            pl.BlockSpec((1, bb, POOL_STATE, D), lambda b, t: (l, b, 0, 0)),
            pl.BlockSpec((1, len(POOL_WINDOWS), POOL_GROUP, POOL_GROUP),
                         lambda b, t: (l, 0, 0, 0)),
            pl.BlockSpec((1, 1, D), lambda b, t: (l, 0, 0)),
        ],
        out_specs=(pl.BlockSpec((bb, tt, D), lambda b, t: (b, t, 0)),
                   pl.BlockSpec((1, bb, POOL_STATE, D), lambda b, t: (0, b, 0, 0))),
        scratch_shapes=[pltpu.VMEM((bb, POOL_BASE + tt, D), F32),
                        pltpu.VMEM((bb, POOL_BASE + tt, 3 * POOL_GROUP), F32),
                        pltpu.VMEM((bb, POOL_BASE + tt, 3 * POOL_GROUP), F32)],
        compiler_params=_params("parallel", "arbitrary"),
        name=f"pool_mixer_{l}",
    )(x, mod, g_norm, prefix, w_pool, pool_scale)


def _rope_tables(pos):
    half = HEAD_DIM // 2
    inv = ROPE_THETA ** (-jnp.arange(half, dtype=F32) / half)
    ang = pos.astype(F32)[:, None] * inv[None, :]
    cos, sin = jnp.cos(ang), jnp.sin(ang)
    zero = jnp.zeros_like(sin)
    c = jnp.tile(cos, (1, 4))
    s_lo = jnp.tile(jnp.concatenate([-sin, zero], axis=1), (1, 2))
    s_hi = jnp.tile(jnp.concatenate([zero, sin], axis=1), (1, 2))
    return c, s_lo, s_hi


def _rope_block(blk, c, s_lo, s_hi):
    return (blk * c + pltpu.roll(blk, LANES - HEAD_DIM // 2, 1) * s_lo
            + pltpu.roll(blk, HEAD_DIM // 2, 1) * s_hi)


def _qproj_kernel(x_ref, mod_ref, g_ref, c_ref, slo_ref, shi_ref, w_ref, q_ref, *, bb, tt):
    x = x_ref[...]
    m = mod_ref[...]
    h = _norm_mod(x, g_ref[:, 0:1, :], m[:, 0:1, :], m[:, 1:2, :])
    q = jnp.dot(h.reshape(bb * tt, D_MODEL).astype(BF16), w_ref[0], preferred_element_type=F32)
    c, s_lo, s_hi = c_ref[...], slo_ref[...], shi_ref[...]
    for p in range(N_PAIRS):
        cs = slice(p * LANES, (p + 1) * LANES)
        q_ref[p] = (_rope_block(q[:, cs], c, s_lo, s_hi) * ATTN_SCALE).astype(BF16)


def _qproj_layer(x, mod, g_norm, tables, w_q, l, j, *, bb, tt):
    B, T, D = x.shape
    nt = T // tt
    M = bb * tt
    kern = functools.partial(_qproj_kernel, bb=bb, tt=tt)
    tab_spec = pl.BlockSpec((M, LANES), lambda b, t: (t, 0))
    return pl.pallas_call(
        kern,
        out_shape=jax.ShapeDtypeStruct((N_PAIRS, B * T, LANES), BF16),
        grid=(B // bb, nt),
        in_specs=[
            pl.BlockSpec((bb, tt, D), lambda b, t: (b, t, 0)),
            pl.BlockSpec((bb, 6, D), lambda b, t: (b, 0, 0)),
            pl.BlockSpec((1, 4, D), lambda b, t: (l, 0, 0)),
            tab_spec, tab_spec, tab_spec,
            pl.BlockSpec((1, D, D), lambda b, t: (j, 0, 0)),
        ],
        out_specs=pl.BlockSpec((N_PAIRS, M, LANES), lambda b, t: (0, b * nt + t, 0)),
        compiler_params=_params("parallel", "parallel"),
        name=f"q_proj_{l}",
    )(x, mod, g_norm, *tables, w_q)


def _kv_kernel(x_ref, mod_ref, g_ref, c_ref, slo_ref, shi_ref, w_ref,
               k_ref, v_ref, kd_ref, vd_ref, *, bb, tt):
    x = x_ref[...]
    m = mod_ref[...]
    h = _norm_mod(x, g_ref[...], m[:, 0:1, :], m[:, 1:2, :])
    kv = jnp.dot(h.reshape(bb * tt, D_MODEL).astype(BF16), w_ref[...],
                 preferred_element_type=F32)
    c, s_lo, s_hi = c_ref[...], slo_ref[...], shi_ref[...]
    lo = lax.broadcasted_iota(jnp.int32, (bb * tt, LANES), 1) < HEAD_DIM
    zero = jnp.zeros((bb * tt, LANES), F32)
    n_blk = N_KV_HEADS // 2
    for p in range(n_blk):
        cs = slice(p * LANES, (p + 1) * LANES)
        k_blk = _rope_block(kv[:, cs], c, s_lo, s_hi)
        v_blk = kv[:, (n_blk + p) * LANES:(n_blk + p + 1) * LANES]
        k_ref[:, cs] = k_blk
        v_ref[:, cs] = v_blk
        for blk, dst in ((k_blk, kd_ref), (v_blk, vd_ref)):
            swapped = pltpu.roll(blk, HEAD_DIM, 1)
            parts = (jnp.where(lo, blk, zero), jnp.where(lo, zero, swapped),
                     jnp.where(lo, swapped, zero), jnp.where(lo, zero, blk))
            for i, part in enumerate(parts):
                dst[:, (4 * p + i) * LANES:(4 * p + i + 1) * LANES] = part.astype(BF16)


def _kv_layer(x, mod, g_kv, tables, w_kv, *, bb, tt):
    B, T, D = x.shape
    nt = T // tt
    M = bb * tt
    KV = N_KV_HEADS * HEAD_DIM
    kern = functools.partial(_kv_kernel, bb=bb, tt=tt)
    tab_spec = pl.BlockSpec((M, LANES), lambda b, t: (t, 0))
    row = lambda b, t: (b * nt + t, 0)
    return pl.pallas_call(
        kern,
        out_shape=(jax.ShapeDtypeStruct((B * T, KV), F32),
                   jax.ShapeDtypeStruct((B * T, KV), F32),
                   jax.ShapeDtypeStruct((B * T, 4 * KV), BF16),
                   jax.ShapeDtypeStruct((B * T, 4 * KV), BF16)),
        grid=(B // bb, nt),
        in_specs=[
            pl.BlockSpec((bb, tt, D), lambda b, t: (b, t, 0)),
            pl.BlockSpec((bb, 2, D), lambda b, t: (b, 0, 0)),
            pl.BlockSpec((1, 1, D), lambda b, t: (0, 0, 0)),
            tab_spec, tab_spec, tab_spec,
            pl.BlockSpec((D, 2 * KV), lambda b, t: (0, 0)),
        ],
        out_specs=(pl.BlockSpec((M, KV), row), pl.BlockSpec((M, KV), row),
                   pl.BlockSpec((M, 4 * KV), row), pl.BlockSpec((M, 4 * KV), row)),
        compiler_params=_params("parallel", "parallel"),
        name="shared_kv",
    )(x, mod, g_kv, *tables, w_kv)


def _sink_softmax(s, sink):
    m = jnp.max(s, axis=-1, keepdims=True)
    e = jnp.exp(s - m)
    den = jnp.sum(e, axis=-1, keepdims=True) + jnp.exp(sink - m)
    return e * (1.0 / den)


def _attn_kernel(sink_ref, *refs, tq, chunked, n_valid, fused):
    i = pl.program_id(1)
    G = PAIRS_PER_KV
    ts = WINDOW if chunked else tq
    nsub = tq // ts
    R = G * ts
    QW = D_MODEL // N_KV_HEADS
    kp = lax.broadcasted_iota(jnp.int32, (R, KEY_SPAN), 1)
    rblk = lax.broadcasted_iota(jnp.int32, (R, 1), 0) // ts
    nt = (((1,), (1,)), ((), ()))
    sink_cols = []
    for j in range(N_KV_HEADS):
        for parity in range(2):
            col = jnp.full((R, 1), sink_ref[2 * G * j + parity], F32)
            for r in range(1, G):
                col = jnp.where(rblk == r, sink_ref[2 * (G * j + r) + parity], col)
            sink_cols.append(col)
    if fused:
        q_ref, k_prev, k_cur, v_prev, v_cur, x_ref, mod_ref, g_ref, wo_ref, o_ref = refs
    elif chunked:
        q_ref, k_prev, k_cur, v_prev, v_cur, o_ref = refs
    else:
        q_ref, kd_ref, vd_ref, o_ref = refs
    if chunked:
        row = lax.broadcasted_iota(jnp.int32, (R, KEY_SPAN), 0)
        cq = (row % ts) // CHUNK
        kb = kp // CHUNK
        in_window = (kb >= cq) & (kb <= cq + WINDOW // CHUNK)

    def finish(rows, mix):
        m = mod_ref[0]
        o_ref[rows, :] = x_ref[rows, :] + _gated_norm(mix, g_ref[0, 1:2, :], m[2:3, :])

    pending = None
    for sub in range(nsub):
        rs = slice(sub * ts, (sub + 1) * ts)
        if not chunked:
            kw, vw = kd_ref[0], vd_ref[0]
            valid = kp < n_valid
        elif sub == 0:
            kw = jnp.concatenate([k_prev[0], k_cur[0, rs, :]], axis=0)
            vw = jnp.concatenate([v_prev[0], v_cur[0, rs, :]], axis=0)
            valid = in_window & (kp >= jnp.where(i > 0, 0, WINDOW))
        else:
            span = slice(sub * ts - WINDOW, (sub + 1) * ts)
            kw, vw = k_cur[0, span, :], v_cur[0, span, :]
            valid = in_window
        bias = jnp.where(valid, 0.0, NEG_INF)
        heads = []
        mix_parts = []
        for j in range(N_KV_HEADS):
            qg = q_ref[G * j:G * (j + 1), rs, :].reshape(R, LANES)
            k_lo = kw[:, (2 * j) * LANES:(2 * j + 1) * LANES]
            k_hi = kw[:, (2 * j + 1) * LANES:(2 * j + 2) * LANES]
            v_lo = vw[:, (2 * j) * LANES:(2 * j + 1) * LANES]
            v_hi = vw[:, (2 * j + 1) * LANES:(2 * j + 2) * LANES]
            s0 = lax.dot_general(qg, k_lo, nt, preferred_element_type=F32) + bias
            s1 = lax.dot_general(qg, k_hi, nt, preferred_element_type=F32) + bias
            if pending is not None:
                mix_parts.append(jnp.dot(pending[1], wo_ref[0, :, j * QW:(j + 1) * QW],
                                         preferred_element_type=F32))
            p0 = _sink_softmax(s0, sink_cols[2 * j])
            p1 = _sink_softmax(s1, sink_cols[2 * j + 1])
            o = (jnp.dot(p0.astype(BF16), v_lo, preferred_element_type=F32)
                 + jnp.dot(p1.astype(BF16), v_hi, preferred_element_type=F32)).astype(BF16)
            if fused:
                heads.extend(o[r * ts:(r + 1) * ts] for r in range(G))
            else:
                o_ref[G * j:G * (j + 1), rs, :] = o.reshape(G, ts, LANES)
        if fused:
            if pending is not None:
                finish(pending[0], jnp.concatenate(mix_parts, axis=1))
            pending = (rs, jnp.concatenate(heads, axis=1))
    if fused:
        finish(pending[0], jnp.dot(pending[1], wo_ref[0], preferred_element_type=F32))


def _attention(q, kd, vd, sinks, *, B, T, tq, chunked, n_valid):
    nq = T // tq
    Tk = kd.shape[1]
    W = kd.shape[2]
    kern = functools.partial(_attn_kernel, tq=tq, chunked=chunked, n_valid=n_valid, fused=False)
    if chunked:
        kv_specs, kv_args = _window_specs(kd, vd, T, tq)
    else:
        assert Tk == KEY_SPAN and nq == 1
        whole = pl.BlockSpec((1, Tk, W), lambda b, i: (b, 0, 0))
        kv_specs, kv_args = [whole, whole], (kd, vd)
    pair_rows = pl.BlockSpec((N_PAIRS, tq, LANES), lambda b, i: (0, b * nq + i, 0))
    return pl.pallas_call(
        kern,
        out_shape=jax.ShapeDtypeStruct((N_PAIRS, B * T, LANES), BF16),
        grid=(B, nq),
        in_specs=[pl.BlockSpec(memory_space=pltpu.SMEM), pair_rows, *kv_specs],
        out_specs=pair_rows,
        compiler_params=_params("parallel", "parallel"),
        name="swa_attention",
    )(sinks, q, *kv_args)


def _window_specs(kd, vd, T, tq):
    W = kd.shape[2]
    assert tq % WINDOW == 0 and kd.shape[1] == T
    nsub = tq // WINDOW
    prev = pl.BlockSpec((1, WINDOW, W), lambda b, i: (b, jnp.maximum(i * nsub - 1, 0), 0))
    cur = pl.BlockSpec((1, tq, W), lambda b, i: (b, i, 0))
    return [prev, cur, prev, cur], (kd, kd, vd, vd)


def _attention_out(q, x, mod, g_norm, w_o, kd, vd, sinks, l, j, *, tq):
    B, T, D = x.shape
    nq = T // tq
    kern = functools.partial(_attn_kernel, tq=tq, chunked=True, n_valid=None, fused=True)
    kv_specs, kv_args = _window_specs(kd, vd, T, tq)
    rows = lambda b, i: (b * nq + i, 0)
    out = pl.pallas_call(
        kern,
        out_shape=jax.ShapeDtypeStruct((B * T, D), F32),
        grid=(B, nq),
        in_specs=[
            pl.BlockSpec(memory_space=pltpu.SMEM),
            pl.BlockSpec((N_PAIRS, tq, LANES), lambda b, i: (0, b * nq + i, 0)),
            *kv_specs,
            pl.BlockSpec((tq, D), rows),
            pl.BlockSpec((1, 6, D), lambda b, i: (b, 0, 0)),
            pl.BlockSpec((1, 4, D), lambda b, i: (l, 0, 0)),
            pl.BlockSpec((1, D, D), lambda b, i: (j, 0, 0)),
        ],
        out_specs=pl.BlockSpec((tq, D), rows),
        compiler_params=_params("parallel", "parallel"),
        name=f"attention_out_{l}",
    )(sinks, q, *kv_args, x.reshape(B * T, D), mod, g_norm, w_o)
    return out.reshape(B, T, D)


def _oproj_kernel(o_ref, x_ref, mod_ref, g_ref, w_ref, y_ref, *, bb, tt):
    o = jnp.concatenate([o_ref[p] for p in range(N_PAIRS)], axis=1)
    mix = jnp.dot(o, w_ref[0], preferred_element_type=F32).reshape(bb, tt, D_MODEL)
    m = mod_ref[...]
    y_ref[...] = x_ref[...] + _gated_norm(mix, g_ref[:, 1:2, :], m[:, 2:3, :])


def _oproj_layer(o, x, mod, g_norm, w_o, l, j, *, bb, tt):
    B, T, D = x.shape
    nt = T // tt
    kern = functools.partial(_oproj_kernel, bb=bb, tt=tt)
    return pl.pallas_call(
        kern,
        out_shape=jax.ShapeDtypeStruct((B, T, D), F32),
        grid=(B // bb, nt),
        in_specs=[
            pl.BlockSpec((N_PAIRS, bb * tt, LANES), lambda b, t: (0, b * nt + t, 0)),
            pl.BlockSpec((bb, tt, D), lambda b, t: (b, t, 0)),
            pl.BlockSpec((bb, 6, D), lambda b, t: (b, 0, 0)),
            pl.BlockSpec((1, 4, D), lambda b, t: (l, 0, 0)),
            pl.BlockSpec((1, D, D), lambda b, t: (j, 0, 0)),
        ],
        out_specs=pl.BlockSpec((bb, tt, D), lambda b, t: (b, t, 0)),
        compiler_params=_params("parallel", "parallel"),
        name=f"o_proj_{l}",
    )(o, x, mod, g_norm, w_o)


def _mlp_kernel(x_ref, mod_ref, g_ref, wu_ref, wd_ref, y_ref, wu_out, wd_out, h_ref, *, bb, tt):
    f = pl.program_id(2)
    m = mod_ref[...]

    @pl.when(f == 0)
    def _():
        h = _norm_mod(x_ref[...], g_ref[:, 2:3, :], m[:, 3:4, :], m[:, 4:5, :])
        h_ref[...] = h.reshape(bb * tt, D_MODEL).astype(BF16)
        y_ref[...] = jnp.zeros_like(y_ref)

    wu = wu_ref[0].astype(BF16)
    wu_out[0] = wu
    u = jnp.dot(h_ref[...], wu, preferred_element_type=F32)
    a = jnp.square(jnp.maximum(u, 0.0)).astype(BF16)
    for n in range(D_MODEL // MLP_OUT_CHUNK):
        cs = slice(n * MLP_OUT_CHUNK, (n + 1) * MLP_OUT_CHUNK)
        wd_n = wd_ref[0, :, cs].astype(BF16)
        wd_out[0, :, cs] = wd_n
        y_ref[:, :, cs] += jnp.dot(a, wd_n, preferred_element_type=F32).reshape(
            bb, tt, MLP_OUT_CHUNK)

    @pl.when(f == pl.num_programs(2) - 1)
    def _():
        y_ref[...] = x_ref[...] + _gated_norm(y_ref[...], g_ref[:, 3:4, :], m[:, 5:6, :])


def _mlp_layer(x, mod, g_norm, w_up, w_down, l, *, bb, tt, tff):
    B, T, D = x.shape
    grid = (B // bb, T // tt, D_FF // tff)
    assert grid[0] * grid[1] == 1
    kern = functools.partial(_mlp_kernel, bb=bb, tt=tt)
    return pl.pallas_call(
        kern,
        out_shape=(jax.ShapeDtypeStruct((B, T, D), F32),
                   jax.ShapeDtypeStruct((1, D, D_FF), BF16),
                   jax.ShapeDtypeStruct((1, D_FF, D), BF16)),
        grid=grid,
        in_specs=[
            pl.BlockSpec((bb, tt, D), lambda b, t, f: (b, t, 0)),
            pl.BlockSpec((bb, 6, D), lambda b, t, f: (b, 0, 0)),
            pl.BlockSpec((1, 4, D), lambda b, t, f: (l, 0, 0)),
            pl.BlockSpec((1, D, tff), lambda b, t, f: (l, 0, f)),
            pl.BlockSpec((1, tff, D), lambda b, t, f: (l, f, 0)),
        ],
        out_specs=(pl.BlockSpec((bb, tt, D), lambda b, t, f: (b, t, 0)),
                   pl.BlockSpec((1, D, tff), lambda b, t, f: (0, 0, f)),
                   pl.BlockSpec((1, tff, D), lambda b, t, f: (0, f, 0))),
        scratch_shapes=[pltpu.VMEM((bb * tt, D), BF16)],
        compiler_params=_params("parallel", "parallel", "arbitrary"),
        name=f"mlp_{l}",
    )(x, mod, g_norm, w_up, w_down)


def _zero_after(x):
    bits = pltpu.bitcast(x, jnp.uint32)
    r = bits[:, 0:LANES]
    for k in range(1, x.shape[1] // LANES):
        r = r | bits[:, k * LANES:(k + 1) * LANES]
    r8 = r[0:SUBLANES]
    for k in range(1, x.shape[0] // SUBLANES):
        r8 = r8 | r[k * SUBLANES:(k + 1) * SUBLANES]
    z = pltpu.bitcast((r8 >> 16) >> 16, F32)
    return jnp.max(z, axis=(0, 1), keepdims=True)


def _mlp_skew_kernel(x0_ref, xn_ref, xp_ref, modn_ref, modp_ref, g_ref, wu_ref, wd_ref, y_ref,
                     h0_ref, h1_ref, acc0_ref, acc1_ref, *, nt, rs):
    s = pl.program_id(0)
    f = pl.program_id(1)
    rows = pl.ds(pl.multiple_of(f * rs, rs), rs)
    g_pre, g_post = g_ref[0, 2:3, :], g_ref[0, 3:4, :]
    h_refs = (h0_ref, h1_ref)
    acc_refs = (acc0_ref, acc1_ref)

    def prologue(x, m):
        return _norm_mod(x, g_pre, m[3:4, :], m[4:5, :])

    def epilogue(acc_ref):
        mp = modp_ref[0]
        y = xp_ref[...] + _gated_norm(acc_ref[rows, :], g_post, mp[5:6, :])
        y_ref[...] = y
        acc_ref[rows, :] = jnp.zeros((rs, D_MODEL), F32)
        return y

    def matmuls(h_ref, acc_ref, anchors):
        u = jnp.dot(h_ref[...], wu_ref[0], preferred_element_type=F32)
        a = jnp.square(jnp.maximum(u, 0.0)).astype(BF16)
        for n in range(D_MODEL // MLP_OUT_CHUNK):
            cs = slice(n * MLP_OUT_CHUNK, (n + 1) * MLP_OUT_CHUNK)
            d = jnp.dot(a, wd_ref[0, :, cs], preferred_element_type=F32)
            if anchors.get(n) is not None:
                d = d + anchors[n]
            acc_ref[:, cs] += d

    @pl.when((s == 0) & (f == 0))
    def _():
        h0_ref[...] = prologue(x0_ref[...], modn_ref[0]).astype(BF16)
        acc0_ref[...] = jnp.zeros_like(acc0_ref)
        acc1_ref[...] = jnp.zeros_like(acc1_ref)

    for slot in range(2):
        other = 1 - slot
        parity = (s % 2) == slot

        if slot == 0:
            @pl.when(s == 0)
            def _():
                h = prologue(xn_ref[...], modn_ref[0])
                h_refs[other][rows, :] = h.astype(BF16)
                matmuls(h_refs[slot], acc_refs[slot], {2: _zero_after(h)})

        @pl.when(parity & (s > 0) & (s < nt))
        def _():
            y = epilogue(acc_refs[other])
            h = prologue(xn_ref[...], modn_ref[0])
            h_refs[other][rows, :] = h.astype(BF16)
            matmuls(h_refs[slot], acc_refs[slot], {0: _zero_after(y), 2: _zero_after(h)})

        if nt % 2 == slot:
            @pl.when(s == nt)
            def _():
                epilogue(acc_refs[other])


def _mlp_layer_skewed(x, mod, g_norm, w_up, w_down, l, *, tm, tff):
    B, T, D = x.shape
    nf = D_FF // tff
    rs = tm // nf
    tpb = T // tm
    nt = B * tpb
    assert tpb >= 2
    x2 = x.reshape(B * T, D)
    nxt = lambda s: jnp.minimum(s + 1, nt - 1)
    prv = lambda s: jnp.maximum(s - 1, 0)
    wf = lambda s, f: jnp.where(s == nt, nf - 1, f)
    kern = functools.partial(_mlp_skew_kernel, nt=nt, rs=rs)
    y = pl.pallas_call(
        kern,
        out_shape=jax.ShapeDtypeStruct((B * T, D), F32),
        grid=(nt + 1, nf),
        in_specs=[
            pl.BlockSpec((tm, D), lambda s, f: (0, 0)),
            pl.BlockSpec((rs, D), lambda s, f: (nxt(s) * nf + f, 0)),
            pl.BlockSpec((rs, D), lambda s, f: (prv(s) * nf + f, 0)),
            pl.BlockSpec((1, 6, D), lambda s, f: (nxt(s) // tpb, 0, 0)),
            pl.BlockSpec((1, 6, D), lambda s, f: (prv(s) // tpb, 0, 0)),
            pl.BlockSpec((1, 4, D), lambda s, f: (l, 0, 0)),
            pl.BlockSpec((1, D, tff), lambda s, f: (0, 0, wf(s, f))),
            pl.BlockSpec((1, tff, D), lambda s, f: (0, wf(s, f), 0)),
        ],
        out_specs=pl.BlockSpec((rs, D), lambda s, f: (jnp.where(s == 0, 0, (s - 1) * nf + f), 0)),
        scratch_shapes=[pltpu.VMEM((tm, D), BF16), pltpu.VMEM((tm, D), BF16),
                        pltpu.VMEM((tm, D), F32), pltpu.VMEM((tm, D), F32)],
        compiler_params=_params("arbitrary", "arbitrary"),
        name=f"mlp_skewed_{l}",
    )(x2, x2, x2, mod, mod, g_norm, w_up, w_down)
    return y.reshape(B, T, D)


def _split_dup(a):
    z = jnp.zeros_like(a)
    lo = jnp.concatenate([a, z], axis=-1)
    hi = jnp.concatenate([z, a], axis=-1)
    out = jnp.stack([lo, hi], axis=3)
    return out.reshape(a.shape[0], a.shape[1], -1).astype(BF16)


def _trunk(x, mods, kvmod, pos0, prefix, kv_past, wts, mlp_bf16, *, bb, tt, tff, tq):
    B, T, D = x.shape
    pos = pos0 + jnp.arange(T)
    tables = tuple(jnp.tile(tb, (bb, 1)) if bb > 1 else tb for tb in _rope_tables(pos))
    new_pool = []
    mlp_out = []
    k = v = kd = vd = None
    for l in range(DEPTH):
        mod = mods[l]
        if l < N_A_LAYERS:
            x, npool = _pool_layer(x, mod, wts["g_norm"], prefix, wts["w_pool"],
                                   wts["pool_scale"], l, bb=bb, tt=tt, pos0=pos0)
            new_pool.append(npool)
        else:
            j = l - N_A_LAYERS
            q = _qproj_layer(x, mod, wts["g_norm"], tables, wts["w_q"], l, j, bb=bb, tt=tt)
            if kv_past is None:
                x = _attention_out(q, x, mod, wts["g_norm"], wts["w_o"], kd, vd,
                                   wts["sinks"][j], l, j, tq=tq)
            else:
                o = _attention(q, kd, vd, wts["sinks"][j], B=B, T=T, tq=tq, chunked=False,
                               n_valid=kv_past[0].shape[1] + T)
                x = _oproj_layer(o, x, mod, wts["g_norm"], wts["w_o"], l, j, bb=bb, tt=tt)
        if mlp_bf16 is None:
            x, wu_l, wd_l = _mlp_layer(x, mod, wts["g_norm"], wts["w_up"], wts["w_down"], l,
                                       bb=bb, tt=tt, tff=tff)
            mlp_out.append((wu_l, wd_l))
        else:
            x = _mlp_layer_skewed(x, mod, wts["g_norm"], *mlp_bf16[l], l, tm=tt, tff=tff)
        if l == N_A_LAYERS - 1:
            k, v, kd, vd = _kv_layer(x, kvmod, wts["g_kv"], tables, wts["w_kv"],
                                     bb=bb, tt=tt)
            W = kd.shape[-1]
            kd = kd.reshape(B, T, W)
            vd = vd.reshape(B, T, W)
            if kv_past is not None:
                back = ((0, 0), (0, KEY_SPAN - kv_past[0].shape[1] - T), (0, 0))
                kd = jnp.pad(jnp.concatenate([_split_dup(kv_past[0]), kd], axis=1), back)
                vd = jnp.pad(jnp.concatenate([_split_dup(kv_past[1]), vd], axis=1), back)
    KV = N_KV_HEADS * HEAD_DIM
    return x, jnp.concatenate(new_pool, axis=0), k.reshape(B, T, KV), v.reshape(B, T, KV), mlp_out


def _prep_weights(g_norm, w_pool, pool_scale, g_kv, w_kv, w_q, sinks, w_o, w_up, w_down):
    D = D_MODEL
    return {
        "g_norm": g_norm,
        "w_pool": w_pool.astype(BF16),
        "pool_scale": pool_scale.reshape(N_A_LAYERS, 1, D),
        "g_kv": g_kv.reshape(1, 1, D),
        "w_kv": w_kv.astype(BF16),
        "w_q": w_q.astype(BF16),
        "sinks": sinks,
        "w_o": w_o.astype(BF16),
        "w_up": w_up,
        "w_down": w_down,
    }


def kernel(x_prompt, x_sample, c_prompt, c_sample, state_pool, cache_k, cache_v, w_mod, b_mod,
           g_norm, w_pool, pool_scale, w_kv_mod, b_kv_mod, g_kv, w_kv, w_q, sinks, w_o, w_up,
           w_down):
    Bp, Bs = x_prompt.shape[0], x_sample.shape[0]
    D = D_MODEL

    c_all = jnp.concatenate(
        [c_prompt, c_sample, jnp.zeros((MOD_ROWS - Bp - Bs, D), F32)], axis=0)
    mod_all = _modulation(c_all, w_mod, b_mod.reshape(DEPTH, 1, 6 * D)).reshape(
        DEPTH, MOD_ROWS, 6, D)
    kvmod_all = _modulation(c_all, w_kv_mod.reshape(1, D, 2 * D),
                            b_kv_mod.reshape(1, 1, 2 * D)).reshape(MOD_ROWS, 2, D)

    wts = _prep_weights(g_norm, w_pool, pool_scale, g_kv, w_kv, w_q, sinks, w_o, w_up, w_down)

    T_s = x_sample.shape[1]
    y_s, pool_s, k_s, v_s, mlp_bf16 = _trunk(
        x_sample, [mod_all[l, Bp:Bp + Bs] for l in range(DEPTH)], kvmod_all[Bp:Bp + Bs],
        PAST_LEN, state_pool, (cache_k, cache_v), wts, None, bb=Bs, tt=T_s, tff=512, tq=T_s)
    prompt_prefix = jnp.zeros((N_A_LAYERS, Bp, POOL_STATE, D), F32)
    y_p, pool_p, k_p, v_p, _ = _trunk(
        x_prompt, [mod_all[l, :Bp] for l in range(DEPTH)], kvmod_all[:Bp], 0, prompt_prefix,
        None, wts, mlp_bf16, bb=1, tt=512, tff=1024, tq=4 * WINDOW)
    keep = min(WINDOW, x_prompt.shape[1])
    heads = (N_KV_HEADS, HEAD_DIM)
    return (y_p, y_s, pool_p, pool_s,
            k_p[:, -keep:].reshape(Bp, keep, *heads), v_p[:, -keep:].reshape(Bp, keep, *heads),
            k_s.reshape(Bs, T_s, *heads), v_s.reshape(Bs, T_s, *heads))
```

```python
import functools

import jax
import jax.numpy as jnp
from jax import lax
from jax.experimental import pallas as pl
from jax.experimental.pallas import tpu as pltpu

F32 = jnp.float32
BF16 = jnp.bfloat16

D_MODEL = 2048
DEPTH = 4
PAST_LEN = 4096
CHUNK = 64
N_A_LAYERS = DEPTH // 2
POOL_WINDOWS = (2, 4, 8, 16)
POOL_GROUP = D_MODEL // len(POOL_WINDOWS)
POOL_STATE = max(POOL_WINDOWS) - 1
SUBLANES = 8
POOL_LEAD = SUBLANES
POOL_BASE = POOL_LEAD + POOL_STATE + 1
HEAD_DIM = 64
N_HEADS = D_MODEL // HEAD_DIM
N_KV_HEADS = N_HEADS // 8
WINDOW = 128
D_FF = 4 * D_MODEL
ROPE_THETA = 10000.0
EPS = 1e-6
ATTN_SCALE = HEAD_DIM ** -0.5
NEG_INF = -1e30

LANES = 128
KEY_SPAN = 2 * LANES
N_PAIRS = N_HEADS // 2
PAIRS_PER_KV = N_PAIRS // N_KV_HEADS
MLP_OUT_CHUNK = 512
MOD_ROWS = 32
VMEM_LIMIT = 56 * 1024 * 1024


def _params(*sem):
    return pltpu.CompilerParams(dimension_semantics=sem, vmem_limit_bytes=VMEM_LIMIT)


def _norm_mod(x, g, shift, scale):
    ms = jnp.mean(x * x, axis=-1, keepdims=True)
    return (x * lax.rsqrt(ms + EPS)) * (g * (1.0 + scale)) + shift


def _gated_norm(y, g, gate):
    ms = jnp.mean(y * y, axis=-1, keepdims=True)
    return (y * lax.rsqrt(ms + EPS)) * (g * gate)


def _mod_kernel(c_ref, w_ref, b_ref, o_ref):
    c = c_ref[...]
    sc = (c * jax.nn.sigmoid(c)).astype(BF16)
    o_ref[0] = jnp.dot(sc, w_ref[0].astype(BF16), preferred_element_type=F32) + b_ref[0]


def _modulation(c_all, w, b, tn=1024):
    L, D, N = w.shape
    return pl.pallas_call(
        _mod_kernel,
        out_shape=jax.ShapeDtypeStruct((L, MOD_ROWS, N), F32),
        grid=(L, N // tn),
        in_specs=[
            pl.BlockSpec((MOD_ROWS, D), lambda l, n: (0, 0)),
            pl.BlockSpec((1, D, tn), lambda l, n: (l, 0, n)),
            pl.BlockSpec((1, 1, tn), lambda l, n: (l, 0, n)),
        ],
        out_specs=pl.BlockSpec((1, MOD_ROWS, tn), lambda l, n: (l, 0, n)),
        compiler_params=_params("parallel", "parallel"),
        name="modulation",
    )(c_all, w, b)


def _pool_kernel(x_ref, mod_ref, g_ref, pre_ref, wp_ref, ps_ref, o_ref, np_ref, hbuf, s1, s2,
                 *, bb, tt, pos0):
    t = pl.program_id(1)
    G = POOL_GROUP
    LEAD, BASE = POOL_LEAD, POOL_BASE
    L = BASE + tt

    @pl.when(t == 0)
    def _():
        hbuf[:, 0:LEAD + 1, :] = jnp.zeros((bb, LEAD + 1, D_MODEL), F32)
        hbuf[:, LEAD + 1:BASE, :] = pre_ref[0]
        s1[:, 0:LEAD, :] = jnp.zeros((bb, LEAD, 3 * G), F32)
        s2[:, 0:LEAD, :] = jnp.zeros((bb, LEAD, 3 * G), F32)

    @pl.when(t > 0)
    def _():
        hbuf[:, LEAD:BASE, :] = hbuf[:, tt + LEAD:tt + BASE, :]

    x = x_ref[...]
    m = mod_ref[...]
    h = _norm_mod(x, g_ref[:, 0:1, :], m[:, 0:1, :], m[:, 1:2, :])
    hbuf[:, BASE:L, :] = h
    np_ref[0] = hbuf[:, L - POOL_STATE:L, :]

    s1[:, LEAD:L, :] = hbuf[:, LEAD:L, G:] + hbuf[:, LEAD - 1:L - 1, G:]
    s2[:, LEAD:L, :] = s1[:, LEAD:L, :] + s1[:, LEAD - 2:L - 2, :]
    s1[:, 2 * LEAD:L, 0:2 * G] = (s2[:, 2 * LEAD:L, G:] +
                                  s2[:, 2 * LEAD - 4:L - 4, G:])
    sums = [
        hbuf[:, BASE:L, 0:G] + hbuf[:, BASE - 1:L - 1, 0:G],
        s2[:, BASE:L, 0:G],
        s1[:, BASE:L, 0:G],
        s1[:, BASE:L, G:2 * G] + s1[:, BASE - 8:L - 8, G:2 * G],
    ]

    pos = pos0 + t * tt + lax.broadcasted_iota(jnp.int32, (1, tt, LANES), 1)
    ys = []
    for g, w in enumerate(POOL_WINDOWS):
        cs = slice(g * POOL_GROUP, (g + 1) * POOL_GROUP)
        inv_cnt = 1.0 / jnp.minimum(w, pos + 1).astype(F32)
        inv_cnt = jnp.concatenate([inv_cnt] * (POOL_GROUP // LANES), axis=-1)
        pooled = sums[g] * inv_cnt - hbuf[:, BASE:L, cs]
        ys.append(jnp.dot(pooled.reshape(bb * tt, POOL_GROUP).astype(BF16), wp_ref[0, g],
                          preferred_element_type=F32))
    y = jnp.concatenate(ys, axis=-1).reshape(bb, tt, D_MODEL) * ps_ref[...]
    o_ref[...] = x + _gated_norm(y, g_ref[:, 1:2, :], m[:, 2:3, :])


def _pool_layer(x, mod, g_norm, prefix, w_pool, pool_scale, l, *, bb, tt, pos0):
    B, T, D = x.shape
    kern = functools.partial(_pool_kernel, bb=bb, tt=tt, pos0=pos0)
    return pl.pallas_call(
        kern,
        out_shape=(jax.ShapeDtypeStruct((B, T, D), F32),
                   jax.ShapeDtypeStruct((1, B, POOL_STATE, D), F32)),
        grid=(B // bb, T // tt),
        in_specs=[
            pl.BlockSpec((bb, tt, D), lambda b, t: (b, t, 0)),
            pl.BlockSpec((bb, 6, D), lambda b, t: (b, 0, 0)),
            pl.BlockSpec((1, 4, D), lambda b, t: (l, 0, 0)),
            pl.BlockSpec((1, bb, POOL_STATE, D), lambda b, t: (l, b, 0, 0)),
            pl.BlockSpec((1, len(POOL_WINDOWS), POOL_GROUP, POOL_GROUP),
                         lambda b, t: (l, 0, 0, 0)),
            pl.BlockSpec((1, 1, D), lambda b, t: (l, 0, 0)),
        ],
        out_specs=(pl.BlockSpec((bb, tt, D), lambda b, t: (b, t, 0)),
                   pl.BlockSpec((1, bb, POOL_STATE, D), lambda b, t: (0, b, 0, 0))),
        scratch_shapes=[pltpu.VMEM((bb, POOL_BASE + tt, D), F32),
                        pltpu.VMEM((bb, POOL_BASE + tt, 3 * POOL_GROUP), F32),
                        pltpu.VMEM((bb, POOL_BASE + tt, 3 * POOL_GROUP), F32)],
        compiler_params=_params("parallel", "arbitrary"),
        name=f"pool_mixer_{l}",
    )(x, mod, g_norm, prefix, w_pool, pool_scale)


def _rope_tables(pos):
    half = HEAD_DIM // 2
    inv = ROPE_THETA ** (-jnp.arange(half, dtype=F32) / half)
    ang = pos.astype(F32)[:, None] * inv[None, :]
    cos, sin = jnp.cos(ang), jnp.sin(ang)
    zero = jnp.zeros_like(sin)
    c = jnp.tile(cos, (1, 4))
    s_lo = jnp.tile(jnp.concatenate([-sin, zero], axis=1), (1, 2))
    s_hi = jnp.tile(jnp.concatenate([zero, sin], axis=1), (1, 2))
    return c, s_lo, s_hi


def _rope_block(blk, c, s_lo, s_hi):
    return (blk * c + pltpu.roll(blk, LANES - HEAD_DIM // 2, 1) * s_lo
            + pltpu.roll(blk, HEAD_DIM // 2, 1) * s_hi)


def _qproj_kernel(x_ref, mod_ref, g_ref, c_ref, slo_ref, shi_ref, w_ref, q_ref, *, bb, tt):
    x = x_ref[...]
    m = mod_ref[...]
    h = _norm_mod(x, g_ref[:, 0:1, :], m[:, 0:1, :], m[:, 1:2, :])
    q = jnp.dot(h.reshape(bb * tt, D_MODEL).astype(BF16), w_ref[0], preferred_element_type=F32)
    c, s_lo, s_hi = c_ref[...], slo_ref[...], shi_ref[...]
    for p in range(N_PAIRS):
        cs = slice(p * LANES, (p + 1) * LANES)
        q_ref[p] = (_rope_block(q[:, cs], c, s_lo, s_hi) * ATTN_SCALE).astype(BF16)


def _qproj_layer(x, mod, g_norm, tables, w_q, l, j, *, bb, tt):
    B, T, D = x.shape
    nt = T // tt
    M = bb * tt
    kern = functools.partial(_qproj_kernel, bb=bb, tt=tt)
    tab_spec = pl.BlockSpec((M, LANES), lambda b, t: (t, 0))
    return pl.pallas_call(
        kern,
        out_shape=jax.ShapeDtypeStruct((N_PAIRS, B * T, LANES), BF16),
        grid=(B // bb, nt),
        in_specs=[
            pl.BlockSpec((bb, tt, D), lambda b, t: (b, t, 0)),
            pl.BlockSpec((bb, 6, D), lambda b, t: (b, 0, 0)),
            pl.BlockSpec((1, 4, D), lambda b, t: (l, 0, 0)),
            tab_spec, tab_spec, tab_spec,
            pl.BlockSpec((1, D, D), lambda b, t: (j, 0, 0)),
        ],
        out_specs=pl.BlockSpec((N_PAIRS, M, LANES), lambda b, t: (0, b * nt + t, 0)),
        compiler_params=_params("parallel", "parallel"),
        name=f"q_proj_{l}",
    )(x, mod, g_norm, *tables, w_q)


def _kv_kernel(x_ref, mod_ref, g_ref, c_ref, slo_ref, shi_ref, w_ref,
               k_ref, v_ref, kd_ref, vd_ref, *, bb, tt):
    x = x_ref[...]
    m = mod_ref[...]
    h = _norm_mod(x, g_ref[...], m[:, 0:1, :], m[:, 1:2, :])
    kv = jnp.dot(h.reshape(bb * tt, D_MODEL).astype(BF16), w_ref[...],
                 preferred_element_type=F32)
    c, s_lo, s_hi = c_ref[...], slo_ref[...], shi_ref[...]
    lo = lax.broadcasted_iota(jnp.int32, (bb * tt, LANES), 1) < HEAD_DIM
    zero = jnp.zeros((bb * tt, LANES), F32)
    n_blk = N_KV_HEADS // 2
    for p in range(n_blk):
        cs = slice(p * LANES, (p + 1) * LANES)
        k_blk = _rope_block(kv[:, cs], c, s_lo, s_hi)
        v_blk = kv[:, (n_blk + p) * LANES:(n_blk + p + 1) * LANES]
        k_ref[:, cs] = k_blk
        v_ref[:, cs] = v_blk
        for blk, dst in ((k_blk, kd_ref), (v_blk, vd_ref)):
            swapped = pltpu.roll(blk, HEAD_DIM, 1)
            parts = (jnp.where(lo, blk, zero), jnp.where(lo, zero, swapped),
                     jnp.where(lo, swapped, zero), jnp.where(lo, zero, blk))
            for i, part in enumerate(parts):
                dst[:, (4 * p + i) * LANES:(4 * p + i + 1) * LANES] = part.astype(BF16)


def _kv_layer(x, mod, g_kv, tables, w_kv, *, bb, tt):
    B, T, D = x.shape
    nt = T // tt
    M = bb * tt
    KV = N_KV_HEADS * HEAD_DIM
    kern = functools.partial(_kv_kernel, bb=bb, tt=tt)
    tab_spec = pl.BlockSpec((M, LANES), lambda b, t: (t, 0))
    row = lambda b, t: (b * nt + t, 0)
    return pl.pallas_call(
        kern,
        out_shape=(jax.ShapeDtypeStruct((B * T, KV), F32),
                   jax.ShapeDtypeStruct((B * T, KV), F32),
                   jax.ShapeDtypeStruct((B * T, 4 * KV), BF16),
                   jax.ShapeDtypeStruct((B * T, 4 * KV), BF16)),
        grid=(B // bb, nt),
        in_specs=[
            pl.BlockSpec((bb, tt, D), lambda b, t: (b, t, 0)),
            pl.BlockSpec((bb, 2, D), lambda b, t: (b, 0, 0)),
            pl.BlockSpec((1, 1, D), lambda b, t: (0, 0, 0)),
            tab_spec, tab_spec, tab_spec,
            pl.BlockSpec((D, 2 * KV), lambda b, t: (0, 0)),
        ],
        out_specs=(pl.BlockSpec((M, KV), row), pl.BlockSpec((M, KV), row),
                   pl.BlockSpec((M, 4 * KV), row), pl.BlockSpec((M, 4 * KV), row)),
        compiler_params=_params("parallel", "parallel"),
        name="shared_kv",
    )(x, mod, g_kv, *tables, w_kv)


def _sink_softmax(s, sink):
    m = jnp.max(s, axis=-1, keepdims=True)
    e = jnp.exp(s - m)
    den = jnp.sum(e, axis=-1, keepdims=True) + jnp.exp(sink - m)
    return e * (1.0 / den)


def _attn_kernel(sink_ref, *refs, tq, chunked, n_valid, fused):
    i = pl.program_id(1)
    G = PAIRS_PER_KV
    ts = WINDOW if chunked else tq
    nsub = tq // ts
    R = G * ts
    QW = D_MODEL // N_KV_HEADS
    kp = lax.broadcasted_iota(jnp.int32, (R, KEY_SPAN), 1)
    rblk = lax.broadcasted_iota(jnp.int32, (R, 1), 0) // ts
    nt = (((1,), (1,)), ((), ()))
    sink_cols = []
    for j in range(N_KV_HEADS):
        for parity in range(2):
            col = jnp.full((R, 1), sink_ref[2 * G * j + parity], F32)
            for r in range(1, G):
                col = jnp.where(rblk == r, sink_ref[2 * (G * j + r) + parity], col)
            sink_cols.append(col)
    if fused:
        q_ref, k_prev, k_cur, v_prev, v_cur, x_ref, mod_ref, g_ref, wo_ref, o_ref = refs
    elif chunked:
        q_ref, k_prev, k_cur, v_prev, v_cur, o_ref = refs
    else:
        q_ref, kd_ref, vd_ref, o_ref = refs
    if chunked:
        row = lax.broadcasted_iota(jnp.int32, (R, KEY_SPAN), 0)
        cq = (row % ts) // CHUNK
        kb = kp // CHUNK
        in_window = (kb >= cq) & (kb <= cq + WINDOW // CHUNK)

    def finish(rows, mix):
        m = mod_ref[0]
        o_ref[rows, :] = x_ref[rows, :] + _gated_norm(mix, g_ref[0, 1:2, :], m[2:3, :])

    pending = None
    for sub in range(nsub):
        rs = slice(sub * ts, (sub + 1) * ts)
        if not chunked:
            kw, vw = kd_ref[0], vd_ref[0]
            valid = kp < n_valid
        elif sub == 0:
            kw = jnp.concatenate([k_prev[0], k_cur[0, rs, :]], axis=0)
            vw = jnp.concatenate([v_prev[0], v_cur[0, rs, :]], axis=0)
            valid = in_window & (kp >= jnp.where(i > 0, 0, WINDOW))
        else:
            span = slice(sub * ts - WINDOW, (sub + 1) * ts)
            kw, vw = k_cur[0, span, :], v_cur[0, span, :]
            valid = in_window
        bias = jnp.where(valid, 0.0, NEG_INF)
        heads = []
        mix_parts = []
        for j in range(N_KV_HEADS):
            qg = q_ref[G * j:G * (j + 1), rs, :].reshape(R, LANES)
            k_lo = kw[:, (2 * j) * LANES:(2 * j + 1) * LANES]
            k_hi = kw[:, (2 * j + 1) * LANES:(2 * j + 2) * LANES]
            v_lo = vw[:, (2 * j) * LANES:(2 * j + 1) * LANES]
            v_hi = vw[:, (2 * j + 1) * LANES:(2 * j + 2) * LANES]
            s0 = lax.dot_general(qg, k_lo, nt, preferred_element_type=F32) + bias
            s1 = lax.dot_general(qg, k_hi, nt, preferred_element_type=F32) + bias
            if pending is not None:
                mix_parts.append(jnp.dot(pending[1], wo_ref[0, :, j * QW:(j + 1) * QW],
                                         preferred_element_type=F32))
            p0 = _sink_softmax(s0, sink_cols[2 * j])
            p1 = _sink_softmax(s1, sink_cols[2 * j + 1])
            o = (jnp.dot(p0.astype(BF16), v_lo, preferred_element_type=F32)
                 + jnp.dot(p1.astype(BF16), v_hi, preferred_element_type=F32)).astype(BF16)
            if fused:
                heads.extend(o[r * ts:(r + 1) * ts] for r in range(G))
            else:
                o_ref[G * j:G * (j + 1), rs, :] = o.reshape(G, ts, LANES)
        if fused:
            if pending is not None:
                finish(pending[0], jnp.concatenate(mix_parts, axis=1))
            pending = (rs, jnp.concatenate(heads, axis=1))
    if fused:
        finish(pending[0], jnp.dot(pending[1], wo_ref[0], preferred_element_type=F32))


def _attention(q, kd, vd, sinks, *, B, T, tq, chunked, n_valid):
    nq = T // tq
    Tk = kd.shape[1]
    W = kd.shape[2]
    kern = functools.partial(_attn_kernel, tq=tq, chunked=chunked, n_valid=n_valid, fused=False)
    if chunked:
        kv_specs, kv_args = _window_specs(kd, vd, T, tq)
    else:
        assert Tk == KEY_SPAN and nq == 1
        whole = pl.BlockSpec((1, Tk, W), lambda b, i: (b, 0, 0))
        kv_specs, kv_args = [whole, whole], (kd, vd)
    pair_rows = pl.BlockSpec((N_PAIRS, tq, LANES), lambda b, i: (0, b * nq + i, 0))
    return pl.pallas_call(
        kern,
        out_shape=jax.ShapeDtypeStruct((N_PAIRS, B * T, LANES), BF16),
        grid=(B, nq),
        in_specs=[pl.BlockSpec(memory_space=pltpu.SMEM), pair_rows, *kv_specs],
        out_specs=pair_rows,
        compiler_params=_params("parallel", "parallel"),
        name="swa_attention",
    )(sinks, q, *kv_args)


def _window_specs(kd, vd, T, tq):
    W = kd.shape[2]
    assert tq % WINDOW == 0 and kd.shape[1] == T
    nsub = tq // WINDOW
    prev = pl.BlockSpec((1, WINDOW, W), lambda b, i: (b, jnp.maximum(i * nsub - 1, 0), 0))
    cur = pl.BlockSpec((1, tq, W), lambda b, i: (b, i, 0))
    return [prev, cur, prev, cur], (kd, kd, vd, vd)


def _attention_out(q, x, mod, g_norm, w_o, kd, vd, sinks, l, j, *, tq):
    B, T, D = x.shape
    nq = T // tq
    kern = functools.partial(_attn_kernel, tq=tq, chunked=True, n_valid=None, fused=True)
    kv_specs, kv_args = _window_specs(kd, vd, T, tq)
    rows = lambda b, i: (b * nq + i, 0)
    out = pl.pallas_call(
        kern,
        out_shape=jax.ShapeDtypeStruct((B * T, D), F32),
        grid=(B, nq),
        in_specs=[
            pl.BlockSpec(memory_space=pltpu.SMEM),
            pl.BlockSpec((N_PAIRS, tq, LANES), lambda b, i: (0, b * nq + i, 0)),
            *kv_specs,
            pl.BlockSpec((tq, D), rows),
            pl.BlockSpec((1, 6, D), lambda b, i: (b, 0, 0)),
            pl.BlockSpec((1, 4, D), lambda b, i: (l, 0, 0)),
            pl.BlockSpec((1, D, D), lambda b, i: (j, 0, 0)),
        ],
        out_specs=pl.BlockSpec((tq, D), rows),
        compiler_params=_params("parallel", "parallel"),
        name=f"attention_out_{l}",
    )(sinks, q, *kv_args, x.reshape(B * T, D), mod, g_norm, w_o)
    return out.reshape(B, T, D)


def _oproj_kernel(o_ref, x_ref, mod_ref, g_ref, w_ref, y_ref, *, bb, tt):
    o = jnp.concatenate([o_ref[p] for p in range(N_PAIRS)], axis=1)
    mix = jnp.dot(o, w_ref[0], preferred_element_type=F32).reshape(bb, tt, D_MODEL)
    m = mod_ref[...]
    y_ref[...] = x_ref[...] + _gated_norm(mix, g_ref[:, 1:2, :], m[:, 2:3, :])


def _oproj_layer(o, x, mod, g_norm, w_o, l, j, *, bb, tt):
    B, T, D = x.shape
    nt = T // tt
    kern = functools.partial(_oproj_kernel, bb=bb, tt=tt)
    return pl.pallas_call(
        kern,
        out_shape=jax.ShapeDtypeStruct((B, T, D), F32),
        grid=(B // bb, nt),
        in_specs=[
            pl.BlockSpec((N_PAIRS, bb * tt, LANES), lambda b, t: (0, b * nt + t, 0)),
            pl.BlockSpec((bb, tt, D), lambda b, t: (b, t, 0)),
            pl.BlockSpec((bb, 6, D), lambda b, t: (b, 0, 0)),
            pl.BlockSpec((1, 4, D), lambda b, t: (l, 0, 0)),
            pl.BlockSpec((1, D, D), lambda b, t: (j, 0, 0)),
        ],
        out_specs=pl.BlockSpec((bb, tt, D), lambda b, t: (b, t, 0)),
        compiler_params=_params("parallel", "parallel"),
        name=f"o_proj_{l}",
    )(o, x, mod, g_norm, w_o)


def _mlp_kernel(x_ref, mod_ref, g_ref, wu_ref, wd_ref, y_ref, wu_out, wd_out, h_ref, *, bb, tt):
    f = pl.program_id(2)
    m = mod_ref[...]

    @pl.when(f == 0)
    def _():
        h = _norm_mod(x_ref[...], g_ref[:, 2:3, :], m[:, 3:4, :], m[:, 4:5, :])
        h_ref[...] = h.reshape(bb * tt, D_MODEL).astype(BF16)
        y_ref[...] = jnp.zeros_like(y_ref)

    wu = wu_ref[0].astype(BF16)
    wu_out[0] = wu
    u = jnp.dot(h_ref[...], wu, preferred_element_type=F32)
    a = jnp.square(jnp.maximum(u, 0.0)).astype(BF16)
    for n in range(D_MODEL // MLP_OUT_CHUNK):
        cs = slice(n * MLP_OUT_CHUNK, (n + 1) * MLP_OUT_CHUNK)
        wd_n = wd_ref[0, :, cs].astype(BF16)
        wd_out[0, :, cs] = wd_n
        y_ref[:, :, cs] += jnp.dot(a, wd_n, preferred_element_type=F32).reshape(
            bb, tt, MLP_OUT_CHUNK)

    @pl.when(f == pl.num_programs(2) - 1)
    def _():
        y_ref[...] = x_ref[...] + _gated_norm(y_ref[...], g_ref[:, 3:4, :], m[:, 5:6, :])


def _mlp_layer(x, mod, g_norm, w_up, w_down, l, *, bb, tt, tff):
    B, T, D = x.shape
    grid = (B // bb, T // tt, D_FF // tff)
    assert grid[0] * grid[1] == 1
    kern = functools.partial(_mlp_kernel, bb=bb, tt=tt)
    return pl.pallas_call(
        kern,
        out_shape=(jax.ShapeDtypeStruct((B, T, D), F32),
                   jax.ShapeDtypeStruct((1, D, D_FF), BF16),
                   jax.ShapeDtypeStruct((1, D_FF, D), BF16)),
        grid=grid,
        in_specs=[
            pl.BlockSpec((bb, tt, D), lambda b, t, f: (b, t, 0)),
            pl.BlockSpec((bb, 6, D), lambda b, t, f: (b, 0, 0)),
            pl.BlockSpec((1, 4, D), lambda b, t, f: (l, 0, 0)),
            pl.BlockSpec((1, D, tff), lambda b, t, f: (l, 0, f)),
            pl.BlockSpec((1, tff, D), lambda b, t, f: (l, f, 0)),
        ],
        out_specs=(pl.BlockSpec((bb, tt, D), lambda b, t, f: (b, t, 0)),
                   pl.BlockSpec((1, D, tff), lambda b, t, f: (0, 0, f)),
                   pl.BlockSpec((1, tff, D), lambda b, t, f: (0, f, 0))),
        scratch_shapes=[pltpu.VMEM((bb * tt, D), BF16)],
        compiler_params=_params("parallel", "parallel", "arbitrary"),
        name=f"mlp_{l}",
    )(x, mod, g_norm, w_up, w_down)


def _zero_after(x):
    bits = pltpu.bitcast(x, jnp.uint32)
    r = bits[:, 0:LANES]
    for k in range(1, x.shape[1] // LANES):
        r = r | bits[:, k * LANES:(k + 1) * LANES]
    r8 = r[0:SUBLANES]
    for k in range(1, x.shape[0] // SUBLANES):
        r8 = r8 | r[k * SUBLANES:(k + 1) * SUBLANES]
    z = pltpu.bitcast((r8 >> 16) >> 16, F32)
    return jnp.max(z, axis=(0, 1), keepdims=True)


def _mlp_skew_kernel(xn_ref, xp_ref, modn_ref, modp_ref, g_ref, wu_ref, wd_ref, y_ref,
                     h0_ref, h1_ref, acc0_ref, acc1_ref, *, nt, rs):
    s = pl.program_id(0)
    f = pl.program_id(1)
    rows = pl.ds(pl.multiple_of(f * rs, rs), rs)
    g_pre, g_post = g_ref[0, 2:3, :], g_ref[0, 3:4, :]
    h_refs = (h0_ref, h1_ref)
    acc_refs = (acc0_ref, acc1_ref)

    def prologue(h_ref):
        m = modn_ref[0]
        h = _norm_mod(xn_ref[...], g_pre, m[3:4, :], m[4:5, :])
        h_ref[rows, :] = h.astype(BF16)
        return h

    def epilogue(acc_ref):
        mp = modp_ref[0]
        y = xp_ref[...] + _gated_norm(acc_ref[rows, :], g_post, mp[5:6, :])
        y_ref[...] = y
        acc_ref[rows, :] = jnp.zeros((rs, D_MODEL), F32)
        return y

    def matmuls(h_ref, acc_ref, anchors):
        u = jnp.dot(h_ref[...], wu_ref[0], preferred_element_type=F32)
        a = jnp.square(jnp.maximum(u, 0.0)).astype(BF16)
        for n in range(D_MODEL // MLP_OUT_CHUNK):
            cs = slice(n * MLP_OUT_CHUNK, (n + 1) * MLP_OUT_CHUNK)
            d = jnp.dot(a, wd_ref[0, :, cs], preferred_element_type=F32)
            if anchors.get(n) is not None:
                d = d + anchors[n]
            acc_ref[:, cs] += d

    @pl.when((s == 0) & (f == 0))
    def _():
        acc0_ref[...] = jnp.zeros_like(acc0_ref)
        acc1_ref[...] = jnp.zeros_like(acc1_ref)

    @pl.when(s == 0)
    def _():
        prologue(h0_ref)

    @pl.when(s == 1)
    def _():
        h = prologue(h1_ref)
        matmuls(h0_ref, acc0_ref, {2: _zero_after(h)})

    for c in range(2):
        o = 1 - c

        @pl.when(((s - 1) % 2 == c) & (s >= 2) & (s < nt))
        def _():
            y = epilogue(acc_refs[o])
            h = prologue(h_refs[o])
            matmuls(h_refs[c], acc_refs[c], {0: _zero_after(y), 2: _zero_after(h)})

    last = (nt - 1) % 2

    @pl.when(s == nt)
    def _():
        y = epilogue(acc_refs[1 - last])
        matmuls(h_refs[last], acc_refs[last], {0: _zero_after(y)})

    @pl.when(s == nt + 1)
    def _():
        epilogue(acc_refs[last])


def _mlp_layer_skewed(x, mod, g_norm, w_up, w_down, l, *, tm, tff):
    B, T, D = x.shape
    nf = D_FF // tff
    rs = tm // nf
    tpb = T // tm
    nt = B * tpb
    assert nt >= 3
    x2 = x.reshape(B * T, D)
    nxt = lambda s: jnp.minimum(s, nt - 1)
    prv = lambda s: jnp.maximum(s - 2, 0)
    wf = lambda s, f: jnp.where(s == 0, 0, jnp.where(s == nt + 1, nf - 1, f))
    kern = functools.partial(_mlp_skew_kernel, nt=nt, rs=rs)
    y = pl.pallas_call(
        kern,
        out_shape=jax.ShapeDtypeStruct((B * T, D), F32),
        grid=(nt + 2, nf),
        in_specs=[
            pl.BlockSpec((rs, D), lambda s, f: (nxt(s) * nf + f, 0)),
            pl.BlockSpec((rs, D), lambda s, f: (prv(s) * nf + f, 0)),
            pl.BlockSpec((1, 6, D), lambda s, f: (nxt(s) // tpb, 0, 0)),
            pl.BlockSpec((1, 6, D), lambda s, f: (prv(s) // tpb, 0, 0)),
            pl.BlockSpec((1, 4, D), lambda s, f: (l, 0, 0)),
            pl.BlockSpec((1, D, tff), lambda s, f: (0, 0, wf(s, f))),
            pl.BlockSpec((1, tff, D), lambda s, f: (0, wf(s, f), 0)),
        ],
        out_specs=pl.BlockSpec((rs, D), lambda s, f: (jnp.where(s < 2, 0, (s - 2) * nf + f), 0)),
        scratch_shapes=[pltpu.VMEM((tm, D), BF16), pltpu.VMEM((tm, D), BF16),
                        pltpu.VMEM((tm, D), F32), pltpu.VMEM((tm, D), F32)],
        compiler_params=_params("arbitrary", "arbitrary"),
        name=f"mlp_skewed_{l}",
    )(x2, x2, mod, mod, g_norm, w_up, w_down)
    return y.reshape(B, T, D)


def _split_dup(a):
    z = jnp.zeros_like(a)
    lo = jnp.concatenate([a, z], axis=-1)
    hi = jnp.concatenate([z, a], axis=-1)
    out = jnp.stack([lo, hi], axis=3)
    return out.reshape(a.shape[0], a.shape[1], -1).astype(BF16)


def _trunk(x, mods, kvmod, pos0, prefix, kv_past, wts, mlp_bf16, *, bb, tt, tff, tq, tm=None):
    B, T, D = x.shape
    pos = pos0 + jnp.arange(T)
    tables = tuple(jnp.tile(tb, (bb, 1)) if bb > 1 else tb for tb in _rope_tables(pos))
    new_pool = []
    mlp_out = []
    k = v = kd = vd = None
    for l in range(DEPTH):
        mod = mods[l]
        if l < N_A_LAYERS:
            x, npool = _pool_layer(x, mod, wts["g_norm"], prefix, wts["w_pool"],
                                   wts["pool_scale"], l, bb=bb, tt=tt, pos0=pos0)
            new_pool.append(npool)
        else:
            j = l - N_A_LAYERS
            q = _qproj_layer(x, mod, wts["g_norm"], tables, wts["w_q"], l, j, bb=bb, tt=tt)
            if kv_past is None:
                x = _attention_out(q, x, mod, wts["g_norm"], wts["w_o"], kd, vd,
                                   wts["sinks"][j], l, j, tq=tq)
            else:
                o = _attention(q, kd, vd, wts["sinks"][j], B=B, T=T, tq=tq, chunked=False,
                               n_valid=kv_past[0].shape[1] + T)
                x = _oproj_layer(o, x, mod, wts["g_norm"], wts["w_o"], l, j, bb=bb, tt=tt)
        if mlp_bf16 is None:
            x, wu_l, wd_l = _mlp_layer(x, mod, wts["g_norm"], wts["w_up"], wts["w_down"], l,
                                       bb=bb, tt=tt, tff=tff)
            mlp_out.append((wu_l, wd_l))
        else:
            x = _mlp_layer_skewed(x, mod, wts["g_norm"], *mlp_bf16[l], l, tm=tm, tff=tff)
        if l == N_A_LAYERS - 1:
            k, v, kd, vd = _kv_layer(x, kvmod, wts["g_kv"], tables, wts["w_kv"],
                                     bb=bb, tt=tt)
            W = kd.shape[-1]
            kd = kd.reshape(B, T, W)
            vd = vd.reshape(B, T, W)
            if kv_past is not None:
                back = ((0, 0), (0, KEY_SPAN - kv_past[0].shape[1] - T), (0, 0))
                kd = jnp.pad(jnp.concatenate([_split_dup(kv_past[0]), kd], axis=1), back)
                vd = jnp.pad(jnp.concatenate([_split_dup(kv_past[1]), vd], axis=1), back)
    KV = N_KV_HEADS * HEAD_DIM
    return x, jnp.concatenate(new_pool, axis=0), k.reshape(B, T, KV), v.reshape(B, T, KV), mlp_out


def _prep_weights(g_norm, w_pool, pool_scale, g_kv, w_kv, w_q, sinks, w_o, w_up, w_down):
    D = D_MODEL
    return {
        "g_norm": g_norm,
        "w_pool": w_pool.astype(BF16),
        "pool_scale": pool_scale.reshape(N_A_LAYERS, 1, D),
        "g_kv": g_kv.reshape(1, 1, D),
        "w_kv": w_kv.astype(BF16),
        "w_q": w_q.astype(BF16),
        "sinks": sinks,
        "w_o": w_o.astype(BF16),
        "w_up": w_up,
        "w_down": w_down,
    }


def kernel(x_prompt, x_sample, c_prompt, c_sample, state_pool, cache_k, cache_v, w_mod, b_mod,
           g_norm, w_pool, pool_scale, w_kv_mod, b_kv_mod, g_kv, w_kv, w_q, sinks, w_o, w_up,
           w_down):
    Bp, Bs = x_prompt.shape[0], x_sample.shape[0]
    D = D_MODEL

    c_all = jnp.concatenate(
        [c_prompt, c_sample, jnp.zeros((MOD_ROWS - Bp - Bs, D), F32)], axis=0)
    mod_all = _modulation(c_all, w_mod, b_mod.reshape(DEPTH, 1, 6 * D)).reshape(
        DEPTH, MOD_ROWS, 6, D)
    kvmod_all = _modulation(c_all, w_kv_mod.reshape(1, D, 2 * D),
                            b_kv_mod.reshape(1, 1, 2 * D)).reshape(MOD_ROWS, 2, D)

    wts = _prep_weights(g_norm, w_pool, pool_scale, g_kv, w_kv, w_q, sinks, w_o, w_up, w_down)

    T_s = x_sample.shape[1]
    y_s, pool_s, k_s, v_s, mlp_bf16 = _trunk(
        x_sample, [mod_all[l, Bp:Bp + Bs] for l in range(DEPTH)], kvmod_all[Bp:Bp + Bs],
        PAST_LEN, state_pool, (cache_k, cache_v), wts, None, bb=Bs, tt=T_s, tff=512, tq=T_s)
    prompt_prefix = jnp.zeros((N_A_LAYERS, Bp, POOL_STATE, D), F32)
    y_p, pool_p, k_p, v_p, _ = _trunk(
        x_prompt, [mod_all[l, :Bp] for l in range(DEPTH)], kvmod_all[:Bp], 0, prompt_prefix,
        None, wts, mlp_bf16, bb=1, tt=512, tff=1024, tq=4 * WINDOW, tm=1024)
    keep = min(WINDOW, x_prompt.shape[1])
    heads = (N_KV_HEADS, HEAD_DIM)
    return (y_p, y_s, pool_p, pool_s,
            k_p[:, -keep:].reshape(Bp, keep, *heads), v_p[:, -keep:].reshape(Bp, keep, *heads),
            k_s.reshape(Bs, T_s, *heads), v_s.reshape(Bs, T_s, *heads))
```

```python
import functools

import jax
import jax.numpy as jnp
from jax import lax
from jax.experimental import pallas as pl
from jax.experimental.pallas import tpu as pltpu

F32 = jnp.float32
BF16 = jnp.bfloat16

D_MODEL = 2048
DEPTH = 4
PAST_LEN = 4096
CHUNK = 64
N_A_LAYERS = DEPTH // 2
POOL_WINDOWS = (2, 4, 8, 16)
POOL_GROUP = D_MODEL // len(POOL_WINDOWS)
POOL_STATE = max(POOL_WINDOWS) - 1
SUBLANES = 8
POOL_LEAD = SUBLANES
POOL_BASE = POOL_LEAD + POOL_STATE + 1
HEAD_DIM = 64
N_HEADS = D_MODEL // HEAD_DIM
N_KV_HEADS = N_HEADS // 8
WINDOW = 128
D_FF = 4 * D_MODEL
ROPE_THETA = 10000.0
EPS = 1e-6
ATTN_SCALE = HEAD_DIM ** -0.5
NEG_INF = -1e30

LANES = 128
KEY_SPAN = 2 * LANES
N_PAIRS = N_HEADS // 2
PAIRS_PER_KV = N_PAIRS // N_KV_HEADS
MLP_OUT_CHUNK = 512
MOD_ROWS = 32
VMEM_LIMIT = 56 * 1024 * 1024


def _params(*sem):
    return pltpu.CompilerParams(dimension_semantics=sem, vmem_limit_bytes=VMEM_LIMIT)


def _norm_mod(x, g, shift, scale):
    ms = jnp.mean(x * x, axis=-1, keepdims=True)
    return (x * lax.rsqrt(ms + EPS)) * (g * (1.0 + scale)) + shift


def _gated_norm(y, g, gate):
    ms = jnp.mean(y * y, axis=-1, keepdims=True)
    return (y * lax.rsqrt(ms + EPS)) * (g * gate)


def _mod_kernel(c_ref, w_ref, b_ref, o_ref):
    c = c_ref[...]
    sc = (c * jax.nn.sigmoid(c)).astype(BF16)
    o_ref[0] = jnp.dot(sc, w_ref[0].astype(BF16), preferred_element_type=F32) + b_ref[0]


def _modulation(c_all, w, b, tn=1024):
    L, D, N = w.shape
    return pl.pallas_call(
        _mod_kernel,
        out_shape=jax.ShapeDtypeStruct((L, MOD_ROWS, N), F32),
        grid=(L, N // tn),
        in_specs=[
            pl.BlockSpec((MOD_ROWS, D), lambda l, n: (0, 0)),
            pl.BlockSpec((1, D, tn), lambda l, n: (l, 0, n)),
            pl.BlockSpec((1, 1, tn), lambda l, n: (l, 0, n)),
        ],
        out_specs=pl.BlockSpec((1, MOD_ROWS, tn), lambda l, n: (l, 0, n)),
        compiler_params=_params("parallel", "parallel"),
        name="modulation",
    )(c_all, w, b)


def _pool_kernel(x_ref, mod_ref, g_ref, pre_ref, wp_ref, ps_ref, o_ref, np_ref, hbuf, s1, s2,
                 *, bb, tt, pos0):
    t = pl.program_id(1)
    G = POOL_GROUP
    LEAD, BASE = POOL_LEAD, POOL_BASE
    L = BASE + tt

    @pl.when(t == 0)
    def _():
        hbuf[:, 0:LEAD + 1, :] = jnp.zeros((bb, LEAD + 1, D_MODEL), F32)
        hbuf[:, LEAD + 1:BASE, :] = pre_ref[0]
        s1[:, 0:LEAD, :] = jnp.zeros((bb, LEAD, 3 * G), F32)
        s2[:, 0:LEAD, :] = jnp.zeros((bb, LEAD, 3 * G), F32)

    @pl.when(t > 0)
    def _():
        hbuf[:, LEAD:BASE, :] = hbuf[:, tt + LEAD:tt + BASE, :]

    x = x_ref[...]
    m = mod_ref[...]
    h = _norm_mod(x, g_ref[:, 0:1, :], m[:, 0:1, :], m[:, 1:2, :])
    hbuf[:, BASE:L, :] = h
    np_ref[0] = hbuf[:, L - POOL_STATE:L, :]

    s1[:, LEAD:L, :] = hbuf[:, LEAD:L, G:] + hbuf[:, LEAD - 1:L - 1, G:]
    s2[:, LEAD:L, :] = s1[:, LEAD:L, :] + s1[:, LEAD - 2:L - 2, :]
    s1[:, 2 * LEAD:L, 0:2 * G] = (s2[:, 2 * LEAD:L, G:] +
                                  s2[:, 2 * LEAD - 4:L - 4, G:])
    sums = [
        hbuf[:, BASE:L, 0:G] + hbuf[:, BASE - 1:L - 1, 0:G],
        s2[:, BASE:L, 0:G],
        s1[:, BASE:L, 0:G],
        s1[:, BASE:L, G:2 * G] + s1[:, BASE - 8:L - 8, G:2 * G],
    ]

    pos = pos0 + t * tt + lax.broadcasted_iota(jnp.int32, (1, tt, LANES), 1)
    ys = []
    for g, w in enumerate(POOL_WINDOWS):
        cs = slice(g * POOL_GROUP, (g + 1) * POOL_GROUP)
        inv_cnt = 1.0 / jnp.minimum(w, pos + 1).astype(F32)
        inv_cnt = jnp.concatenate([inv_cnt] * (POOL_GROUP // LANES), axis=-1)
        pooled = sums[g] * inv_cnt - hbuf[:, BASE:L, cs]
        ys.append(jnp.dot(pooled.reshape(bb * tt, POOL_GROUP).astype(BF16), wp_ref[0, g],
                          preferred_element_type=F32))
    y = jnp.concatenate(ys, axis=-1).reshape(bb, tt, D_MODEL) * ps_ref[...]
    o_ref[...] = x + _gated_norm(y, g_ref[:, 1:2, :], m[:, 2:3, :])


def _pool_layer(x, mod, g_norm, prefix, w_pool, pool_scale, l, *, bb, tt, pos0):
    B, T, D = x.shape
    kern = functools.partial(_pool_kernel, bb=bb, tt=tt, pos0=pos0)
    return pl.pallas_call(
        kern,
        out_shape=(jax.ShapeDtypeStruct((B, T, D), F32),
                   jax.ShapeDtypeStruct((1, B, POOL_STATE, D), F32)),
        grid=(B // bb, T // tt),
        in_specs=[
            pl.BlockSpec((bb, tt, D), lambda b, t: (b, t, 0)),
            pl.BlockSpec((bb, 6, D), lambda b, t: (b, 0, 0)),
            pl.BlockSpec((1, 4, D), lambda b, t: (l, 0, 0)),
            pl.BlockSpec((1, bb, POOL_STATE, D), lambda b, t: (l, b, 0, 0)),
            pl.BlockSpec((1, len(POOL_WINDOWS), POOL_GROUP, POOL_GROUP),
                         lambda b, t: (l, 0, 0, 0)),
            pl.BlockSpec((1, 1, D), lambda b, t: (l, 0, 0)),
        ],
        out_specs=(pl.BlockSpec((bb, tt, D), lambda b, t: (b, t, 0)),
                   pl.BlockSpec((1, bb, POOL_STATE, D), lambda b, t: (0, b, 0, 0))),
        scratch_shapes=[pltpu.VMEM((bb, POOL_BASE + tt, D), F32),
                        pltpu.VMEM((bb, POOL_BASE + tt, 3 * POOL_GROUP), F32),
                        pltpu.VMEM((bb, POOL_BASE + tt, 3 * POOL_GROUP), F32)],
        compiler_params=_params("parallel", "arbitrary"),
        name=f"pool_mixer_{l}",
    )(x, mod, g_norm, prefix, w_pool, pool_scale)


def _rope_tables(pos):
    half = HEAD_DIM // 2
    inv = ROPE_THETA ** (-jnp.arange(half, dtype=F32) / half)
    ang = pos.astype(F32)[:, None] * inv[None, :]
    cos, sin = jnp.cos(ang), jnp.sin(ang)
    zero = jnp.zeros_like(sin)
    c = jnp.tile(cos, (1, 4))
    s_lo = jnp.tile(jnp.concatenate([-sin, zero], axis=1), (1, 2))
    s_hi = jnp.tile(jnp.concatenate([zero, sin], axis=1), (1, 2))
    return c, s_lo, s_hi


def _rope_block(blk, c, s_lo, s_hi):
    return (blk * c + pltpu.roll(blk, LANES - HEAD_DIM // 2, 1) * s_lo
            + pltpu.roll(blk, HEAD_DIM // 2, 1) * s_hi)


def _qproj_kernel(x_ref, mod_ref, g_ref, c_ref, slo_ref, shi_ref, w_ref, q_ref, *, bb, tt):
    x = x_ref[...]
    m = mod_ref[...]
    h = _norm_mod(x, g_ref[:, 0:1, :], m[:, 0:1, :], m[:, 1:2, :])
    q = jnp.dot(h.reshape(bb * tt, D_MODEL).astype(BF16), w_ref[0], preferred_element_type=F32)
    c, s_lo, s_hi = c_ref[...], slo_ref[...], shi_ref[...]
    for p in range(N_PAIRS):
        cs = slice(p * LANES, (p + 1) * LANES)
        q_ref[p] = (_rope_block(q[:, cs], c, s_lo, s_hi) * ATTN_SCALE).astype(BF16)


def _qproj_layer(x, mod, g_norm, tables, w_q, l, j, *, bb, tt):
    B, T, D = x.shape
    nt = T // tt
    M = bb * tt
    kern = functools.partial(_qproj_kernel, bb=bb, tt=tt)
    tab_spec = pl.BlockSpec((M, LANES), lambda b, t: (t, 0))
    return pl.pallas_call(
        kern,
        out_shape=jax.ShapeDtypeStruct((N_PAIRS, B * T, LANES), BF16),
        grid=(B // bb, nt),
        in_specs=[
            pl.BlockSpec((bb, tt, D), lambda b, t: (b, t, 0)),
            pl.BlockSpec((bb, 6, D), lambda b, t: (b, 0, 0)),
            pl.BlockSpec((1, 4, D), lambda b, t: (l, 0, 0)),
            tab_spec, tab_spec, tab_spec,
            pl.BlockSpec((1, D, D), lambda b, t: (j, 0, 0)),
        ],
        out_specs=pl.BlockSpec((N_PAIRS, M, LANES), lambda b, t: (0, b * nt + t, 0)),
        compiler_params=_params("parallel", "parallel"),
        name=f"q_proj_{l}",
    )(x, mod, g_norm, *tables, w_q)


def _kv_kernel(x_ref, mod_ref, g_ref, c_ref, slo_ref, shi_ref, w_ref,
               k_ref, v_ref, kd_ref, vd_ref, *, bb, tt):
    x = x_ref[...]
    m = mod_ref[...]
    h = _norm_mod(x, g_ref[...], m[:, 0:1, :], m[:, 1:2, :])
    kv = jnp.dot(h.reshape(bb * tt, D_MODEL).astype(BF16), w_ref[...],
                 preferred_element_type=F32)
    c, s_lo, s_hi = c_ref[...], slo_ref[...], shi_ref[...]
    lo = lax.broadcasted_iota(jnp.int32, (bb * tt, LANES), 1) < HEAD_DIM
    zero = jnp.zeros((bb * tt, LANES), F32)
    n_blk = N_KV_HEADS // 2
    for p in range(n_blk):
        cs = slice(p * LANES, (p + 1) * LANES)
        k_blk = _rope_block(kv[:, cs], c, s_lo, s_hi)
        v_blk = kv[:, (n_blk + p) * LANES:(n_blk + p + 1) * LANES]
        k_ref[:, cs] = k_blk
        v_ref[:, cs] = v_blk
        for blk, dst in ((k_blk, kd_ref), (v_blk, vd_ref)):
            swapped = pltpu.roll(blk, HEAD_DIM, 1)
            parts = (jnp.where(lo, blk, zero), jnp.where(lo, zero, swapped),
                     jnp.where(lo, swapped, zero), jnp.where(lo, zero, blk))
            for i, part in enumerate(parts):
                dst[:, (4 * p + i) * LANES:(4 * p + i + 1) * LANES] = part.astype(BF16)


def _kv_layer(x, mod, g_kv, tables, w_kv, *, bb, tt):
    B, T, D = x.shape
    nt = T // tt
    M = bb * tt
    KV = N_KV_HEADS * HEAD_DIM
    kern = functools.partial(_kv_kernel, bb=bb, tt=tt)
    tab_spec = pl.BlockSpec((M, LANES), lambda b, t: (t, 0))
    row = lambda b, t: (b * nt + t, 0)
    return pl.pallas_call(
        kern,
        out_shape=(jax.ShapeDtypeStruct((B * T, KV), F32),
                   jax.ShapeDtypeStruct((B * T, KV), F32),
                   jax.ShapeDtypeStruct((B * T, 4 * KV), BF16),
                   jax.ShapeDtypeStruct((B * T, 4 * KV), BF16)),
        grid=(B // bb, nt),
        in_specs=[
            pl.BlockSpec((bb, tt, D), lambda b, t: (b, t, 0)),
            pl.BlockSpec((bb, 2, D), lambda b, t: (b, 0, 0)),
            pl.BlockSpec((1, 1, D), lambda b, t: (0, 0, 0)),
            tab_spec, tab_spec, tab_spec,
            pl.BlockSpec((D, 2 * KV), lambda b, t: (0, 0)),
        ],
        out_specs=(pl.BlockSpec((M, KV), row), pl.BlockSpec((M, KV), row),
                   pl.BlockSpec((M, 4 * KV), row), pl.BlockSpec((M, 4 * KV), row)),
        compiler_params=_params("parallel", "parallel"),
        name="shared_kv",
    )(x, mod, g_kv, *tables, w_kv)


def _sink_softmax(s, sink):
    m = jnp.max(s, axis=-1, keepdims=True)
    e = jnp.exp(s - m)
    den = jnp.sum(e, axis=-1, keepdims=True) + jnp.exp(sink - m)
    return e * (1.0 / den)


def _attn_kernel(sink_ref, *refs, tq, chunked, n_valid, fused):
    i = pl.program_id(1)
    G = PAIRS_PER_KV
    ts = WINDOW if chunked else tq
    nsub = tq // ts
    R = G * ts
    QW = D_MODEL // N_KV_HEADS
    kp = lax.broadcasted_iota(jnp.int32, (R, KEY_SPAN), 1)
    rblk = lax.broadcasted_iota(jnp.int32, (R, 1), 0) // ts
    nt = (((1,), (1,)), ((), ()))
    sink_cols = []
    for j in range(N_KV_HEADS):
        for parity in range(2):
            col = jnp.full((R, 1), sink_ref[2 * G * j + parity], F32)
            for r in range(1, G):
                col = jnp.where(rblk == r, sink_ref[2 * (G * j + r) + parity], col)
            sink_cols.append(col)
    if fused:
        q_ref, k_prev, k_cur, v_prev, v_cur, x_ref, mod_ref, g_ref, wo_ref, o_ref = refs
    elif chunked:
        q_ref, k_prev, k_cur, v_prev, v_cur, o_ref = refs
    else:
        q_ref, kd_ref, vd_ref, o_ref = refs
    if chunked:
        row = lax.broadcasted_iota(jnp.int32, (R, KEY_SPAN), 0)
        cq = (row % ts) // CHUNK
        kb = kp // CHUNK
        in_window = (kb >= cq) & (kb <= cq + WINDOW // CHUNK)

    def finish(rows, mix):
        m = mod_ref[0]
        o_ref[rows, :] = x_ref[rows, :] + _gated_norm(mix, g_ref[0, 1:2, :], m[2:3, :])

    pending = None
    for sub in range(nsub):
        rs = slice(sub * ts, (sub + 1) * ts)
        if not chunked:
            kw, vw = kd_ref[0], vd_ref[0]
            valid = kp < n_valid
        elif sub == 0:
            kw = jnp.concatenate([k_prev[0], k_cur[0, rs, :]], axis=0)
            vw = jnp.concatenate([v_prev[0], v_cur[0, rs, :]], axis=0)
            valid = in_window & (kp >= jnp.where(i > 0, 0, WINDOW))
        else:
            span = slice(sub * ts - WINDOW, (sub + 1) * ts)
            kw, vw = k_cur[0, span, :], v_cur[0, span, :]
            valid = in_window
        bias = jnp.where(valid, 0.0, NEG_INF)
        heads = []
        mix_parts = []
        for j in range(N_KV_HEADS):
            qg = q_ref[G * j:G * (j + 1), rs, :].reshape(R, LANES)
            k_lo = kw[:, (2 * j) * LANES:(2 * j + 1) * LANES]
            k_hi = kw[:, (2 * j + 1) * LANES:(2 * j + 2) * LANES]
            v_lo = vw[:, (2 * j) * LANES:(2 * j + 1) * LANES]
            v_hi = vw[:, (2 * j + 1) * LANES:(2 * j + 2) * LANES]
            s0 = lax.dot_general(qg, k_lo, nt, preferred_element_type=F32) + bias
            s1 = lax.dot_general(qg, k_hi, nt, preferred_element_type=F32) + bias
            if pending is not None:
                mix_parts.append(jnp.dot(pending[1], wo_ref[0, :, j * QW:(j + 1) * QW],
                                         preferred_element_type=F32))
            p0 = _sink_softmax(s0, sink_cols[2 * j])
            p1 = _sink_softmax(s1, sink_cols[2 * j + 1])
            o = (jnp.dot(p0.astype(BF16), v_lo, preferred_element_type=F32)
                 + jnp.dot(p1.astype(BF16), v_hi, preferred_element_type=F32)).astype(BF16)
            if fused:
                heads.extend(o[r * ts:(r + 1) * ts] for r in range(G))
            else:
                o_ref[G * j:G * (j + 1), rs, :] = o.reshape(G, ts, LANES)
        if fused:
            if pending is not None:
                finish(pending[0], jnp.concatenate(mix_parts, axis=1))
            pending = (rs, jnp.concatenate(heads, axis=1))
    if fused:
        finish(pending[0], jnp.dot(pending[1], wo_ref[0], preferred_element_type=F32))


def _attention(q, kd, vd, sinks, *, B, T, tq, chunked, n_valid):
    nq = T // tq
    Tk = kd.shape[1]
    W = kd.shape[2]
    kern = functools.partial(_attn_kernel, tq=tq, chunked=chunked, n_valid=n_valid, fused=False)
    if chunked:
        kv_specs, kv_args = _window_specs(kd, vd, T, tq)
    else:
        assert Tk == KEY_SPAN and nq == 1
        whole = pl.BlockSpec((1, Tk, W), lambda b, i: (b, 0, 0))
        kv_specs, kv_args = [whole, whole], (kd, vd)
    pair_rows = pl.BlockSpec((N_PAIRS, tq, LANES), lambda b, i: (0, b * nq + i, 0))
    return pl.pallas_call(
        kern,
        out_shape=jax.ShapeDtypeStruct((N_PAIRS, B * T, LANES), BF16),
        grid=(B, nq),
        in_specs=[pl.BlockSpec(memory_space=pltpu.SMEM), pair_rows, *kv_specs],
        out_specs=pair_rows,
        compiler_params=_params("parallel", "parallel"),
        name="swa_attention",
    )(sinks, q, *kv_args)


def _window_specs(kd, vd, T, tq):
    W = kd.shape[2]
    assert tq % WINDOW == 0 and kd.shape[1] == T
    nsub = tq // WINDOW
    prev = pl.BlockSpec((1, WINDOW, W), lambda b, i: (b, jnp.maximum(i * nsub - 1, 0), 0))
    cur = pl.BlockSpec((1, tq, W), lambda b, i: (b, i, 0))
    return [prev, cur, prev, cur], (kd, kd, vd, vd)


def _attention_out(q, x, mod, g_norm, w_o, kd, vd, sinks, l, j, *, tq):
    B, T, D = x.shape
    nq = T // tq
    kern = functools.partial(_attn_kernel, tq=tq, chunked=True, n_valid=None, fused=True)
    kv_specs, kv_args = _window_specs(kd, vd, T, tq)
    rows = lambda b, i: (b * nq + i, 0)
    out = pl.pallas_call(
        kern,
        out_shape=jax.ShapeDtypeStruct((B * T, D), F32),
        grid=(B, nq),
        in_specs=[
            pl.BlockSpec(memory_space=pltpu.SMEM),
            pl.BlockSpec((N_PAIRS, tq, LANES), lambda b, i: (0, b * nq + i, 0)),
            *kv_specs,
            pl.BlockSpec((tq, D), rows),
            pl.BlockSpec((1, 6, D), lambda b, i: (b, 0, 0)),
            pl.BlockSpec((1, 4, D), lambda b, i: (l, 0, 0)),
            pl.BlockSpec((1, D, D), lambda b, i: (j, 0, 0)),
        ],
        out_specs=pl.BlockSpec((tq, D), rows),
        compiler_params=_params("parallel", "parallel"),
        name=f"attention_out_{l}",
    )(sinks, q, *kv_args, x.reshape(B * T, D), mod, g_norm, w_o)
    return out.reshape(B, T, D)


def _oproj_kernel(o_ref, x_ref, mod_ref, g_ref, w_ref, y_ref, *, bb, tt):
    o = jnp.concatenate([o_ref[p] for p in range(N_PAIRS)], axis=1)
    mix = jnp.dot(o, w_ref[0], preferred_element_type=F32).reshape(bb, tt, D_MODEL)
    m = mod_ref[...]
    y_ref[...] = x_ref[...] + _gated_norm(mix, g_ref[:, 1:2, :], m[:, 2:3, :])


def _oproj_layer(o, x, mod, g_norm, w_o, l, j, *, bb, tt):
    B, T, D = x.shape
    nt = T // tt
    kern = functools.partial(_oproj_kernel, bb=bb, tt=tt)
    return pl.pallas_call(
        kern,
        out_shape=jax.ShapeDtypeStruct((B, T, D), F32),
        grid=(B // bb, nt),
        in_specs=[
            pl.BlockSpec((N_PAIRS, bb * tt, LANES), lambda b, t: (0, b * nt + t, 0)),
            pl.BlockSpec((bb, tt, D), lambda b, t: (b, t, 0)),
            pl.BlockSpec((bb, 6, D), lambda b, t: (b, 0, 0)),
            pl.BlockSpec((1, 4, D), lambda b, t: (l, 0, 0)),
            pl.BlockSpec((1, D, D), lambda b, t: (j, 0, 0)),
        ],
        out_specs=pl.BlockSpec((bb, tt, D), lambda b, t: (b, t, 0)),
        compiler_params=_params("parallel", "parallel"),
        name=f"o_proj_{l}",
    )(o, x, mod, g_norm, w_o)


def _mlp_kernel(x_ref, mod_ref, g_ref, wu_ref, wd_ref, y_ref, *rest, bb, tt, cast):
    f = pl.program_id(2)
    m = mod_ref[...]
    if cast:
        wu_out, wd_out, h_ref = rest
    else:
        h_ref, = rest

    @pl.when(f == 0)
    def _():
        h = _norm_mod(x_ref[...], g_ref[:, 2:3, :], m[:, 3:4, :], m[:, 4:5, :])
        h_ref[...] = h.reshape(bb * tt, D_MODEL).astype(BF16)
        y_ref[...] = jnp.zeros_like(y_ref)

    wu = wu_ref[0]
    if cast:
        wu = wu.astype(BF16)
        wu_out[0] = wu
    u = jnp.dot(h_ref[...], wu, preferred_element_type=F32)
    a = jnp.square(jnp.maximum(u, 0.0)).astype(BF16)
    for n in range(D_MODEL // MLP_OUT_CHUNK):
        cs = slice(n * MLP_OUT_CHUNK, (n + 1) * MLP_OUT_CHUNK)
        wd_n = wd_ref[0, :, cs]
        if cast:
            wd_n = wd_n.astype(BF16)
            wd_out[0, :, cs] = wd_n
        y_ref[:, :, cs] += jnp.dot(a, wd_n, preferred_element_type=F32).reshape(
            bb, tt, MLP_OUT_CHUNK)

    @pl.when(f == pl.num_programs(2) - 1)
    def _():
        y_ref[...] = x_ref[...] + _gated_norm(y_ref[...], g_ref[:, 3:4, :], m[:, 5:6, :])


def _mlp_layer(x, mod, g_norm, w_up, w_down, l, *, bb, tt, tff, cast):
    B, T, D = x.shape
    grid = (B // bb, T // tt, D_FF // tff)
    kern = functools.partial(_mlp_kernel, bb=bb, tt=tt, cast=cast)
    wl = l if cast else 0
    y_shape = jax.ShapeDtypeStruct((B, T, D), F32)
    y_spec = pl.BlockSpec((bb, tt, D), lambda b, t, f: (b, t, 0))
    if cast:
        assert grid[0] * grid[1] == 1
        out_shape = (y_shape, jax.ShapeDtypeStruct((1, D, D_FF), BF16),
                     jax.ShapeDtypeStruct((1, D_FF, D), BF16))
        out_specs = (y_spec, pl.BlockSpec((1, D, tff), lambda b, t, f: (0, 0, f)),
                     pl.BlockSpec((1, tff, D), lambda b, t, f: (0, f, 0)))
    else:
        out_shape, out_specs = y_shape, y_spec
    return pl.pallas_call(
        kern,
        out_shape=out_shape,
        grid=grid,
        in_specs=[
            pl.BlockSpec((bb, tt, D), lambda b, t, f: (b, t, 0)),
            pl.BlockSpec((bb, 6, D), lambda b, t, f: (b, 0, 0)),
            pl.BlockSpec((1, 4, D), lambda b, t, f: (l, 0, 0)),
            pl.BlockSpec((1, D, tff), lambda b, t, f: (wl, 0, f)),
            pl.BlockSpec((1, tff, D), lambda b, t, f: (wl, f, 0)),
        ],
        out_specs=out_specs,
        scratch_shapes=[pltpu.VMEM((bb * tt, D), BF16)],
        compiler_params=_params("parallel", "parallel", "arbitrary"),
        name=f"mlp_{l}",
    )(x, mod, g_norm, w_up, w_down)


def _zero_after(x):
    bits = pltpu.bitcast(x, jnp.uint32)
    r = bits[:, 0:LANES]
    for k in range(1, x.shape[1] // LANES):
        r = r | bits[:, k * LANES:(k + 1) * LANES]
    r8 = r[0:SUBLANES]
    for k in range(1, x.shape[0] // SUBLANES):
        r8 = r8 | r[k * SUBLANES:(k + 1) * SUBLANES]
    z = pltpu.bitcast((r8 >> 16) >> 16, F32)
    return jnp.max(z, axis=(0, 1), keepdims=True)


def _mlp_skew_kernel(xn_ref, xp_ref, modn_ref, modp_ref, g_ref, wu_ref, wd_ref, *rest, nt, rs,
                     cast_next):
    s = pl.program_id(0)
    f = pl.program_id(1)
    if cast_next:
        nu_ref, nd_ref, y_ref, nu_out, nd_out, h0_ref, h1_ref, acc0_ref, acc1_ref = rest
    else:
        y_ref, h0_ref, h1_ref, acc0_ref, acc1_ref = rest
    rows = pl.ds(pl.multiple_of(f * rs, rs), rs)
    g_pre, g_post = g_ref[0, 2:3, :], g_ref[0, 3:4, :]
    h_refs = (h0_ref, h1_ref)
    acc_refs = (acc0_ref, acc1_ref)

    def prologue(h_ref):
        m = modn_ref[0]
        h = _norm_mod(xn_ref[...], g_pre, m[3:4, :], m[4:5, :])
        h_ref[rows, :] = h.astype(BF16)
        return h

    def epilogue(acc_ref):
        mp = modp_ref[0]
        y = xp_ref[...] + _gated_norm(acc_ref[rows, :], g_post, mp[5:6, :])
        y_ref[...] = y
        acc_ref[rows, :] = jnp.zeros((rs, D_MODEL), F32)
        return y

    def matmuls(h_ref, acc_ref, anchors):
        if cast_next:
            nu_out[...] = nu_ref[...].astype(BF16)
            nd_out[...] = nd_ref[...].astype(BF16)
        u = jnp.dot(h_ref[...], wu_ref[0], preferred_element_type=F32)
        a = jnp.square(jnp.maximum(u, 0.0)).astype(BF16)
        for n in range(D_MODEL // MLP_OUT_CHUNK):
            cs = slice(n * MLP_OUT_CHUNK, (n + 1) * MLP_OUT_CHUNK)
            d = jnp.dot(a, wd_ref[0, :, cs], preferred_element_type=F32)
            if anchors.get(n) is not None:
                d = d + anchors[n]
            acc_ref[:, cs] += d

    @pl.when((s == 0) & (f == 0))
    def _():
        acc0_ref[...] = jnp.zeros_like(acc0_ref)
        acc1_ref[...] = jnp.zeros_like(acc1_ref)

    @pl.when(s == 0)
    def _():
        prologue(h0_ref)

    @pl.when(s == 1)
    def _():
        h = prologue(h1_ref)
        matmuls(h0_ref, acc0_ref, {2: _zero_after(h)})

    for c in range(2):
        o = 1 - c

        @pl.when(((s - 1) % 2 == c) & (s >= 2) & (s < nt))
        def _():
            y = epilogue(acc_refs[o])
            h = prologue(h_refs[o])
            matmuls(h_refs[c], acc_refs[c], {0: _zero_after(y), 2: _zero_after(h)})

    last = (nt - 1) % 2

    @pl.when(s == nt)
    def _():
        y = epilogue(acc_refs[1 - last])
        matmuls(h_refs[last], acc_refs[last], {0: _zero_after(y)})

    @pl.when(s == nt + 1)
    def _():
        epilogue(acc_refs[last])


def _mlp_layer_skewed(x, mod, g_norm, w_up, w_down, l, *, tm, tff, next_f32=None):
    B, T, D = x.shape
    nf = D_FF // tff
    rs = tm // nf
    tpb = T // tm
    nt = B * tpb
    assert nt >= 3
    x2 = x.reshape(B * T, D)
    nxt = lambda s: jnp.minimum(s, nt - 1)
    prv = lambda s: jnp.maximum(s - 2, 0)
    wf = lambda s, f: jnp.where(s == 0, 0, jnp.where(s == nt + 1, nf - 1, f))
    cast_next = next_f32 is not None
    kern = functools.partial(_mlp_skew_kernel, nt=nt, rs=rs, cast_next=cast_next)
    in_specs = [
        pl.BlockSpec((rs, D), lambda s, f: (nxt(s) * nf + f, 0)),
        pl.BlockSpec((rs, D), lambda s, f: (prv(s) * nf + f, 0)),
        pl.BlockSpec((1, 6, D), lambda s, f: (nxt(s) // tpb, 0, 0)),
        pl.BlockSpec((1, 6, D), lambda s, f: (prv(s) // tpb, 0, 0)),
        pl.BlockSpec((1, 4, D), lambda s, f: (l, 0, 0)),
        pl.BlockSpec((1, D, tff), lambda s, f: (0, 0, wf(s, f))),
        pl.BlockSpec((1, tff, D), lambda s, f: (0, wf(s, f), 0)),
    ]
    args = [x2, x2, mod, mod, g_norm, w_up, w_down]
    out_specs = [pl.BlockSpec((rs, D), lambda s, f: (jnp.where(s < 2, 0, (s - 2) * nf + f), 0))]
    out_shape = [jax.ShapeDtypeStruct((B * T, D), F32)]
    if cast_next:
        n_slabs = nt * nf
        ru, rd = D // n_slabs, D_FF // n_slabs
        slab = lambda s, f: jnp.clip((s - 1) * nf + f, 0, n_slabs - 1)
        up_spec = lambda: pl.BlockSpec((1, ru, D_FF), lambda s, f: (l + 1, slab(s, f), 0))
        dn_spec = lambda: pl.BlockSpec((1, rd, D), lambda s, f: (l + 1, slab(s, f), 0))
        in_specs += [up_spec(), dn_spec()]
        args += list(next_f32)
        out_specs += [pl.BlockSpec((1, ru, D_FF), lambda s, f: (0, slab(s, f), 0)),
                      pl.BlockSpec((1, rd, D), lambda s, f: (0, slab(s, f), 0))]
        out_shape += [jax.ShapeDtypeStruct((1, D, D_FF), BF16),
                      jax.ShapeDtypeStruct((1, D_FF, D), BF16)]
    outs = pl.pallas_call(
        kern,
        out_shape=tuple(out_shape),
        grid=(nt + 2, nf),
        in_specs=in_specs,
        out_specs=tuple(out_specs),
        scratch_shapes=[pltpu.VMEM((tm, D), BF16), pltpu.VMEM((tm, D), BF16),
                        pltpu.VMEM((tm, D), F32), pltpu.VMEM((tm, D), F32)],
        compiler_params=_params("arbitrary", "arbitrary"),
        name=f"mlp_skewed_{l}",
    )(*args)
    y = outs[0].reshape(B, T, D)
    return (y, outs[1], outs[2]) if cast_next else (y, None, None)


def _split_dup(a):
    z = jnp.zeros_like(a)
    lo = jnp.concatenate([a, z], axis=-1)
    hi = jnp.concatenate([z, a], axis=-1)
    out = jnp.stack([lo, hi], axis=3)
    return out.reshape(a.shape[0], a.shape[1], -1).astype(BF16)


class _Path:
    def __init__(self, x, mods, kvmod, pos0, prefix, kv_past, *, bb, tt, tq):
        self.x, self.mods, self.kvmod, self.pos0 = x, mods, kvmod, pos0
        self.prefix, self.kv_past = prefix, kv_past
        self.bb, self.tt, self.tq = bb, tt, tq
        pos = pos0 + jnp.arange(x.shape[1])
        self.tables = tuple(jnp.tile(tb, (bb, 1)) if bb > 1 else tb for tb in _rope_tables(pos))
        self.new_pool = []
        self.k = self.v = self.kd = self.vd = None

    def mixer(self, l, wts):
        B, T, _ = self.x.shape
        mod, bb, tt = self.mods[l], self.bb, self.tt
        if l < N_A_LAYERS:
            self.x, npool = _pool_layer(self.x, mod, wts["g_norm"], self.prefix, wts["w_pool"],
                                        wts["pool_scale"], l, bb=bb, tt=tt, pos0=self.pos0)
            self.new_pool.append(npool)
            return
        j = l - N_A_LAYERS
        q = _qproj_layer(self.x, mod, wts["g_norm"], self.tables, wts["w_q"], l, j, bb=bb, tt=tt)
        if self.kv_past is None:
            self.x = _attention_out(q, self.x, mod, wts["g_norm"], wts["w_o"], self.kd, self.vd,
                                    wts["sinks"][j], l, j, tq=self.tq)
        else:
            o = _attention(q, self.kd, self.vd, wts["sinks"][j], B=B, T=T, tq=self.tq,
                           chunked=False, n_valid=self.kv_past[0].shape[1] + T)
            self.x = _oproj_layer(o, self.x, mod, wts["g_norm"], wts["w_o"], l, j, bb=bb, tt=tt)

    def shared_kv(self, wts):
        B, T, _ = self.x.shape
        k, v, kd, vd = _kv_layer(self.x, self.kvmod, wts["g_kv"], self.tables, wts["w_kv"],
                                 bb=self.bb, tt=self.tt)
        W = kd.shape[-1]
        kd, vd = kd.reshape(B, T, W), vd.reshape(B, T, W)
        if self.kv_past is not None:
            back = ((0, 0), (0, KEY_SPAN - self.kv_past[0].shape[1] - T), (0, 0))
            kd = jnp.pad(jnp.concatenate([_split_dup(self.kv_past[0]), kd], axis=1), back)
            vd = jnp.pad(jnp.concatenate([_split_dup(self.kv_past[1]), vd], axis=1), back)
        KV = N_KV_HEADS * HEAD_DIM
        self.k, self.v, self.kd, self.vd = k.reshape(B, T, KV), v.reshape(B, T, KV), kd, vd


def _forward(prompt, sample, wts, *, tm, tff_prompt, tff_cast, tff_sample):
    w_f32 = (wts["w_up"], wts["w_down"])
    w_bf16 = None
    for l in range(DEPTH):
        sample.mixer(l, wts)
        if l == 0:
            sample.x, wu, wd = _mlp_layer(sample.x, sample.mods[l], wts["g_norm"], *w_f32, l,
                                          bb=sample.bb, tt=sample.tt, tff=tff_cast, cast=True)
            w_bf16 = (wu, wd)
        else:
            sample.x = _mlp_layer(sample.x, sample.mods[l], wts["g_norm"], *w_bf16, l,
                                  bb=sample.bb, tt=sample.tt, tff=tff_sample, cast=False)
        prompt.mixer(l, wts)
        prompt.x, wu, wd = _mlp_layer_skewed(
            prompt.x, prompt.mods[l], wts["g_norm"], *w_bf16, l, tm=tm, tff=tff_prompt,
            next_f32=w_f32 if l + 1 < DEPTH else None)
        w_bf16 = (wu, wd)
        if l == N_A_LAYERS - 1:
            sample.shared_kv(wts)
            prompt.shared_kv(wts)


def _prep_weights(g_norm, w_pool, pool_scale, g_kv, w_kv, w_q, sinks, w_o, w_up, w_down):
    D = D_MODEL
    return {
        "g_norm": g_norm,
        "w_pool": w_pool.astype(BF16),
        "pool_scale": pool_scale.reshape(N_A_LAYERS, 1, D),
        "g_kv": g_kv.reshape(1, 1, D),
        "w_kv": w_kv.astype(BF16),
        "w_q": w_q.astype(BF16),
        "sinks": sinks,
        "w_o": w_o.astype(BF16),
        "w_up": w_up,
        "w_down": w_down,
    }


def kernel(x_prompt, x_sample, c_prompt, c_sample, state_pool, cache_k, cache_v, w_mod, b_mod,
           g_norm, w_pool, pool_scale, w_kv_mod, b_kv_mod, g_kv, w_kv, w_q, sinks, w_o, w_up,
           w_down):
    Bp, Bs = x_prompt.shape[0], x_sample.shape[0]
    T_s = x_sample.shape[1]
    D = D_MODEL

    c_all = jnp.concatenate(
        [c_prompt, c_sample, jnp.zeros((MOD_ROWS - Bp - Bs, D), F32)], axis=0)
    mod_all = _modulation(c_all, w_mod, b_mod.reshape(DEPTH, 1, 6 * D)).reshape(
        DEPTH, MOD_ROWS, 6, D)
    kvmod_all = _modulation(c_all, w_kv_mod.reshape(1, D, 2 * D),
                            b_kv_mod.reshape(1, 1, 2 * D)).reshape(MOD_ROWS, 2, D)
    wts = _prep_weights(g_norm, w_pool, pool_scale, g_kv, w_kv, w_q, sinks, w_o, w_up, w_down)

    prompt = _Path(x_prompt, [mod_all[l, :Bp] for l in range(DEPTH)], kvmod_all[:Bp], 0,
                   jnp.zeros((N_A_LAYERS, Bp, POOL_STATE, D), F32), None,
                   bb=1, tt=512, tq=4 * WINDOW)
    sample = _Path(x_sample, [mod_all[l, Bp:Bp + Bs] for l in range(DEPTH)],
                   kvmod_all[Bp:Bp + Bs], PAST_LEN, state_pool, (cache_k, cache_v),
                   bb=Bs, tt=T_s, tq=T_s)
    _forward(prompt, sample, wts, tm=1024, tff_prompt=1024, tff_cast=512, tff_sample=1024)

    keep = min(WINDOW, x_prompt.shape[1])
    heads = (N_KV_HEADS, HEAD_DIM)
    return (prompt.x, sample.x,
            jnp.concatenate(prompt.new_pool, axis=0), jnp.concatenate(sample.new_pool, axis=0),
            prompt.k[:, -keep:].reshape(Bp, keep, *heads),
            prompt.v[:, -keep:].reshape(Bp, keep, *heads),
            sample.k.reshape(Bs, T_s, *heads), sample.v.reshape(Bs, T_s, *heads))
```

```python
import functools

import jax
import jax.numpy as jnp
from jax import lax
from jax.experimental import pallas as pl
from jax.experimental.pallas import tpu as pltpu

F32 = jnp.float32
BF16 = jnp.bfloat16

D_MODEL = 2048
DEPTH = 4
PAST_LEN = 4096
CHUNK = 64
N_A_LAYERS = DEPTH // 2
POOL_WINDOWS = (2, 4, 8, 16)
POOL_GROUP = D_MODEL // len(POOL_WINDOWS)
POOL_STATE = max(POOL_WINDOWS) - 1
SUBLANES = 8
POOL_LEAD = SUBLANES
POOL_BASE = POOL_LEAD + POOL_STATE + 1
HEAD_DIM = 64
N_HEADS = D_MODEL // HEAD_DIM
N_KV_HEADS = N_HEADS // 8
WINDOW = 128
D_FF = 4 * D_MODEL
ROPE_THETA = 10000.0
EPS = 1e-6
ATTN_SCALE = HEAD_DIM ** -0.5
NEG_INF = -1e30

LANES = 128
KEY_SPAN = 2 * LANES
N_PAIRS = N_HEADS // 2
PAIRS_PER_KV = N_PAIRS // N_KV_HEADS
MLP_OUT_CHUNK = 512
MOD_ROWS = 32
VMEM_LIMIT = 56 * 1024 * 1024


def _params(*sem):
    return pltpu.CompilerParams(dimension_semantics=sem, vmem_limit_bytes=VMEM_LIMIT)


def _norm_mod(x, g, shift, scale):
    ms = jnp.mean(x * x, axis=-1, keepdims=True)
    return (x * lax.rsqrt(ms + EPS)) * (g * (1.0 + scale)) + shift


def _gated_norm(y, g, gate):
    ms = jnp.mean(y * y, axis=-1, keepdims=True)
    return (y * lax.rsqrt(ms + EPS)) * (g * gate)


def _mod_kernel(c_ref, w_ref, b_ref, o_ref):
    c = c_ref[...]
    sc = (c * jax.nn.sigmoid(c)).astype(BF16)
    o_ref[0] = jnp.dot(sc, w_ref[0].astype(BF16), preferred_element_type=F32) + b_ref[0]


def _modulation(c_all, w, b, tn=2048):
    L, D, N = w.shape
    return pl.pallas_call(
        _mod_kernel,
        out_shape=jax.ShapeDtypeStruct((L, MOD_ROWS, N), F32),
        grid=(L, N // tn),
        in_specs=[
            pl.BlockSpec((MOD_ROWS, D), lambda l, n: (0, 0)),
            pl.BlockSpec((1, D, tn), lambda l, n: (l, 0, n)),
            pl.BlockSpec((1, 1, tn), lambda l, n: (l, 0, n)),
        ],
        out_specs=pl.BlockSpec((1, MOD_ROWS, tn), lambda l, n: (l, 0, n)),
        compiler_params=_params("parallel", "parallel"),
        name="modulation",
    )(c_all, w, b)


def _pool_kernel(x_ref, mod_ref, g_ref, pre_ref, wp_ref, ps_ref, o_ref, np_ref, hbuf, s1, s2,
                 *, bb, tt, pos0):
    t = pl.program_id(1)
    G = POOL_GROUP
    LEAD, BASE = POOL_LEAD, POOL_BASE
    L = BASE + tt

    @pl.when(t == 0)
    def _():
        hbuf[:, 0:LEAD + 1, :] = jnp.zeros((bb, LEAD + 1, D_MODEL), F32)
        hbuf[:, LEAD + 1:BASE, :] = pre_ref[0]
        s1[:, 0:LEAD, :] = jnp.zeros((bb, LEAD, 3 * G), F32)
        s2[:, 0:LEAD, :] = jnp.zeros((bb, LEAD, 3 * G), F32)

    @pl.when(t > 0)
    def _():
        hbuf[:, LEAD:BASE, :] = hbuf[:, tt + LEAD:tt + BASE, :]

    x = x_ref[...]
    m = mod_ref[...]
    h = _norm_mod(x, g_ref[:, 0:1, :], m[:, 0:1, :], m[:, 1:2, :])
    hbuf[:, BASE:L, :] = h
    np_ref[0] = hbuf[:, L - POOL_STATE:L, :]

    s1[:, LEAD:L, :] = hbuf[:, LEAD:L, G:] + hbuf[:, LEAD - 1:L - 1, G:]
    s2[:, LEAD:L, :] = s1[:, LEAD:L, :] + s1[:, LEAD - 2:L - 2, :]
    s1[:, 2 * LEAD:L, 0:2 * G] = (s2[:, 2 * LEAD:L, G:] +
                                  s2[:, 2 * LEAD - 4:L - 4, G:])
    sums = [
        hbuf[:, BASE:L, 0:G] + hbuf[:, BASE - 1:L - 1, 0:G],
        s2[:, BASE:L, 0:G],
        s1[:, BASE:L, 0:G],
        s1[:, BASE:L, G:2 * G] + s1[:, BASE - 8:L - 8, G:2 * G],
    ]

    pos = pos0 + t * tt + lax.broadcasted_iota(jnp.int32, (1, tt, LANES), 1)
    ys = []
    for g, w in enumerate(POOL_WINDOWS):
        cs = slice(g * POOL_GROUP, (g + 1) * POOL_GROUP)
        inv_cnt = 1.0 / jnp.minimum(w, pos + 1).astype(F32)
        inv_cnt = jnp.concatenate([inv_cnt] * (POOL_GROUP // LANES), axis=-1)
        pooled = sums[g] * inv_cnt - hbuf[:, BASE:L, cs]
        ys.append(jnp.dot(pooled.reshape(bb * tt, POOL_GROUP).astype(BF16), wp_ref[0, g],
                          preferred_element_type=F32))
    y = jnp.concatenate(ys, axis=-1).reshape(bb, tt, D_MODEL) * ps_ref[...]
    o_ref[...] = x + _gated_norm(y, g_ref[:, 1:2, :], m[:, 2:3, :])


def _pool_layer(x, mod, g_norm, prefix, w_pool, pool_scale, l, *, bb, tt, pos0):
    B, T, D = x.shape
    kern = functools.partial(_pool_kernel, bb=bb, tt=tt, pos0=pos0)
    return pl.pallas_call(
        kern,
        out_shape=(jax.ShapeDtypeStruct((B, T, D), F32),
                   jax.ShapeDtypeStruct((1, B, POOL_STATE, D), F32)),
        grid=(B // bb, T // tt),
        in_specs=[
            pl.BlockSpec((bb, tt, D), lambda b, t: (b, t, 0)),
            pl.BlockSpec((bb, 6, D), lambda b, t: (b, 0, 0)),
            pl.BlockSpec((1, 4, D), lambda b, t: (l, 0, 0)),
            pl.BlockSpec((1, bb, POOL_STATE, D), lambda b, t: (l, b, 0, 0)),
            pl.BlockSpec((1, len(POOL_WINDOWS), POOL_GROUP, POOL_GROUP),
                         lambda b, t: (l, 0, 0, 0)),
            pl.BlockSpec((1, 1, D), lambda b, t: (l, 0, 0)),
        ],
        out_specs=(pl.BlockSpec((bb, tt, D), lambda b, t: (b, t, 0)),
                   pl.BlockSpec((1, bb, POOL_STATE, D), lambda b, t: (0, b, 0, 0))),
        scratch_shapes=[pltpu.VMEM((bb, POOL_BASE + tt, D), F32),
                        pltpu.VMEM((bb, POOL_BASE + tt, 3 * POOL_GROUP), F32),
                        pltpu.VMEM((bb, POOL_BASE + tt, 3 * POOL_GROUP), F32)],
        compiler_params=_params("parallel", "arbitrary"),
        name=f"pool_mixer_{l}",
    )(x, mod, g_norm, prefix, w_pool, pool_scale)


def _rope_tables(pos):
    half = HEAD_DIM // 2
    inv = ROPE_THETA ** (-jnp.arange(half, dtype=F32) / half)
    ang = pos.astype(F32)[:, None] * inv[None, :]
    cos, sin = jnp.cos(ang), jnp.sin(ang)
    zero = jnp.zeros_like(sin)
    c = jnp.tile(cos, (1, 4))
    s_lo = jnp.tile(jnp.concatenate([-sin, zero], axis=1), (1, 2))
    s_hi = jnp.tile(jnp.concatenate([zero, sin], axis=1), (1, 2))
    return c, s_lo, s_hi


def _rope_block(blk, c, s_lo, s_hi):
    return (blk * c + pltpu.roll(blk, LANES - HEAD_DIM // 2, 1) * s_lo
            + pltpu.roll(blk, HEAD_DIM // 2, 1) * s_hi)


def _qproj_kernel(x_ref, mod_ref, g_ref, c_ref, slo_ref, shi_ref, w_ref, q_ref, *, bb, tt):
    x = x_ref[...]
    m = mod_ref[...]
    h = _norm_mod(x, g_ref[:, 0:1, :], m[:, 0:1, :], m[:, 1:2, :])
    q = jnp.dot(h.reshape(bb * tt, D_MODEL).astype(BF16), w_ref[0], preferred_element_type=F32)
    c, s_lo, s_hi = c_ref[...], slo_ref[...], shi_ref[...]
    for p in range(N_PAIRS):
        cs = slice(p * LANES, (p + 1) * LANES)
        q_ref[p] = (_rope_block(q[:, cs], c, s_lo, s_hi) * ATTN_SCALE).astype(BF16)


def _qproj_layer(x, mod, g_norm, tables, w_q, l, j, *, bb, tt):
    B, T, D = x.shape
    nt = T // tt
    M = bb * tt
    kern = functools.partial(_qproj_kernel, bb=bb, tt=tt)
    tab_spec = pl.BlockSpec((M, LANES), lambda b, t: (t, 0))
    return pl.pallas_call(
        kern,
        out_shape=jax.ShapeDtypeStruct((N_PAIRS, B * T, LANES), BF16),
        grid=(B // bb, nt),
        in_specs=[
            pl.BlockSpec((bb, tt, D), lambda b, t: (b, t, 0)),
            pl.BlockSpec((bb, 6, D), lambda b, t: (b, 0, 0)),
            pl.BlockSpec((1, 4, D), lambda b, t: (l, 0, 0)),
            tab_spec, tab_spec, tab_spec,
            pl.BlockSpec((1, D, D), lambda b, t: (j, 0, 0)),
        ],
        out_specs=pl.BlockSpec((N_PAIRS, M, LANES), lambda b, t: (0, b * nt + t, 0)),
        compiler_params=_params("parallel", "parallel"),
        name=f"q_proj_{l}",
    )(x, mod, g_norm, *tables, w_q)


def _kv_kernel(x_ref, mod_ref, g_ref, c_ref, slo_ref, shi_ref, w_ref,
               k_ref, v_ref, kd_ref, vd_ref, *, bb, tt):
    x = x_ref[...]
    m = mod_ref[...]
    h = _norm_mod(x, g_ref[...], m[:, 0:1, :], m[:, 1:2, :])
    kv = jnp.dot(h.reshape(bb * tt, D_MODEL).astype(BF16), w_ref[...],
                 preferred_element_type=F32)
    c, s_lo, s_hi = c_ref[...], slo_ref[...], shi_ref[...]
    lo = lax.broadcasted_iota(jnp.int32, (bb * tt, LANES), 1) < HEAD_DIM
    zero = jnp.zeros((bb * tt, LANES), F32)
    n_blk = N_KV_HEADS // 2
    for p in range(n_blk):
        cs = slice(p * LANES, (p + 1) * LANES)
        k_blk = _rope_block(kv[:, cs], c, s_lo, s_hi)
        v_blk = kv[:, (n_blk + p) * LANES:(n_blk + p + 1) * LANES]
        k_ref[:, cs] = k_blk
        v_ref[:, cs] = v_blk
        for blk, dst in ((k_blk, kd_ref), (v_blk, vd_ref)):
            swapped = pltpu.roll(blk, HEAD_DIM, 1)
            parts = (jnp.where(lo, blk, zero), jnp.where(lo, zero, swapped),
                     jnp.where(lo, swapped, zero), jnp.where(lo, zero, blk))
            for i, part in enumerate(parts):
                dst[:, (4 * p + i) * LANES:(4 * p + i + 1) * LANES] = part.astype(BF16)


def _kv_layer(x, mod, g_kv, tables, w_kv, *, bb, tt):
    B, T, D = x.shape
    nt = T // tt
    M = bb * tt
    KV = N_KV_HEADS * HEAD_DIM
    kern = functools.partial(_kv_kernel, bb=bb, tt=tt)
    tab_spec = pl.BlockSpec((M, LANES), lambda b, t: (t, 0))
    row = lambda b, t: (b * nt + t, 0)
    newest = lambda b, t: (b, 0)
    return pl.pallas_call(
        kern,
        out_shape=(jax.ShapeDtypeStruct((B // bb * M, KV), F32),
                   jax.ShapeDtypeStruct((B // bb * M, KV), F32),
                   jax.ShapeDtypeStruct((B * T, 4 * KV), BF16),
                   jax.ShapeDtypeStruct((B * T, 4 * KV), BF16)),
        grid=(B // bb, nt),
        in_specs=[
            pl.BlockSpec((bb, tt, D), lambda b, t: (b, t, 0)),
            pl.BlockSpec((bb, 2, D), lambda b, t: (b, 0, 0)),
            pl.BlockSpec((1, 1, D), lambda b, t: (0, 0, 0)),
            tab_spec, tab_spec, tab_spec,
            pl.BlockSpec((D, 2 * KV), lambda b, t: (0, 0)),
        ],
        out_specs=(pl.BlockSpec((M, KV), newest), pl.BlockSpec((M, KV), newest),
                   pl.BlockSpec((M, 4 * KV), row), pl.BlockSpec((M, 4 * KV), row)),
        compiler_params=_params("parallel", "arbitrary"),
        name="shared_kv",
    )(x, mod, g_kv, *tables, w_kv)


def _sink_softmax(s, sink):
    m = jnp.max(s, axis=-1, keepdims=True)
    e = jnp.exp(s - m)
    den = jnp.sum(e, axis=-1, keepdims=True) + jnp.exp(sink - m)
    return e * (1.0 / den)


def _attn_kernel(sink_ref, *refs, tq, chunked, n_valid, fused):
    i = pl.program_id(1)
    G = PAIRS_PER_KV
    ts = WINDOW if chunked else tq
    nsub = tq // ts
    R = G * ts
    QW = D_MODEL // N_KV_HEADS
    kp = lax.broadcasted_iota(jnp.int32, (R, KEY_SPAN), 1)
    rblk = lax.broadcasted_iota(jnp.int32, (R, 1), 0) // ts
    nt = (((1,), (1,)), ((), ()))
    sink_cols = []
    for j in range(N_KV_HEADS):
        for parity in range(2):
            col = jnp.full((R, 1), sink_ref[2 * G * j + parity], F32)
            for r in range(1, G):
                col = jnp.where(rblk == r, sink_ref[2 * (G * j + r) + parity], col)
            sink_cols.append(col)
    if fused:
        q_ref, k_prev, k_cur, v_prev, v_cur, x_ref, mod_ref, g_ref, wo_ref, o_ref = refs
    elif chunked:
        q_ref, k_prev, k_cur, v_prev, v_cur, o_ref = refs
    else:
        q_ref, kd_ref, vd_ref, o_ref = refs
    if chunked:
        row = lax.broadcasted_iota(jnp.int32, (R, KEY_SPAN), 0)
        cq = (row % ts) // CHUNK
        kb = kp // CHUNK
        in_window = (kb >= cq) & (kb <= cq + WINDOW // CHUNK)

    def finish(rows, mix):
        m = mod_ref[0]
        o_ref[rows, :] = x_ref[rows, :] + _gated_norm(mix, g_ref[0, 1:2, :], m[2:3, :])

    pending = None
    for sub in range(nsub):
        rs = slice(sub * ts, (sub + 1) * ts)
        if not chunked:
            kw, vw = kd_ref[0], vd_ref[0]
            valid = kp < n_valid
        elif sub == 0:
            kw = jnp.concatenate([k_prev[0], k_cur[0, rs, :]], axis=0)
            vw = jnp.concatenate([v_prev[0], v_cur[0, rs, :]], axis=0)
            valid = in_window & (kp >= jnp.where(i > 0, 0, WINDOW))
        else:
            span = slice(sub * ts - WINDOW, (sub + 1) * ts)
            kw, vw = k_cur[0, span, :], v_cur[0, span, :]
            valid = in_window
        bias = jnp.where(valid, 0.0, NEG_INF)
        heads = []
        mix_parts = []
        for j in range(N_KV_HEADS):
            qg = q_ref[G * j:G * (j + 1), rs, :].reshape(R, LANES)
            k_lo = kw[:, (2 * j) * LANES:(2 * j + 1) * LANES]
            k_hi = kw[:, (2 * j + 1) * LANES:(2 * j + 2) * LANES]
            v_lo = vw[:, (2 * j) * LANES:(2 * j + 1) * LANES]
            v_hi = vw[:, (2 * j + 1) * LANES:(2 * j + 2) * LANES]
            s0 = lax.dot_general(qg, k_lo, nt, preferred_element_type=F32) + bias
            s1 = lax.dot_general(qg, k_hi, nt, preferred_element_type=F32) + bias
            if pending is not None:
                mix_parts.append(jnp.dot(pending[1], wo_ref[0, :, j * QW:(j + 1) * QW],
                                         preferred_element_type=F32))
            p0 = _sink_softmax(s0, sink_cols[2 * j])
            p1 = _sink_softmax(s1, sink_cols[2 * j + 1])
            o = (jnp.dot(p0.astype(BF16), v_lo, preferred_element_type=F32)
                 + jnp.dot(p1.astype(BF16), v_hi, preferred_element_type=F32)).astype(BF16)
            if fused:
                heads.extend(o[r * ts:(r + 1) * ts] for r in range(G))
            else:
                o_ref[G * j:G * (j + 1), rs, :] = o.reshape(G, ts, LANES)
        if fused:
            if pending is not None:
                finish(pending[0], jnp.concatenate(mix_parts, axis=1))
            pending = (rs, jnp.concatenate(heads, axis=1))
    if fused:
        finish(pending[0], jnp.dot(pending[1], wo_ref[0], preferred_element_type=F32))


def _attention(q, kd, vd, sinks, *, B, T, tq, chunked, n_valid):
    nq = T // tq
    Tk = kd.shape[1]
    W = kd.shape[2]
    kern = functools.partial(_attn_kernel, tq=tq, chunked=chunked, n_valid=n_valid, fused=False)
    if chunked:
        kv_specs, kv_args = _window_specs(kd, vd, T, tq)
    else:
        assert Tk == KEY_SPAN and nq == 1
        whole = pl.BlockSpec((1, Tk, W), lambda b, i: (b, 0, 0))
        kv_specs, kv_args = [whole, whole], (kd, vd)
    pair_rows = pl.BlockSpec((N_PAIRS, tq, LANES), lambda b, i: (0, b * nq + i, 0))
    return pl.pallas_call(
        kern,
        out_shape=jax.ShapeDtypeStruct((N_PAIRS, B * T, LANES), BF16),
        grid=(B, nq),
        in_specs=[pl.BlockSpec(memory_space=pltpu.SMEM), pair_rows, *kv_specs],
        out_specs=pair_rows,
        compiler_params=_params("parallel", "parallel"),
        name="swa_attention",
    )(sinks, q, *kv_args)


def _window_specs(kd, vd, T, tq):
    W = kd.shape[2]
    assert tq % WINDOW == 0 and kd.shape[1] == T
    nsub = tq // WINDOW
    prev = pl.BlockSpec((1, WINDOW, W), lambda b, i: (b, jnp.maximum(i * nsub - 1, 0), 0))
    cur = pl.BlockSpec((1, tq, W), lambda b, i: (b, i, 0))
    return [prev, cur, prev, cur], (kd, kd, vd, vd)


def _attention_out(q, x, mod, g_norm, w_o, kd, vd, sinks, l, j, *, tq):
    B, T, D = x.shape
    nq = T // tq
    kern = functools.partial(_attn_kernel, tq=tq, chunked=True, n_valid=None, fused=True)
    kv_specs, kv_args = _window_specs(kd, vd, T, tq)
    rows = lambda b, i: (b * nq + i, 0)
    out = pl.pallas_call(
        kern,
        out_shape=jax.ShapeDtypeStruct((B * T, D), F32),
        grid=(B, nq),
        in_specs=[
            pl.BlockSpec(memory_space=pltpu.SMEM),
            pl.BlockSpec((N_PAIRS, tq, LANES), lambda b, i: (0, b * nq + i, 0)),
            *kv_specs,
            pl.BlockSpec((tq, D), rows),
            pl.BlockSpec((1, 6, D), lambda b, i: (b, 0, 0)),
            pl.BlockSpec((1, 4, D), lambda b, i: (l, 0, 0)),
            pl.BlockSpec((1, D, D), lambda b, i: (j, 0, 0)),
        ],
        out_specs=pl.BlockSpec((tq, D), rows),
        compiler_params=_params("parallel", "parallel"),
        name=f"attention_out_{l}",
    )(sinks, q, *kv_args, x.reshape(B * T, D), mod, g_norm, w_o)
    return out.reshape(B, T, D)


def _oproj_kernel(o_ref, x_ref, mod_ref, g_ref, w_ref, y_ref, *, bb, tt):
    o = jnp.concatenate([o_ref[p] for p in range(N_PAIRS)], axis=1)
    mix = jnp.dot(o, w_ref[0], preferred_element_type=F32).reshape(bb, tt, D_MODEL)
    m = mod_ref[...]
    y_ref[...] = x_ref[...] + _gated_norm(mix, g_ref[:, 1:2, :], m[:, 2:3, :])


def _oproj_layer(o, x, mod, g_norm, w_o, l, j, *, bb, tt):
    B, T, D = x.shape
    nt = T // tt
    kern = functools.partial(_oproj_kernel, bb=bb, tt=tt)
    return pl.pallas_call(
        kern,
        out_shape=jax.ShapeDtypeStruct((B, T, D), F32),
        grid=(B // bb, nt),
        in_specs=[
            pl.BlockSpec((N_PAIRS, bb * tt, LANES), lambda b, t: (0, b * nt + t, 0)),
            pl.BlockSpec((bb, tt, D), lambda b, t: (b, t, 0)),
            pl.BlockSpec((bb, 6, D), lambda b, t: (b, 0, 0)),
            pl.BlockSpec((1, 4, D), lambda b, t: (l, 0, 0)),
            pl.BlockSpec((1, D, D), lambda b, t: (j, 0, 0)),
        ],
        out_specs=pl.BlockSpec((bb, tt, D), lambda b, t: (b, t, 0)),
        compiler_params=_params("parallel", "parallel"),
        name=f"o_proj_{l}",
    )(o, x, mod, g_norm, w_o)


def _mlp_kernel(x_ref, mod_ref, g_ref, wu_ref, wd_ref, y_ref, *rest, bb, tt, cast):
    f = pl.program_id(2)
    m = mod_ref[...]
    if cast:
        wu_out, wd_out, h_ref = rest
    else:
        h_ref, = rest

    @pl.when(f == 0)
    def _():
        h = _norm_mod(x_ref[...], g_ref[:, 2:3, :], m[:, 3:4, :], m[:, 4:5, :])
        h_ref[...] = h.reshape(bb * tt, D_MODEL).astype(BF16)
        y_ref[...] = jnp.zeros_like(y_ref)

    wu = wu_ref[0]
    if cast:
        wu = wu.astype(BF16)
        wu_out[0] = wu
    u = jnp.dot(h_ref[...], wu, preferred_element_type=F32)
    a = jnp.square(jnp.maximum(u, 0.0)).astype(BF16)
    for n in range(D_MODEL // MLP_OUT_CHUNK):
        cs = slice(n * MLP_OUT_CHUNK, (n + 1) * MLP_OUT_CHUNK)
        wd_n = wd_ref[0, :, cs]
        if cast:
            wd_n = wd_n.astype(BF16)
            wd_out[0, :, cs] = wd_n
        y_ref[:, :, cs] += jnp.dot(a, wd_n, preferred_element_type=F32).reshape(
            bb, tt, MLP_OUT_CHUNK)

    @pl.when(f == pl.num_programs(2) - 1)
    def _():
        y_ref[...] = x_ref[...] + _gated_norm(y_ref[...], g_ref[:, 3:4, :], m[:, 5:6, :])


def _mlp_layer(x, mod, g_norm, w_up, w_down, l, *, bb, tt, tff, cast):
    B, T, D = x.shape
    grid = (B // bb, T // tt, D_FF // tff)
    kern = functools.partial(_mlp_kernel, bb=bb, tt=tt, cast=cast)
    wl = l if cast else 0
    y_shape = jax.ShapeDtypeStruct((B, T, D), F32)
    y_spec = pl.BlockSpec((bb, tt, D), lambda b, t, f: (b, t, 0))
    if cast:
        assert grid[0] * grid[1] == 1
        out_shape = (y_shape, jax.ShapeDtypeStruct((1, D, D_FF), BF16),
                     jax.ShapeDtypeStruct((1, D_FF, D), BF16))
        out_specs = (y_spec, pl.BlockSpec((1, D, tff), lambda b, t, f: (0, 0, f)),
                     pl.BlockSpec((1, tff, D), lambda b, t, f: (0, f, 0)))
    else:
        out_shape, out_specs = y_shape, y_spec
    return pl.pallas_call(
        kern,
        out_shape=out_shape,
        grid=grid,
        in_specs=[
            pl.BlockSpec((bb, tt, D), lambda b, t, f: (b, t, 0)),
            pl.BlockSpec((bb, 6, D), lambda b, t, f: (b, 0, 0)),
            pl.BlockSpec((1, 4, D), lambda b, t, f: (l, 0, 0)),
            pl.BlockSpec((1, D, tff), lambda b, t, f: (wl, 0, f)),
            pl.BlockSpec((1, tff, D), lambda b, t, f: (wl, f, 0)),
        ],
        out_specs=out_specs,
        scratch_shapes=[pltpu.VMEM((bb * tt, D), BF16)],
        compiler_params=_params("parallel", "parallel", "arbitrary"),
        name=f"mlp_{l}",
    )(x, mod, g_norm, w_up, w_down)


def _zero_after(x):
    bits = pltpu.bitcast(x, jnp.uint32)
    r = bits[:, 0:LANES]
    for k in range(1, x.shape[1] // LANES):
        r = r | bits[:, k * LANES:(k + 1) * LANES]
    r8 = r[0:SUBLANES]
    for k in range(1, x.shape[0] // SUBLANES):
        r8 = r8 | r[k * SUBLANES:(k + 1) * SUBLANES]
    z = pltpu.bitcast((r8 >> 16) >> 16, F32)
    return jnp.max(z, axis=(0, 1), keepdims=True)


def _mlp_skew_kernel(xn_ref, xp_ref, modn_ref, modp_ref, g_ref, wu_ref, wd_ref, *rest, nt, rs,
                     cast_next):
    s = pl.program_id(0)
    f = pl.program_id(1)
    if cast_next:
        nu_ref, nd_ref, y_ref, nu_out, nd_out, h0_ref, h1_ref, acc0_ref, acc1_ref = rest
    else:
        y_ref, h0_ref, h1_ref, acc0_ref, acc1_ref = rest
    rows = pl.ds(pl.multiple_of(f * rs, rs), rs)
    g_pre, g_post = g_ref[0, 2:3, :], g_ref[0, 3:4, :]
    h_refs = (h0_ref, h1_ref)
    acc_refs = (acc0_ref, acc1_ref)

    def prologue(h_ref):
        m = modn_ref[0]
        h = _norm_mod(xn_ref[...], g_pre, m[3:4, :], m[4:5, :])
        h_ref[rows, :] = h.astype(BF16)
        return h

    def epilogue(acc_ref):
        mp = modp_ref[0]
        y = xp_ref[...] + _gated_norm(acc_ref[rows, :], g_post, mp[5:6, :])
        y_ref[...] = y
        acc_ref[rows, :] = jnp.zeros((rs, D_MODEL), F32)
        return y

    def matmuls(h_ref, acc_ref, anchors):
        if cast_next:
            nu_out[...] = nu_ref[...].astype(BF16)
            nd_out[...] = nd_ref[...].astype(BF16)
        u = jnp.dot(h_ref[...], wu_ref[0], preferred_element_type=F32)
        a = jnp.square(jnp.maximum(u, 0.0)).astype(BF16)
        for n in range(D_MODEL // MLP_OUT_CHUNK):
            cs = slice(n * MLP_OUT_CHUNK, (n + 1) * MLP_OUT_CHUNK)
            d = jnp.dot(a, wd_ref[0, :, cs], preferred_element_type=F32)
            if anchors.get(n) is not None:
                d = d + anchors[n]
            acc_ref[:, cs] += d

    @pl.when((s == 0) & (f == 0))
    def _():
        acc0_ref[...] = jnp.zeros_like(acc0_ref)
        acc1_ref[...] = jnp.zeros_like(acc1_ref)

    @pl.when(s == 0)
    def _():
        prologue(h0_ref)

    @pl.when(s == 1)
    def _():
        h = prologue(h1_ref)
        matmuls(h0_ref, acc0_ref, {2: _zero_after(h)})

    for c in range(2):
        o = 1 - c

        @pl.when(((s - 1) % 2 == c) & (s >= 2) & (s < nt))
        def _():
            y = epilogue(acc_refs[o])
            h = prologue(h_refs[o])
            matmuls(h_refs[c], acc_refs[c], {0: _zero_after(y), 2: _zero_after(h)})

    last = (nt - 1) % 2

    @pl.when(s == nt)
    def _():
        y = epilogue(acc_refs[1 - last])
        matmuls(h_refs[last], acc_refs[last], {0: _zero_after(y)})

    @pl.when(s == nt + 1)
    def _():
        epilogue(acc_refs[last])


def _mlp_layer_skewed(x, mod, g_norm, w_up, w_down, l, *, tm, tff, next_f32=None):
    B, T, D = x.shape
    nf = D_FF // tff
    rs = tm // nf
    tpb = T // tm
    nt = B * tpb
    assert nt >= 3
    x2 = x.reshape(B * T, D)
    nxt = lambda s: jnp.minimum(s, nt - 1)
    prv = lambda s: jnp.maximum(s - 2, 0)
    wf = lambda s, f: jnp.where(s == 0, 0, jnp.where(s == nt + 1, nf - 1, f))
    cast_next = next_f32 is not None
    kern = functools.partial(_mlp_skew_kernel, nt=nt, rs=rs, cast_next=cast_next)
    in_specs = [
        pl.BlockSpec((rs, D), lambda s, f: (nxt(s) * nf + f, 0)),
        pl.BlockSpec((rs, D), lambda s, f: (prv(s) * nf + f, 0)),
        pl.BlockSpec((1, 6, D), lambda s, f: (nxt(s) // tpb, 0, 0)),
        pl.BlockSpec((1, 6, D), lambda s, f: (prv(s) // tpb, 0, 0)),
        pl.BlockSpec((1, 4, D), lambda s, f: (l, 0, 0)),
        pl.BlockSpec((1, D, tff), lambda s, f: (0, 0, wf(s, f))),
        pl.BlockSpec((1, tff, D), lambda s, f: (0, wf(s, f), 0)),
    ]
    args = [x2, x2, mod, mod, g_norm, w_up, w_down]
    out_specs = [pl.BlockSpec((rs, D), lambda s, f: (jnp.where(s < 2, 0, (s - 2) * nf + f), 0))]
    out_shape = [jax.ShapeDtypeStruct((B * T, D), F32)]
    if cast_next:
        n_slabs = nt * nf
        ru, rd = D // n_slabs, D_FF // n_slabs
        slab = lambda s, f: jnp.clip((s - 1) * nf + f, 0, n_slabs - 1)
        up_spec = lambda: pl.BlockSpec((1, ru, D_FF), lambda s, f: (l + 1, slab(s, f), 0))
        dn_spec = lambda: pl.BlockSpec((1, rd, D), lambda s, f: (l + 1, slab(s, f), 0))
        in_specs += [up_spec(), dn_spec()]
        args += list(next_f32)
        out_specs += [pl.BlockSpec((1, ru, D_FF), lambda s, f: (0, slab(s, f), 0)),
                      pl.BlockSpec((1, rd, D), lambda s, f: (0, slab(s, f), 0))]
        out_shape += [jax.ShapeDtypeStruct((1, D, D_FF), BF16),
                      jax.ShapeDtypeStruct((1, D_FF, D), BF16)]
    outs = pl.pallas_call(
        kern,
        out_shape=tuple(out_shape),
        grid=(nt + 2, nf),
        in_specs=in_specs,
        out_specs=tuple(out_specs),
        scratch_shapes=[pltpu.VMEM((tm, D), BF16), pltpu.VMEM((tm, D), BF16),
                        pltpu.VMEM((tm, D), F32), pltpu.VMEM((tm, D), F32)],
        compiler_params=_params("arbitrary", "arbitrary"),
        name=f"mlp_skewed_{l}",
    )(*args)
    y = outs[0].reshape(B, T, D)
    return (y, outs[1], outs[2]) if cast_next else (y, None, None)


def _split_dup(a):
    z = jnp.zeros_like(a)
    lo = jnp.concatenate([a, z], axis=-1)
    hi = jnp.concatenate([z, a], axis=-1)
    out = jnp.stack([lo, hi], axis=3)
    return out.reshape(a.shape[0], a.shape[1], -1).astype(BF16)


class _Path:
    def __init__(self, x, mods, kvmod, pos0, prefix, kv_past, *, bb, tt, tq):
        self.x, self.mods, self.kvmod, self.pos0 = x, mods, kvmod, pos0
        self.prefix, self.kv_past = prefix, kv_past
        self.bb, self.tt, self.tq = bb, tt, tq
        pos = pos0 + jnp.arange(x.shape[1])
        self.tables = tuple(jnp.tile(tb, (bb, 1)) if bb > 1 else tb for tb in _rope_tables(pos))
        self.new_pool = []
        self.k = self.v = self.kd = self.vd = None

    def mixer(self, l, wts):
        B, T, _ = self.x.shape
        mod, bb, tt = self.mods[l], self.bb, self.tt
        if l < N_A_LAYERS:
            self.x, npool = _pool_layer(self.x, mod, wts["g_norm"], self.prefix, wts["w_pool"],
                                        wts["pool_scale"], l, bb=bb, tt=tt, pos0=self.pos0)
            self.new_pool.append(npool)
            return
        j = l - N_A_LAYERS
        q = _qproj_layer(self.x, mod, wts["g_norm"], self.tables, wts["w_q"], l, j, bb=bb, tt=tt)
        if self.kv_past is None:
            self.x = _attention_out(q, self.x, mod, wts["g_norm"], wts["w_o"], self.kd, self.vd,
                                    wts["sinks"][j], l, j, tq=self.tq)
        else:
            o = _attention(q, self.kd, self.vd, wts["sinks"][j], B=B, T=T, tq=self.tq,
                           chunked=False, n_valid=self.kv_past[0].shape[1] + T)
            self.x = _oproj_layer(o, self.x, mod, wts["g_norm"], wts["w_o"], l, j, bb=bb, tt=tt)

    def shared_kv(self, wts):
        B, T, _ = self.x.shape
        k, v, kd, vd = _kv_layer(self.x, self.kvmod, wts["g_kv"], self.tables, wts["w_kv"],
                                 bb=self.bb, tt=self.tt)
        W = kd.shape[-1]
        kd, vd = kd.reshape(B, T, W), vd.reshape(B, T, W)
        if self.kv_past is not None:
            back = ((0, 0), (0, KEY_SPAN - self.kv_past[0].shape[1] - T), (0, 0))
            kd = jnp.pad(jnp.concatenate([_split_dup(self.kv_past[0]), kd], axis=1), back)
            vd = jnp.pad(jnp.concatenate([_split_dup(self.kv_past[1]), vd], axis=1), back)
        KV = N_KV_HEADS * HEAD_DIM
        self.k, self.v = k.reshape(B, self.tt, KV), v.reshape(B, self.tt, KV)
        self.kd, self.vd = kd, vd


def _forward(prompt, sample, wts, *, tm, tff_prompt, tff_cast, tff_sample):
    w_f32 = (wts["w_up"], wts["w_down"])
    w_bf16 = None
    for l in range(DEPTH):
        sample.mixer(l, wts)
        if l == 0:
            sample.x, wu, wd = _mlp_layer(sample.x, sample.mods[l], wts["g_norm"], *w_f32, l,
                                          bb=sample.bb, tt=sample.tt, tff=tff_cast, cast=True)
            w_bf16 = (wu, wd)
        else:
            sample.x = _mlp_layer(sample.x, sample.mods[l], wts["g_norm"], *w_bf16, l,
                                  bb=sample.bb, tt=sample.tt, tff=tff_sample, cast=False)
        prompt.mixer(l, wts)
        prompt.x, wu, wd = _mlp_layer_skewed(
            prompt.x, prompt.mods[l], wts["g_norm"], *w_bf16, l, tm=tm, tff=tff_prompt,
            next_f32=w_f32 if l + 1 < DEPTH else None)
        w_bf16 = (wu, wd)
        if l == N_A_LAYERS - 1:
            sample.shared_kv(wts)
            prompt.shared_kv(wts)


def _prep_weights(g_norm, w_pool, pool_scale, g_kv, w_kv, w_q, sinks, w_o, w_up, w_down):
    D = D_MODEL
    return {
        "g_norm": g_norm,
        "w_pool": w_pool.astype(BF16),
        "pool_scale": pool_scale.reshape(N_A_LAYERS, 1, D),
        "g_kv": g_kv.reshape(1, 1, D),
        "w_kv": w_kv.astype(BF16),
        "w_q": w_q.astype(BF16),
        "sinks": sinks,
        "w_o": w_o.astype(BF16),
        "w_up": w_up,
        "w_down": w_down,
    }


def kernel(x_prompt, x_sample, c_prompt, c_sample, state_pool, cache_k, cache_v, w_mod, b_mod,
           g_norm, w_pool, pool_scale, w_kv_mod, b_kv_mod, g_kv, w_kv, w_q, sinks, w_o, w_up,
           w_down):
    Bp, Bs = x_prompt.shape[0], x_sample.shape[0]
    T_s = x_sample.shape[1]
    D = D_MODEL

    c_all = jnp.concatenate(
        [c_prompt, c_sample, jnp.zeros((MOD_ROWS - Bp - Bs, D), F32)], axis=0)
    mod_all = _modulation(c_all, w_mod, b_mod.reshape(DEPTH, 1, 6 * D)).reshape(
        DEPTH, MOD_ROWS, 6, D)
    kvmod_all = _modulation(c_all, w_kv_mod.reshape(1, D, 2 * D),
                            b_kv_mod.reshape(1, 1, 2 * D)).reshape(MOD_ROWS, 2, D)
    wts = _prep_weights(g_norm, w_pool, pool_scale, g_kv, w_kv, w_q, sinks, w_o, w_up, w_down)

    prompt = _Path(x_prompt, [mod_all[l, :Bp] for l in range(DEPTH)], kvmod_all[:Bp], 0,
                   jnp.zeros((N_A_LAYERS, Bp, POOL_STATE, D), F32), None,
                   bb=1, tt=512, tq=4 * WINDOW)
    sample = _Path(x_sample, [mod_all[l, Bp:Bp + Bs] for l in range(DEPTH)],
                   kvmod_all[Bp:Bp + Bs], PAST_LEN, state_pool, (cache_k, cache_v),
                   bb=Bs, tt=T_s, tq=T_s)
    _forward(prompt, sample, wts, tm=1024, tff_prompt=1024, tff_cast=512, tff_sample=1024)

    keep = min(WINDOW, x_prompt.shape[1])
    heads = (N_KV_HEADS, HEAD_DIM)
    return (prompt.x, sample.x,
            jnp.concatenate(prompt.new_pool, axis=0), jnp.concatenate(sample.new_pool, axis=0),
            prompt.k[:, -keep:].reshape(Bp, keep, *heads),
            prompt.v[:, -keep:].reshape(Bp, keep, *heads),
            sample.k.reshape(Bs, T_s, *heads), sample.v.reshape(Bs, T_s, *heads))
```

```python
import functools

import jax
import jax.numpy as jnp
from jax import lax
from jax.experimental import pallas as pl
from jax.experimental.pallas import tpu as pltpu

F32 = jnp.float32
BF16 = jnp.bfloat16

D_MODEL = 2048
DEPTH = 4
PAST_LEN = 4096
CHUNK = 64
N_A_LAYERS = DEPTH // 2
POOL_WINDOWS = (2, 4, 8, 16)
POOL_GROUP = D_MODEL // len(POOL_WINDOWS)
POOL_STATE = max(POOL_WINDOWS) - 1
SUBLANES = 8
POOL_LEAD = SUBLANES
POOL_BASE = POOL_LEAD + POOL_STATE + 1
HEAD_DIM = 64
N_HEADS = D_MODEL // HEAD_DIM
N_KV_HEADS = N_HEADS // 8
WINDOW = 128
D_FF = 4 * D_MODEL
ROPE_THETA = 10000.0
EPS = 1e-6
ATTN_SCALE = HEAD_DIM ** -0.5
NEG_INF = -1e30

LANES = 128
KEY_SPAN = 2 * LANES
N_PAIRS = N_HEADS // 2
PAIRS_PER_KV = N_PAIRS // N_KV_HEADS
MLP_OUT_CHUNK = 512
MOD_ROWS = 32
VMEM_LIMIT = 56 * 1024 * 1024


def _params(*sem):
    return pltpu.CompilerParams(dimension_semantics=sem, vmem_limit_bytes=VMEM_LIMIT)


def _norm_mod(x, g, shift, scale):
    ms = jnp.mean(x * x, axis=-1, keepdims=True)
    return (x * lax.rsqrt(ms + EPS)) * (g * (1.0 + scale)) + shift


def _gated_norm(y, g, gate):
    ms = jnp.mean(y * y, axis=-1, keepdims=True)
    return (y * lax.rsqrt(ms + EPS)) * (g * gate)


def _mod_kernel(c_ref, w_ref, b_ref, o_ref):
    c = c_ref[...]
    sc = (c * jax.nn.sigmoid(c)).astype(BF16)
    o_ref[0] = jnp.dot(sc, w_ref[0].astype(BF16), preferred_element_type=F32) + b_ref[0]


def _modulation(c_all, w, b, tn=1024):
    L, D, N = w.shape
    return pl.pallas_call(
        _mod_kernel,
        out_shape=jax.ShapeDtypeStruct((L, MOD_ROWS, N), F32),
        grid=(L, N // tn),
        in_specs=[
            pl.BlockSpec((MOD_ROWS, D), lambda l, n: (0, 0)),
            pl.BlockSpec((1, D, tn), lambda l, n: (l, 0, n)),
            pl.BlockSpec((1, 1, tn), lambda l, n: (l, 0, n)),
        ],
        out_specs=pl.BlockSpec((1, MOD_ROWS, tn), lambda l, n: (l, 0, n)),
        compiler_params=_params("parallel", "parallel"),
        name="modulation",
    )(c_all, w, b)


def _pool_kernel(x_ref, mod_ref, g_ref, pre_ref, wp_ref, ps_ref, o_ref, np_ref, hbuf, s1, s2,
                 *, bb, tt, pos0):
    t = pl.program_id(1)
    G = POOL_GROUP
    LEAD, BASE = POOL_LEAD, POOL_BASE
    L = BASE + tt

    @pl.when(t == 0)
    def _():
        hbuf[:, 0:LEAD + 1, :] = jnp.zeros((bb, LEAD + 1, D_MODEL), F32)
        hbuf[:, LEAD + 1:BASE, :] = pre_ref[0]
        s1[:, 0:LEAD, :] = jnp.zeros((bb, LEAD, 3 * G), F32)
        s2[:, 0:LEAD, :] = jnp.zeros((bb, LEAD, 3 * G), F32)

    @pl.when(t > 0)
    def _():
        hbuf[:, LEAD:BASE, :] = hbuf[:, tt + LEAD:tt + BASE, :]

    x = x_ref[...]
    m = mod_ref[...]
    h = _norm_mod(x, g_ref[:, 0:1, :], m[:, 0:1, :], m[:, 1:2, :])
    hbuf[:, BASE:L, :] = h
    np_ref[0] = hbuf[:, L - POOL_STATE:L, :]

    s1[:, LEAD:L, :] = hbuf[:, LEAD:L, G:] + hbuf[:, LEAD - 1:L - 1, G:]
    s2[:, LEAD:L, :] = s1[:, LEAD:L, :] + s1[:, LEAD - 2:L - 2, :]
    s1[:, 2 * LEAD:L, 0:2 * G] = (s2[:, 2 * LEAD:L, G:] +
                                  s2[:, 2 * LEAD - 4:L - 4, G:])
    sums = [
        hbuf[:, BASE:L, 0:G] + hbuf[:, BASE - 1:L - 1, 0:G],
        s2[:, BASE:L, 0:G],
        s1[:, BASE:L, 0:G],
        s1[:, BASE:L, G:2 * G] + s1[:, BASE - 8:L - 8, G:2 * G],
    ]

    pos = pos0 + t * tt + lax.broadcasted_iota(jnp.int32, (1, tt, LANES), 1)
    ys = []
    for g, w in enumerate(POOL_WINDOWS):
        cs = slice(g * POOL_GROUP, (g + 1) * POOL_GROUP)
        inv_cnt = 1.0 / jnp.minimum(w, pos + 1).astype(F32)
        inv_cnt = jnp.concatenate([inv_cnt] * (POOL_GROUP // LANES), axis=-1)
        pooled = sums[g] * inv_cnt - hbuf[:, BASE:L, cs]
        ys.append(jnp.dot(pooled.reshape(bb * tt, POOL_GROUP).astype(BF16), wp_ref[0, g],
                          preferred_element_type=F32))
    y = jnp.concatenate(ys, axis=-1).reshape(bb, tt, D_MODEL) * ps_ref[...]
    o_ref[...] = x + _gated_norm(y, g_ref[:, 1:2, :], m[:, 2:3, :])


def _pool_layer(x, mod, g_norm, prefix, w_pool, pool_scale, l, *, bb, tt, pos0):
    B, T, D = x.shape
    kern = functools.partial(_pool_kernel, bb=bb, tt=tt, pos0=pos0)
    return pl.pallas_call(
        kern,
        out_shape=(jax.ShapeDtypeStruct((B, T, D), F32),
                   jax.ShapeDtypeStruct((1, B, POOL_STATE, D), F32)),
        grid=(B // bb, T // tt),
        in_specs=[
            pl.BlockSpec((bb, tt, D), lambda b, t: (b, t, 0)),
            pl.BlockSpec((bb, 6, D), lambda b, t: (b, 0, 0)),
            pl.BlockSpec((1, 4, D), lambda b, t: (l, 0, 0)),
            pl.BlockSpec((1, bb, POOL_STATE, D), lambda b, t: (l, b, 0, 0)),
            pl.BlockSpec((1, len(POOL_WINDOWS), POOL_GROUP, POOL_GROUP),
                         lambda b, t: (l, 0, 0, 0)),
            pl.BlockSpec((1, 1, D), lambda b, t: (l, 0, 0)),
        ],
        out_specs=(pl.BlockSpec((bb, tt, D), lambda b, t: (b, t, 0)),
                   pl.BlockSpec((1, bb, POOL_STATE, D), lambda b, t: (0, b, 0, 0))),
        scratch_shapes=[pltpu.VMEM((bb, POOL_BASE + tt, D), F32),
                        pltpu.VMEM((bb, POOL_BASE + tt, 3 * POOL_GROUP), F32),
                        pltpu.VMEM((bb, POOL_BASE + tt, 3 * POOL_GROUP), F32)],
        compiler_params=_params("parallel", "arbitrary"),
        name=f"pool_mixer_{l}",
    )(x, mod, g_norm, prefix, w_pool, pool_scale)


def _rope_tables(pos):
    half = HEAD_DIM // 2
    inv = ROPE_THETA ** (-jnp.arange(half, dtype=F32) / half)
    ang = pos.astype(F32)[:, None] * inv[None, :]
    cos, sin = jnp.cos(ang), jnp.sin(ang)
    zero = jnp.zeros_like(sin)
    c = jnp.tile(cos, (1, 4))
    s_lo = jnp.tile(jnp.concatenate([-sin, zero], axis=1), (1, 2))
    s_hi = jnp.tile(jnp.concatenate([zero, sin], axis=1), (1, 2))
    return c, s_lo, s_hi


def _rope_block(blk, c, s_lo, s_hi):
    return (blk * c + pltpu.roll(blk, LANES - HEAD_DIM // 2, 1) * s_lo
            + pltpu.roll(blk, HEAD_DIM // 2, 1) * s_hi)


def _qproj_kernel(x_ref, mod_ref, g_ref, c_ref, slo_ref, shi_ref, w_ref, q_ref, *, bb, tt):
    x = x_ref[...]
    m = mod_ref[...]
    h = _norm_mod(x, g_ref[:, 0:1, :], m[:, 0:1, :], m[:, 1:2, :])
    q = jnp.dot(h.reshape(bb * tt, D_MODEL).astype(BF16), w_ref[0], preferred_element_type=F32)
    c, s_lo, s_hi = c_ref[...], slo_ref[...], shi_ref[...]
    for p in range(N_PAIRS):
        cs = slice(p * LANES, (p + 1) * LANES)
        q_ref[p] = (_rope_block(q[:, cs], c, s_lo, s_hi) * ATTN_SCALE).astype(BF16)


def _qproj_layer(x, mod, g_norm, tables, w_q, l, j, *, bb, tt):
    B, T, D = x.shape
    nt = T // tt
    M = bb * tt
    kern = functools.partial(_qproj_kernel, bb=bb, tt=tt)
    tab_spec = pl.BlockSpec((M, LANES), lambda b, t: (t, 0))
    return pl.pallas_call(
        kern,
        out_shape=jax.ShapeDtypeStruct((N_PAIRS, B * T, LANES), BF16),
        grid=(B // bb, nt),
        in_specs=[
            pl.BlockSpec((bb, tt, D), lambda b, t: (b, t, 0)),
            pl.BlockSpec((bb, 6, D), lambda b, t: (b, 0, 0)),
            pl.BlockSpec((1, 4, D), lambda b, t: (l, 0, 0)),
            tab_spec, tab_spec, tab_spec,
            pl.BlockSpec((1, D, D), lambda b, t: (j, 0, 0)),
        ],
        out_specs=pl.BlockSpec((N_PAIRS, M, LANES), lambda b, t: (0, b * nt + t, 0)),
        compiler_params=_params("parallel", "parallel"),
        name=f"q_proj_{l}",
    )(x, mod, g_norm, *tables, w_q)


def _kv_kernel(x_ref, mod_ref, g_ref, c_ref, slo_ref, shi_ref, w_ref,
               k_ref, v_ref, kd_ref, vd_ref, *, bb, tt):
    x = x_ref[...]
    m = mod_ref[...]
    h = _norm_mod(x, g_ref[...], m[:, 0:1, :], m[:, 1:2, :])
    kv = jnp.dot(h.reshape(bb * tt, D_MODEL).astype(BF16), w_ref[...],
                 preferred_element_type=F32)
    c, s_lo, s_hi = c_ref[...], slo_ref[...], shi_ref[...]
    lo = lax.broadcasted_iota(jnp.int32, (bb * tt, LANES), 1) < HEAD_DIM
    zero = jnp.zeros((bb * tt, LANES), F32)
    n_blk = N_KV_HEADS // 2
    for p in range(n_blk):
        cs = slice(p * LANES, (p + 1) * LANES)
        k_blk = _rope_block(kv[:, cs], c, s_lo, s_hi)
        v_blk = kv[:, (n_blk + p) * LANES:(n_blk + p + 1) * LANES]
        k_ref[:, cs] = k_blk
        v_ref[:, cs] = v_blk
        for blk, dst in ((k_blk, kd_ref), (v_blk, vd_ref)):
            swapped = pltpu.roll(blk, HEAD_DIM, 1)
            parts = (jnp.where(lo, blk, zero), jnp.where(lo, zero, swapped),
                     jnp.where(lo, swapped, zero), jnp.where(lo, zero, blk))
            for i, part in enumerate(parts):
                dst[:, (4 * p + i) * LANES:(4 * p + i + 1) * LANES] = part.astype(BF16)


def _kv_layer(x, mod, g_kv, tables, w_kv, *, bb, tt):
    B, T, D = x.shape
    nt = T // tt
    M = bb * tt
    KV = N_KV_HEADS * HEAD_DIM
    kern = functools.partial(_kv_kernel, bb=bb, tt=tt)
    tab_spec = pl.BlockSpec((M, LANES), lambda b, t: (t, 0))
    row = lambda b, t: (b * nt + t, 0)
    newest = lambda b, t: (b, 0)
    return pl.pallas_call(
        kern,
        out_shape=(jax.ShapeDtypeStruct((B // bb * M, KV), F32),
                   jax.ShapeDtypeStruct((B // bb * M, KV), F32),
                   jax.ShapeDtypeStruct((B * T, 4 * KV), BF16),
                   jax.ShapeDtypeStruct((B * T, 4 * KV), BF16)),
        grid=(B // bb, nt),
        in_specs=[
            pl.BlockSpec((bb, tt, D), lambda b, t: (b, t, 0)),
            pl.BlockSpec((bb, 2, D), lambda b, t: (b, 0, 0)),
            pl.BlockSpec((1, 1, D), lambda b, t: (0, 0, 0)),
            tab_spec, tab_spec, tab_spec,
            pl.BlockSpec((D, 2 * KV), lambda b, t: (0, 0)),
        ],
        out_specs=(pl.BlockSpec((M, KV), newest), pl.BlockSpec((M, KV), newest),
                   pl.BlockSpec((M, 4 * KV), row), pl.BlockSpec((M, 4 * KV), row)),
        compiler_params=_params("parallel", "arbitrary"),
        name="shared_kv",
    )(x, mod, g_kv, *tables, w_kv)


def _sink_softmax(s, sink):
    m = jnp.max(s, axis=-1, keepdims=True)
    e = jnp.exp(s - m)
    den = jnp.sum(e, axis=-1, keepdims=True) + jnp.exp(sink - m)
    return e * (1.0 / den)


O_PIECE = 2 * LANES
O_ROWS = 2 * WINDOW


def _attn_kernel(sink_ref, *refs, tq, fused, n_valid=None, nq=None, ns=None):
    G = PAIRS_PER_KV
    ts = WINDOW if fused else tq
    nsub = tq // ts
    R = G * ts
    kp = lax.broadcasted_iota(jnp.int32, (R, KEY_SPAN), 1)
    rblk = lax.broadcasted_iota(jnp.int32, (R, 1), 0) // ts
    nt = (((1,), (1,)), ((), ()))
    sink_cols = []
    for j in range(N_KV_HEADS):
        for parity in range(2):
            col = jnp.full((R, 1), sink_ref[2 * G * j + parity], F32)
            for r in range(1, G):
                col = jnp.where(rblk == r, sink_ref[2 * (G * j + r) + parity], col)
            sink_cols.append(col)
    if fused:
        (q_ref, k_prev, k_cur, v_prev, v_cur, x_ref, mod_ref, g_ref, wo_ref, o_ref,
         pend_ref, mix_ref) = refs
        g = pl.program_id(0)
        first_in_stream = (g % nq) == 0
        slot = g % 2
        row = lax.broadcasted_iota(jnp.int32, (R, KEY_SPAN), 0)
        cq = (row % ts) // CHUNK
        kb = kp // CHUNK
        in_window = (kb >= cq) & (kb <= cq + WINDOW // CHUNK)
        n_pieces = (D_MODEL // O_PIECE) * (tq // O_ROWS)
        pieces_per_group = n_pieces / (nsub * N_KV_HEADS)
    else:
        q_ref, kd_ref, vd_ref, o_ref = refs

    def keys_of(sub):
        rs = slice(sub * ts, (sub + 1) * ts)
        if not fused:
            kw, vw = kd_ref[0], vd_ref[0]
            valid = kp < n_valid
        elif sub == 0:
            kw = jnp.concatenate([k_prev[0], k_cur[0, rs, :]], axis=0)
            vw = jnp.concatenate([v_prev[0], v_cur[0, rs, :]], axis=0)
            valid = in_window & (kp >= jnp.where(first_in_stream, WINDOW, 0))
        else:
            span = slice(sub * ts - WINDOW, (sub + 1) * ts)
            kw, vw = k_cur[0, span, :], v_cur[0, span, :]
            valid = in_window
        return rs, kw, vw, jnp.where(valid, 0.0, NEG_INF)

    def scores(keys, j):
        rs, kw, _, bias = keys
        qg = q_ref[G * j:G * (j + 1), rs, :].reshape(R, LANES)
        k_lo = kw[:, (2 * j) * LANES:(2 * j + 1) * LANES]
        k_hi = kw[:, (2 * j + 1) * LANES:(2 * j + 2) * LANES]
        return (lax.dot_general(qg, k_lo, nt, preferred_element_type=F32) + bias,
                lax.dot_general(qg, k_hi, nt, preferred_element_type=F32) + bias)

    def attend():
        groups = [(sub, j) for sub in range(nsub) for j in range(N_KV_HEADS)]
        keys = {0: keys_of(0)}
        ahead = scores(keys[0], 0)
        due, issued = 0.0, 0
        heads = []
        for n, (sub, j) in enumerate(groups):
            s0, s1 = ahead
            if n + 1 < len(groups):
                sub_n, j_n = groups[n + 1]
                if sub_n not in keys:
                    keys[sub_n] = keys_of(sub_n)
                ahead = scores(keys[sub_n], j_n)
            if fused:
                due += pieces_per_group
                while due >= 1.0:
                    project(issued)
                    issued += 1
                    due -= 1.0
            rs, _, vw, _ = keys[sub]
            v_lo = vw[:, (2 * j) * LANES:(2 * j + 1) * LANES]
            v_hi = vw[:, (2 * j + 1) * LANES:(2 * j + 2) * LANES]
            p0 = _sink_softmax(s0, sink_cols[2 * j])
            p1 = _sink_softmax(s1, sink_cols[2 * j + 1])
            o = (jnp.dot(p0.astype(BF16), v_lo, preferred_element_type=F32)
                 + jnp.dot(p1.astype(BF16), v_hi, preferred_element_type=F32)).astype(BF16)
            if not fused:
                o_ref[G * j:G * (j + 1), rs, :] = o.reshape(G, ts, LANES)
                continue
            heads.extend(o[r * ts:(r + 1) * ts] for r in range(G))
            if j == N_KV_HEADS - 1:
                pend_ref[slot, rs, :] = jnp.concatenate(heads, axis=1)
                heads = []
        if fused:
            finish()

    if not fused:
        attend()
        return

    def project(piece):
        rb, cb = divmod(piece, D_MODEL // O_PIECE)
        rows = slice(rb * O_ROWS, (rb + 1) * O_ROWS)
        cs = slice(cb * O_PIECE, (cb + 1) * O_PIECE)
        mix_ref[rows, cs] = jnp.dot(pend_ref[1 - slot, rows, :], wo_ref[0, :, cs],
                                    preferred_element_type=F32)

    def finish():
        m = mod_ref[0]
        o_ref[...] = x_ref[...] + _gated_norm(mix_ref[...], g_ref[0, 1:2, :], m[2:3, :])

    @pl.when(g == 0)
    def _():
        pend_ref[1] = jnp.zeros((tq, D_MODEL), BF16)

    @pl.when(g < ns)
    def _():
        attend()

    @pl.when(g == ns)
    def _():
        mix_ref[...] = jnp.dot(pend_ref[1 - slot], wo_ref[0], preferred_element_type=F32)
        finish()


def _attention(q, kd, vd, sinks, *, B, T, n_valid):
    W = kd.shape[2]
    assert kd.shape[1] == KEY_SPAN
    kern = functools.partial(_attn_kernel, tq=T, fused=False, n_valid=n_valid)
    whole = pl.BlockSpec((1, KEY_SPAN, W), lambda b: (b, 0, 0))
    pair_rows = pl.BlockSpec((N_PAIRS, T, LANES), lambda b: (0, b, 0))
    return pl.pallas_call(
        kern,
        out_shape=jax.ShapeDtypeStruct((N_PAIRS, B * T, LANES), BF16),
        grid=(B,),
        in_specs=[pl.BlockSpec(memory_space=pltpu.SMEM), pair_rows, whole, whole],
        out_specs=pair_rows,
        compiler_params=_params("parallel"),
        name="swa_attention",
    )(sinks, q, kd, vd)


def _attention_out(q, x, mod, g_norm, w_o, kd, vd, sinks, l, j, *, tq):
    B, T, D = x.shape
    W = kd.shape[2]
    assert tq % O_ROWS == 0 and kd.shape[1] == T
    nsub = tq // WINDOW
    nq = T // tq
    ns = B * nq
    kern = functools.partial(_attn_kernel, tq=tq, fused=True, nq=nq, ns=ns)
    cur_tile = lambda g: jnp.minimum(g, ns - 1)
    out_tile = lambda g: jnp.maximum(g - 1, 0)
    prev = pl.BlockSpec((1, WINDOW, W), lambda g: (
        cur_tile(g) // nq, jnp.maximum((cur_tile(g) % nq) * nsub - 1, 0), 0))
    cur = pl.BlockSpec((1, tq, W), lambda g: (cur_tile(g) // nq, cur_tile(g) % nq, 0))
    out = pl.pallas_call(
        kern,
        out_shape=jax.ShapeDtypeStruct((B * T, D), F32),
        grid=(ns + 1,),
        in_specs=[
            pl.BlockSpec(memory_space=pltpu.SMEM),
            pl.BlockSpec((N_PAIRS, tq, LANES), lambda g: (0, cur_tile(g), 0)),
            prev, cur, prev, cur,
            pl.BlockSpec((tq, D), lambda g: (out_tile(g), 0)),
            pl.BlockSpec((1, 6, D), lambda g: (out_tile(g) // nq, 0, 0)),
            pl.BlockSpec((1, 4, D), lambda g: (l, 0, 0)),
            pl.BlockSpec((1, D, D), lambda g: (j, 0, 0)),
        ],
        out_specs=pl.BlockSpec((tq, D), lambda g: (out_tile(g), 0)),
        scratch_shapes=[pltpu.VMEM((2, tq, D), BF16), pltpu.VMEM((tq, D), F32)],
        compiler_params=_params("arbitrary"),
        name=f"attention_out_{l}",
    )(sinks, q, kd, kd, vd, vd, x.reshape(B * T, D), mod, g_norm, w_o)
    return out.reshape(B, T, D)


def _oproj_kernel(o_ref, x_ref, mod_ref, g_ref, w_ref, y_ref, *, bb, tt):
    o = jnp.concatenate([o_ref[p] for p in range(N_PAIRS)], axis=1)
    mix = jnp.dot(o, w_ref[0], preferred_element_type=F32).reshape(bb, tt, D_MODEL)
    m = mod_ref[...]
    y_ref[...] = x_ref[...] + _gated_norm(mix, g_ref[:, 1:2, :], m[:, 2:3, :])


def _oproj_layer(o, x, mod, g_norm, w_o, l, j, *, bb, tt):
    B, T, D = x.shape
    nt = T // tt
    kern = functools.partial(_oproj_kernel, bb=bb, tt=tt)
    return pl.pallas_call(
        kern,
        out_shape=jax.ShapeDtypeStruct((B, T, D), F32),
        grid=(B // bb, nt),
        in_specs=[
            pl.BlockSpec((N_PAIRS, bb * tt, LANES), lambda b, t: (0, b * nt + t, 0)),
            pl.BlockSpec((bb, tt, D), lambda b, t: (b, t, 0)),
            pl.BlockSpec((bb, 6, D), lambda b, t: (b, 0, 0)),
            pl.BlockSpec((1, 4, D), lambda b, t: (l, 0, 0)),
            pl.BlockSpec((1, D, D), lambda b, t: (j, 0, 0)),
        ],
        out_specs=pl.BlockSpec((bb, tt, D), lambda b, t: (b, t, 0)),
        compiler_params=_params("parallel", "parallel"),
        name=f"o_proj_{l}",
    )(o, x, mod, g_norm, w_o)


def _mlp_kernel(x_ref, mod_ref, g_ref, wu_ref, wd_ref, y_ref, *rest, bb, tt, cast):
    f = pl.program_id(2)
    m = mod_ref[...]
    if cast:
        wu_out, wd_out, h_ref = rest
    else:
        h_ref, = rest

    @pl.when(f == 0)
    def _():
        h = _norm_mod(x_ref[...], g_ref[:, 2:3, :], m[:, 3:4, :], m[:, 4:5, :])
        h_ref[...] = h.reshape(bb * tt, D_MODEL).astype(BF16)
        y_ref[...] = jnp.zeros_like(y_ref)

    wu = wu_ref[0]
    if cast:
        wu = wu.astype(BF16)
        wu_out[0] = wu
    u = jnp.dot(h_ref[...], wu, preferred_element_type=F32)
    a = jnp.square(jnp.maximum(u, 0.0)).astype(BF16)
    for n in range(D_MODEL // MLP_OUT_CHUNK):
        cs = slice(n * MLP_OUT_CHUNK, (n + 1) * MLP_OUT_CHUNK)
        wd_n = wd_ref[0, :, cs]
        if cast:
            wd_n = wd_n.astype(BF16)
            wd_out[0, :, cs] = wd_n
        y_ref[:, :, cs] += jnp.dot(a, wd_n, preferred_element_type=F32).reshape(
            bb, tt, MLP_OUT_CHUNK)

    @pl.when(f == pl.num_programs(2) - 1)
    def _():
        y_ref[...] = x_ref[...] + _gated_norm(y_ref[...], g_ref[:, 3:4, :], m[:, 5:6, :])


def _mlp_layer(x, mod, g_norm, w_up, w_down, l, *, bb, tt, tff, cast):
    B, T, D = x.shape
    grid = (B // bb, T // tt, D_FF // tff)
    kern = functools.partial(_mlp_kernel, bb=bb, tt=tt, cast=cast)
    wl = l if cast else 0
    y_shape = jax.ShapeDtypeStruct((B, T, D), F32)
    y_spec = pl.BlockSpec((bb, tt, D), lambda b, t, f: (b, t, 0))
    if cast:
        assert grid[0] * grid[1] == 1
        out_shape = (y_shape, jax.ShapeDtypeStruct((1, D, D_FF), BF16),
                     jax.ShapeDtypeStruct((1, D_FF, D), BF16))
        out_specs = (y_spec, pl.BlockSpec((1, D, tff), lambda b, t, f: (0, 0, f)),
                     pl.BlockSpec((1, tff, D), lambda b, t, f: (0, f, 0)))
    else:
        out_shape, out_specs = y_shape, y_spec
    return pl.pallas_call(
        kern,
        out_shape=out_shape,
        grid=grid,
        in_specs=[
            pl.BlockSpec((bb, tt, D), lambda b, t, f: (b, t, 0)),
            pl.BlockSpec((bb, 6, D), lambda b, t, f: (b, 0, 0)),
            pl.BlockSpec((1, 4, D), lambda b, t, f: (l, 0, 0)),
            pl.BlockSpec((1, D, tff), lambda b, t, f: (wl, 0, f)),
            pl.BlockSpec((1, tff, D), lambda b, t, f: (wl, f, 0)),
        ],
        out_specs=out_specs,
        scratch_shapes=[pltpu.VMEM((bb * tt, D), BF16)],
        compiler_params=_params("parallel", "parallel", "arbitrary"),
        name=f"mlp_{l}",
    )(x, mod, g_norm, w_up, w_down)


def _zero_after(x):
    bits = pltpu.bitcast(x, jnp.uint32)
    r = bits[:, 0:LANES]
    for k in range(1, x.shape[1] // LANES):
        r = r | bits[:, k * LANES:(k + 1) * LANES]
    r8 = r[0:SUBLANES]
    for k in range(1, x.shape[0] // SUBLANES):
        r8 = r8 | r[k * SUBLANES:(k + 1) * SUBLANES]
    z = pltpu.bitcast((r8 >> 16) >> 16, F32)
    return jnp.max(z, axis=(0, 1), keepdims=True)


def _mlp_skew_kernel(xn_ref, xp_ref, modn_ref, modp_ref, g_ref, wu_ref, wd_ref, *rest, nt, rs,
                     cast_next):
    s = pl.program_id(0)
    f = pl.program_id(1)
    if cast_next:
        nu_ref, nd_ref, y_ref, nu_out, nd_out, h0_ref, h1_ref, acc0_ref, acc1_ref = rest
    else:
        y_ref, h0_ref, h1_ref, acc0_ref, acc1_ref = rest
    rows = pl.ds(pl.multiple_of(f * rs, rs), rs)
    g_pre, g_post = g_ref[0, 2:3, :], g_ref[0, 3:4, :]
    h_refs = (h0_ref, h1_ref)
    acc_refs = (acc0_ref, acc1_ref)

    def prologue(h_ref):
        m = modn_ref[0]
        h = _norm_mod(xn_ref[...], g_pre, m[3:4, :], m[4:5, :])
        h_ref[rows, :] = h.astype(BF16)
        return h

    def epilogue(acc_ref):
        mp = modp_ref[0]
        y = xp_ref[...] + _gated_norm(acc_ref[rows, :], g_post, mp[5:6, :])
        y_ref[...] = y
        acc_ref[rows, :] = jnp.zeros((rs, D_MODEL), F32)
        return y

    def matmuls(h_ref, acc_ref, anchors):
        if cast_next:
            nu_out[...] = nu_ref[...].astype(BF16)
            nd_out[...] = nd_ref[...].astype(BF16)
        u = jnp.dot(h_ref[...], wu_ref[0], preferred_element_type=F32)
        a = jnp.square(jnp.maximum(u, 0.0)).astype(BF16)
        for n in range(D_MODEL // MLP_OUT_CHUNK):
            cs = slice(n * MLP_OUT_CHUNK, (n + 1) * MLP_OUT_CHUNK)
            d = jnp.dot(a, wd_ref[0, :, cs], preferred_element_type=F32)
            if anchors.get(n) is not None:
                d = d + anchors[n]
            acc_ref[:, cs] += d

    @pl.when((s == 0) & (f == 0))
    def _():
        acc0_ref[...] = jnp.zeros_like(acc0_ref)
        acc1_ref[...] = jnp.zeros_like(acc1_ref)

    @pl.when(s == 0)
    def _():
        prologue(h0_ref)

    @pl.when(s == 1)
    def _():
        h = prologue(h1_ref)
        matmuls(h0_ref, acc0_ref, {2: _zero_after(h)})

    for c in range(2):
        o = 1 - c

        @pl.when(((s - 1) % 2 == c) & (s >= 2) & (s < nt))
        def _():
            y = epilogue(acc_refs[o])
            h = prologue(h_refs[o])
            matmuls(h_refs[c], acc_refs[c], {0: _zero_after(y), 2: _zero_after(h)})

    last = (nt - 1) % 2

    @pl.when(s == nt)
    def _():
        y = epilogue(acc_refs[1 - last])
        matmuls(h_refs[last], acc_refs[last], {0: _zero_after(y)})

    @pl.when(s == nt + 1)
    def _():
        epilogue(acc_refs[last])


def _mlp_layer_skewed(x, mod, g_norm, w_up, w_down, l, *, tm, tff, next_f32=None):
    B, T, D = x.shape
    nf = D_FF // tff
    rs = tm // nf
    tpb = T // tm
    nt = B * tpb
    assert nt >= 3
    x2 = x.reshape(B * T, D)
    nxt = lambda s: jnp.minimum(s, nt - 1)
    prv = lambda s: jnp.maximum(s - 2, 0)
    wf = lambda s, f: jnp.where(s == 0, 0, jnp.where(s == nt + 1, nf - 1, f))
    cast_next = next_f32 is not None
    kern = functools.partial(_mlp_skew_kernel, nt=nt, rs=rs, cast_next=cast_next)
    in_specs = [
        pl.BlockSpec((rs, D), lambda s, f: (nxt(s) * nf + f, 0)),
        pl.BlockSpec((rs, D), lambda s, f: (prv(s) * nf + f, 0)),
        pl.BlockSpec((1, 6, D), lambda s, f: (nxt(s) // tpb, 0, 0)),
        pl.BlockSpec((1, 6, D), lambda s, f: (prv(s) // tpb, 0, 0)),
        pl.BlockSpec((1, 4, D), lambda s, f: (l, 0, 0)),
        pl.BlockSpec((1, D, tff), lambda s, f: (0, 0, wf(s, f))),
        pl.BlockSpec((1, tff, D), lambda s, f: (0, wf(s, f), 0)),
    ]
    args = [x2, x2, mod, mod, g_norm, w_up, w_down]
    out_specs = [pl.BlockSpec((rs, D), lambda s, f: (jnp.where(s < 2, 0, (s - 2) * nf + f), 0))]
    out_shape = [jax.ShapeDtypeStruct((B * T, D), F32)]
    if cast_next:
        n_slabs = nt * nf
        ru, rd = D // n_slabs, D_FF // n_slabs
        slab = lambda s, f: jnp.clip((s - 1) * nf + f, 0, n_slabs - 1)
        up_spec = lambda: pl.BlockSpec((1, ru, D_FF), lambda s, f: (l + 1, slab(s, f), 0))
        dn_spec = lambda: pl.BlockSpec((1, rd, D), lambda s, f: (l + 1, slab(s, f), 0))
        in_specs += [up_spec(), dn_spec()]
        args += list(next_f32)
        out_specs += [pl.BlockSpec((1, ru, D_FF), lambda s, f: (0, slab(s, f), 0)),
                      pl.BlockSpec((1, rd, D), lambda s, f: (0, slab(s, f), 0))]
        out_shape += [jax.ShapeDtypeStruct((1, D, D_FF), BF16),
                      jax.ShapeDtypeStruct((1, D_FF, D), BF16)]
    outs = pl.pallas_call(
        kern,
        out_shape=tuple(out_shape),
        grid=(nt + 2, nf),
        in_specs=in_specs,
        out_specs=tuple(out_specs),
        scratch_shapes=[pltpu.VMEM((tm, D), BF16), pltpu.VMEM((tm, D), BF16),
                        pltpu.VMEM((tm, D), F32), pltpu.VMEM((tm, D), F32)],
        compiler_params=_params("arbitrary", "arbitrary"),
        name=f"mlp_skewed_{l}",
    )(*args)
    y = outs[0].reshape(B, T, D)
    return (y, outs[1], outs[2]) if cast_next else (y, None, None)


def _split_dup(a):
    z = jnp.zeros_like(a)
    lo = jnp.concatenate([a, z], axis=-1)
    hi = jnp.concatenate([z, a], axis=-1)
    out = jnp.stack([lo, hi], axis=3)
    return out.reshape(a.shape[0], a.shape[1], -1).astype(BF16)


class _Path:
    def __init__(self, x, mods, kvmod, pos0, prefix, kv_past, *, bb, tt, tq):
        self.x, self.mods, self.kvmod, self.pos0 = x, mods, kvmod, pos0
        self.prefix, self.kv_past = prefix, kv_past
        self.bb, self.tt, self.tq = bb, tt, tq
        pos = pos0 + jnp.arange(x.shape[1])
        self.tables = tuple(jnp.tile(tb, (bb, 1)) if bb > 1 else tb for tb in _rope_tables(pos))
        self.new_pool = []
        self.k = self.v = self.kd = self.vd = None

    def mixer(self, l, wts):
        B, T, _ = self.x.shape
        mod, bb, tt = self.mods[l], self.bb, self.tt
        if l < N_A_LAYERS:
            self.x, npool = _pool_layer(self.x, mod, wts["g_norm"], self.prefix, wts["w_pool"],
                                        wts["pool_scale"], l, bb=bb, tt=tt, pos0=self.pos0)
            self.new_pool.append(npool)
            return
        j = l - N_A_LAYERS
        q = _qproj_layer(self.x, mod, wts["g_norm"], self.tables, wts["w_q"], l, j, bb=bb, tt=tt)
        if self.kv_past is None:
            self.x = _attention_out(q, self.x, mod, wts["g_norm"], wts["w_o"], self.kd, self.vd,
                                    wts["sinks"][j], l, j, tq=self.tq)
        else:
            o = _attention(q, self.kd, self.vd, wts["sinks"][j], B=B, T=T,
                           n_valid=self.kv_past[0].shape[1] + T)
            self.x = _oproj_layer(o, self.x, mod, wts["g_norm"], wts["w_o"], l, j, bb=bb, tt=tt)

    def shared_kv(self, wts):
        B, T, _ = self.x.shape
        k, v, kd, vd = _kv_layer(self.x, self.kvmod, wts["g_kv"], self.tables, wts["w_kv"],
                                 bb=self.bb, tt=self.tt)
        W = kd.shape[-1]
        kd, vd = kd.reshape(B, T, W), vd.reshape(B, T, W)
        if self.kv_past is not None:
            back = ((0, 0), (0, KEY_SPAN - self.kv_past[0].shape[1] - T), (0, 0))
            kd = jnp.pad(jnp.concatenate([_split_dup(self.kv_past[0]), kd], axis=1), back)
            vd = jnp.pad(jnp.concatenate([_split_dup(self.kv_past[1]), vd], axis=1), back)
        KV = N_KV_HEADS * HEAD_DIM
        self.k, self.v = k.reshape(B, self.tt, KV), v.reshape(B, self.tt, KV)
        self.kd, self.vd = kd, vd


def _forward(prompt, sample, wts, *, tm, tff_prompt, tff_cast, tff_sample):
    w_f32 = (wts["w_up"], wts["w_down"])
    w_bf16 = None
    for l in range(DEPTH):
        sample.mixer(l, wts)
        if l == 0:
            sample.x, wu, wd = _mlp_layer(sample.x, sample.mods[l], wts["g_norm"], *w_f32, l,
                                          bb=sample.bb, tt=sample.tt, tff=tff_cast, cast=True)
            w_bf16 = (wu, wd)
        else:
            sample.x = _mlp_layer(sample.x, sample.mods[l], wts["g_norm"], *w_bf16, l,
                                  bb=sample.bb, tt=sample.tt, tff=tff_sample, cast=False)
        prompt.mixer(l, wts)
        prompt.x, wu, wd = _mlp_layer_skewed(
            prompt.x, prompt.mods[l], wts["g_norm"], *w_bf16, l, tm=tm, tff=tff_prompt,
            next_f32=w_f32 if l + 1 < DEPTH else None)
        w_bf16 = (wu, wd)
        if l == N_A_LAYERS - 1:
            sample.shared_kv(wts)
            prompt.shared_kv(wts)


def _prep_weights(g_norm, w_pool, pool_scale, g_kv, w_kv, w_q, sinks, w_o, w_up, w_down):
    D = D_MODEL
    return {
        "g_norm": g_norm,
        "w_pool": w_pool.astype(BF16),
        "pool_scale": pool_scale.reshape(N_A_LAYERS, 1, D),
        "g_kv": g_kv.reshape(1, 1, D),
        "w_kv": w_kv.astype(BF16),
        "w_q": w_q.astype(BF16),
        "sinks": sinks,
        "w_o": w_o.astype(BF16),
        "w_up": w_up,
        "w_down": w_down,
    }


def kernel(x_prompt, x_sample, c_prompt, c_sample, state_pool, cache_k, cache_v, w_mod, b_mod,
           g_norm, w_pool, pool_scale, w_kv_mod, b_kv_mod, g_kv, w_kv, w_q, sinks, w_o, w_up,
           w_down):
    Bp, Bs = x_prompt.shape[0], x_sample.shape[0]
    T_s = x_sample.shape[1]
    D = D_MODEL

    c_all = jnp.concatenate(
        [c_prompt, c_sample, jnp.zeros((MOD_ROWS - Bp - Bs, D), F32)], axis=0)
    mod_all = _modulation(c_all, w_mod, b_mod.reshape(DEPTH, 1, 6 * D)).reshape(
        DEPTH, MOD_ROWS, 6, D)
    kvmod_all = _modulation(c_all, w_kv_mod.reshape(1, D, 2 * D),
                            b_kv_mod.reshape(1, 1, 2 * D)).reshape(MOD_ROWS, 2, D)
    wts = _prep_weights(g_norm, w_pool, pool_scale, g_kv, w_kv, w_q, sinks, w_o, w_up, w_down)

    prompt = _Path(x_prompt, [mod_all[l, :Bp] for l in range(DEPTH)], kvmod_all[:Bp], 0,
                   jnp.zeros((N_A_LAYERS, Bp, POOL_STATE, D), F32), None,
                   bb=1, tt=512, tq=4 * WINDOW)
    sample = _Path(x_sample, [mod_all[l, Bp:Bp + Bs] for l in range(DEPTH)],
                   kvmod_all[Bp:Bp + Bs], PAST_LEN, state_pool, (cache_k, cache_v),
                   bb=Bs, tt=T_s, tq=T_s)
    _forward(prompt, sample, wts, tm=1024, tff_prompt=1024, tff_cast=512, tff_sample=1024)

    keep = min(WINDOW, x_prompt.shape[1])
    heads = (N_KV_HEADS, HEAD_DIM)
    return (prompt.x, sample.x,
            jnp.concatenate(prompt.new_pool, axis=0), jnp.concatenate(sample.new_pool, axis=0),
            prompt.k[:, -keep:].reshape(Bp, keep, *heads),
            prompt.v[:, -keep:].reshape(Bp, keep, *heads),
            sample.k.reshape(Bs, T_s, *heads), sample.v.reshape(Bs, T_s, *heads))
```

```python
import functools

import jax
import jax.numpy as jnp
from jax import lax
from jax.experimental import pallas as pl
from jax.experimental.pallas import tpu as pltpu

F32 = jnp.float32
BF16 = jnp.bfloat16

D_MODEL = 2048
DEPTH = 4
PAST_LEN = 4096
CHUNK = 64
N_A_LAYERS = DEPTH // 2
POOL_WINDOWS = (2, 4, 8, 16)
POOL_GROUP = D_MODEL // len(POOL_WINDOWS)
POOL_STATE = max(POOL_WINDOWS) - 1
SUBLANES = 8
POOL_LEAD = SUBLANES
POOL_BASE = POOL_LEAD + POOL_STATE + 1
HEAD_DIM = 64
N_HEADS = D_MODEL // HEAD_DIM
N_KV_HEADS = N_HEADS // 8
WINDOW = 128
D_FF = 4 * D_MODEL
ROPE_THETA = 10000.0
EPS = 1e-6
ATTN_SCALE = HEAD_DIM ** -0.5
NEG_INF = -1e30

LANES = 128
KEY_SPAN = 2 * LANES
N_PAIRS = N_HEADS // 2
PAIRS_PER_KV = N_PAIRS // N_KV_HEADS
MLP_OUT_CHUNK = 512
MOD_ROWS = 32
VMEM_LIMIT = 56 * 1024 * 1024


def _params(*sem):
    return pltpu.CompilerParams(dimension_semantics=sem, vmem_limit_bytes=VMEM_LIMIT)


def _norm_mod(x, g, shift, scale):
    ms = jnp.mean(x * x, axis=-1, keepdims=True)
    return (x * lax.rsqrt(ms + EPS)) * (g * (1.0 + scale)) + shift


def _gated_norm(y, g, gate):
    ms = jnp.mean(y * y, axis=-1, keepdims=True)
    return (y * lax.rsqrt(ms + EPS)) * (g * gate)


def _mod_kernel(c_ref, w_ref, b_ref, o_ref):
    c = c_ref[...]
    sc = (c * jax.nn.sigmoid(c)).astype(BF16)
    o_ref[0] = jnp.dot(sc, w_ref[0].astype(BF16), preferred_element_type=F32) + b_ref[0]


def _modulation(c_all, w, b, tn=2048):
    L, D, N = w.shape
    return pl.pallas_call(
        _mod_kernel,
        out_shape=jax.ShapeDtypeStruct((L, MOD_ROWS, N), F32),
        grid=(L, N // tn),
        in_specs=[
            pl.BlockSpec((MOD_ROWS, D), lambda l, n: (0, 0)),
            pl.BlockSpec((1, D, tn), lambda l, n: (l, 0, n)),
            pl.BlockSpec((1, 1, tn), lambda l, n: (l, 0, n)),
        ],
        out_specs=pl.BlockSpec((1, MOD_ROWS, tn), lambda l, n: (l, 0, n)),
        compiler_params=_params("parallel", "parallel"),
        name="modulation",
    )(c_all, w, b)


def _pool_kernel(x_ref, mod_ref, g_ref, pre_ref, wp_ref, ps_ref, o_ref, np_ref, hbuf, s1, s2,
                 *, bb, tt, pos0):
    t = pl.program_id(1)
    G = POOL_GROUP
    LEAD, BASE = POOL_LEAD, POOL_BASE
    L = BASE + tt

    @pl.when(t == 0)
    def _():
        hbuf[:, 0:LEAD + 1, :] = jnp.zeros((bb, LEAD + 1, D_MODEL), F32)
        hbuf[:, LEAD + 1:BASE, :] = pre_ref[0]
        s1[:, 0:LEAD, :] = jnp.zeros((bb, LEAD, 3 * G), F32)
        s2[:, 0:LEAD, :] = jnp.zeros((bb, LEAD, 3 * G), F32)

    @pl.when(t > 0)
    def _():
        hbuf[:, LEAD:BASE, :] = hbuf[:, tt + LEAD:tt + BASE, :]

    x = x_ref[...]
    m = mod_ref[...]
    h = _norm_mod(x, g_ref[:, 0:1, :], m[:, 0:1, :], m[:, 1:2, :])
    hbuf[:, BASE:L, :] = h
    np_ref[0] = hbuf[:, L - POOL_STATE:L, :]

    s1[:, LEAD:L, :] = hbuf[:, LEAD:L, G:] + hbuf[:, LEAD - 1:L - 1, G:]
    s2[:, LEAD:L, :] = s1[:, LEAD:L, :] + s1[:, LEAD - 2:L - 2, :]
    s1[:, 2 * LEAD:L, 0:2 * G] = (s2[:, 2 * LEAD:L, G:] +
                                  s2[:, 2 * LEAD - 4:L - 4, G:])
    sums = [
        hbuf[:, BASE:L, 0:G] + hbuf[:, BASE - 1:L - 1, 0:G],
        s2[:, BASE:L, 0:G],
        s1[:, BASE:L, 0:G],
        s1[:, BASE:L, G:2 * G] + s1[:, BASE - 8:L - 8, G:2 * G],
    ]

    pos = pos0 + t * tt + lax.broadcasted_iota(jnp.int32, (1, tt, LANES), 1)
    ys = []
    for g, w in enumerate(POOL_WINDOWS):
        cs = slice(g * POOL_GROUP, (g + 1) * POOL_GROUP)
        inv_cnt = 1.0 / jnp.minimum(w, pos + 1).astype(F32)
        inv_cnt = jnp.concatenate([inv_cnt] * (POOL_GROUP // LANES), axis=-1)
        pooled = sums[g] * inv_cnt - hbuf[:, BASE:L, cs]
        ys.append(jnp.dot(pooled.reshape(bb * tt, POOL_GROUP).astype(BF16), wp_ref[0, g],
                          preferred_element_type=F32))
    y = jnp.concatenate(ys, axis=-1).reshape(bb, tt, D_MODEL) * ps_ref[...]
    o_ref[...] = x + _gated_norm(y, g_ref[:, 1:2, :], m[:, 2:3, :])


def _pool_layer(x, mod, g_norm, prefix, w_pool, pool_scale, l, *, bb, tt, pos0):
    B, T, D = x.shape
    kern = functools.partial(_pool_kernel, bb=bb, tt=tt, pos0=pos0)
    return pl.pallas_call(
        kern,
        out_shape=(jax.ShapeDtypeStruct((B, T, D), F32),
                   jax.ShapeDtypeStruct((1, B, POOL_STATE, D), F32)),
        grid=(B // bb, T // tt),
        in_specs=[
            pl.BlockSpec((bb, tt, D), lambda b, t: (b, t, 0)),
            pl.BlockSpec((bb, 6, D), lambda b, t: (b, 0, 0)),
            pl.BlockSpec((1, 4, D), lambda b, t: (l, 0, 0)),
            pl.BlockSpec((1, bb, POOL_STATE, D), lambda b, t: (l, b, 0, 0)),
            pl.BlockSpec((1, len(POOL_WINDOWS), POOL_GROUP, POOL_GROUP),
                         lambda b, t: (l, 0, 0, 0)),
            pl.BlockSpec((1, 1, D), lambda b, t: (l, 0, 0)),
        ],
        out_specs=(pl.BlockSpec((bb, tt, D), lambda b, t: (b, t, 0)),
                   pl.BlockSpec((1, bb, POOL_STATE, D), lambda b, t: (0, b, 0, 0))),
        scratch_shapes=[pltpu.VMEM((bb, POOL_BASE + tt, D), F32),
                        pltpu.VMEM((bb, POOL_BASE + tt, 3 * POOL_GROUP), F32),
                        pltpu.VMEM((bb, POOL_BASE + tt, 3 * POOL_GROUP), F32)],
        compiler_params=_params("parallel", "arbitrary"),
        name=f"pool_mixer_{l}",
    )(x, mod, g_norm, prefix, w_pool, pool_scale)


def _rope_tables(pos):
    half = HEAD_DIM // 2
    inv = ROPE_THETA ** (-jnp.arange(half, dtype=F32) / half)
    ang = pos.astype(F32)[:, None] * inv[None, :]
    cos, sin = jnp.cos(ang), jnp.sin(ang)
    zero = jnp.zeros_like(sin)
    c = jnp.tile(cos, (1, 4))
    s_lo = jnp.tile(jnp.concatenate([-sin, zero], axis=1), (1, 2))
    s_hi = jnp.tile(jnp.concatenate([zero, sin], axis=1), (1, 2))
    return c, s_lo, s_hi


def _rope_block(blk, c, s_lo, s_hi):
    return (blk * c + pltpu.roll(blk, LANES - HEAD_DIM // 2, 1) * s_lo
            + pltpu.roll(blk, HEAD_DIM // 2, 1) * s_hi)


def _qproj_kernel(x_ref, mod_ref, g_ref, c_ref, slo_ref, shi_ref, w_ref, q_ref, *, bb, tt):
    x = x_ref[...]
    m = mod_ref[...]
    h = _norm_mod(x, g_ref[:, 0:1, :], m[:, 0:1, :], m[:, 1:2, :])
    q = jnp.dot(h.reshape(bb * tt, D_MODEL).astype(BF16), w_ref[0], preferred_element_type=F32)
    c, s_lo, s_hi = c_ref[...], slo_ref[...], shi_ref[...]
    for p in range(N_PAIRS):
        cs = slice(p * LANES, (p + 1) * LANES)
        q_ref[p] = (_rope_block(q[:, cs], c, s_lo, s_hi) * ATTN_SCALE).astype(BF16)


def _qproj_layer(x, mod, g_norm, tables, w_q, l, j, *, bb, tt):
    B, T, D = x.shape
    nt = T // tt
    M = bb * tt
    kern = functools.partial(_qproj_kernel, bb=bb, tt=tt)
    tab_spec = pl.BlockSpec((M, LANES), lambda b, t: (t, 0))
    return pl.pallas_call(
        kern,
        out_shape=jax.ShapeDtypeStruct((N_PAIRS, B * T, LANES), BF16),
        grid=(B // bb, nt),
        in_specs=[
            pl.BlockSpec((bb, tt, D), lambda b, t: (b, t, 0)),
            pl.BlockSpec((bb, 6, D), lambda b, t: (b, 0, 0)),
            pl.BlockSpec((1, 4, D), lambda b, t: (l, 0, 0)),
            tab_spec, tab_spec, tab_spec,
            pl.BlockSpec((1, D, D), lambda b, t: (j, 0, 0)),
        ],
        out_specs=pl.BlockSpec((N_PAIRS, M, LANES), lambda b, t: (0, b * nt + t, 0)),
        compiler_params=_params("parallel", "parallel"),
        name=f"q_proj_{l}",
    )(x, mod, g_norm, *tables, w_q)


def _kv_kernel(x_ref, mod_ref, g_ref, c_ref, slo_ref, shi_ref, w_ref,
               k_ref, v_ref, kd_ref, vd_ref, *, bb, tt):
    x = x_ref[...]
    m = mod_ref[...]
    h = _norm_mod(x, g_ref[...], m[:, 0:1, :], m[:, 1:2, :])
    kv = jnp.dot(h.reshape(bb * tt, D_MODEL).astype(BF16), w_ref[...],
                 preferred_element_type=F32)
    c, s_lo, s_hi = c_ref[...], slo_ref[...], shi_ref[...]
    lo = lax.broadcasted_iota(jnp.int32, (bb * tt, LANES), 1) < HEAD_DIM
    zero = jnp.zeros((bb * tt, LANES), F32)
    n_blk = N_KV_HEADS // 2
    for p in range(n_blk):
        cs = slice(p * LANES, (p + 1) * LANES)
        k_blk = _rope_block(kv[:, cs], c, s_lo, s_hi)
        v_blk = kv[:, (n_blk + p) * LANES:(n_blk + p + 1) * LANES]
        k_ref[:, cs] = k_blk
        v_ref[:, cs] = v_blk
        for blk, dst in ((k_blk, kd_ref), (v_blk, vd_ref)):
            swapped = pltpu.roll(blk, HEAD_DIM, 1)
            parts = (jnp.where(lo, blk, zero), jnp.where(lo, zero, swapped),
                     jnp.where(lo, swapped, zero), jnp.where(lo, zero, blk))
            for i, part in enumerate(parts):
                dst[:, (4 * p + i) * LANES:(4 * p + i + 1) * LANES] = part.astype(BF16)


def _kv_layer(x, mod, g_kv, tables, w_kv, *, bb, tt):
    B, T, D = x.shape
    nt = T // tt
    M = bb * tt
    KV = N_KV_HEADS * HEAD_DIM
    kern = functools.partial(_kv_kernel, bb=bb, tt=tt)
    tab_spec = pl.BlockSpec((M, LANES), lambda b, t: (t, 0))
    row = lambda b, t: (b * nt + t, 0)
    newest = lambda b, t: (b, 0)
    return pl.pallas_call(
        kern,
        out_shape=(jax.ShapeDtypeStruct((B // bb * M, KV), F32),
                   jax.ShapeDtypeStruct((B // bb * M, KV), F32),
                   jax.ShapeDtypeStruct((B * T, 4 * KV), BF16),
                   jax.ShapeDtypeStruct((B * T, 4 * KV), BF16)),
        grid=(B // bb, nt),
        in_specs=[
            pl.BlockSpec((bb, tt, D), lambda b, t: (b, t, 0)),
            pl.BlockSpec((bb, 2, D), lambda b, t: (b, 0, 0)),
            pl.BlockSpec((1, 1, D), lambda b, t: (0, 0, 0)),
            tab_spec, tab_spec, tab_spec,
            pl.BlockSpec((D, 2 * KV), lambda b, t: (0, 0)),
        ],
        out_specs=(pl.BlockSpec((M, KV), newest), pl.BlockSpec((M, KV), newest),
                   pl.BlockSpec((M, 4 * KV), row), pl.BlockSpec((M, 4 * KV), row)),
        compiler_params=_params("parallel", "arbitrary"),
        name="shared_kv",
    )(x, mod, g_kv, *tables, w_kv)


def _sink_softmax(s, sink):
    m = jnp.max(s, axis=-1, keepdims=True)
    e = jnp.exp(s - m)
    den = jnp.sum(e, axis=-1, keepdims=True) + jnp.exp(sink - m)
    return e * (1.0 / den)


def _attn_kernel(sink_ref, *refs, tq, chunked, n_valid, fused):
    i = pl.program_id(1)
    G = PAIRS_PER_KV
    ts = WINDOW if chunked else tq
    nsub = tq // ts
    R = G * ts
    QW = D_MODEL // N_KV_HEADS
    kp = lax.broadcasted_iota(jnp.int32, (R, KEY_SPAN), 1)
    rblk = lax.broadcasted_iota(jnp.int32, (R, 1), 0) // ts
    nt = (((1,), (1,)), ((), ()))
    sink_cols = []
    for j in range(N_KV_HEADS):
        for parity in range(2):
            col = jnp.full((R, 1), sink_ref[2 * G * j + parity], F32)
            for r in range(1, G):
                col = jnp.where(rblk == r, sink_ref[2 * (G * j + r) + parity], col)
            sink_cols.append(col)
    if fused:
        q_ref, k_prev, k_cur, v_prev, v_cur, x_ref, mod_ref, g_ref, wo_ref, o_ref = refs
    elif chunked:
        q_ref, k_prev, k_cur, v_prev, v_cur, o_ref = refs
    else:
        q_ref, kd_ref, vd_ref, o_ref = refs
    if chunked:
        row = lax.broadcasted_iota(jnp.int32, (R, KEY_SPAN), 0)
        cq = (row % ts) // CHUNK
        kb = kp // CHUNK
        in_window = (kb >= cq) & (kb <= cq + WINDOW // CHUNK)

    def finish(rows, mix):
        m = mod_ref[0]
        o_ref[rows, :] = x_ref[rows, :] + _gated_norm(mix, g_ref[0, 1:2, :], m[2:3, :])

    def keys_of(sub):
        rs = slice(sub * ts, (sub + 1) * ts)
        if not chunked:
            kw, vw = kd_ref[0], vd_ref[0]
            valid = kp < n_valid
        elif sub == 0:
            kw = jnp.concatenate([k_prev[0], k_cur[0, rs, :]], axis=0)
            vw = jnp.concatenate([v_prev[0], v_cur[0, rs, :]], axis=0)
            valid = in_window & (kp >= jnp.where(i > 0, 0, WINDOW))
        else:
            span = slice(sub * ts - WINDOW, (sub + 1) * ts)
            kw, vw = k_cur[0, span, :], v_cur[0, span, :]
            valid = in_window
        return rs, kw, vw, jnp.where(valid, 0.0, NEG_INF)

    def scores(keys, j):
        rs, kw, _, bias = keys
        qg = q_ref[G * j:G * (j + 1), rs, :].reshape(R, LANES)
        k_lo = kw[:, (2 * j) * LANES:(2 * j + 1) * LANES]
        k_hi = kw[:, (2 * j + 1) * LANES:(2 * j + 2) * LANES]
        return (lax.dot_general(qg, k_lo, nt, preferred_element_type=F32) + bias,
                lax.dot_general(qg, k_hi, nt, preferred_element_type=F32) + bias)

    groups = [(sub, j) for sub in range(nsub) for j in range(N_KV_HEADS)]
    keys = {0: keys_of(0)}
    ahead = scores(keys[0], 0)
    pending = None
    heads, mix_parts = [], []
    for n, (sub, j) in enumerate(groups):
        s0, s1 = ahead
        if n + 1 < len(groups):
            sub_n, j_n = groups[n + 1]
            if sub_n not in keys:
                keys[sub_n] = keys_of(sub_n)
            ahead = scores(keys[sub_n], j_n)
        if pending is not None:
            mix_parts.append(jnp.dot(pending[1], wo_ref[0, :, j * QW:(j + 1) * QW],
                                     preferred_element_type=F32))
        rs, _, vw, _ = keys[sub]
        v_lo = vw[:, (2 * j) * LANES:(2 * j + 1) * LANES]
        v_hi = vw[:, (2 * j + 1) * LANES:(2 * j + 2) * LANES]
        p0 = _sink_softmax(s0, sink_cols[2 * j])
        p1 = _sink_softmax(s1, sink_cols[2 * j + 1])
        o = (jnp.dot(p0.astype(BF16), v_lo, preferred_element_type=F32)
             + jnp.dot(p1.astype(BF16), v_hi, preferred_element_type=F32)).astype(BF16)
        if not fused:
            o_ref[G * j:G * (j + 1), rs, :] = o.reshape(G, ts, LANES)
            continue
        heads.extend(o[r * ts:(r + 1) * ts] for r in range(G))
        if j == N_KV_HEADS - 1:
            if pending is not None:
                finish(pending[0], jnp.concatenate(mix_parts, axis=1))
            pending = (rs, jnp.concatenate(heads, axis=1))
            heads, mix_parts = [], []
    if fused:
        finish(pending[0], jnp.dot(pending[1], wo_ref[0], preferred_element_type=F32))


def _attention(q, kd, vd, sinks, *, B, T, tq, chunked, n_valid):
    nq = T // tq
    Tk = kd.shape[1]
    W = kd.shape[2]
    kern = functools.partial(_attn_kernel, tq=tq, chunked=chunked, n_valid=n_valid, fused=False)
    if chunked:
        kv_specs, kv_args = _window_specs(kd, vd, T, tq)
    else:
        assert Tk == KEY_SPAN and nq == 1
        whole = pl.BlockSpec((1, Tk, W), lambda b, i: (b, 0, 0))
        kv_specs, kv_args = [whole, whole], (kd, vd)
    pair_rows = pl.BlockSpec((N_PAIRS, tq, LANES), lambda b, i: (0, b * nq + i, 0))
    return pl.pallas_call(
        kern,
        out_shape=jax.ShapeDtypeStruct((N_PAIRS, B * T, LANES), BF16),
        grid=(B, nq),
        in_specs=[pl.BlockSpec(memory_space=pltpu.SMEM), pair_rows, *kv_specs],
        out_specs=pair_rows,
        compiler_params=_params("parallel", "parallel"),
        name="swa_attention",
    )(sinks, q, *kv_args)


def _window_specs(kd, vd, T, tq):
    W = kd.shape[2]
    assert tq % WINDOW == 0 and kd.shape[1] == T
    nsub = tq // WINDOW
    prev = pl.BlockSpec((1, WINDOW, W), lambda b, i: (b, jnp.maximum(i * nsub - 1, 0), 0))
    cur = pl.BlockSpec((1, tq, W), lambda b, i: (b, i, 0))
    return [prev, cur, prev, cur], (kd, kd, vd, vd)


def _attention_out(q, x, mod, g_norm, w_o, kd, vd, sinks, l, j, *, tq):
    B, T, D = x.shape
    nq = T // tq
    kern = functools.partial(_attn_kernel, tq=tq, chunked=True, n_valid=None, fused=True)
    kv_specs, kv_args = _window_specs(kd, vd, T, tq)
    rows = lambda b, i: (b * nq + i, 0)
    out = pl.pallas_call(
        kern,
        out_shape=jax.ShapeDtypeStruct((B * T, D), F32),
        grid=(B, nq),
        in_specs=[
            pl.BlockSpec(memory_space=pltpu.SMEM),
            pl.BlockSpec((N_PAIRS, tq, LANES), lambda b, i: (0, b * nq + i, 0)),
            *kv_specs,
            pl.BlockSpec((tq, D), rows),
            pl.BlockSpec((1, 6, D), lambda b, i: (b, 0, 0)),
            pl.BlockSpec((1, 4, D), lambda b, i: (l, 0, 0)),
            pl.BlockSpec((1, D, D), lambda b, i: (j, 0, 0)),
        ],
        out_specs=pl.BlockSpec((tq, D), rows),
        compiler_params=_params("parallel", "parallel"),
        name=f"attention_out_{l}",
    )(sinks, q, *kv_args, x.reshape(B * T, D), mod, g_norm, w_o)
    return out.reshape(B, T, D)


def _oproj_kernel(o_ref, x_ref, mod_ref, g_ref, w_ref, y_ref, *, bb, tt):
    o = jnp.concatenate([o_ref[p] for p in range(N_PAIRS)], axis=1)
    mix = jnp.dot(o, w_ref[0], preferred_element_type=F32).reshape(bb, tt, D_MODEL)
    m = mod_ref[...]
    y_ref[...] = x_ref[...] + _gated_norm(mix, g_ref[:, 1:2, :], m[:, 2:3, :])


def _oproj_layer(o, x, mod, g_norm, w_o, l, j, *, bb, tt):
    B, T, D = x.shape
    nt = T // tt
    kern = functools.partial(_oproj_kernel, bb=bb, tt=tt)
    return pl.pallas_call(
        kern,
        out_shape=jax.ShapeDtypeStruct((B, T, D), F32),
        grid=(B // bb, nt),
        in_specs=[
            pl.BlockSpec((N_PAIRS, bb * tt, LANES), lambda b, t: (0, b * nt + t, 0)),
            pl.BlockSpec((bb, tt, D), lambda b, t: (b, t, 0)),
            pl.BlockSpec((bb, 6, D), lambda b, t: (b, 0, 0)),
            pl.BlockSpec((1, 4, D), lambda b, t: (l, 0, 0)),
            pl.BlockSpec((1, D, D), lambda b, t: (j, 0, 0)),
        ],
        out_specs=pl.BlockSpec((bb, tt, D), lambda b, t: (b, t, 0)),
        compiler_params=_params("parallel", "parallel"),
        name=f"o_proj_{l}",
    )(o, x, mod, g_norm, w_o)


def _mlp_kernel(x_ref, mod_ref, g_ref, wu_ref, wd_ref, y_ref, *rest, bb, tt, cast):
    f = pl.program_id(2)
    m = mod_ref[...]
    if cast:
        wu_out, wd_out, h_ref = rest
    else:
        h_ref, = rest

    @pl.when(f == 0)
    def _():
        h = _norm_mod(x_ref[...], g_ref[:, 2:3, :], m[:, 3:4, :], m[:, 4:5, :])
        h_ref[...] = h.reshape(bb * tt, D_MODEL).astype(BF16)
        y_ref[...] = jnp.zeros_like(y_ref)

    wu = wu_ref[0]
    if cast:
        wu = wu.astype(BF16)
        wu_out[0] = wu
    u = jnp.dot(h_ref[...], wu, preferred_element_type=F32)
    a = jnp.square(jnp.maximum(u, 0.0)).astype(BF16)
    for n in range(D_MODEL // MLP_OUT_CHUNK):
        cs = slice(n * MLP_OUT_CHUNK, (n + 1) * MLP_OUT_CHUNK)
        wd_n = wd_ref[0, :, cs]
        if cast:
            wd_n = wd_n.astype(BF16)
            wd_out[0, :, cs] = wd_n
        y_ref[:, :, cs] += jnp.dot(a, wd_n, preferred_element_type=F32).reshape(
            bb, tt, MLP_OUT_CHUNK)

    @pl.when(f == pl.num_programs(2) - 1)
    def _():
        y_ref[...] = x_ref[...] + _gated_norm(y_ref[...], g_ref[:, 3:4, :], m[:, 5:6, :])


def _mlp_layer(x, mod, g_norm, w_up, w_down, l, *, bb, tt, tff, cast):
    B, T, D = x.shape
    grid = (B // bb, T // tt, D_FF // tff)
    kern = functools.partial(_mlp_kernel, bb=bb, tt=tt, cast=cast)
    wl = l if cast else 0
    y_shape = jax.ShapeDtypeStruct((B, T, D), F32)
    y_spec = pl.BlockSpec((bb, tt, D), lambda b, t, f: (b, t, 0))
    if cast:
        assert grid[0] * grid[1] == 1
        out_shape = (y_shape, jax.ShapeDtypeStruct((1, D, D_FF), BF16),
                     jax.ShapeDtypeStruct((1, D_FF, D), BF16))
        out_specs = (y_spec, pl.BlockSpec((1, D, tff), lambda b, t, f: (0, 0, f)),
                     pl.BlockSpec((1, tff, D), lambda b, t, f: (0, f, 0)))
    else:
        out_shape, out_specs = y_shape, y_spec
    return pl.pallas_call(
        kern,
        out_shape=out_shape,
        grid=grid,
        in_specs=[
            pl.BlockSpec((bb, tt, D), lambda b, t, f: (b, t, 0)),
            pl.BlockSpec((bb, 6, D), lambda b, t, f: (b, 0, 0)),
            pl.BlockSpec((1, 4, D), lambda b, t, f: (l, 0, 0)),
            pl.BlockSpec((1, D, tff), lambda b, t, f: (wl, 0, f)),
            pl.BlockSpec((1, tff, D), lambda b, t, f: (wl, f, 0)),
        ],
        out_specs=out_specs,
        scratch_shapes=[pltpu.VMEM((bb * tt, D), BF16)],
        compiler_params=_params("parallel", "parallel", "arbitrary"),
        name=f"mlp_{l}",
    )(x, mod, g_norm, w_up, w_down)


def _zero_after(x):
    bits = pltpu.bitcast(x, jnp.uint32)
    r = bits[:, 0:LANES]
    for k in range(1, x.shape[1] // LANES):
        r = r | bits[:, k * LANES:(k + 1) * LANES]
    r8 = r[0:SUBLANES]
    for k in range(1, x.shape[0] // SUBLANES):
        r8 = r8 | r[k * SUBLANES:(k + 1) * SUBLANES]
    z = pltpu.bitcast((r8 >> 16) >> 16, F32)
    return jnp.max(z, axis=(0, 1), keepdims=True)


def _mlp_skew_kernel(xn_ref, xp_ref, modn_ref, modp_ref, g_ref, wu_ref, wd_ref, *rest, nt, rs,
                     cast_next):
    s = pl.program_id(0)
    f = pl.program_id(1)
    if cast_next:
        nu_ref, nd_ref, y_ref, nu_out, nd_out, h0_ref, h1_ref, acc0_ref, acc1_ref = rest
    else:
        y_ref, h0_ref, h1_ref, acc0_ref, acc1_ref = rest
    rows = pl.ds(pl.multiple_of(f * rs, rs), rs)
    g_pre, g_post = g_ref[0, 2:3, :], g_ref[0, 3:4, :]
    h_refs = (h0_ref, h1_ref)
    acc_refs = (acc0_ref, acc1_ref)

    def prologue(h_ref):
        m = modn_ref[0]
        h = _norm_mod(xn_ref[...], g_pre, m[3:4, :], m[4:5, :])
        h_ref[rows, :] = h.astype(BF16)
        return h

    def epilogue(acc_ref):
        mp = modp_ref[0]
        y = xp_ref[...] + _gated_norm(acc_ref[rows, :], g_post, mp[5:6, :])
        y_ref[...] = y
        acc_ref[rows, :] = jnp.zeros((rs, D_MODEL), F32)
        return y

    def matmuls(h_ref, acc_ref, anchors):
        if cast_next:
            nu_out[...] = nu_ref[...].astype(BF16)
            nd_out[...] = nd_ref[...].astype(BF16)
        u = jnp.dot(h_ref[...], wu_ref[0], preferred_element_type=F32)
        a = jnp.square(jnp.maximum(u, 0.0)).astype(BF16)
        for n in range(D_MODEL // MLP_OUT_CHUNK):
            cs = slice(n * MLP_OUT_CHUNK, (n + 1) * MLP_OUT_CHUNK)
            d = jnp.dot(a, wd_ref[0, :, cs], preferred_element_type=F32)
            if anchors.get(n) is not None:
                d = d + anchors[n]
            acc_ref[:, cs] += d

    @pl.when((s == 0) & (f == 0))
    def _():
        acc0_ref[...] = jnp.zeros_like(acc0_ref)
        acc1_ref[...] = jnp.zeros_like(acc1_ref)

    @pl.when(s == 0)
    def _():
        prologue(h0_ref)

    @pl.when(s == 1)
    def _():
        h = prologue(h1_ref)
        matmuls(h0_ref, acc0_ref, {2: _zero_after(h)})

    for c in range(2):
        o = 1 - c

        @pl.when(((s - 1) % 2 == c) & (s >= 2) & (s < nt))
        def _():
            y = epilogue(acc_refs[o])
            h = prologue(h_refs[o])
            matmuls(h_refs[c], acc_refs[c], {0: _zero_after(y), 2: _zero_after(h)})

    last = (nt - 1) % 2

    @pl.when(s == nt)
    def _():
        y = epilogue(acc_refs[1 - last])
        matmuls(h_refs[last], acc_refs[last], {0: _zero_after(y)})

    @pl.when(s == nt + 1)
    def _():
        epilogue(acc_refs[last])


def _mlp_layer_skewed(x, mod, g_norm, w_up, w_down, l, *, tm, tff, next_f32=None):
    B, T, D = x.shape
    nf = D_FF // tff
    rs = tm // nf
    tpb = T // tm
    nt = B * tpb
    assert nt >= 3
    x2 = x.reshape(B * T, D)
    nxt = lambda s: jnp.minimum(s, nt - 1)
    prv = lambda s: jnp.maximum(s - 2, 0)
    wf = lambda s, f: jnp.where(s == 0, 0, jnp.where(s == nt + 1, nf - 1, f))
    cast_next = next_f32 is not None
    kern = functools.partial(_mlp_skew_kernel, nt=nt, rs=rs, cast_next=cast_next)
    in_specs = [
        pl.BlockSpec((rs, D), lambda s, f: (nxt(s) * nf + f, 0)),
        pl.BlockSpec((rs, D), lambda s, f: (prv(s) * nf + f, 0)),
        pl.BlockSpec((1, 6, D), lambda s, f: (nxt(s) // tpb, 0, 0)),
        pl.BlockSpec((1, 6, D), lambda s, f: (prv(s) // tpb, 0, 0)),
        pl.BlockSpec((1, 4, D), lambda s, f: (l, 0, 0)),
        pl.BlockSpec((1, D, tff), lambda s, f: (0, 0, wf(s, f))),
        pl.BlockSpec((1, tff, D), lambda s, f: (0, wf(s, f), 0)),
    ]
    args = [x2, x2, mod, mod, g_norm, w_up, w_down]
    out_specs = [pl.BlockSpec((rs, D), lambda s, f: (jnp.where(s < 2, 0, (s - 2) * nf + f), 0))]
    out_shape = [jax.ShapeDtypeStruct((B * T, D), F32)]
    if cast_next:
        n_slabs = nt * nf
        ru, rd = D // n_slabs, D_FF // n_slabs
        slab = lambda s, f: jnp.clip((s - 1) * nf + f, 0, n_slabs - 1)
        up_spec = lambda: pl.BlockSpec((1, ru, D_FF), lambda s, f: (l + 1, slab(s, f), 0))
        dn_spec = lambda: pl.BlockSpec((1, rd, D), lambda s, f: (l + 1, slab(s, f), 0))
        in_specs += [up_spec(), dn_spec()]
        args += list(next_f32)
        out_specs += [pl.BlockSpec((1, ru, D_FF), lambda s, f: (0, slab(s, f), 0)),
                      pl.BlockSpec((1, rd, D), lambda s, f: (0, slab(s, f), 0))]
        out_shape += [jax.ShapeDtypeStruct((1, D, D_FF), BF16),
                      jax.ShapeDtypeStruct((1, D_FF, D), BF16)]
    outs = pl.pallas_call(
        kern,
        out_shape=tuple(out_shape),
        grid=(nt + 2, nf),
        in_specs=in_specs,
        out_specs=tuple(out_specs),
        scratch_shapes=[pltpu.VMEM((tm, D), BF16), pltpu.VMEM((tm, D), BF16),
                        pltpu.VMEM((tm, D), F32), pltpu.VMEM((tm, D), F32)],
        compiler_params=_params("arbitrary", "arbitrary"),
        name=f"mlp_skewed_{l}",
    )(*args)
    y = outs[0].reshape(B, T, D)
    return (y, outs[1], outs[2]) if cast_next else (y, None, None)


def _split_dup(a):
    z = jnp.zeros_like(a)
    lo = jnp.concatenate([a, z], axis=-1)
    hi = jnp.concatenate([z, a], axis=-1)
    out = jnp.stack([lo, hi], axis=3)
    return out.reshape(a.shape[0], a.shape[1], -1).astype(BF16)


class _Path:
    def __init__(self, x, mods, kvmod, pos0, prefix, kv_past, *, bb, tt, tq):
        self.x, self.mods, self.kvmod, self.pos0 = x, mods, kvmod, pos0
        self.prefix, self.kv_past = prefix, kv_past
        self.bb, self.tt, self.tq = bb, tt, tq
        pos = pos0 + jnp.arange(x.shape[1])
        self.tables = tuple(jnp.tile(tb, (bb, 1)) if bb > 1 else tb for tb in _rope_tables(pos))
        self.new_pool = []
        self.k = self.v = self.kd = self.vd = None

    def mixer(self, l, wts):
        B, T, _ = self.x.shape
        mod, bb, tt = self.mods[l], self.bb, self.tt
        if l < N_A_LAYERS:
            self.x, npool = _pool_layer(self.x, mod, wts["g_norm"], self.prefix, wts["w_pool"],
                                        wts["pool_scale"], l, bb=bb, tt=tt, pos0=self.pos0)
            self.new_pool.append(npool)
            return
        j = l - N_A_LAYERS
        q = _qproj_layer(self.x, mod, wts["g_norm"], self.tables, wts["w_q"], l, j, bb=bb, tt=tt)
        if self.kv_past is None:
            self.x = _attention_out(q, self.x, mod, wts["g_norm"], wts["w_o"], self.kd, self.vd,
                                    wts["sinks"][j], l, j, tq=self.tq)
        else:
            o = _attention(q, self.kd, self.vd, wts["sinks"][j], B=B, T=T, tq=self.tq,
                           chunked=False, n_valid=self.kv_past[0].shape[1] + T)
            self.x = _oproj_layer(o, self.x, mod, wts["g_norm"], wts["w_o"], l, j, bb=bb, tt=tt)

    def shared_kv(self, wts):
        B, T, _ = self.x.shape
        k, v, kd, vd = _kv_layer(self.x, self.kvmod, wts["g_kv"], self.tables, wts["w_kv"],
                                 bb=self.bb, tt=self.tt)
        W = kd.shape[-1]
        kd, vd = kd.reshape(B, T, W), vd.reshape(B, T, W)
        if self.kv_past is not None:
            back = ((0, 0), (0, KEY_SPAN - self.kv_past[0].shape[1] - T), (0, 0))
            kd = jnp.pad(jnp.concatenate([_split_dup(self.kv_past[0]), kd], axis=1), back)
            vd = jnp.pad(jnp.concatenate([_split_dup(self.kv_past[1]), vd], axis=1), back)
        KV = N_KV_HEADS * HEAD_DIM
        self.k, self.v = k.reshape(B, self.tt, KV), v.reshape(B, self.tt, KV)
        self.kd, self.vd = kd, vd


def _forward(prompt, sample, wts, *, tm, tff_prompt, tff_cast, tff_sample):
    w_f32 = (wts["w_up"], wts["w_down"])
    w_bf16 = None
    for l in range(DEPTH):
        sample.mixer(l, wts)
        if l == 0:
            sample.x, wu, wd = _mlp_layer(sample.x, sample.mods[l], wts["g_norm"], *w_f32, l,
                                          bb=sample.bb, tt=sample.tt, tff=tff_cast, cast=True)
            w_bf16 = (wu, wd)
        else:
            sample.x = _mlp_layer(sample.x, sample.mods[l], wts["g_norm"], *w_bf16, l,
                                  bb=sample.bb, tt=sample.tt, tff=tff_sample, cast=False)
        prompt.mixer(l, wts)
        prompt.x, wu, wd = _mlp_layer_skewed(
            prompt.x, prompt.mods[l], wts["g_norm"], *w_bf16, l, tm=tm, tff=tff_prompt,
            next_f32=w_f32 if l + 1 < DEPTH else None)
        w_bf16 = (wu, wd)
        if l == N_A_LAYERS - 1:
            sample.shared_kv(wts)
            prompt.shared_kv(wts)


def _prep_weights(g_norm, w_pool, pool_scale, g_kv, w_kv, w_q, sinks, w_o, w_up, w_down):
    D = D_MODEL
    return {
        "g_norm": g_norm,
        "w_pool": w_pool.astype(BF16),
        "pool_scale": pool_scale.reshape(N_A_LAYERS, 1, D),
        "g_kv": g_kv.reshape(1, 1, D),
        "w_kv": w_kv.astype(BF16),
        "w_q": w_q.astype(BF16),
        "sinks": sinks,
        "w_o": w_o.astype(BF16),
        "w_up": w_up,
        "w_down": w_down,
    }


def kernel(x_prompt, x_sample, c_prompt, c_sample, state_pool, cache_k, cache_v, w_mod, b_mod,
           g_norm, w_pool, pool_scale, w_kv_mod, b_kv_mod, g_kv, w_kv, w_q, sinks, w_o, w_up,
           w_down):
    Bp, Bs = x_prompt.shape[0], x_sample.shape[0]
    T_s = x_sample.shape[1]
    D = D_MODEL

    c_all = jnp.concatenate(
        [c_prompt, c_sample, jnp.zeros((MOD_ROWS - Bp - Bs, D), F32)], axis=0)
    mod_all = _modulation(c_all, w_mod, b_mod.reshape(DEPTH, 1, 6 * D)).reshape(
        DEPTH, MOD_ROWS, 6, D)
    kvmod_all = _modulation(c_all, w_kv_mod.reshape(1, D, 2 * D),
                            b_kv_mod.reshape(1, 1, 2 * D)).reshape(MOD_ROWS, 2, D)
    wts = _prep_weights(g_norm, w_pool, pool_scale, g_kv, w_kv, w_q, sinks, w_o, w_up, w_down)

    prompt = _Path(x_prompt, [mod_all[l, :Bp] for l in range(DEPTH)], kvmod_all[:Bp], 0,
                   jnp.zeros((N_A_LAYERS, Bp, POOL_STATE, D), F32), None,
                   bb=1, tt=512, tq=4 * WINDOW)
    sample = _Path(x_sample, [mod_all[l, Bp:Bp + Bs] for l in range(DEPTH)],
                   kvmod_all[Bp:Bp + Bs], PAST_LEN, state_pool, (cache_k, cache_v),
                   bb=Bs, tt=T_s, tq=T_s)
    _forward(prompt, sample, wts, tm=1024, tff_prompt=1024, tff_cast=512, tff_sample=1024)

    keep = min(WINDOW, x_prompt.shape[1])
    heads = (N_KV_HEADS, HEAD_DIM)
    return (prompt.x, sample.x,
            jnp.concatenate(prompt.new_pool, axis=0), jnp.concatenate(sample.new_pool, axis=0),
            prompt.k[:, -keep:].reshape(Bp, keep, *heads),
            prompt.v[:, -keep:].reshape(Bp, keep, *heads),
            sample.k.reshape(Bs, T_s, *heads), sample.v.reshape(Bs, T_s, *heads))
```

```python
import functools

import jax
import jax.numpy as jnp
from jax import lax
from jax.experimental import pallas as pl
from jax.experimental.pallas import tpu as pltpu

F32 = jnp.float32
BF16 = jnp.bfloat16

D_MODEL = 2048
DEPTH = 4
PAST_LEN = 4096
CHUNK = 64
N_A_LAYERS = DEPTH // 2
POOL_WINDOWS = (2, 4, 8, 16)
POOL_GROUP = D_MODEL // len(POOL_WINDOWS)
POOL_STATE = max(POOL_WINDOWS) - 1
SUBLANES = 8
POOL_LEAD = SUBLANES
POOL_BASE = POOL_LEAD + POOL_STATE + 1
HEAD_DIM = 64
N_HEADS = D_MODEL // HEAD_DIM
N_KV_HEADS = N_HEADS // 8
WINDOW = 128
D_FF = 4 * D_MODEL
ROPE_THETA = 10000.0
EPS = 1e-6
ATTN_SCALE = HEAD_DIM ** -0.5
NEG_INF = -1e30

LANES = 128
KEY_SPAN = 2 * LANES
N_PAIRS = N_HEADS // 2
PAIRS_PER_KV = N_PAIRS // N_KV_HEADS
MLP_OUT_CHUNK = 512
MOD_ROWS = 32
VMEM_LIMIT = 56 * 1024 * 1024


def _params(*sem):
    return pltpu.CompilerParams(dimension_semantics=sem, vmem_limit_bytes=VMEM_LIMIT)


def _norm_mod(x, g, shift, scale):
    ms = jnp.mean(x * x, axis=-1, keepdims=True)
    return (x * lax.rsqrt(ms + EPS)) * (g * (1.0 + scale)) + shift


def _gated_norm(y, g, gate):
    ms = jnp.mean(y * y, axis=-1, keepdims=True)
    return (y * lax.rsqrt(ms + EPS)) * (g * gate)


def _mod_kernel(c_ref, w_ref, b_ref, o_ref):
    c = c_ref[...]
    sc = (c * jax.nn.sigmoid(c)).astype(BF16)
    o_ref[0] = jnp.dot(sc, w_ref[0].astype(BF16), preferred_element_type=F32) + b_ref[0]


def _modulation(c_all, w, b, tn=2048):
    L, D, N = w.shape
    return pl.pallas_call(
        _mod_kernel,
        out_shape=jax.ShapeDtypeStruct((L, MOD_ROWS, N), F32),
        grid=(L, N // tn),
        in_specs=[
            pl.BlockSpec((MOD_ROWS, D), lambda l, n: (0, 0)),
            pl.BlockSpec((1, D, tn), lambda l, n: (l, 0, n)),
            pl.BlockSpec((1, 1, tn), lambda l, n: (l, 0, n)),
        ],
        out_specs=pl.BlockSpec((1, MOD_ROWS, tn), lambda l, n: (l, 0, n)),
        compiler_params=_params("parallel", "parallel"),
        name="modulation",
    )(c_all, w, b)


def _pool_kernel(x_ref, mod_ref, g_ref, pre_ref, wp_ref, ps_ref, o_ref, np_ref, hbuf, s1, s2,
                 *, bb, tt, pos0):
    t = pl.program_id(1)
    G = POOL_GROUP
    LEAD, BASE = POOL_LEAD, POOL_BASE
    L = BASE + tt

    @pl.when(t == 0)
    def _():
        hbuf[:, 0:LEAD + 1, :] = jnp.zeros((bb, LEAD + 1, D_MODEL), F32)
        hbuf[:, LEAD + 1:BASE, :] = pre_ref[0]
        s1[:, 0:LEAD, :] = jnp.zeros((bb, LEAD, 3 * G), F32)
        s2[:, 0:LEAD, :] = jnp.zeros((bb, LEAD, 3 * G), F32)

    @pl.when(t > 0)
    def _():
        hbuf[:, LEAD:BASE, :] = hbuf[:, tt + LEAD:tt + BASE, :]

    x = x_ref[...]
    m = mod_ref[...]
    h = _norm_mod(x, g_ref[:, 0:1, :], m[:, 0:1, :], m[:, 1:2, :])
    hbuf[:, BASE:L, :] = h
    np_ref[0] = hbuf[:, L - POOL_STATE:L, :]

    s1[:, LEAD:L, :] = hbuf[:, LEAD:L, G:] + hbuf[:, LEAD - 1:L - 1, G:]
    s2[:, LEAD:L, :] = s1[:, LEAD:L, :] + s1[:, LEAD - 2:L - 2, :]
    s1[:, 2 * LEAD:L, 0:2 * G] = (s2[:, 2 * LEAD:L, G:] +
                                  s2[:, 2 * LEAD - 4:L - 4, G:])
    sums = [
        hbuf[:, BASE:L, 0:G] + hbuf[:, BASE - 1:L - 1, 0:G],
        s2[:, BASE:L, 0:G],
        s1[:, BASE:L, 0:G],
        s1[:, BASE:L, G:2 * G] + s1[:, BASE - 8:L - 8, G:2 * G],
    ]

    pos = pos0 + t * tt + lax.broadcasted_iota(jnp.int32, (1, tt, LANES), 1)
    ys = []
    for g, w in enumerate(POOL_WINDOWS):
        cs = slice(g * POOL_GROUP, (g + 1) * POOL_GROUP)
        inv_cnt = 1.0 / jnp.minimum(w, pos + 1).astype(F32)
        inv_cnt = jnp.concatenate([inv_cnt] * (POOL_GROUP // LANES), axis=-1)
        pooled = sums[g] * inv_cnt - hbuf[:, BASE:L, cs]
        ys.append(jnp.dot(pooled.reshape(bb * tt, POOL_GROUP).astype(BF16), wp_ref[0, g],
                          preferred_element_type=F32))
    y = jnp.concatenate(ys, axis=-1).reshape(bb, tt, D_MODEL) * ps_ref[...]
    o_ref[...] = x + _gated_norm(y, g_ref[:, 1:2, :], m[:, 2:3, :])


def _pool_layer(x, mod, g_norm, prefix, w_pool, pool_scale, l, *, bb, tt, pos0):
    B, T, D = x.shape
    kern = functools.partial(_pool_kernel, bb=bb, tt=tt, pos0=pos0)
    return pl.pallas_call(
        kern,
        out_shape=(jax.ShapeDtypeStruct((B, T, D), F32),
                   jax.ShapeDtypeStruct((1, B, POOL_STATE, D), F32)),
        grid=(B // bb, T // tt),
        in_specs=[
            pl.BlockSpec((bb, tt, D), lambda b, t: (b, t, 0)),
            pl.BlockSpec((bb, 6, D), lambda b, t: (b, 0, 0)),
            pl.BlockSpec((1, 4, D), lambda b, t: (l, 0, 0)),
            pl.BlockSpec((1, bb, POOL_STATE, D), lambda b, t: (l, b, 0, 0)),
            pl.BlockSpec((1, len(POOL_WINDOWS), POOL_GROUP, POOL_GROUP),
                         lambda b, t: (l, 0, 0, 0)),
            pl.BlockSpec((1, 1, D), lambda b, t: (l, 0, 0)),
        ],
        out_specs=(pl.BlockSpec((bb, tt, D), lambda b, t: (b, t, 0)),
                   pl.BlockSpec((1, bb, POOL_STATE, D), lambda b, t: (0, b, 0, 0))),
        scratch_shapes=[pltpu.VMEM((bb, POOL_BASE + tt, D), F32),
                        pltpu.VMEM((bb, POOL_BASE + tt, 3 * POOL_GROUP), F32),
                        pltpu.VMEM((bb, POOL_BASE + tt, 3 * POOL_GROUP), F32)],
        compiler_params=_params("parallel", "arbitrary"),
        name=f"pool_mixer_{l}",
    )(x, mod, g_norm, prefix, w_pool, pool_scale)


def _rope_tables(pos):
    half = HEAD_DIM // 2
    inv = ROPE_THETA ** (-jnp.arange(half, dtype=F32) / half)
    ang = pos.astype(F32)[:, None] * inv[None, :]
    cos, sin = jnp.cos(ang), jnp.sin(ang)
    zero = jnp.zeros_like(sin)
    c = jnp.tile(cos, (1, 4))
    s_lo = jnp.tile(jnp.concatenate([-sin, zero], axis=1), (1, 2))
    s_hi = jnp.tile(jnp.concatenate([zero, sin], axis=1), (1, 2))
    return c, s_lo, s_hi


def _rope_block(blk, c, s_lo, s_hi):
    return (blk * c + pltpu.roll(blk, LANES - HEAD_DIM // 2, 1) * s_lo
            + pltpu.roll(blk, HEAD_DIM // 2, 1) * s_hi)


def _qproj_kernel(x_ref, mod_ref, g_ref, c_ref, slo_ref, shi_ref, w_ref, q_ref, *, bb, tt):
    x = x_ref[...]
    m = mod_ref[...]
    h = _norm_mod(x, g_ref[:, 0:1, :], m[:, 0:1, :], m[:, 1:2, :])
    q = jnp.dot(h.reshape(bb * tt, D_MODEL).astype(BF16), w_ref[0], preferred_element_type=F32)
    c, s_lo, s_hi = c_ref[...], slo_ref[...], shi_ref[...]
    for p in range(N_PAIRS):
        cs = slice(p * LANES, (p + 1) * LANES)
        q_ref[p] = (_rope_block(q[:, cs], c, s_lo, s_hi) * ATTN_SCALE).astype(BF16)


def _qproj_layer(x, mod, g_norm, tables, w_q, l, j, *, bb, tt):
    B, T, D = x.shape
    nt = T // tt
    M = bb * tt
    kern = functools.partial(_qproj_kernel, bb=bb, tt=tt)
    tab_spec = pl.BlockSpec((M, LANES), lambda b, t: (t, 0))
    return pl.pallas_call(
        kern,
        out_shape=jax.ShapeDtypeStruct((N_PAIRS, B * T, LANES), BF16),
        grid=(B // bb, nt),
        in_specs=[
            pl.BlockSpec((bb, tt, D), lambda b, t: (b, t, 0)),
            pl.BlockSpec((bb, 6, D), lambda b, t: (b, 0, 0)),
            pl.BlockSpec((1, 4, D), lambda b, t: (l, 0, 0)),
            tab_spec, tab_spec, tab_spec,
            pl.BlockSpec((1, D, D), lambda b, t: (j, 0, 0)),
        ],
        out_specs=pl.BlockSpec((N_PAIRS, M, LANES), lambda b, t: (0, b * nt + t, 0)),
        compiler_params=_params("parallel", "parallel"),
        name=f"q_proj_{l}",
    )(x, mod, g_norm, *tables, w_q)


def _masked_pairs(a):
    M = a.shape[0]
    lo = lax.broadcasted_iota(jnp.int32, (M, LANES), 1) < HEAD_DIM
    zero = jnp.zeros((M, LANES), F32)
    out = []
    for p in range(N_KV_HEADS // 2):
        blk = a[:, p * LANES:(p + 1) * LANES]
        swapped = pltpu.roll(blk, HEAD_DIM, 1)
        out += [jnp.where(lo, blk, zero), jnp.where(lo, zero, swapped),
                jnp.where(lo, swapped, zero), jnp.where(lo, zero, blk)]
    return jnp.concatenate([part.astype(BF16) for part in out], axis=1)


def _kv_kernel(x_ref, mod_ref, g_ref, c_ref, slo_ref, shi_ref, w_ref,
               k_ref, v_ref, kd_ref, vd_ref, *, bb, tt):
    x = x_ref[...]
    m = mod_ref[...]
    h = _norm_mod(x, g_ref[...], m[:, 0:1, :], m[:, 1:2, :])
    kv = jnp.dot(h.reshape(bb * tt, D_MODEL).astype(BF16), w_ref[...],
                 preferred_element_type=F32)
    c, s_lo, s_hi = c_ref[...], slo_ref[...], shi_ref[...]
    lo = lax.broadcasted_iota(jnp.int32, (bb * tt, LANES), 1) < HEAD_DIM
    zero = jnp.zeros((bb * tt, LANES), F32)
    n_blk = N_KV_HEADS // 2
    for p in range(n_blk):
        cs = slice(p * LANES, (p + 1) * LANES)
        k_blk = _rope_block(kv[:, cs], c, s_lo, s_hi)
        v_blk = kv[:, (n_blk + p) * LANES:(n_blk + p + 1) * LANES]
        k_ref[:, cs] = k_blk
        v_ref[:, cs] = v_blk
        for blk, dst in ((k_blk, kd_ref), (v_blk, vd_ref)):
            swapped = pltpu.roll(blk, HEAD_DIM, 1)
            parts = (jnp.where(lo, blk, zero), jnp.where(lo, zero, swapped),
                     jnp.where(lo, swapped, zero), jnp.where(lo, zero, blk))
            for i, part in enumerate(parts):
                dst[:, :, (4 * p + i) * LANES:(4 * p + i + 1) * LANES] = (
                    part.astype(BF16).reshape(bb, tt, LANES))


def _kv_layer(x, mod, g_kv, tables, w_kv, *, bb, tt):
    B, T, D = x.shape
    nt = T // tt
    M = bb * tt
    KV = N_KV_HEADS * HEAD_DIM
    kern = functools.partial(_kv_kernel, bb=bb, tt=tt)
    tab_spec = pl.BlockSpec((M, LANES), lambda b, t: (t, 0))
    tile = lambda b, t: (b, t, 0)
    newest = lambda b, t: (b, 0)
    return pl.pallas_call(
        kern,
        out_shape=(jax.ShapeDtypeStruct((B // bb * M, KV), F32),
                   jax.ShapeDtypeStruct((B // bb * M, KV), F32),
                   jax.ShapeDtypeStruct((B, T, 4 * KV), BF16),
                   jax.ShapeDtypeStruct((B, T, 4 * KV), BF16)),
        grid=(B // bb, nt),
        in_specs=[
            pl.BlockSpec((bb, tt, D), lambda b, t: (b, t, 0)),
            pl.BlockSpec((bb, 2, D), lambda b, t: (b, 0, 0)),
            pl.BlockSpec((1, 1, D), lambda b, t: (0, 0, 0)),
            tab_spec, tab_spec, tab_spec,
            pl.BlockSpec((D, 2 * KV), lambda b, t: (0, 0)),
        ],
        out_specs=(pl.BlockSpec((M, KV), newest), pl.BlockSpec((M, KV), newest),
                   pl.BlockSpec((bb, tt, 4 * KV), tile), pl.BlockSpec((bb, tt, 4 * KV), tile)),
        compiler_params=_params("parallel", "arbitrary"),
        name="shared_kv",
    )(x, mod, g_kv, *tables, w_kv)


def _sink_softmax(s, sink):
    m = jnp.max(s, axis=-1, keepdims=True)
    e = jnp.exp(s - m)
    den = jnp.sum(e, axis=-1, keepdims=True) + jnp.exp(sink - m)
    return e * (1.0 / den)


def _attn_kernel(sink_ref, *refs, tq, chunked, n_valid, fused):
    i = pl.program_id(1)
    G = PAIRS_PER_KV
    ts = WINDOW if chunked else tq
    nsub = tq // ts
    R = G * ts
    QW = D_MODEL // N_KV_HEADS
    kp = lax.broadcasted_iota(jnp.int32, (R, KEY_SPAN), 1)
    rblk = lax.broadcasted_iota(jnp.int32, (R, 1), 0) // ts
    nt = (((1,), (1,)), ((), ()))
    sink_cols = []
    for j in range(N_KV_HEADS):
        for parity in range(2):
            col = jnp.full((R, 1), sink_ref[2 * G * j + parity], F32)
            for r in range(1, G):
                col = jnp.where(rblk == r, sink_ref[2 * (G * j + r) + parity], col)
            sink_cols.append(col)
    if fused:
        q_ref, k_prev, k_cur, v_prev, v_cur, x_ref, mod_ref, g_ref, wo_ref, o_ref = refs
    elif chunked:
        q_ref, k_prev, k_cur, v_prev, v_cur, o_ref = refs
    else:
        q_ref, ck_ref, cv_ref, kd_ref, vd_ref, o_ref = refs
    if chunked:
        row = lax.broadcasted_iota(jnp.int32, (R, KEY_SPAN), 0)
        cq = (row % ts) // CHUNK
        kb = kp // CHUNK
        in_window = (kb >= cq) & (kb <= cq + WINDOW // CHUNK)

    def finish(rows, mix):
        m = mod_ref[0]
        o_ref[rows, :] = x_ref[rows, :] + _gated_norm(mix, g_ref[0, 1:2, :], m[2:3, :])

    def with_cache(c_ref, new_ref):
        past = _masked_pairs(c_ref[0])
        pad = jnp.zeros((KEY_SPAN - past.shape[0] - tq, past.shape[1]), BF16)
        return jnp.concatenate([past, new_ref[0], pad], axis=0)

    def keys_of(sub):
        rs = slice(sub * ts, (sub + 1) * ts)
        if not chunked:
            kw, vw = with_cache(ck_ref, kd_ref), with_cache(cv_ref, vd_ref)
            valid = kp < n_valid
        elif sub == 0:
            kw = jnp.concatenate([k_prev[0], k_cur[0, rs, :]], axis=0)
            vw = jnp.concatenate([v_prev[0], v_cur[0, rs, :]], axis=0)
            valid = in_window & (kp >= jnp.where(i > 0, 0, WINDOW))
        else:
            span = slice(sub * ts - WINDOW, (sub + 1) * ts)
            kw, vw = k_cur[0, span, :], v_cur[0, span, :]
            valid = in_window
        return rs, kw, vw, jnp.where(valid, 0.0, NEG_INF)

    def scores(keys, j):
        rs, kw, _, bias = keys
        qg = q_ref[G * j:G * (j + 1), rs, :].reshape(R, LANES)
        k_lo = kw[:, (2 * j) * LANES:(2 * j + 1) * LANES]
        k_hi = kw[:, (2 * j + 1) * LANES:(2 * j + 2) * LANES]
        return (lax.dot_general(qg, k_lo, nt, preferred_element_type=F32) + bias,
                lax.dot_general(qg, k_hi, nt, preferred_element_type=F32) + bias)

    groups = [(sub, j) for sub in range(nsub) for j in range(N_KV_HEADS)]
    keys = {0: keys_of(0)}
    ahead = scores(keys[0], 0)
    pending = None
    heads, mix_parts = [], []
    for n, (sub, j) in enumerate(groups):
        s0, s1 = ahead
        if n + 1 < len(groups):
            sub_n, j_n = groups[n + 1]
            if sub_n not in keys:
                keys[sub_n] = keys_of(sub_n)
            ahead = scores(keys[sub_n], j_n)
        if pending is not None:
            mix_parts.append(jnp.dot(pending[1], wo_ref[0, :, j * QW:(j + 1) * QW],
                                     preferred_element_type=F32))
        rs, _, vw, _ = keys[sub]
        v_lo = vw[:, (2 * j) * LANES:(2 * j + 1) * LANES]
        v_hi = vw[:, (2 * j + 1) * LANES:(2 * j + 2) * LANES]
        p0 = _sink_softmax(s0, sink_cols[2 * j])
        p1 = _sink_softmax(s1, sink_cols[2 * j + 1])
        o = (jnp.dot(p0.astype(BF16), v_lo, preferred_element_type=F32)
             + jnp.dot(p1.astype(BF16), v_hi, preferred_element_type=F32)).astype(BF16)
        if not fused:
            o_ref[G * j:G * (j + 1), rs, :] = o.reshape(G, ts, LANES)
            continue
        heads.extend(o[r * ts:(r + 1) * ts] for r in range(G))
        if j == N_KV_HEADS - 1:
            if pending is not None:
                finish(pending[0], jnp.concatenate(mix_parts, axis=1))
            pending = (rs, jnp.concatenate(heads, axis=1))
            heads, mix_parts = [], []
    if fused:
        finish(pending[0], jnp.dot(pending[1], wo_ref[0], preferred_element_type=F32))


def _attention(q, cache_k, cache_v, kd, vd, sinks):
    B, T, W = kd.shape
    S = cache_k.shape[1]
    assert S + T <= KEY_SPAN
    kern = functools.partial(_attn_kernel, tq=T, chunked=False, n_valid=S + T, fused=False)
    past = pl.BlockSpec((1, S, cache_k.shape[2]), lambda b, i: (b, 0, 0))
    new = pl.BlockSpec((1, T, W), lambda b, i: (b, 0, 0))
    pair_rows = pl.BlockSpec((N_PAIRS, T, LANES), lambda b, i: (0, b, 0))
    return pl.pallas_call(
        kern,
        out_shape=jax.ShapeDtypeStruct((N_PAIRS, B * T, LANES), BF16),
        grid=(B, 1),
        in_specs=[pl.BlockSpec(memory_space=pltpu.SMEM), pair_rows, past, past, new, new],
        out_specs=pair_rows,
        compiler_params=_params("parallel", "parallel"),
        name="swa_attention",
    )(sinks, q, cache_k, cache_v, kd, vd)


def _window_specs(kd, vd, T, tq):
    W = kd.shape[2]
    assert tq % WINDOW == 0 and kd.shape[1] == T
    nsub = tq // WINDOW
    prev = pl.BlockSpec((1, WINDOW, W), lambda b, i: (b, jnp.maximum(i * nsub - 1, 0), 0))
    cur = pl.BlockSpec((1, tq, W), lambda b, i: (b, i, 0))
    return [prev, cur, prev, cur], (kd, kd, vd, vd)


def _attention_out(q, x, mod, g_norm, w_o, kd, vd, sinks, l, j, *, tq):
    B, T, D = x.shape
    nq = T // tq
    kern = functools.partial(_attn_kernel, tq=tq, chunked=True, n_valid=None, fused=True)
    kv_specs, kv_args = _window_specs(kd, vd, T, tq)
    rows = lambda b, i: (b * nq + i, 0)
    out = pl.pallas_call(
        kern,
        out_shape=jax.ShapeDtypeStruct((B * T, D), F32),
        grid=(B, nq),
        in_specs=[
            pl.BlockSpec(memory_space=pltpu.SMEM),
            pl.BlockSpec((N_PAIRS, tq, LANES), lambda b, i: (0, b * nq + i, 0)),
            *kv_specs,
            pl.BlockSpec((tq, D), rows),
            pl.BlockSpec((1, 6, D), lambda b, i: (b, 0, 0)),
            pl.BlockSpec((1, 4, D), lambda b, i: (l, 0, 0)),
            pl.BlockSpec((1, D, D), lambda b, i: (j, 0, 0)),
        ],
        out_specs=pl.BlockSpec((tq, D), rows),
        compiler_params=_params("parallel", "parallel"),
        name=f"attention_out_{l}",
    )(sinks, q, *kv_args, x.reshape(B * T, D), mod, g_norm, w_o)
    return out.reshape(B, T, D)


def _oproj_kernel(o_ref, x_ref, mod_ref, g_ref, w_ref, y_ref, *, bb, tt):
    o = jnp.concatenate([o_ref[p] for p in range(N_PAIRS)], axis=1)
    mix = jnp.dot(o, w_ref[0], preferred_element_type=F32).reshape(bb, tt, D_MODEL)
    m = mod_ref[...]
    y_ref[...] = x_ref[...] + _gated_norm(mix, g_ref[:, 1:2, :], m[:, 2:3, :])


def _oproj_layer(o, x, mod, g_norm, w_o, l, j, *, bb, tt):
    B, T, D = x.shape
    nt = T // tt
    kern = functools.partial(_oproj_kernel, bb=bb, tt=tt)
    return pl.pallas_call(
        kern,
        out_shape=jax.ShapeDtypeStruct((B, T, D), F32),
        grid=(B // bb, nt),
        in_specs=[
            pl.BlockSpec((N_PAIRS, bb * tt, LANES), lambda b, t: (0, b * nt + t, 0)),
            pl.BlockSpec((bb, tt, D), lambda b, t: (b, t, 0)),
            pl.BlockSpec((bb, 6, D), lambda b, t: (b, 0, 0)),
            pl.BlockSpec((1, 4, D), lambda b, t: (l, 0, 0)),
            pl.BlockSpec((1, D, D), lambda b, t: (j, 0, 0)),
        ],
        out_specs=pl.BlockSpec((bb, tt, D), lambda b, t: (b, t, 0)),
        compiler_params=_params("parallel", "parallel"),
        name=f"o_proj_{l}",
    )(o, x, mod, g_norm, w_o)


def _mlp_kernel(x_ref, mod_ref, g_ref, wu_ref, wd_ref, y_ref, *rest, bb, tt, cast):
    f = pl.program_id(2)
    m = mod_ref[...]
    if cast:
        wu_out, wd_out, h_ref = rest
    else:
        h_ref, = rest

    @pl.when(f == 0)
    def _():
        h = _norm_mod(x_ref[...], g_ref[:, 2:3, :], m[:, 3:4, :], m[:, 4:5, :])
        h_ref[...] = h.reshape(bb * tt, D_MODEL).astype(BF16)
        y_ref[...] = jnp.zeros_like(y_ref)

    wu = wu_ref[0]
    if cast:
        wu = wu.astype(BF16)
        wu_out[0] = wu
    u = jnp.dot(h_ref[...], wu, preferred_element_type=F32)
    a = jnp.square(jnp.maximum(u, 0.0)).astype(BF16)
    for n in range(D_MODEL // MLP_OUT_CHUNK):
        cs = slice(n * MLP_OUT_CHUNK, (n + 1) * MLP_OUT_CHUNK)
        wd_n = wd_ref[0, :, cs]
        if cast:
            wd_n = wd_n.astype(BF16)
            wd_out[0, :, cs] = wd_n
        y_ref[:, :, cs] += jnp.dot(a, wd_n, preferred_element_type=F32).reshape(
            bb, tt, MLP_OUT_CHUNK)

    @pl.when(f == pl.num_programs(2) - 1)
    def _():
        y_ref[...] = x_ref[...] + _gated_norm(y_ref[...], g_ref[:, 3:4, :], m[:, 5:6, :])


def _mlp_layer(x, mod, g_norm, w_up, w_down, l, *, bb, tt, tff, cast):
    B, T, D = x.shape
    grid = (B // bb, T // tt, D_FF // tff)
    kern = functools.partial(_mlp_kernel, bb=bb, tt=tt, cast=cast)
    wl = l if cast else 0
    y_shape = jax.ShapeDtypeStruct((B, T, D), F32)
    y_spec = pl.BlockSpec((bb, tt, D), lambda b, t, f: (b, t, 0))
    if cast:
        assert grid[0] * grid[1] == 1
        out_shape = (y_shape, jax.ShapeDtypeStruct((1, D, D_FF), BF16),
                     jax.ShapeDtypeStruct((1, D_FF, D), BF16))
        out_specs = (y_spec, pl.BlockSpec((1, D, tff), lambda b, t, f: (0, 0, f)),
                     pl.BlockSpec((1, tff, D), lambda b, t, f: (0, f, 0)))
    else:
        out_shape, out_specs = y_shape, y_spec
    return pl.pallas_call(
        kern,
        out_shape=out_shape,
        grid=grid,
        in_specs=[
            pl.BlockSpec((bb, tt, D), lambda b, t, f: (b, t, 0)),
            pl.BlockSpec((bb, 6, D), lambda b, t, f: (b, 0, 0)),
            pl.BlockSpec((1, 4, D), lambda b, t, f: (l, 0, 0)),
            pl.BlockSpec((1, D, tff), lambda b, t, f: (wl, 0, f)),
            pl.BlockSpec((1, tff, D), lambda b, t, f: (wl, f, 0)),
        ],
        out_specs=out_specs,
        scratch_shapes=[pltpu.VMEM((bb * tt, D), BF16)],
        compiler_params=_params("parallel", "parallel", "arbitrary"),
        name=f"mlp_{l}",
    )(x, mod, g_norm, w_up, w_down)


def _zero_after(x):
    bits = pltpu.bitcast(x, jnp.uint32)
    r = bits[:, 0:LANES]
    for k in range(1, x.shape[1] // LANES):
        r = r | bits[:, k * LANES:(k + 1) * LANES]
    r8 = r[0:SUBLANES]
    for k in range(1, x.shape[0] // SUBLANES):
        r8 = r8 | r[k * SUBLANES:(k + 1) * SUBLANES]
    z = pltpu.bitcast((r8 >> 16) >> 16, F32)
    return jnp.max(z, axis=(0, 1), keepdims=True)


def _mlp_skew_kernel(xn_ref, xp_ref, modn_ref, modp_ref, g_ref, wu_ref, wd_ref, *rest, nt, rs,
                     cast_next):
    s = pl.program_id(0)
    f = pl.program_id(1)
    if cast_next:
        nu_ref, nd_ref, y_ref, nu_out, nd_out, h0_ref, h1_ref, acc0_ref, acc1_ref = rest
    else:
        y_ref, h0_ref, h1_ref, acc0_ref, acc1_ref = rest
    rows = pl.ds(pl.multiple_of(f * rs, rs), rs)
    g_pre, g_post = g_ref[0, 2:3, :], g_ref[0, 3:4, :]
    h_refs = (h0_ref, h1_ref)
    acc_refs = (acc0_ref, acc1_ref)

    def prologue(h_ref):
        m = modn_ref[0]
        h = _norm_mod(xn_ref[...], g_pre, m[3:4, :], m[4:5, :])
        h_ref[rows, :] = h.astype(BF16)
        return h

    def epilogue(acc_ref):
        mp = modp_ref[0]
        y = xp_ref[...] + _gated_norm(acc_ref[rows, :], g_post, mp[5:6, :])
        y_ref[...] = y
        acc_ref[rows, :] = jnp.zeros((rs, D_MODEL), F32)
        return y

    def matmuls(h_ref, acc_ref, anchors):
        if cast_next:
            nu_out[...] = nu_ref[...].astype(BF16)
            nd_out[...] = nd_ref[...].astype(BF16)
        u = jnp.dot(h_ref[...], wu_ref[0], preferred_element_type=F32)
        a = jnp.square(jnp.maximum(u, 0.0)).astype(BF16)
        for n in range(D_MODEL // MLP_OUT_CHUNK):
            cs = slice(n * MLP_OUT_CHUNK, (n + 1) * MLP_OUT_CHUNK)
            d = jnp.dot(a, wd_ref[0, :, cs], preferred_element_type=F32)
            if anchors.get(n) is not None:
                d = d + anchors[n]
            acc_ref[:, cs] += d

    @pl.when((s == 0) & (f == 0))
    def _():
        acc0_ref[...] = jnp.zeros_like(acc0_ref)
        acc1_ref[...] = jnp.zeros_like(acc1_ref)

    @pl.when(s == 0)
    def _():
        prologue(h0_ref)

    @pl.when(s == 1)
    def _():
        h = prologue(h1_ref)
        matmuls(h0_ref, acc0_ref, {2: _zero_after(h)})

    for c in range(2):
        o = 1 - c

        @pl.when(((s - 1) % 2 == c) & (s >= 2) & (s < nt))
        def _():
            y = epilogue(acc_refs[o])
            h = prologue(h_refs[o])
            matmuls(h_refs[c], acc_refs[c], {0: _zero_after(y), 2: _zero_after(h)})

    last = (nt - 1) % 2

    @pl.when(s == nt)
    def _():
        y = epilogue(acc_refs[1 - last])
        matmuls(h_refs[last], acc_refs[last], {0: _zero_after(y)})

    @pl.when(s == nt + 1)
    def _():
        epilogue(acc_refs[last])


def _mlp_layer_skewed(x, mod, g_norm, w_up, w_down, l, *, tm, tff, next_f32=None):
    B, T, D = x.shape
    nf = D_FF // tff
    rs = tm // nf
    tpb = T // tm
    nt = B * tpb
    assert nt >= 3
    x2 = x.reshape(B * T, D)
    nxt = lambda s: jnp.minimum(s, nt - 1)
    prv = lambda s: jnp.maximum(s - 2, 0)
    wf = lambda s, f: jnp.where(s == 0, 0, jnp.where(s == nt + 1, nf - 1, f))
    cast_next = next_f32 is not None
    kern = functools.partial(_mlp_skew_kernel, nt=nt, rs=rs, cast_next=cast_next)
    in_specs = [
        pl.BlockSpec((rs, D), lambda s, f: (nxt(s) * nf + f, 0)),
        pl.BlockSpec((rs, D), lambda s, f: (prv(s) * nf + f, 0)),
        pl.BlockSpec((1, 6, D), lambda s, f: (nxt(s) // tpb, 0, 0)),
        pl.BlockSpec((1, 6, D), lambda s, f: (prv(s) // tpb, 0, 0)),
        pl.BlockSpec((1, 4, D), lambda s, f: (l, 0, 0)),
        pl.BlockSpec((1, D, tff), lambda s, f: (0, 0, wf(s, f))),
        pl.BlockSpec((1, tff, D), lambda s, f: (0, wf(s, f), 0)),
    ]
    args = [x2, x2, mod, mod, g_norm, w_up, w_down]
    out_specs = [pl.BlockSpec((rs, D), lambda s, f: (jnp.where(s < 2, 0, (s - 2) * nf + f), 0))]
    out_shape = [jax.ShapeDtypeStruct((B * T, D), F32)]
    if cast_next:
        n_slabs = nt * nf
        ru, rd = D // n_slabs, D_FF // n_slabs
        slab = lambda s, f: jnp.clip((s - 1) * nf + f, 0, n_slabs - 1)
        up_spec = lambda: pl.BlockSpec((1, ru, D_FF), lambda s, f: (l + 1, slab(s, f), 0))
        dn_spec = lambda: pl.BlockSpec((1, rd, D), lambda s, f: (l + 1, slab(s, f), 0))
        in_specs += [up_spec(), dn_spec()]
        args += list(next_f32)
        out_specs += [pl.BlockSpec((1, ru, D_FF), lambda s, f: (0, slab(s, f), 0)),
                      pl.BlockSpec((1, rd, D), lambda s, f: (0, slab(s, f), 0))]
        out_shape += [jax.ShapeDtypeStruct((1, D, D_FF), BF16),
                      jax.ShapeDtypeStruct((1, D_FF, D), BF16)]
    outs = pl.pallas_call(
        kern,
        out_shape=tuple(out_shape),
        grid=(nt + 2, nf),
        in_specs=in_specs,
        out_specs=tuple(out_specs),
        scratch_shapes=[pltpu.VMEM((tm, D), BF16), pltpu.VMEM((tm, D), BF16),
                        pltpu.VMEM((tm, D), F32), pltpu.VMEM((tm, D), F32)],
        compiler_params=_params("arbitrary", "arbitrary"),
        name=f"mlp_skewed_{l}",
    )(*args)
    y = outs[0].reshape(B, T, D)
    return (y, outs[1], outs[2]) if cast_next else (y, None, None)


class _Path:
    def __init__(self, x, mods, kvmod, pos0, prefix, kv_past, *, bb, tt, tq):
        self.x, self.mods, self.kvmod, self.pos0 = x, mods, kvmod, pos0
        self.prefix, self.kv_past = prefix, kv_past
        self.bb, self.tt, self.tq = bb, tt, tq
        pos = pos0 + jnp.arange(x.shape[1])
        self.tables = tuple(jnp.tile(tb, (bb, 1)) if bb > 1 else tb for tb in _rope_tables(pos))
        self.new_pool = []
        self.k = self.v = self.kd = self.vd = None

    def mixer(self, l, wts):
        B, T, _ = self.x.shape
        mod, bb, tt = self.mods[l], self.bb, self.tt
        if l < N_A_LAYERS:
            self.x, npool = _pool_layer(self.x, mod, wts["g_norm"], self.prefix, wts["w_pool"],
                                        wts["pool_scale"], l, bb=bb, tt=tt, pos0=self.pos0)
            self.new_pool.append(npool)
            return
        j = l - N_A_LAYERS
        q = _qproj_layer(self.x, mod, wts["g_norm"], self.tables, wts["w_q"], l, j, bb=bb, tt=tt)
        if self.kv_past is None:
            self.x = _attention_out(q, self.x, mod, wts["g_norm"], wts["w_o"], self.kd, self.vd,
                                    wts["sinks"][j], l, j, tq=self.tq)
        else:
            past_k, past_v = (c.reshape(B, c.shape[1], -1) for c in self.kv_past)
            o = _attention(q, past_k, past_v, self.kd, self.vd, wts["sinks"][j])
            self.x = _oproj_layer(o, self.x, mod, wts["g_norm"], wts["w_o"], l, j, bb=bb, tt=tt)

    def shared_kv(self, wts):
        B, T, _ = self.x.shape
        k, v, kd, vd = _kv_layer(self.x, self.kvmod, wts["g_kv"], self.tables, wts["w_kv"],
                                 bb=self.bb, tt=self.tt)
        KV = N_KV_HEADS * HEAD_DIM
        self.k, self.v = k.reshape(B, self.tt, KV), v.reshape(B, self.tt, KV)
        self.kd, self.vd = kd, vd


def _forward(prompt, sample, wts, *, tm, tff_prompt, tff_cast, tff_sample):
    w_f32 = (wts["w_up"], wts["w_down"])
    w_bf16 = None
    for l in range(DEPTH):
        sample.mixer(l, wts)
        if l == 0:
            sample.x, wu, wd = _mlp_layer(sample.x, sample.mods[l], wts["g_norm"], *w_f32, l,
                                          bb=sample.bb, tt=sample.tt, tff=tff_cast, cast=True)
            w_bf16 = (wu, wd)
        else:
            sample.x = _mlp_layer(sample.x, sample.mods[l], wts["g_norm"], *w_bf16, l,
                                  bb=sample.bb, tt=sample.tt, tff=tff_sample, cast=False)
        prompt.mixer(l, wts)
        prompt.x, wu, wd = _mlp_layer_skewed(
            prompt.x, prompt.mods[l], wts["g_norm"], *w_bf16, l, tm=tm, tff=tff_prompt,
            next_f32=w_f32 if l + 1 < DEPTH else None)
        w_bf16 = (wu, wd)
        if l == N_A_LAYERS - 1:
            sample.shared_kv(wts)
            prompt.shared_kv(wts)


def _prep_weights(g_norm, w_pool, pool_scale, g_kv, w_kv, w_q, sinks, w_o, w_up, w_down):
    D = D_MODEL
    return {
        "g_norm": g_norm,
        "w_pool": w_pool.astype(BF16),
        "pool_scale": pool_scale.reshape(N_A_LAYERS, 1, D),
        "g_kv": g_kv.reshape(1, 1, D),
        "w_kv": w_kv.astype(BF16),
        "w_q": w_q.astype(BF16),
        "sinks": sinks,
        "w_o": w_o.astype(BF16),
        "w_up": w_up,
        "w_down": w_down,
    }


def kernel(x_prompt, x_sample, c_prompt, c_sample, state_pool, cache_k, cache_v, w_mod, b_mod,
           g_norm, w_pool, pool_scale, w_kv_mod, b_kv_mod, g_kv, w_kv, w_q, sinks, w_o, w_up,
           w_down):
    Bp, Bs = x_prompt.shape[0], x_sample.shape[0]
    T_s = x_sample.shape[1]
    D = D_MODEL

    c_all = jnp.concatenate(
        [c_prompt, c_sample, jnp.zeros((MOD_ROWS - Bp - Bs, D), F32)], axis=0)
    mod_all = _modulation(c_all, w_mod, b_mod.reshape(DEPTH, 1, 6 * D)).reshape(
        DEPTH, MOD_ROWS, 6, D)
    kvmod_all = _modulation(c_all, w_kv_mod.reshape(1, D, 2 * D),
                            b_kv_mod.reshape(1, 1, 2 * D)).reshape(MOD_ROWS, 2, D)
    wts = _prep_weights(g_norm, w_pool, pool_scale, g_kv, w_kv, w_q, sinks, w_o, w_up, w_down)

    prompt = _Path(x_prompt, [mod_all[l, :Bp] for l in range(DEPTH)], kvmod_all[:Bp], 0,
                   jnp.zeros((N_A_LAYERS, Bp, POOL_STATE, D), F32), None,
                   bb=1, tt=512, tq=4 * WINDOW)
    sample = _Path(x_sample, [mod_all[l, Bp:Bp + Bs] for l in range(DEPTH)],
                   kvmod_all[Bp:Bp + Bs], PAST_LEN, state_pool, (cache_k, cache_v),
                   bb=Bs, tt=T_s, tq=T_s)
    _forward(prompt, sample, wts, tm=1024, tff_prompt=1024, tff_cast=512, tff_sample=1024)

    keep = min(WINDOW, x_prompt.shape[1])
    heads = (N_KV_HEADS, HEAD_DIM)
    return (prompt.x, sample.x,
            jnp.concatenate(prompt.new_pool, axis=0), jnp.concatenate(sample.new_pool, axis=0),
            prompt.k[:, -keep:].reshape(Bp, keep, *heads),
            prompt.v[:, -keep:].reshape(Bp, keep, *heads),
            sample.k.reshape(Bs, T_s, *heads), sample.v.reshape(Bs, T_s, *heads))
```

```python
import functools

import jax
import jax.numpy as jnp
from jax import lax
from jax.experimental import pallas as pl
from jax.experimental.pallas import tpu as pltpu

F32 = jnp.float32
BF16 = jnp.bfloat16

D_MODEL = 2048
DEPTH = 4
PAST_LEN = 4096
CHUNK = 64
N_A_LAYERS = DEPTH // 2
POOL_WINDOWS = (2, 4, 8, 16)
POOL_GROUP = D_MODEL // len(POOL_WINDOWS)
POOL_STATE = max(POOL_WINDOWS) - 1
SUBLANES = 8
POOL_LEAD = SUBLANES
POOL_BASE = POOL_LEAD + POOL_STATE + 1
HEAD_DIM = 64
N_HEADS = D_MODEL // HEAD_DIM
N_KV_HEADS = N_HEADS // 8
WINDOW = 128
D_FF = 4 * D_MODEL
ROPE_THETA = 10000.0
EPS = 1e-6
ATTN_SCALE = HEAD_DIM ** -0.5
NEG_INF = -1e30

LANES = 128
KEY_SPAN = 2 * LANES
N_PAIRS = N_HEADS // 2
PAIRS_PER_KV = N_PAIRS // N_KV_HEADS
MLP_OUT_CHUNK = 512
MOD_ROWS = 32
VMEM_LIMIT = 56 * 1024 * 1024


def _params(*sem):
    return pltpu.CompilerParams(dimension_semantics=sem, vmem_limit_bytes=VMEM_LIMIT)


def _norm_mod(x, g, shift, scale):
    ms = jnp.mean(x * x, axis=-1, keepdims=True)
    return (x * lax.rsqrt(ms + EPS)) * (g * (1.0 + scale)) + shift


def _gated_norm(y, g, gate):
    ms = jnp.mean(y * y, axis=-1, keepdims=True)
    return (y * lax.rsqrt(ms + EPS)) * (g * gate)


def _mod_kernel(c_ref, w_ref, b_ref, o_ref):
    c = c_ref[...]
    sc = (c * jax.nn.sigmoid(c)).astype(BF16)
    o_ref[0] = jnp.dot(sc, w_ref[0].astype(BF16), preferred_element_type=F32) + b_ref[0]


def _modulation(c_all, w, b, tn=2048):
    L, D, N = w.shape
    return pl.pallas_call(
        _mod_kernel,
        out_shape=jax.ShapeDtypeStruct((L, MOD_ROWS, N), F32),
        grid=(L, N // tn),
        in_specs=[
            pl.BlockSpec((MOD_ROWS, D), lambda l, n: (0, 0)),
            pl.BlockSpec((1, D, tn), lambda l, n: (l, 0, n)),
            pl.BlockSpec((1, 1, tn), lambda l, n: (l, 0, n)),
        ],
        out_specs=pl.BlockSpec((1, MOD_ROWS, tn), lambda l, n: (l, 0, n)),
        compiler_params=_params("parallel", "parallel"),
        name="modulation",
    )(c_all, w, b)


def _pool_kernel(x_ref, mod_ref, g_ref, pre_ref, wp_ref, ps_ref, o_ref, np_ref, hbuf, s1, s2,
                 *, bb, tt, pos0):
    t = pl.program_id(1)
    G = POOL_GROUP
    LEAD, BASE = POOL_LEAD, POOL_BASE
    L = BASE + tt

    @pl.when(t == 0)
    def _():
        hbuf[:, 0:LEAD + 1, :] = jnp.zeros((bb, LEAD + 1, D_MODEL), F32)
        hbuf[:, LEAD + 1:BASE, :] = pre_ref[0]
        s1[:, 0:LEAD, :] = jnp.zeros((bb, LEAD, 3 * G), F32)
        s2[:, 0:LEAD, :] = jnp.zeros((bb, LEAD, 3 * G), F32)

    @pl.when(t > 0)
    def _():
        hbuf[:, LEAD:BASE, :] = hbuf[:, tt + LEAD:tt + BASE, :]

    x = x_ref[...]
    m = mod_ref[...]
    h = _norm_mod(x, g_ref[:, 0:1, :], m[:, 0:1, :], m[:, 1:2, :])
    hbuf[:, BASE:L, :] = h
    np_ref[0] = hbuf[:, L - POOL_STATE:L, :]

    s1[:, LEAD:L, :] = hbuf[:, LEAD:L, G:] + hbuf[:, LEAD - 1:L - 1, G:]
    s2[:, LEAD:L, :] = s1[:, LEAD:L, :] + s1[:, LEAD - 2:L - 2, :]
    s1[:, 2 * LEAD:L, 0:2 * G] = (s2[:, 2 * LEAD:L, G:] +
                                  s2[:, 2 * LEAD - 4:L - 4, G:])
    sums = [
        hbuf[:, BASE:L, 0:G] + hbuf[:, BASE - 1:L - 1, 0:G],
        s2[:, BASE:L, 0:G],
        s1[:, BASE:L, 0:G],
        s1[:, BASE:L, G:2 * G] + s1[:, BASE - 8:L - 8, G:2 * G],
    ]

    pos = pos0 + t * tt + lax.broadcasted_iota(jnp.int32, (1, tt, LANES), 1)
    ys = []
    for g, w in enumerate(POOL_WINDOWS):
        cs = slice(g * POOL_GROUP, (g + 1) * POOL_GROUP)
        inv_cnt = 1.0 / jnp.minimum(w, pos + 1).astype(F32)
        inv_cnt = jnp.concatenate([inv_cnt] * (POOL_GROUP // LANES), axis=-1)
        pooled = sums[g] * inv_cnt - hbuf[:, BASE:L, cs]
        ys.append(jnp.dot(pooled.reshape(bb * tt, POOL_GROUP).astype(BF16), wp_ref[0, g],
                          preferred_element_type=F32))
    y = jnp.concatenate(ys, axis=-1).reshape(bb, tt, D_MODEL) * ps_ref[...]
    o_ref[...] = x + _gated_norm(y, g_ref[:, 1:2, :], m[:, 2:3, :])


def _pool_layer(x, mod, g_norm, prefix, w_pool, pool_scale, l, *, bb, tt, pos0):
    B, T, D = x.shape
    kern = functools.partial(_pool_kernel, bb=bb, tt=tt, pos0=pos0)
    return pl.pallas_call(
        kern,
        out_shape=(jax.ShapeDtypeStruct((B, T, D), F32),
                   jax.ShapeDtypeStruct((1, B, POOL_STATE, D), F32)),
        grid=(B // bb, T // tt),
        in_specs=[
            pl.BlockSpec((bb, tt, D), lambda b, t: (b, t, 0)),
            pl.BlockSpec((bb, 6, D), lambda b, t: (b, 0, 0)),
            pl.BlockSpec((1, 4, D), lambda b, t: (l, 0, 0)),
            pl.BlockSpec((1, bb, POOL_STATE, D), lambda b, t: (l, b, 0, 0)),
            pl.BlockSpec((1, len(POOL_WINDOWS), POOL_GROUP, POOL_GROUP),
                         lambda b, t: (l, 0, 0, 0)),
            pl.BlockSpec((1, 1, D), lambda b, t: (l, 0, 0)),
        ],
        out_specs=(pl.BlockSpec((bb, tt, D), lambda b, t: (b, t, 0)),
                   pl.BlockSpec((1, bb, POOL_STATE, D), lambda b, t: (0, b, 0, 0))),
        scratch_shapes=[pltpu.VMEM((bb, POOL_BASE + tt, D), F32),
                        pltpu.VMEM((bb, POOL_BASE + tt, 3 * POOL_GROUP), F32),
                        pltpu.VMEM((bb, POOL_BASE + tt, 3 * POOL_GROUP), F32)],
        compiler_params=_params("parallel", "arbitrary"),
        name=f"pool_mixer_{l}",
    )(x, mod, g_norm, prefix, w_pool, pool_scale)


def _rope_tables(pos):
    half = HEAD_DIM // 2
    inv = ROPE_THETA ** (-jnp.arange(half, dtype=F32) / half)
    ang = pos.astype(F32)[:, None] * inv[None, :]
    cos, sin = jnp.cos(ang), jnp.sin(ang)
    zero = jnp.zeros_like(sin)
    c = jnp.tile(cos, (1, 4))
    s_lo = jnp.tile(jnp.concatenate([-sin, zero], axis=1), (1, 2))
    s_hi = jnp.tile(jnp.concatenate([zero, sin], axis=1), (1, 2))
    return c, s_lo, s_hi


def _rope_block(blk, c, s_lo, s_hi):
    return (blk * c + pltpu.roll(blk, LANES - HEAD_DIM // 2, 1) * s_lo
            + pltpu.roll(blk, HEAD_DIM // 2, 1) * s_hi)


def _qproj_kernel(x_ref, mod_ref, g_ref, c_ref, slo_ref, shi_ref, w_ref, q_ref, *, bb, tt):
    x = x_ref[...]
    m = mod_ref[...]
    h = _norm_mod(x, g_ref[:, 0:1, :], m[:, 0:1, :], m[:, 1:2, :])
    q = jnp.dot(h.reshape(bb * tt, D_MODEL).astype(BF16), w_ref[0], preferred_element_type=F32)
    c, s_lo, s_hi = c_ref[...], slo_ref[...], shi_ref[...]
    for p in range(N_PAIRS):
        cs = slice(p * LANES, (p + 1) * LANES)
        q_ref[p] = (_rope_block(q[:, cs], c, s_lo, s_hi) * ATTN_SCALE).astype(BF16)


def _qproj_layer(x, mod, g_norm, tables, w_q, l, j, *, bb, tt):
    B, T, D = x.shape
    nt = T // tt
    M = bb * tt
    kern = functools.partial(_qproj_kernel, bb=bb, tt=tt)
    tab_spec = pl.BlockSpec((M, LANES), lambda b, t: (t, 0))
    return pl.pallas_call(
        kern,
        out_shape=jax.ShapeDtypeStruct((N_PAIRS, B * T, LANES), BF16),
        grid=(B // bb, nt),
        in_specs=[
            pl.BlockSpec((bb, tt, D), lambda b, t: (b, t, 0)),
            pl.BlockSpec((bb, 6, D), lambda b, t: (b, 0, 0)),
            pl.BlockSpec((1, 4, D), lambda b, t: (l, 0, 0)),
            tab_spec, tab_spec, tab_spec,
            pl.BlockSpec((1, D, D), lambda b, t: (j, 0, 0)),
        ],
        out_specs=pl.BlockSpec((N_PAIRS, M, LANES), lambda b, t: (0, b * nt + t, 0)),
        compiler_params=_params("parallel", "parallel"),
        name=f"q_proj_{l}",
    )(x, mod, g_norm, *tables, w_q)


def _masked_pairs(a):
    M = a.shape[0]
    lo = lax.broadcasted_iota(jnp.int32, (M, LANES), 1) < HEAD_DIM
    zero = jnp.zeros((M, LANES), F32)
    out = []
    for p in range(N_KV_HEADS // 2):
        blk = a[:, p * LANES:(p + 1) * LANES]
        swapped = pltpu.roll(blk, HEAD_DIM, 1)
        out += [jnp.where(lo, blk, zero), jnp.where(lo, zero, swapped),
                jnp.where(lo, swapped, zero), jnp.where(lo, zero, blk)]
    return jnp.concatenate([part.astype(BF16) for part in out], axis=1)


def _kv_kernel(x_ref, mod_ref, g_ref, c_ref, slo_ref, shi_ref, w_ref,
               k_ref, v_ref, kd_ref, vd_ref, *, bb, tt):
    x = x_ref[...]
    m = mod_ref[...]
    h = _norm_mod(x, g_ref[...], m[:, 0:1, :], m[:, 1:2, :])
    kv = jnp.dot(h.reshape(bb * tt, D_MODEL).astype(BF16), w_ref[...],
                 preferred_element_type=F32)
    c, s_lo, s_hi = c_ref[...], slo_ref[...], shi_ref[...]
    KV = N_KV_HEADS * HEAD_DIM
    k = jnp.concatenate([_rope_block(kv[:, p * LANES:(p + 1) * LANES], c, s_lo, s_hi)
                         for p in range(KV // LANES)], axis=1)
    v = kv[:, KV:]
    k_ref[...] = k
    v_ref[...] = v
    kd_ref[...] = _masked_pairs(k).reshape(bb, tt, 4 * KV)
    vd_ref[...] = _masked_pairs(v).reshape(bb, tt, 4 * KV)


def _kv_layer(x, mod, g_kv, tables, w_kv, *, bb, tt):
    B, T, D = x.shape
    nt = T // tt
    M = bb * tt
    KV = N_KV_HEADS * HEAD_DIM
    kern = functools.partial(_kv_kernel, bb=bb, tt=tt)
    tab_spec = pl.BlockSpec((M, LANES), lambda b, t: (t, 0))
    tile = lambda b, t: (b, t, 0)
    newest = lambda b, t: (b, 0)
    return pl.pallas_call(
        kern,
        out_shape=(jax.ShapeDtypeStruct((B // bb * M, KV), F32),
                   jax.ShapeDtypeStruct((B // bb * M, KV), F32),
                   jax.ShapeDtypeStruct((B, T, 4 * KV), BF16),
                   jax.ShapeDtypeStruct((B, T, 4 * KV), BF16)),
        grid=(B // bb, nt),
        in_specs=[
            pl.BlockSpec((bb, tt, D), lambda b, t: (b, t, 0)),
            pl.BlockSpec((bb, 2, D), lambda b, t: (b, 0, 0)),
            pl.BlockSpec((1, 1, D), lambda b, t: (0, 0, 0)),
            tab_spec, tab_spec, tab_spec,
            pl.BlockSpec((D, 2 * KV), lambda b, t: (0, 0)),
        ],
        out_specs=(pl.BlockSpec((M, KV), newest), pl.BlockSpec((M, KV), newest),
                   pl.BlockSpec((bb, tt, 4 * KV), tile), pl.BlockSpec((bb, tt, 4 * KV), tile)),
        compiler_params=_params("parallel", "arbitrary"),
        name="shared_kv",
    )(x, mod, g_kv, *tables, w_kv)


def _sink_softmax(s, sink):
    m = jnp.max(s, axis=-1, keepdims=True)
    e = jnp.exp(s - m)
    den = jnp.sum(e, axis=-1, keepdims=True) + jnp.exp(sink - m)
    return e * (1.0 / den)


def _attn_kernel(sink_ref, *refs, tq, fused, n_valid=None):
    i = pl.program_id(1)
    G = PAIRS_PER_KV
    ts = WINDOW if fused else tq
    nsub = tq // ts
    R = G * ts
    QW = D_MODEL // N_KV_HEADS
    kp = lax.broadcasted_iota(jnp.int32, (R, KEY_SPAN), 1)
    rblk = lax.broadcasted_iota(jnp.int32, (R, 1), 0) // ts
    nt = (((1,), (1,)), ((), ()))
    sink_cols = []
    for j in range(N_KV_HEADS):
        for parity in range(2):
            col = jnp.full((R, 1), sink_ref[2 * G * j + parity], F32)
            for r in range(1, G):
                col = jnp.where(rblk == r, sink_ref[2 * (G * j + r) + parity], col)
            sink_cols.append(col)
    if fused:
        q_ref, k_prev, k_cur, v_prev, v_cur, x_ref, mod_ref, g_ref, wo_ref, o_ref = refs
        row = lax.broadcasted_iota(jnp.int32, (R, KEY_SPAN), 0)
        cq = (row % ts) // CHUNK
        kb = kp // CHUNK
        in_window = (kb >= cq) & (kb <= cq + WINDOW // CHUNK)
    else:
        q_ref, ck_ref, cv_ref, kd_ref, vd_ref, o_ref = refs

    def finish(rows, mix):
        m = mod_ref[0]
        o_ref[rows, :] = x_ref[rows, :] + _gated_norm(mix, g_ref[0, 1:2, :], m[2:3, :])

    def with_cache(c_ref, new_ref):
        past = _masked_pairs(c_ref[0])
        pad = jnp.zeros((KEY_SPAN - past.shape[0] - tq, past.shape[1]), BF16)
        return jnp.concatenate([past, new_ref[0], pad], axis=0)

    def keys_of(sub):
        rs = slice(sub * ts, (sub + 1) * ts)
        if not fused:
            kw, vw = with_cache(ck_ref, kd_ref), with_cache(cv_ref, vd_ref)
            valid = kp < n_valid
        elif sub == 0:
            kw = jnp.concatenate([k_prev[0], k_cur[0, rs, :]], axis=0)
            vw = jnp.concatenate([v_prev[0], v_cur[0, rs, :]], axis=0)
            valid = in_window & (kp >= jnp.where(i > 0, 0, WINDOW))
        else:
            span = slice(sub * ts - WINDOW, (sub + 1) * ts)
            kw, vw = k_cur[0, span, :], v_cur[0, span, :]
            valid = in_window
        return rs, kw, vw, jnp.where(valid, 0.0, NEG_INF)

    def scores(keys, j):
        rs, kw, _, bias = keys
        qg = q_ref[G * j:G * (j + 1), rs, :].reshape(R, LANES)
        k_lo = kw[:, (2 * j) * LANES:(2 * j + 1) * LANES]
        k_hi = kw[:, (2 * j + 1) * LANES:(2 * j + 2) * LANES]
        return (lax.dot_general(qg, k_lo, nt, preferred_element_type=F32) + bias,
                lax.dot_general(qg, k_hi, nt, preferred_element_type=F32) + bias)

    groups = [(sub, j) for sub in range(nsub) for j in range(N_KV_HEADS)]
    keys = {0: keys_of(0)}
    ahead = scores(keys[0], 0)
    pending = None
    heads, mix_parts = [], []
    for n, (sub, j) in enumerate(groups):
        s0, s1 = ahead
        if n + 1 < len(groups):
            sub_n, j_n = groups[n + 1]
            if sub_n not in keys:
                keys[sub_n] = keys_of(sub_n)
            ahead = scores(keys[sub_n], j_n)
        if pending is not None:
            mix_parts.append(jnp.dot(pending[1], wo_ref[0, :, j * QW:(j + 1) * QW],
                                     preferred_element_type=F32))
        rs, _, vw, _ = keys[sub]
        v_lo = vw[:, (2 * j) * LANES:(2 * j + 1) * LANES]
        v_hi = vw[:, (2 * j + 1) * LANES:(2 * j + 2) * LANES]
        p0 = _sink_softmax(s0, sink_cols[2 * j])
        p1 = _sink_softmax(s1, sink_cols[2 * j + 1])
        o = (jnp.dot(p0.astype(BF16), v_lo, preferred_element_type=F32)
             + jnp.dot(p1.astype(BF16), v_hi, preferred_element_type=F32)).astype(BF16)
        if not fused:
            o_ref[G * j:G * (j + 1), rs, :] = o.reshape(G, ts, LANES)
            continue
        heads.extend(o[r * ts:(r + 1) * ts] for r in range(G))
        if j == N_KV_HEADS - 1:
            if pending is not None:
                finish(pending[0], jnp.concatenate(mix_parts, axis=1))
            pending = (rs, jnp.concatenate(heads, axis=1))
            heads, mix_parts = [], []
    if fused:
        finish(pending[0], jnp.dot(pending[1], wo_ref[0], preferred_element_type=F32))


def _attention(q, cache_k, cache_v, kd, vd, sinks):
    B, T, W = kd.shape
    S = cache_k.shape[1]
    assert S + T <= KEY_SPAN
    kern = functools.partial(_attn_kernel, tq=T, fused=False, n_valid=S + T)
    past = pl.BlockSpec((1, S, cache_k.shape[2]), lambda b, i: (b, 0, 0))
    new = pl.BlockSpec((1, T, W), lambda b, i: (b, 0, 0))
    pair_rows = pl.BlockSpec((N_PAIRS, T, LANES), lambda b, i: (0, b, 0))
    return pl.pallas_call(
        kern,
        out_shape=jax.ShapeDtypeStruct((N_PAIRS, B * T, LANES), BF16),
        grid=(B, 1),
        in_specs=[pl.BlockSpec(memory_space=pltpu.SMEM), pair_rows, past, past, new, new],
        out_specs=pair_rows,
        compiler_params=_params("parallel", "parallel"),
        name="swa_attention",
    )(sinks, q, cache_k, cache_v, kd, vd)


def _window_specs(kd, vd, T, tq):
    W = kd.shape[2]
    assert tq % WINDOW == 0 and kd.shape[1] == T
    nsub = tq // WINDOW
    prev = pl.BlockSpec((1, WINDOW, W), lambda b, i: (b, jnp.maximum(i * nsub - 1, 0), 0))
    cur = pl.BlockSpec((1, tq, W), lambda b, i: (b, i, 0))
    return [prev, cur, prev, cur], (kd, kd, vd, vd)


def _attention_out(q, x, mod, g_norm, w_o, kd, vd, sinks, l, j, *, tq):
    B, T, D = x.shape
    nq = T // tq
    kern = functools.partial(_attn_kernel, tq=tq, fused=True)
    kv_specs, kv_args = _window_specs(kd, vd, T, tq)
    rows = lambda b, i: (b * nq + i, 0)
    out = pl.pallas_call(
        kern,
        out_shape=jax.ShapeDtypeStruct((B * T, D), F32),
        grid=(B, nq),
        in_specs=[
            pl.BlockSpec(memory_space=pltpu.SMEM),
            pl.BlockSpec((N_PAIRS, tq, LANES), lambda b, i: (0, b * nq + i, 0)),
            *kv_specs,
            pl.BlockSpec((tq, D), rows),
            pl.BlockSpec((1, 6, D), lambda b, i: (b, 0, 0)),
            pl.BlockSpec((1, 4, D), lambda b, i: (l, 0, 0)),
            pl.BlockSpec((1, D, D), lambda b, i: (j, 0, 0)),
        ],
        out_specs=pl.BlockSpec((tq, D), rows),
        compiler_params=_params("parallel", "parallel"),
        name=f"attention_out_{l}",
    )(sinks, q, *kv_args, x.reshape(B * T, D), mod, g_norm, w_o)
    return out.reshape(B, T, D)


def _oproj_kernel(o_ref, x_ref, mod_ref, g_ref, w_ref, y_ref, *, bb, tt):
    o = jnp.concatenate([o_ref[p] for p in range(N_PAIRS)], axis=1)
    mix = jnp.dot(o, w_ref[0], preferred_element_type=F32).reshape(bb, tt, D_MODEL)
    m = mod_ref[...]
    y_ref[...] = x_ref[...] + _gated_norm(mix, g_ref[:, 1:2, :], m[:, 2:3, :])


def _oproj_layer(o, x, mod, g_norm, w_o, l, j, *, bb, tt):
    B, T, D = x.shape
    nt = T // tt
    kern = functools.partial(_oproj_kernel, bb=bb, tt=tt)
    return pl.pallas_call(
        kern,
        out_shape=jax.ShapeDtypeStruct((B, T, D), F32),
        grid=(B // bb, nt),
        in_specs=[
            pl.BlockSpec((N_PAIRS, bb * tt, LANES), lambda b, t: (0, b * nt + t, 0)),
            pl.BlockSpec((bb, tt, D), lambda b, t: (b, t, 0)),
            pl.BlockSpec((bb, 6, D), lambda b, t: (b, 0, 0)),
            pl.BlockSpec((1, 4, D), lambda b, t: (l, 0, 0)),
            pl.BlockSpec((1, D, D), lambda b, t: (j, 0, 0)),
        ],
        out_specs=pl.BlockSpec((bb, tt, D), lambda b, t: (b, t, 0)),
        compiler_params=_params("parallel", "parallel"),
        name=f"o_proj_{l}",
    )(o, x, mod, g_norm, w_o)


def _mlp_kernel(x_ref, mod_ref, g_ref, wu_ref, wd_ref, y_ref, *rest, bb, tt, cast):
    f = pl.program_id(2)
    m = mod_ref[...]
    if cast:
        wu_out, wd_out, h_ref = rest
    else:
        h_ref, = rest

    @pl.when(f == 0)
    def _():
        h = _norm_mod(x_ref[...], g_ref[:, 2:3, :], m[:, 3:4, :], m[:, 4:5, :])
        h_ref[...] = h.reshape(bb * tt, D_MODEL).astype(BF16)
        y_ref[...] = jnp.zeros_like(y_ref)

    wu = wu_ref[0]
    if cast:
        wu = wu.astype(BF16)
        wu_out[0] = wu
    u = jnp.dot(h_ref[...], wu, preferred_element_type=F32)
    a = jnp.square(jnp.maximum(u, 0.0)).astype(BF16)
    for n in range(D_MODEL // MLP_OUT_CHUNK):
        cs = slice(n * MLP_OUT_CHUNK, (n + 1) * MLP_OUT_CHUNK)
        wd_n = wd_ref[0, :, cs]
        if cast:
            wd_n = wd_n.astype(BF16)
            wd_out[0, :, cs] = wd_n
        y_ref[:, :, cs] += jnp.dot(a, wd_n, preferred_element_type=F32).reshape(
            bb, tt, MLP_OUT_CHUNK)

    @pl.when(f == pl.num_programs(2) - 1)
    def _():
        y_ref[...] = x_ref[...] + _gated_norm(y_ref[...], g_ref[:, 3:4, :], m[:, 5:6, :])


def _mlp_layer(x, mod, g_norm, w_up, w_down, l, *, bb, tt, tff, cast):
    B, T, D = x.shape
    grid = (B // bb, T // tt, D_FF // tff)
    kern = functools.partial(_mlp_kernel, bb=bb, tt=tt, cast=cast)
    wl = l if cast else 0
    y_shape = jax.ShapeDtypeStruct((B, T, D), F32)
    y_spec = pl.BlockSpec((bb, tt, D), lambda b, t, f: (b, t, 0))
    if cast:
        assert grid[0] * grid[1] == 1
        out_shape = (y_shape, jax.ShapeDtypeStruct((1, D, D_FF), BF16),
                     jax.ShapeDtypeStruct((1, D_FF, D), BF16))
        out_specs = (y_spec, pl.BlockSpec((1, D, tff), lambda b, t, f: (0, 0, f)),
                     pl.BlockSpec((1, tff, D), lambda b, t, f: (0, f, 0)))
    else:
        out_shape, out_specs = y_shape, y_spec
    return pl.pallas_call(
        kern,
        out_shape=out_shape,
        grid=grid,
        in_specs=[
            pl.BlockSpec((bb, tt, D), lambda b, t, f: (b, t, 0)),
            pl.BlockSpec((bb, 6, D), lambda b, t, f: (b, 0, 0)),
            pl.BlockSpec((1, 4, D), lambda b, t, f: (l, 0, 0)),
            pl.BlockSpec((1, D, tff), lambda b, t, f: (wl, 0, f)),
            pl.BlockSpec((1, tff, D), lambda b, t, f: (wl, f, 0)),
        ],
        out_specs=out_specs,
        scratch_shapes=[pltpu.VMEM((bb * tt, D), BF16)],
        compiler_params=_params("parallel", "parallel", "arbitrary"),
        name=f"mlp_{l}",
    )(x, mod, g_norm, w_up, w_down)


def _zero_after(x):
    bits = pltpu.bitcast(x, jnp.uint32)
    r = bits[:, 0:LANES]
    for k in range(1, x.shape[1] // LANES):
        r = r | bits[:, k * LANES:(k + 1) * LANES]
    r8 = r[0:SUBLANES]
    for k in range(1, x.shape[0] // SUBLANES):
        r8 = r8 | r[k * SUBLANES:(k + 1) * SUBLANES]
    z = pltpu.bitcast((r8 >> 16) >> 16, F32)
    return jnp.max(z, axis=(0, 1), keepdims=True)


def _mlp_skew_kernel(xn_ref, xp_ref, modn_ref, modp_ref, g_ref, wu_ref, wd_ref, *rest, nt, rs,
                     cast_next):
    s = pl.program_id(0)
    f = pl.program_id(1)
    if cast_next:
        nu_ref, nd_ref, y_ref, nu_out, nd_out, h0_ref, h1_ref, acc0_ref, acc1_ref = rest
    else:
        y_ref, h0_ref, h1_ref, acc0_ref, acc1_ref = rest
    rows = pl.ds(pl.multiple_of(f * rs, rs), rs)
    g_pre, g_post = g_ref[0, 2:3, :], g_ref[0, 3:4, :]
    h_refs = (h0_ref, h1_ref)
    acc_refs = (acc0_ref, acc1_ref)

    def prologue(h_ref):
        m = modn_ref[0]
        h = _norm_mod(xn_ref[...], g_pre, m[3:4, :], m[4:5, :])
        h_ref[rows, :] = h.astype(BF16)
        return h

    def epilogue(acc_ref):
        mp = modp_ref[0]
        y = xp_ref[...] + _gated_norm(acc_ref[rows, :], g_post, mp[5:6, :])
        y_ref[...] = y
        acc_ref[rows, :] = jnp.zeros((rs, D_MODEL), F32)
        return y

    def matmuls(h_ref, acc_ref, anchors):
        if cast_next:
            nu_out[...] = nu_ref[...].astype(BF16)
            nd_out[...] = nd_ref[...].astype(BF16)
        u = jnp.dot(h_ref[...], wu_ref[0], preferred_element_type=F32)
        a = jnp.square(jnp.maximum(u, 0.0)).astype(BF16)
        for n in range(D_MODEL // MLP_OUT_CHUNK):
            cs = slice(n * MLP_OUT_CHUNK, (n + 1) * MLP_OUT_CHUNK)
            d = jnp.dot(a, wd_ref[0, :, cs], preferred_element_type=F32)
            if anchors.get(n) is not None:
                d = d + anchors[n]
            acc_ref[:, cs] += d

    @pl.when((s == 0) & (f == 0))
    def _():
        acc0_ref[...] = jnp.zeros_like(acc0_ref)
        acc1_ref[...] = jnp.zeros_like(acc1_ref)

    @pl.when(s == 0)
    def _():
        prologue(h0_ref)

    @pl.when(s == 1)
    def _():
        h = prologue(h1_ref)
        matmuls(h0_ref, acc0_ref, {2: _zero_after(h)})

    for c in range(2):
        o = 1 - c

        @pl.when(((s - 1) % 2 == c) & (s >= 2) & (s < nt))
        def _():
            y = epilogue(acc_refs[o])
            h = prologue(h_refs[o])
            matmuls(h_refs[c], acc_refs[c], {0: _zero_after(y), 2: _zero_after(h)})

    last = (nt - 1) % 2

    @pl.when(s == nt)
    def _():
        y = epilogue(acc_refs[1 - last])
        matmuls(h_refs[last], acc_refs[last], {0: _zero_after(y)})

    @pl.when(s == nt + 1)
    def _():
        epilogue(acc_refs[last])


def _mlp_layer_skewed(x, mod, g_norm, w_up, w_down, l, *, tm, tff, next_f32=None):
    B, T, D = x.shape
    nf = D_FF // tff
    rs = tm // nf
    tpb = T // tm
    nt = B * tpb
    assert nt >= 3
    x2 = x.reshape(B * T, D)
    nxt = lambda s: jnp.minimum(s, nt - 1)
    prv = lambda s: jnp.maximum(s - 2, 0)
    wf = lambda s, f: jnp.where(s == 0, 0, jnp.where(s == nt + 1, nf - 1, f))
    cast_next = next_f32 is not None
    kern = functools.partial(_mlp_skew_kernel, nt=nt, rs=rs, cast_next=cast_next)
    in_specs = [
        pl.BlockSpec((rs, D), lambda s, f: (nxt(s) * nf + f, 0)),
        pl.BlockSpec((rs, D), lambda s, f: (prv(s) * nf + f, 0)),
        pl.BlockSpec((1, 6, D), lambda s, f: (nxt(s) // tpb, 0, 0)),
        pl.BlockSpec((1, 6, D), lambda s, f: (prv(s) // tpb, 0, 0)),
        pl.BlockSpec((1, 4, D), lambda s, f: (l, 0, 0)),
        pl.BlockSpec((1, D, tff), lambda s, f: (0, 0, wf(s, f))),
        pl.BlockSpec((1, tff, D), lambda s, f: (0, wf(s, f), 0)),
    ]
    args = [x2, x2, mod, mod, g_norm, w_up, w_down]
    out_specs = [pl.BlockSpec((rs, D), lambda s, f: (jnp.where(s < 2, 0, (s - 2) * nf + f), 0))]
    out_shape = [jax.ShapeDtypeStruct((B * T, D), F32)]
    if cast_next:
        n_slabs = nt * nf
        ru, rd = D // n_slabs, D_FF // n_slabs
        slab = lambda s, f: jnp.clip((s - 1) * nf + f, 0, n_slabs - 1)
        up_spec = lambda: pl.BlockSpec((1, ru, D_FF), lambda s, f: (l + 1, slab(s, f), 0))
        dn_spec = lambda: pl.BlockSpec((1, rd, D), lambda s, f: (l + 1, slab(s, f), 0))
        in_specs += [up_spec(), dn_spec()]
        args += list(next_f32)
        out_specs += [pl.BlockSpec((1, ru, D_FF), lambda s, f: (0, slab(s, f), 0)),
                      pl.BlockSpec((1, rd, D), lambda s, f: (0, slab(s, f), 0))]
        out_shape += [jax.ShapeDtypeStruct((1, D, D_FF), BF16),
                      jax.ShapeDtypeStruct((1, D_FF, D), BF16)]
    outs = pl.pallas_call(
        kern,
        out_shape=tuple(out_shape),
        grid=(nt + 2, nf),
        in_specs=in_specs,
        out_specs=tuple(out_specs),
        scratch_shapes=[pltpu.VMEM((tm, D), BF16), pltpu.VMEM((tm, D), BF16),
                        pltpu.VMEM((tm, D), F32), pltpu.VMEM((tm, D), F32)],
        compiler_params=_params("arbitrary", "arbitrary"),
        name=f"mlp_skewed_{l}",
    )(*args)
    y = outs[0].reshape(B, T, D)
    return (y, outs[1], outs[2]) if cast_next else (y, None, None)


class _Path:
    def __init__(self, x, mods, kvmod, pos0, prefix, kv_past, *, bb, tt, tq):
        self.x, self.mods, self.kvmod, self.pos0 = x, mods, kvmod, pos0
        self.prefix, self.kv_past = prefix, kv_past
        self.bb, self.tt, self.tq = bb, tt, tq
        pos = pos0 + jnp.arange(x.shape[1])
        self.tables = tuple(jnp.tile(tb, (bb, 1)) if bb > 1 else tb for tb in _rope_tables(pos))
        self.new_pool = []
        self.k = self.v = self.kd = self.vd = None

    def mixer(self, l, wts):
        B, T, _ = self.x.shape
        mod, bb, tt = self.mods[l], self.bb, self.tt
        if l < N_A_LAYERS:
            self.x, npool = _pool_layer(self.x, mod, wts["g_norm"], self.prefix, wts["w_pool"],
                                        wts["pool_scale"], l, bb=bb, tt=tt, pos0=self.pos0)
            self.new_pool.append(npool)
            return
        j = l - N_A_LAYERS
        q = _qproj_layer(self.x, mod, wts["g_norm"], self.tables, wts["w_q"], l, j, bb=bb, tt=tt)
        if self.kv_past is None:
            self.x = _attention_out(q, self.x, mod, wts["g_norm"], wts["w_o"], self.kd, self.vd,
                                    wts["sinks"][j], l, j, tq=self.tq)
        else:
            past_k, past_v = (c.reshape(B, c.shape[1], -1) for c in self.kv_past)
            o = _attention(q, past_k, past_v, self.kd, self.vd, wts["sinks"][j])
            self.x = _oproj_layer(o, self.x, mod, wts["g_norm"], wts["w_o"], l, j, bb=bb, tt=tt)

    def shared_kv(self, wts):
        B, T, _ = self.x.shape
        k, v, kd, vd = _kv_layer(self.x, self.kvmod, wts["g_kv"], self.tables, wts["w_kv"],
                                 bb=self.bb, tt=self.tt)
        KV = N_KV_HEADS * HEAD_DIM
        self.k, self.v = k.reshape(B, self.tt, KV), v.reshape(B, self.tt, KV)
        self.kd, self.vd = kd, vd


def _forward(prompt, sample, wts, *, tm, tff_prompt, tff_cast, tff_sample):
    w_f32 = (wts["w_up"], wts["w_down"])
    w_bf16 = None
    for l in range(DEPTH):
        sample.mixer(l, wts)
        if l == 0:
            sample.x, wu, wd = _mlp_layer(sample.x, sample.mods[l], wts["g_norm"], *w_f32, l,
                                          bb=sample.bb, tt=sample.tt, tff=tff_cast, cast=True)
            w_bf16 = (wu, wd)
        else:
            sample.x = _mlp_layer(sample.x, sample.mods[l], wts["g_norm"], *w_bf16, l,
                                  bb=sample.bb, tt=sample.tt, tff=tff_sample, cast=False)
        prompt.mixer(l, wts)
        prompt.x, wu, wd = _mlp_layer_skewed(
            prompt.x, prompt.mods[l], wts["g_norm"], *w_bf16, l, tm=tm, tff=tff_prompt,
            next_f32=w_f32 if l + 1 < DEPTH else None)
        w_bf16 = (wu, wd)
        if l == N_A_LAYERS - 1:
            sample.shared_kv(wts)
            prompt.shared_kv(wts)


def _prep_weights(g_norm, w_pool, pool_scale, g_kv, w_kv, w_q, sinks, w_o, w_up, w_down):
    D = D_MODEL
    return {
        "g_norm": g_norm,
        "w_pool": w_pool.astype(BF16),
        "pool_scale": pool_scale.reshape(N_A_LAYERS, 1, D),
        "g_kv": g_kv.reshape(1, 1, D),
        "w_kv": w_kv.astype(BF16),
        "w_q": w_q.astype(BF16),
        "sinks": sinks,
        "w_o": w_o.astype(BF16),
        "w_up": w_up,
        "w_down": w_down,
    }


def kernel(x_prompt, x_sample, c_prompt, c_sample, state_pool, cache_k, cache_v, w_mod, b_mod,
           g_norm, w_pool, pool_scale, w_kv_mod, b_kv_mod, g_kv, w_kv, w_q, sinks, w_o, w_up,
           w_down):
    Bp, Bs = x_prompt.shape[0], x_sample.shape[0]
    T_s = x_sample.shape[1]
    D = D_MODEL

    c_all = jnp.concatenate(
        [c_prompt, c_sample, jnp.zeros((MOD_ROWS - Bp - Bs, D), F32)], axis=0)
    mod_all = _modulation(c_all, w_mod, b_mod.reshape(DEPTH, 1, 6 * D)).reshape(
        DEPTH, MOD_ROWS, 6, D)
    kvmod_all = _modulation(c_all, w_kv_mod.reshape(1, D, 2 * D),
                            b_kv_mod.reshape(1, 1, 2 * D)).reshape(MOD_ROWS, 2, D)
    wts = _prep_weights(g_norm, w_pool, pool_scale, g_kv, w_kv, w_q, sinks, w_o, w_up, w_down)

    prompt = _Path(x_prompt, [mod_all[l, :Bp] for l in range(DEPTH)], kvmod_all[:Bp], 0,
                   jnp.zeros((N_A_LAYERS, Bp, POOL_STATE, D), F32), None,
                   bb=1, tt=512, tq=4 * WINDOW)
    sample = _Path(x_sample, [mod_all[l, Bp:Bp + Bs] for l in range(DEPTH)],
                   kvmod_all[Bp:Bp + Bs], PAST_LEN, state_pool, (cache_k, cache_v),
                   bb=Bs, tt=T_s, tq=T_s)
    _forward(prompt, sample, wts, tm=1024, tff_prompt=1024, tff_cast=512, tff_sample=1024)

    keep = min(WINDOW, x_prompt.shape[1])
    heads = (N_KV_HEADS, HEAD_DIM)
    return (prompt.x, sample.x,
            jnp.concatenate(prompt.new_pool, axis=0), jnp.concatenate(sample.new_pool, axis=0),
            prompt.k[:, -keep:].reshape(Bp, keep, *heads),
            prompt.v[:, -keep:].reshape(Bp, keep, *heads),
            sample.k.reshape(Bs, T_s, *heads), sample.v.reshape(Bs, T_s, *heads))
```

```python
import functools

import jax
import jax.numpy as jnp
from jax import lax
from jax.experimental import pallas as pl
from jax.experimental.pallas import tpu as pltpu

F32 = jnp.float32
BF16 = jnp.bfloat16

D_MODEL = 2048
DEPTH = 4
PAST_LEN = 4096
CHUNK = 64
N_A_LAYERS = DEPTH // 2
POOL_WINDOWS = (2, 4, 8, 16)
POOL_GROUP = D_MODEL // len(POOL_WINDOWS)
POOL_STATE = max(POOL_WINDOWS) - 1
SUBLANES = 8
POOL_LEAD = SUBLANES
POOL_BASE = POOL_LEAD + POOL_STATE + 1
HEAD_DIM = 64
N_HEADS = D_MODEL // HEAD_DIM
N_KV_HEADS = N_HEADS // 8
WINDOW = 128
D_FF = 4 * D_MODEL
ROPE_THETA = 10000.0
EPS = 1e-6
ATTN_SCALE = HEAD_DIM ** -0.5
NEG_INF = -1e30

LANES = 128
KEY_SPAN = 2 * LANES
N_PAIRS = N_HEADS // 2
PAIRS_PER_KV = N_PAIRS // N_KV_HEADS
MLP_OUT_CHUNK = 512
MOD_ROWS = 32
VMEM_LIMIT = 56 * 1024 * 1024


def _params(*sem):
    return pltpu.CompilerParams(dimension_semantics=sem, vmem_limit_bytes=VMEM_LIMIT)


def _norm_mod(x, g, shift, scale):
    ms = jnp.mean(x * x, axis=-1, keepdims=True)
    return (x * lax.rsqrt(ms + EPS)) * (g * (1.0 + scale)) + shift


def _gated_norm(y, g, gate):
    ms = jnp.mean(y * y, axis=-1, keepdims=True)
    return (y * lax.rsqrt(ms + EPS)) * (g * gate)


def _mod_kernel(c_ref, w_ref, b_ref, o_ref):
    c = c_ref[...]
    sc = (c * jax.nn.sigmoid(c)).astype(BF16)
    o_ref[0] = jnp.dot(sc, w_ref[0].astype(BF16), preferred_element_type=F32) + b_ref[0]


def _modulation(c_all, w, b, tn=2048):
    L, D, N = w.shape
    return pl.pallas_call(
        _mod_kernel,
        out_shape=jax.ShapeDtypeStruct((L, MOD_ROWS, N), F32),
        grid=(L, N // tn),
        in_specs=[
            pl.BlockSpec((MOD_ROWS, D), lambda l, n: (0, 0)),
            pl.BlockSpec((1, D, tn), lambda l, n: (l, 0, n)),
            pl.BlockSpec((1, 1, tn), lambda l, n: (l, 0, n)),
        ],
        out_specs=pl.BlockSpec((1, MOD_ROWS, tn), lambda l, n: (l, 0, n)),
        compiler_params=_params("parallel", "parallel"),
        name="modulation",
    )(c_all, w, b)


def _pool_kernel(x_ref, mod_ref, g_ref, pre_ref, wp_ref, ps_ref, o_ref, np_ref, hbuf, s1, s2,
                 *, bb, tt, pos0):
    t = pl.program_id(1)
    G = POOL_GROUP
    LEAD, BASE = POOL_LEAD, POOL_BASE
    L = BASE + tt

    @pl.when(t == 0)
    def _():
        hbuf[:, 0:LEAD + 1, :] = jnp.zeros((bb, LEAD + 1, D_MODEL), F32)
        hbuf[:, LEAD + 1:BASE, :] = pre_ref[0]
        s1[:, 0:LEAD, :] = jnp.zeros((bb, LEAD, 3 * G), F32)
        s2[:, 0:LEAD, :] = jnp.zeros((bb, LEAD, 3 * G), F32)

    @pl.when(t > 0)
    def _():
        hbuf[:, LEAD:BASE, :] = hbuf[:, tt + LEAD:tt + BASE, :]

    x = x_ref[...]
    m = mod_ref[...]
    h = _norm_mod(x, g_ref[:, 0:1, :], m[:, 0:1, :], m[:, 1:2, :])
    hbuf[:, BASE:L, :] = h
    np_ref[0] = hbuf[:, L - POOL_STATE:L, :]

    s1[:, LEAD:L, :] = hbuf[:, LEAD:L, G:] + hbuf[:, LEAD - 1:L - 1, G:]
    s2[:, LEAD:L, :] = s1[:, LEAD:L, :] + s1[:, LEAD - 2:L - 2, :]
    s1[:, 2 * LEAD:L, 0:2 * G] = (s2[:, 2 * LEAD:L, G:] +
                                  s2[:, 2 * LEAD - 4:L - 4, G:])
    sums = [
        hbuf[:, BASE:L, 0:G] + hbuf[:, BASE - 1:L - 1, 0:G],
        s2[:, BASE:L, 0:G],
        s1[:, BASE:L, 0:G],
        s1[:, BASE:L, G:2 * G] + s1[:, BASE - 8:L - 8, G:2 * G],
    ]

    pos = pos0 + t * tt + lax.broadcasted_iota(jnp.int32, (1, tt, LANES), 1)
    ys = []
    for g, w in enumerate(POOL_WINDOWS):
        cs = slice(g * POOL_GROUP, (g + 1) * POOL_GROUP)
        inv_cnt = 1.0 / jnp.minimum(w, pos + 1).astype(F32)
        inv_cnt = jnp.concatenate([inv_cnt] * (POOL_GROUP // LANES), axis=-1)
        pooled = sums[g] * inv_cnt - hbuf[:, BASE:L, cs]
        ys.append(jnp.dot(pooled.reshape(bb * tt, POOL_GROUP).astype(BF16), wp_ref[0, g],
                          preferred_element_type=F32))
    y = jnp.concatenate(ys, axis=-1).reshape(bb, tt, D_MODEL) * ps_ref[...]
    o_ref[...] = x + _gated_norm(y, g_ref[:, 1:2, :], m[:, 2:3, :])


def _pool_layer(x, mod, g_norm, prefix, w_pool, pool_scale, l, *, bb, tt, pos0):
    B, T, D = x.shape
    kern = functools.partial(_pool_kernel, bb=bb, tt=tt, pos0=pos0)
    return pl.pallas_call(
        kern,
        out_shape=(jax.ShapeDtypeStruct((B, T, D), F32),
                   jax.ShapeDtypeStruct((1, B, POOL_STATE, D), F32)),
        grid=(B // bb, T // tt),
        in_specs=[
            pl.BlockSpec((bb, tt, D), lambda b, t: (b, t, 0)),
            pl.BlockSpec((bb, 6, D), lambda b, t: (b, 0, 0)),
            pl.BlockSpec((1, 4, D), lambda b, t: (l, 0, 0)),
            pl.BlockSpec((1, bb, POOL_STATE, D), lambda b, t: (l, b, 0, 0)),
            pl.BlockSpec((1, len(POOL_WINDOWS), POOL_GROUP, POOL_GROUP),
                         lambda b, t: (l, 0, 0, 0)),
            pl.BlockSpec((1, 1, D), lambda b, t: (l, 0, 0)),
        ],
        out_specs=(pl.BlockSpec((bb, tt, D), lambda b, t: (b, t, 0)),
                   pl.BlockSpec((1, bb, POOL_STATE, D), lambda b, t: (0, b, 0, 0))),
        scratch_shapes=[pltpu.VMEM((bb, POOL_BASE + tt, D), F32),
                        pltpu.VMEM((bb, POOL_BASE + tt, 3 * POOL_GROUP), F32),
                        pltpu.VMEM((bb, POOL_BASE + tt, 3 * POOL_GROUP), F32)],
        compiler_params=_params("parallel", "arbitrary"),
        name=f"pool_mixer_{l}",
    )(x, mod, g_norm, prefix, w_pool, pool_scale)


def _rope_tables(pos):
    half = HEAD_DIM // 2
    inv = ROPE_THETA ** (-jnp.arange(half, dtype=F32) / half)
    ang = pos.astype(F32)[:, None] * inv[None, :]
    cos, sin = jnp.cos(ang), jnp.sin(ang)
    zero = jnp.zeros_like(sin)
    c = jnp.tile(cos, (1, 4))
    s_lo = jnp.tile(jnp.concatenate([-sin, zero], axis=1), (1, 2))
    s_hi = jnp.tile(jnp.concatenate([zero, sin], axis=1), (1, 2))
    return c, s_lo, s_hi


def _rope_block(blk, c, s_lo, s_hi):
    return (blk * c + pltpu.roll(blk, LANES - HEAD_DIM // 2, 1) * s_lo
            + pltpu.roll(blk, HEAD_DIM // 2, 1) * s_hi)


def _qproj_kernel(x_ref, mod_ref, g_ref, c_ref, slo_ref, shi_ref, w_ref, q_ref, *, bb, tt):
    x = x_ref[...]
    m = mod_ref[...]
    h = _norm_mod(x, g_ref[:, 0:1, :], m[:, 0:1, :], m[:, 1:2, :])
    q = jnp.dot(h.reshape(bb * tt, D_MODEL).astype(BF16), w_ref[0], preferred_element_type=F32)
    c, s_lo, s_hi = c_ref[...], slo_ref[...], shi_ref[...]
    for p in range(N_PAIRS):
        cs = slice(p * LANES, (p + 1) * LANES)
        q_ref[p] = (_rope_block(q[:, cs], c, s_lo, s_hi) * ATTN_SCALE).astype(BF16)


def _qproj_layer(x, mod, g_norm, tables, w_q, l, j, *, bb, tt):
    B, T, D = x.shape
    nt = T // tt
    M = bb * tt
    kern = functools.partial(_qproj_kernel, bb=bb, tt=tt)
    tab_spec = pl.BlockSpec((M, LANES), lambda b, t: (t, 0))
    return pl.pallas_call(
        kern,
        out_shape=jax.ShapeDtypeStruct((N_PAIRS, B * T, LANES), BF16),
        grid=(B // bb, nt),
        in_specs=[
            pl.BlockSpec((bb, tt, D), lambda b, t: (b, t, 0)),
            pl.BlockSpec((bb, 6, D), lambda b, t: (b, 0, 0)),
            pl.BlockSpec((1, 4, D), lambda b, t: (l, 0, 0)),
            tab_spec, tab_spec, tab_spec,
            pl.BlockSpec((1, D, D), lambda b, t: (j, 0, 0)),
        ],
        out_specs=pl.BlockSpec((N_PAIRS, M, LANES), lambda b, t: (0, b * nt + t, 0)),
        compiler_params=_params("parallel", "parallel"),
        name=f"q_proj_{l}",
    )(x, mod, g_norm, *tables, w_q)


def _masked_pairs(a):
    M = a.shape[0]
    lo = lax.broadcasted_iota(jnp.int32, (M, LANES), 1) < HEAD_DIM
    zero = jnp.zeros((M, LANES), F32)
    out = []
    for p in range(N_KV_HEADS // 2):
        blk = a[:, p * LANES:(p + 1) * LANES]
        swapped = pltpu.roll(blk, HEAD_DIM, 1)
        out += [jnp.where(lo, blk, zero), jnp.where(lo, zero, swapped),
                jnp.where(lo, swapped, zero), jnp.where(lo, zero, blk)]
    return jnp.concatenate([part.astype(BF16) for part in out], axis=1)


def _kv_kernel(x_ref, mod_ref, g_ref, c_ref, slo_ref, shi_ref, w_ref,
               k_ref, v_ref, kd_ref, vd_ref, *, bb, tt):
    x = x_ref[...]
    m = mod_ref[...]
    h = _norm_mod(x, g_ref[...], m[:, 0:1, :], m[:, 1:2, :])
    kv = jnp.dot(h.reshape(bb * tt, D_MODEL).astype(BF16), w_ref[...],
                 preferred_element_type=F32)
    c, s_lo, s_hi = c_ref[...], slo_ref[...], shi_ref[...]
    KV = N_KV_HEADS * HEAD_DIM
    k = jnp.concatenate([_rope_block(kv[:, p * LANES:(p + 1) * LANES], c, s_lo, s_hi)
                         for p in range(KV // LANES)], axis=1)
    v = kv[:, KV:]
    k_ref[...] = k
    v_ref[...] = v
    kd_ref[...] = _masked_pairs(k).reshape(bb, tt, 4 * KV)
    vd_ref[...] = _masked_pairs(v).reshape(bb, tt, 4 * KV)


def _kv_layer(x, mod, g_kv, tables, w_kv, *, bb, tt):
    B, T, D = x.shape
    nt = T // tt
    M = bb * tt
    KV = N_KV_HEADS * HEAD_DIM
    kern = functools.partial(_kv_kernel, bb=bb, tt=tt)
    tab_spec = pl.BlockSpec((M, LANES), lambda b, t: (t, 0))
    tile = lambda b, t: (b, t, 0)
    newest = lambda b, t: (b, 0)
    return pl.pallas_call(
        kern,
        out_shape=(jax.ShapeDtypeStruct((B // bb * M, KV), F32),
                   jax.ShapeDtypeStruct((B // bb * M, KV), F32),
                   jax.ShapeDtypeStruct((B, T, 4 * KV), BF16),
                   jax.ShapeDtypeStruct((B, T, 4 * KV), BF16)),
        grid=(B // bb, nt),
        in_specs=[
            pl.BlockSpec((bb, tt, D), lambda b, t: (b, t, 0)),
            pl.BlockSpec((bb, 2, D), lambda b, t: (b, 0, 0)),
            pl.BlockSpec((1, 1, D), lambda b, t: (0, 0, 0)),
            tab_spec, tab_spec, tab_spec,
            pl.BlockSpec((D, 2 * KV), lambda b, t: (0, 0)),
        ],
        out_specs=(pl.BlockSpec((M, KV), newest), pl.BlockSpec((M, KV), newest),
                   pl.BlockSpec((bb, tt, 4 * KV), tile), pl.BlockSpec((bb, tt, 4 * KV), tile)),
        compiler_params=_params("parallel", "arbitrary"),
        name="shared_kv",
    )(x, mod, g_kv, *tables, w_kv)


def _sink_softmax(s, sink):
    m = jnp.max(s, axis=-1, keepdims=True)
    e = jnp.exp(s - m)
    den = jnp.sum(e, axis=-1, keepdims=True) + jnp.exp(sink - m)
    return e * (1.0 / den)


def _attn_kernel(sink_ref, *refs, tq, fused, n_valid=None):
    i = pl.program_id(1)
    G = PAIRS_PER_KV
    ts = WINDOW if fused else tq
    nsub = tq // ts
    R = G * ts
    QW = D_MODEL // N_KV_HEADS
    kp = lax.broadcasted_iota(jnp.int32, (R, KEY_SPAN), 1)
    rblk = lax.broadcasted_iota(jnp.int32, (R, 1), 0) // ts
    nt = (((1,), (1,)), ((), ()))
    sink_cols = []
    for j in range(N_KV_HEADS):
        for parity in range(2):
            col = jnp.full((R, 1), sink_ref[2 * G * j + parity], F32)
            for r in range(1, G):
                col = jnp.where(rblk == r, sink_ref[2 * (G * j + r) + parity], col)
            sink_cols.append(col)
    if fused:
        q_ref, k_prev, k_cur, v_prev, v_cur, x_ref, mod_ref, g_ref, wo_ref, o_ref = refs
        row = lax.broadcasted_iota(jnp.int32, (R, KEY_SPAN), 0)
        cq = (row % ts) // CHUNK
        kb = kp // CHUNK
        in_window = (kb >= cq) & (kb <= cq + WINDOW // CHUNK)
    else:
        q_ref, ck_ref, cv_ref, kd_ref, vd_ref, o_ref = refs

    def finish(rows, mix):
        m = mod_ref[0]
        o_ref[rows, :] = x_ref[rows, :] + _gated_norm(mix, g_ref[0, 1:2, :], m[2:3, :])

    def with_cache(c_ref, new_ref):
        past = _masked_pairs(c_ref[0])
        pad = jnp.zeros((KEY_SPAN - past.shape[0] - tq, past.shape[1]), BF16)
        return jnp.concatenate([past, new_ref[0], pad], axis=0)

    def keys_of(sub):
        rs = slice(sub * ts, (sub + 1) * ts)
        if not fused:
            kw, vw = with_cache(ck_ref, kd_ref), with_cache(cv_ref, vd_ref)
            valid = kp < n_valid
        elif sub == 0:
            kw = jnp.concatenate([k_prev[0], k_cur[0, rs, :]], axis=0)
            vw = jnp.concatenate([v_prev[0], v_cur[0, rs, :]], axis=0)
            valid = in_window & (kp >= jnp.where(i > 0, 0, WINDOW))
        else:
            span = slice(sub * ts - WINDOW, (sub + 1) * ts)
            kw, vw = k_cur[0, span, :], v_cur[0, span, :]
            valid = in_window
        return rs, kw, vw, jnp.where(valid, 0.0, NEG_INF)

    def scores(keys, j):
        rs, kw, _, bias = keys
        qg = q_ref[G * j:G * (j + 1), rs, :].reshape(R, LANES)
        k_lo = kw[:, (2 * j) * LANES:(2 * j + 1) * LANES]
        k_hi = kw[:, (2 * j + 1) * LANES:(2 * j + 2) * LANES]
        return (lax.dot_general(qg, k_lo, nt, preferred_element_type=F32) + bias,
                lax.dot_general(qg, k_hi, nt, preferred_element_type=F32) + bias)

    groups = [(sub, j) for sub in range(nsub) for j in range(N_KV_HEADS)]
    keys = {0: keys_of(0)}
    ahead = scores(keys[0], 0)
    pending = None
    heads, mix_parts = [], []
    for n, (sub, j) in enumerate(groups):
        s0, s1 = ahead
        if n + 1 < len(groups):
            sub_n, j_n = groups[n + 1]
            if sub_n not in keys:
                keys[sub_n] = keys_of(sub_n)
            ahead = scores(keys[sub_n], j_n)
        if pending is not None:
            mix_parts.append(jnp.dot(pending[1], wo_ref[0, :, j * QW:(j + 1) * QW],
                                     preferred_element_type=F32))
        rs, _, vw, _ = keys[sub]
        v_lo = vw[:, (2 * j) * LANES:(2 * j + 1) * LANES]
        v_hi = vw[:, (2 * j + 1) * LANES:(2 * j + 2) * LANES]
        p0 = _sink_softmax(s0, sink_cols[2 * j])
        p1 = _sink_softmax(s1, sink_cols[2 * j + 1])
        o = (jnp.dot(p0.astype(BF16), v_lo, preferred_element_type=F32)
             + jnp.dot(p1.astype(BF16), v_hi, preferred_element_type=F32)).astype(BF16)
        if not fused:
            o_ref[G * j:G * (j + 1), rs, :] = o.reshape(G, ts, LANES)
            continue
        heads.extend(o[r * ts:(r + 1) * ts] for r in range(G))
        if j == N_KV_HEADS - 1:
            if pending is not None:
                finish(pending[0], jnp.concatenate(mix_parts, axis=1))
            pending = (rs, jnp.concatenate(heads, axis=1))
            heads, mix_parts = [], []
    if fused:
        finish(pending[0], jnp.dot(pending[1], wo_ref[0], preferred_element_type=F32))


def _attention(q, cache_k, cache_v, kd, vd, sinks):
    B, T, W = kd.shape
    S = cache_k.shape[1]
    assert S + T <= KEY_SPAN
    kern = functools.partial(_attn_kernel, tq=T, fused=False, n_valid=S + T)
    past = pl.BlockSpec((1, S, cache_k.shape[2]), lambda b, i: (b, 0, 0))
    new = pl.BlockSpec((1, T, W), lambda b, i: (b, 0, 0))
    pair_rows = pl.BlockSpec((N_PAIRS, T, LANES), lambda b, i: (0, b, 0))
    return pl.pallas_call(
        kern,
        out_shape=jax.ShapeDtypeStruct((N_PAIRS, B * T, LANES), BF16),
        grid=(B, 1),
        in_specs=[pl.BlockSpec(memory_space=pltpu.SMEM), pair_rows, past, past, new, new],
        out_specs=pair_rows,
        compiler_params=_params("parallel", "parallel"),
        name="swa_attention",
    )(sinks, q, cache_k, cache_v, kd, vd)


def _window_specs(kd, vd, T, tq):
    W = kd.shape[2]
    assert tq % WINDOW == 0 and kd.shape[1] == T
    nsub = tq // WINDOW
    prev = pl.BlockSpec((1, WINDOW, W), lambda b, i: (b, jnp.maximum(i * nsub - 1, 0), 0))
    cur = pl.BlockSpec((1, tq, W), lambda b, i: (b, i, 0))
    return [prev, cur, prev, cur], (kd, kd, vd, vd)


def _attention_out(q, x, mod, g_norm, w_o, kd, vd, sinks, l, j, *, tq):
    B, T, D = x.shape
    nq = T // tq
    kern = functools.partial(_attn_kernel, tq=tq, fused=True)
    kv_specs, kv_args = _window_specs(kd, vd, T, tq)
    rows = lambda b, i: (b * nq + i, 0)
    out = pl.pallas_call(
        kern,
        out_shape=jax.ShapeDtypeStruct((B * T, D), F32),
        grid=(B, nq),
        in_specs=[
            pl.BlockSpec(memory_space=pltpu.SMEM),
            pl.BlockSpec((N_PAIRS, tq, LANES), lambda b, i: (0, b * nq + i, 0)),
            *kv_specs,
            pl.BlockSpec((tq, D), rows),
            pl.BlockSpec((1, 6, D), lambda b, i: (b, 0, 0)),
            pl.BlockSpec((1, 4, D), lambda b, i: (l, 0, 0)),
            pl.BlockSpec((1, D, D), lambda b, i: (j, 0, 0)),
        ],
        out_specs=pl.BlockSpec((tq, D), rows),
        compiler_params=_params("parallel", "parallel"),
        name=f"attention_out_{l}",
    )(sinks, q, *kv_args, x.reshape(B * T, D), mod, g_norm, w_o)
    return out.reshape(B, T, D)


def _oproj_kernel(o_ref, x_ref, mod_ref, g_ref, w_ref, y_ref, *, bb, tt):
    o = jnp.concatenate([o_ref[p] for p in range(N_PAIRS)], axis=1)
    mix = jnp.dot(o, w_ref[0], preferred_element_type=F32).reshape(bb, tt, D_MODEL)
    m = mod_ref[...]
    y_ref[...] = x_ref[...] + _gated_norm(mix, g_ref[:, 1:2, :], m[:, 2:3, :])


def _oproj_layer(o, x, mod, g_norm, w_o, l, j, *, bb, tt):
    B, T, D = x.shape
    nt = T // tt
    kern = functools.partial(_oproj_kernel, bb=bb, tt=tt)
    return pl.pallas_call(
        kern,
        out_shape=jax.ShapeDtypeStruct((B, T, D), F32),
        grid=(B // bb, nt),
        in_specs=[
            pl.BlockSpec((N_PAIRS, bb * tt, LANES), lambda b, t: (0, b * nt + t, 0)),
            pl.BlockSpec((bb, tt, D), lambda b, t: (b, t, 0)),
            pl.BlockSpec((bb, 6, D), lambda b, t: (b, 0, 0)),
            pl.BlockSpec((1, 4, D), lambda b, t: (l, 0, 0)),
            pl.BlockSpec((1, D, D), lambda b, t: (j, 0, 0)),
        ],
        out_specs=pl.BlockSpec((bb, tt, D), lambda b, t: (b, t, 0)),
        compiler_params=_params("parallel", "parallel"),
        name=f"o_proj_{l}",
    )(o, x, mod, g_norm, w_o)


def _mlp_kernel(x_ref, mod_ref, g_ref, wu_ref, wd_ref, y_ref, *rest, bb, tt, cast):
    f = pl.program_id(2)
    m = mod_ref[...]
    if cast:
        wu_out, wd_out, h_ref = rest
    else:
        h_ref, = rest

    @pl.when(f == 0)
    def _():
        h = _norm_mod(x_ref[...], g_ref[:, 2:3, :], m[:, 3:4, :], m[:, 4:5, :])
        h_ref[...] = h.reshape(bb * tt, D_MODEL).astype(BF16)
        y_ref[...] = jnp.zeros_like(y_ref)

    wu = wu_ref[0]
    if cast:
        wu = wu.astype(BF16)
        wu_out[0] = wu
    u = jnp.dot(h_ref[...], wu, preferred_element_type=F32)
    a = jnp.square(jnp.maximum(u, 0.0)).astype(BF16)
    for n in range(D_MODEL // MLP_OUT_CHUNK):
        cs = slice(n * MLP_OUT_CHUNK, (n + 1) * MLP_OUT_CHUNK)
        wd_n = wd_ref[0, :, cs]
        if cast:
            wd_n = wd_n.astype(BF16)
            wd_out[0, :, cs] = wd_n
        y_ref[:, :, cs] += jnp.dot(a, wd_n, preferred_element_type=F32).reshape(
            bb, tt, MLP_OUT_CHUNK)

    @pl.when(f == pl.num_programs(2) - 1)
    def _():
        y_ref[...] = x_ref[...] + _gated_norm(y_ref[...], g_ref[:, 3:4, :], m[:, 5:6, :])


def _mlp_layer(x, mod, g_norm, w_up, w_down, l, *, bb, tt, tff, cast):
    B, T, D = x.shape
    grid = (B // bb, T // tt, D_FF // tff)
    kern = functools.partial(_mlp_kernel, bb=bb, tt=tt, cast=cast)
    wl = l if cast else 0
    y_shape = jax.ShapeDtypeStruct((B, T, D), F32)
    y_spec = pl.BlockSpec((bb, tt, D), lambda b, t, f: (b, t, 0))
    if cast:
        assert grid[0] * grid[1] == 1
        out_shape = (y_shape, jax.ShapeDtypeStruct((1, D, D_FF), BF16),
                     jax.ShapeDtypeStruct((1, D_FF, D), BF16))
        out_specs = (y_spec, pl.BlockSpec((1, D, tff), lambda b, t, f: (0, 0, f)),
                     pl.BlockSpec((1, tff, D), lambda b, t, f: (0, f, 0)))
    else:
        out_shape, out_specs = y_shape, y_spec
    return pl.pallas_call(
        kern,
        out_shape=out_shape,
        grid=grid,
        in_specs=[
            pl.BlockSpec((bb, tt, D), lambda b, t, f: (b, t, 0)),
            pl.BlockSpec((bb, 6, D), lambda b, t, f: (b, 0, 0)),
            pl.BlockSpec((1, 4, D), lambda b, t, f: (l, 0, 0)),
            pl.BlockSpec((1, D, tff), lambda b, t, f: (wl, 0, f)),
            pl.BlockSpec((1, tff, D), lambda b, t, f: (wl, f, 0)),
        ],
        out_specs=out_specs,
        scratch_shapes=[pltpu.VMEM((bb * tt, D), BF16)],
        compiler_params=_params("parallel", "parallel", "arbitrary"),
        name=f"mlp_{l}",
    )(x, mod, g_norm, w_up, w_down)


def _zero_after(x):
    bits = pltpu.bitcast(x, jnp.uint32)
    r = bits[:, 0:LANES]
    for k in range(1, x.shape[1] // LANES):
        r = r | bits[:, k * LANES:(k + 1) * LANES]
    r8 = r[0:SUBLANES]
    for k in range(1, x.shape[0] // SUBLANES):
        r8 = r8 | r[k * SUBLANES:(k + 1) * SUBLANES]
    z = pltpu.bitcast((r8 >> 16) >> 16, F32)
    return jnp.max(z, axis=(0, 1), keepdims=True)


def _mlp_skew_kernel(xn_ref, xp_ref, modn_ref, modp_ref, g_ref, wu_ref, wd_ref, *rest, nt, rs,
                     cast_next):
    s = pl.program_id(0)
    f = pl.program_id(1)
    if cast_next:
        nu_ref, nd_ref, y_ref, nu_out, nd_out, h0_ref, h1_ref, acc0_ref, acc1_ref = rest
    else:
        y_ref, h0_ref, h1_ref, acc0_ref, acc1_ref = rest
    rows = pl.ds(pl.multiple_of(f * rs, rs), rs)
    g_pre, g_post = g_ref[0, 2:3, :], g_ref[0, 3:4, :]
    h_refs = (h0_ref, h1_ref)
    acc_refs = (acc0_ref, acc1_ref)

    def prologue(h_ref):
        m = modn_ref[0]
        h = _norm_mod(xn_ref[...], g_pre, m[3:4, :], m[4:5, :])
        h_ref[rows, :] = h.astype(BF16)
        return h

    def epilogue(acc_ref):
        mp = modp_ref[0]
        y = xp_ref[...] + _gated_norm(acc_ref[rows, :], g_post, mp[5:6, :])
        y_ref[...] = y
        acc_ref[rows, :] = jnp.zeros((rs, D_MODEL), F32)
        return y

    def matmuls(h_ref, acc_ref, anchors):
        u = jnp.dot(h_ref[...], wu_ref[0], preferred_element_type=F32)
        a = jnp.square(jnp.maximum(u, 0.0)).astype(BF16)
        if cast_next:
            nu_out[...] = nu_ref[...].astype(BF16)
            nd_out[...] = nd_ref[...].astype(BF16)
        for n in range(D_MODEL // MLP_OUT_CHUNK):
            cs = slice(n * MLP_OUT_CHUNK, (n + 1) * MLP_OUT_CHUNK)
            d = jnp.dot(a, wd_ref[0, :, cs], preferred_element_type=F32)
            if anchors.get(n) is not None:
                d = d + anchors[n]
            acc_ref[:, cs] += d

    @pl.when((s == 0) & (f == 0))
    def _():
        acc0_ref[...] = jnp.zeros_like(acc0_ref)
        acc1_ref[...] = jnp.zeros_like(acc1_ref)

    @pl.when(s == 0)
    def _():
        prologue(h0_ref)

    @pl.when(s == 1)
    def _():
        h = prologue(h1_ref)
        matmuls(h0_ref, acc0_ref, {2: _zero_after(h)})

    for c in range(2):
        o = 1 - c

        @pl.when(((s - 1) % 2 == c) & (s >= 2) & (s < nt))
        def _():
            y = epilogue(acc_refs[o])
            h = prologue(h_refs[o])
            matmuls(h_refs[c], acc_refs[c], {0: _zero_after(y), 2: _zero_after(h)})

    last = (nt - 1) % 2

    @pl.when(s == nt)
    def _():
        y = epilogue(acc_refs[1 - last])
        matmuls(h_refs[last], acc_refs[last], {0: _zero_after(y)})

    @pl.when(s == nt + 1)
    def _():
        epilogue(acc_refs[last])


def _mlp_layer_skewed(x, mod, g_norm, w_up, w_down, l, *, tm, tff, next_f32=None):
    B, T, D = x.shape
    nf = D_FF // tff
    rs = tm // nf
    tpb = T // tm
    nt = B * tpb
    assert nt >= 3
    x2 = x.reshape(B * T, D)
    nxt = lambda s: jnp.minimum(s, nt - 1)
    prv = lambda s: jnp.maximum(s - 2, 0)
    wf = lambda s, f: jnp.where(s == 0, 0, jnp.where(s == nt + 1, nf - 1, f))
    cast_next = next_f32 is not None
    kern = functools.partial(_mlp_skew_kernel, nt=nt, rs=rs, cast_next=cast_next)
    in_specs = [
        pl.BlockSpec((rs, D), lambda s, f: (nxt(s) * nf + f, 0)),
        pl.BlockSpec((rs, D), lambda s, f: (prv(s) * nf + f, 0)),
        pl.BlockSpec((1, 6, D), lambda s, f: (nxt(s) // tpb, 0, 0)),
        pl.BlockSpec((1, 6, D), lambda s, f: (prv(s) // tpb, 0, 0)),
        pl.BlockSpec((1, 4, D), lambda s, f: (l, 0, 0)),
        pl.BlockSpec((1, D, tff), lambda s, f: (0, 0, wf(s, f))),
        pl.BlockSpec((1, tff, D), lambda s, f: (0, wf(s, f), 0)),
    ]
    args = [x2, x2, mod, mod, g_norm, w_up, w_down]
    out_specs = [pl.BlockSpec((rs, D), lambda s, f: (jnp.where(s < 2, 0, (s - 2) * nf + f), 0))]
    out_shape = [jax.ShapeDtypeStruct((B * T, D), F32)]
    if cast_next:
        n_slabs = nt * nf
        ru, rd = D // n_slabs, D_FF // n_slabs
        slab = lambda s, f: jnp.clip((s - 1) * nf + f, 0, n_slabs - 1)
        up_spec = lambda: pl.BlockSpec((1, ru, D_FF), lambda s, f: (l + 1, slab(s, f), 0))
        dn_spec = lambda: pl.BlockSpec((1, rd, D), lambda s, f: (l + 1, slab(s, f), 0))
        in_specs += [up_spec(), dn_spec()]
        args += list(next_f32)
        out_specs += [pl.BlockSpec((1, ru, D_FF), lambda s, f: (0, slab(s, f), 0)),
                      pl.BlockSpec((1, rd, D), lambda s, f: (0, slab(s, f), 0))]
        out_shape += [jax.ShapeDtypeStruct((1, D, D_FF), BF16),
                      jax.ShapeDtypeStruct((1, D_FF, D), BF16)]
    outs = pl.pallas_call(
        kern,
        out_shape=tuple(out_shape),
        grid=(nt + 2, nf),
        in_specs=in_specs,
        out_specs=tuple(out_specs),
        scratch_shapes=[pltpu.VMEM((tm, D), BF16), pltpu.VMEM((tm, D), BF16),
                        pltpu.VMEM((tm, D), F32), pltpu.VMEM((tm, D), F32)],
        compiler_params=_params("arbitrary", "arbitrary"),
        name=f"mlp_skewed_{l}",
    )(*args)
    y = outs[0].reshape(B, T, D)
    return (y, outs[1], outs[2]) if cast_next else (y, None, None)


class _Path:
    def __init__(self, x, mods, kvmod, pos0, prefix, kv_past, *, bb, tt, tq):
        self.x, self.mods, self.kvmod, self.pos0 = x, mods, kvmod, pos0
        self.prefix, self.kv_past = prefix, kv_past
        self.bb, self.tt, self.tq = bb, tt, tq
        pos = pos0 + jnp.arange(x.shape[1])
        self.tables = tuple(jnp.tile(tb, (bb, 1)) if bb > 1 else tb for tb in _rope_tables(pos))
        self.new_pool = []
        self.k = self.v = self.kd = self.vd = None

    def mixer(self, l, wts):
        B, T, _ = self.x.shape
        mod, bb, tt = self.mods[l], self.bb, self.tt
        if l < N_A_LAYERS:
            self.x, npool = _pool_layer(self.x, mod, wts["g_norm"], self.prefix, wts["w_pool"],
                                        wts["pool_scale"], l, bb=bb, tt=tt, pos0=self.pos0)
            self.new_pool.append(npool)
            return
        j = l - N_A_LAYERS
        q = _qproj_layer(self.x, mod, wts["g_norm"], self.tables, wts["w_q"], l, j, bb=bb, tt=tt)
        if self.kv_past is None:
            self.x = _attention_out(q, self.x, mod, wts["g_norm"], wts["w_o"], self.kd, self.vd,
                                    wts["sinks"][j], l, j, tq=self.tq)
        else:
            past_k, past_v = (c.reshape(B, c.shape[1], -1) for c in self.kv_past)
            o = _attention(q, past_k, past_v, self.kd, self.vd, wts["sinks"][j])
            self.x = _oproj_layer(o, self.x, mod, wts["g_norm"], wts["w_o"], l, j, bb=bb, tt=tt)

    def shared_kv(self, wts):
        B, T, _ = self.x.shape
        k, v, kd, vd = _kv_layer(self.x, self.kvmod, wts["g_kv"], self.tables, wts["w_kv"],
                                 bb=self.bb, tt=self.tt)
        KV = N_KV_HEADS * HEAD_DIM
        self.k, self.v = k.reshape(B, self.tt, KV), v.reshape(B, self.tt, KV)
        self.kd, self.vd = kd, vd


def _forward(prompt, sample, wts, *, tm, tff_prompt, tff_cast, tff_sample):
    w_f32 = (wts["w_up"], wts["w_down"])
    w_bf16 = None
    for l in range(DEPTH):
        sample.mixer(l, wts)
        if l == 0:
            sample.x, wu, wd = _mlp_layer(sample.x, sample.mods[l], wts["g_norm"], *w_f32, l,
                                          bb=sample.bb, tt=sample.tt, tff=tff_cast, cast=True)
            w_bf16 = (wu, wd)
        else:
            sample.x = _mlp_layer(sample.x, sample.mods[l], wts["g_norm"], *w_bf16, l,
                                  bb=sample.bb, tt=sample.tt, tff=tff_sample, cast=False)
        prompt.mixer(l, wts)
        prompt.x, wu, wd = _mlp_layer_skewed(
            prompt.x, prompt.mods[l], wts["g_norm"], *w_bf16, l, tm=tm, tff=tff_prompt,
            next_f32=w_f32 if l + 1 < DEPTH else None)
        w_bf16 = (wu, wd)
        if l == N_A_LAYERS - 1:
            sample.shared_kv(wts)
            prompt.shared_kv(wts)


def _prep_weights(g_norm, w_pool, pool_scale, g_kv, w_kv, w_q, sinks, w_o, w_up, w_down):
    D = D_MODEL
    return {
        "g_norm": g_norm,
        "w_pool": w_pool.astype(BF16),
        "pool_scale": pool_scale.reshape(N_A_LAYERS, 1, D),
        "g_kv": g_kv.reshape(1, 1, D),
        "w_kv": w_kv.astype(BF16),
        "w_q": w_q.astype(BF16),
        "sinks": sinks,
        "w_o": w_o.astype(BF16),
        "w_up": w_up,
        "w_down": w_down,
    }


def kernel(x_prompt, x_sample, c_prompt, c_sample, state_pool, cache_k, cache_v, w_mod, b_mod,
           g_norm, w_pool, pool_scale, w_kv_mod, b_kv_mod, g_kv, w_kv, w_q, sinks, w_o, w_up,
           w_down):
    Bp, Bs = x_prompt.shape[0], x_sample.shape[0]
    T_s = x_sample.shape[1]
    D = D_MODEL

    c_all = jnp.concatenate(
        [c_prompt, c_sample, jnp.zeros((MOD_ROWS - Bp - Bs, D), F32)], axis=0)
    mod_all = _modulation(c_all, w_mod, b_mod.reshape(DEPTH, 1, 6 * D)).reshape(
        DEPTH, MOD_ROWS, 6, D)
    kvmod_all = _modulation(c_all, w_kv_mod.reshape(1, D, 2 * D),
                            b_kv_mod.reshape(1, 1, 2 * D)).reshape(MOD_ROWS, 2, D)
    wts = _prep_weights(g_norm, w_pool, pool_scale, g_kv, w_kv, w_q, sinks, w_o, w_up, w_down)

    prompt = _Path(x_prompt, [mod_all[l, :Bp] for l in range(DEPTH)], kvmod_all[:Bp], 0,
                   jnp.zeros((N_A_LAYERS, Bp, POOL_STATE, D), F32), None,
                   bb=1, tt=512, tq=4 * WINDOW)
    sample = _Path(x_sample, [mod_all[l, Bp:Bp + Bs] for l in range(DEPTH)],
                   kvmod_all[Bp:Bp + Bs], PAST_LEN, state_pool, (cache_k, cache_v),
                   bb=Bs, tt=T_s, tq=T_s)
    _forward(prompt, sample, wts, tm=1024, tff_prompt=1024, tff_cast=512, tff_sample=1024)

    keep = min(WINDOW, x_prompt.shape[1])
    heads = (N_KV_HEADS, HEAD_DIM)
    return (prompt.x, sample.x,
            jnp.concatenate(prompt.new_pool, axis=0), jnp.concatenate(sample.new_pool, axis=0),
            prompt.k[:, -keep:].reshape(Bp, keep, *heads),
            prompt.v[:, -keep:].reshape(Bp, keep, *heads),
            sample.k.reshape(Bs, T_s, *heads), sample.v.reshape(Bs, T_s, *heads))
```

```python
import functools

import jax
import jax.numpy as jnp
from jax import lax
from jax.experimental import pallas as pl
from jax.experimental.pallas import tpu as pltpu

F32 = jnp.float32
BF16 = jnp.bfloat16

D_MODEL = 2048
DEPTH = 4
PAST_LEN = 4096
CHUNK = 64
N_A_LAYERS = DEPTH // 2
POOL_WINDOWS = (2, 4, 8, 16)
POOL_GROUP = D_MODEL // len(POOL_WINDOWS)
POOL_STATE = max(POOL_WINDOWS) - 1
SUBLANES = 8
POOL_LEAD = SUBLANES
POOL_BASE = POOL_LEAD + POOL_STATE + 1
HEAD_DIM = 64
N_HEADS = D_MODEL // HEAD_DIM
N_KV_HEADS = N_HEADS // 8
WINDOW = 128
D_FF = 4 * D_MODEL
ROPE_THETA = 10000.0
EPS = 1e-6
ATTN_SCALE = HEAD_DIM ** -0.5
NEG_INF = -1e30

LANES = 128
KEY_SPAN = 2 * LANES
N_PAIRS = N_HEADS // 2
PAIRS_PER_KV = N_PAIRS // N_KV_HEADS
MLP_OUT_CHUNK = 512
MOD_ROWS = 32
VMEM_LIMIT = 56 * 1024 * 1024


def _params(*sem):
    return pltpu.CompilerParams(dimension_semantics=sem, vmem_limit_bytes=VMEM_LIMIT)


def _norm_mod(x, g, shift, scale):
    ms = jnp.mean(x * x, axis=-1, keepdims=True)
    return (x * lax.rsqrt(ms + EPS)) * (g * (1.0 + scale)) + shift


def _gated_norm(y, g, gate):
    ms = jnp.mean(y * y, axis=-1, keepdims=True)
    return (y * lax.rsqrt(ms + EPS)) * (g * gate)


def _mod_kernel(c_ref, w_ref, b_ref, o_ref):
    c = c_ref[...]
    sc = (c * jax.nn.sigmoid(c)).astype(BF16)
    o_ref[0] = jnp.dot(sc, w_ref[0].astype(BF16), preferred_element_type=F32) + b_ref[0]


def _modulation(c_all, w, b, tn=2048):
    L, D, N = w.shape
    return pl.pallas_call(
        _mod_kernel,
        out_shape=jax.ShapeDtypeStruct((L, MOD_ROWS, N), F32),
        grid=(L, N // tn),
        in_specs=[
            pl.BlockSpec((MOD_ROWS, D), lambda l, n: (0, 0)),
            pl.BlockSpec((1, D, tn), lambda l, n: (l, 0, n)),
            pl.BlockSpec((1, 1, tn), lambda l, n: (l, 0, n)),
        ],
        out_specs=pl.BlockSpec((1, MOD_ROWS, tn), lambda l, n: (l, 0, n)),
        compiler_params=_params("parallel", "parallel"),
        name="modulation",
    )(c_all, w, b)


def _pool_kernel(x_ref, mod_ref, g_ref, pre_ref, wp_ref, ps_ref, o_ref, np_ref, hbuf, s1, s2,
                 *, bb, tt, pos0):
    t = pl.program_id(1)
    G = POOL_GROUP
    LEAD, BASE = POOL_LEAD, POOL_BASE
    L = BASE + tt

    @pl.when(t == 0)
    def _():
        hbuf[:, 0:LEAD + 1, :] = jnp.zeros((bb, LEAD + 1, D_MODEL), F32)
        hbuf[:, LEAD + 1:BASE, :] = pre_ref[0]
        s1[:, 0:LEAD, :] = jnp.zeros((bb, LEAD, 3 * G), F32)
        s2[:, 0:LEAD, :] = jnp.zeros((bb, LEAD, 3 * G), F32)

    @pl.when(t > 0)
    def _():
        hbuf[:, LEAD:BASE, :] = hbuf[:, tt + LEAD:tt + BASE, :]

    x = x_ref[...]
    m = mod_ref[...]
    h = _norm_mod(x, g_ref[:, 0:1, :], m[:, 0:1, :], m[:, 1:2, :])
    hbuf[:, BASE:L, :] = h
    np_ref[0] = hbuf[:, L - POOL_STATE:L, :]

    s1[:, LEAD:L, :] = hbuf[:, LEAD:L, G:] + hbuf[:, LEAD - 1:L - 1, G:]
    s2[:, LEAD:L, :] = s1[:, LEAD:L, :] + s1[:, LEAD - 2:L - 2, :]
    s1[:, 2 * LEAD:L, 0:2 * G] = (s2[:, 2 * LEAD:L, G:] +
                                  s2[:, 2 * LEAD - 4:L - 4, G:])
    sums = [
        hbuf[:, BASE:L, 0:G] + hbuf[:, BASE - 1:L - 1, 0:G],
        s2[:, BASE:L, 0:G],
        s1[:, BASE:L, 0:G],
        s1[:, BASE:L, G:2 * G] + s1[:, BASE - 8:L - 8, G:2 * G],
    ]

    pos = pos0 + t * tt + lax.broadcasted_iota(jnp.int32, (1, tt, LANES), 1)
    ys = []
    for g, w in enumerate(POOL_WINDOWS):
        cs = slice(g * POOL_GROUP, (g + 1) * POOL_GROUP)
        inv_cnt = 1.0 / jnp.minimum(w, pos + 1).astype(F32)
        inv_cnt = jnp.concatenate([inv_cnt] * (POOL_GROUP // LANES), axis=-1)
        pooled = sums[g] * inv_cnt - hbuf[:, BASE:L, cs]
        ys.append(jnp.dot(pooled.reshape(bb * tt, POOL_GROUP).astype(BF16), wp_ref[0, g],
                          preferred_element_type=F32))
    y = jnp.concatenate(ys, axis=-1).reshape(bb, tt, D_MODEL) * ps_ref[...]
    o_ref[...] = x + _gated_norm(y, g_ref[:, 1:2, :], m[:, 2:3, :])


def _pool_layer(x, mod, g_norm, prefix, w_pool, pool_scale, l, *, bb, tt, pos0):
    B, T, D = x.shape
    kern = functools.partial(_pool_kernel, bb=bb, tt=tt, pos0=pos0)
    return pl.pallas_call(
        kern,
        out_shape=(jax.ShapeDtypeStruct((B, T, D), F32),
                   jax.ShapeDtypeStruct((1, B, POOL_STATE, D), F32)),
        grid=(B // bb, T // tt),
        in_specs=[
            pl.BlockSpec((bb, tt, D), lambda b, t: (b, t, 0)),
            pl.BlockSpec((bb, 6, D), lambda b, t: (b, 0, 0)),
            pl.BlockSpec((1, 4, D), lambda b, t: (l, 0, 0)),
            pl.BlockSpec((1, bb, POOL_STATE, D), lambda b, t: (l, b, 0, 0)),
            pl.BlockSpec((1, len(POOL_WINDOWS), POOL_GROUP, POOL_GROUP),
                         lambda b, t: (l, 0, 0, 0)),
            pl.BlockSpec((1, 1, D), lambda b, t: (l, 0, 0)),
        ],
        out_specs=(pl.BlockSpec((bb, tt, D), lambda b, t: (b, t, 0)),
                   pl.BlockSpec((1, bb, POOL_STATE, D), lambda b, t: (0, b, 0, 0))),
        scratch_shapes=[pltpu.VMEM((bb, POOL_BASE + tt, D), F32),
                        pltpu.VMEM((bb, POOL_BASE + tt, 3 * POOL_GROUP), F32),
                        pltpu.VMEM((bb, POOL_BASE + tt, 3 * POOL_GROUP), F32)],
        compiler_params=_params("parallel", "arbitrary"),
        name=f"pool_mixer_{l}",
    )(x, mod, g_norm, prefix, w_pool, pool_scale)


def _rope_tables(pos):
    half = HEAD_DIM // 2
    inv = ROPE_THETA ** (-jnp.arange(half, dtype=F32) / half)
    ang = pos.astype(F32)[:, None] * inv[None, :]
    cos, sin = jnp.cos(ang), jnp.sin(ang)
    zero = jnp.zeros_like(sin)
    c = jnp.tile(cos, (1, 4))
    s_lo = jnp.tile(jnp.concatenate([-sin, zero], axis=1), (1, 2))
    s_hi = jnp.tile(jnp.concatenate([zero, sin], axis=1), (1, 2))
    return c, s_lo, s_hi


def _rope_block(blk, c, s_lo, s_hi):
    return (blk * c + pltpu.roll(blk, LANES - HEAD_DIM // 2, 1) * s_lo
            + pltpu.roll(blk, HEAD_DIM // 2, 1) * s_hi)


def _qproj_kernel(x_ref, mod_ref, g_ref, c_ref, slo_ref, shi_ref, w_ref, q_ref, *, bb, tt):
    x = x_ref[...]
    m = mod_ref[...]
    h = _norm_mod(x, g_ref[:, 0:1, :], m[:, 0:1, :], m[:, 1:2, :])
    q = jnp.dot(h.reshape(bb * tt, D_MODEL).astype(BF16), w_ref[0], preferred_element_type=F32)
    c, s_lo, s_hi = c_ref[...], slo_ref[...], shi_ref[...]
    for p in range(N_PAIRS):
        cs = slice(p * LANES, (p + 1) * LANES)
        q_ref[p] = (_rope_block(q[:, cs], c, s_lo, s_hi) * ATTN_SCALE).astype(BF16)


def _qproj_layer(x, mod, g_norm, tables, w_q, l, j, *, bb, tt):
    B, T, D = x.shape
    nt = T // tt
    M = bb * tt
    kern = functools.partial(_qproj_kernel, bb=bb, tt=tt)
    tab_spec = pl.BlockSpec((M, LANES), lambda b, t: (t, 0))
    return pl.pallas_call(
        kern,
        out_shape=jax.ShapeDtypeStruct((N_PAIRS, B * T, LANES), BF16),
        grid=(B // bb, nt),
        in_specs=[
            pl.BlockSpec((bb, tt, D), lambda b, t: (b, t, 0)),
            pl.BlockSpec((bb, 6, D), lambda b, t: (b, 0, 0)),
            pl.BlockSpec((1, 4, D), lambda b, t: (l, 0, 0)),
            tab_spec, tab_spec, tab_spec,
            pl.BlockSpec((1, D, D), lambda b, t: (j, 0, 0)),
        ],
        out_specs=pl.BlockSpec((N_PAIRS, M, LANES), lambda b, t: (0, b * nt + t, 0)),
        compiler_params=_params("parallel", "parallel"),
        name=f"q_proj_{l}",
    )(x, mod, g_norm, *tables, w_q)


def _masked_pairs(a):
    M = a.shape[0]
    lo = lax.broadcasted_iota(jnp.int32, (M, LANES), 1) < HEAD_DIM
    zero = jnp.zeros((M, LANES), F32)
    out = []
    for p in range(N_KV_HEADS // 2):
        blk = a[:, p * LANES:(p + 1) * LANES]
        swapped = pltpu.roll(blk, HEAD_DIM, 1)
        out += [jnp.where(lo, blk, zero), jnp.where(lo, zero, swapped),
                jnp.where(lo, swapped, zero), jnp.where(lo, zero, blk)]
    return jnp.concatenate([part.astype(BF16) for part in out], axis=1)


def _kv_kernel(x_ref, mod_ref, g_ref, c_ref, slo_ref, shi_ref, w_ref,
               k_ref, v_ref, kd_ref, vd_ref, *, bb, tt):
    x = x_ref[...]
    m = mod_ref[...]
    h = _norm_mod(x, g_ref[...], m[:, 0:1, :], m[:, 1:2, :])
    kv = jnp.dot(h.reshape(bb * tt, D_MODEL).astype(BF16), w_ref[...],
                 preferred_element_type=F32)
    c, s_lo, s_hi = c_ref[...], slo_ref[...], shi_ref[...]
    KV = N_KV_HEADS * HEAD_DIM
    k = jnp.concatenate([_rope_block(kv[:, p * LANES:(p + 1) * LANES], c, s_lo, s_hi)
                         for p in range(KV // LANES)], axis=1)
    v = kv[:, KV:]
    k_ref[...] = k
    v_ref[...] = v
    kd_ref[...] = _masked_pairs(k).reshape(bb, tt, 4 * KV)
    vd_ref[...] = _masked_pairs(v).reshape(bb, tt, 4 * KV)


def _qkv_kernel(x_ref, mod_ref, g_ref, c_ref, slo_ref, shi_ref, wq_ref, kvmod_ref, gkv_ref,
                wkv_ref, q_ref, k_ref, v_ref, kd_ref, vd_ref, *, bb, tt):
    _qproj_kernel(x_ref, mod_ref, g_ref, c_ref, slo_ref, shi_ref, wq_ref, q_ref, bb=bb, tt=tt)
    _kv_kernel(x_ref, kvmod_ref, gkv_ref, c_ref, slo_ref, shi_ref, wkv_ref,
               k_ref, v_ref, kd_ref, vd_ref, bb=bb, tt=tt)


def _qkv_layer(x, mod, g_norm, tables, w_q, l, j, kvmod, g_kv, w_kv, *, bb, tt):
    B, T, D = x.shape
    nt = T // tt
    M = bb * tt
    KV = N_KV_HEADS * HEAD_DIM
    kern = functools.partial(_qkv_kernel, bb=bb, tt=tt)
    tab_spec = pl.BlockSpec((M, LANES), lambda b, t: (t, 0))
    tile = lambda b, t: (b, t, 0)
    newest = lambda b, t: (b, 0)
    return pl.pallas_call(
        kern,
        out_shape=(jax.ShapeDtypeStruct((N_PAIRS, B * T, LANES), BF16),
                   jax.ShapeDtypeStruct((B // bb * M, KV), F32),
                   jax.ShapeDtypeStruct((B // bb * M, KV), F32),
                   jax.ShapeDtypeStruct((B, T, 4 * KV), BF16),
                   jax.ShapeDtypeStruct((B, T, 4 * KV), BF16)),
        grid=(B // bb, nt),
        in_specs=[
            pl.BlockSpec((bb, tt, D), tile),
            pl.BlockSpec((bb, 6, D), lambda b, t: (b, 0, 0)),
            pl.BlockSpec((1, 4, D), lambda b, t: (l, 0, 0)),
            tab_spec, tab_spec, tab_spec,
            pl.BlockSpec((1, D, D), lambda b, t: (j, 0, 0)),
            pl.BlockSpec((bb, 2, D), lambda b, t: (b, 0, 0)),
            pl.BlockSpec((1, 1, D), lambda b, t: (0, 0, 0)),
            pl.BlockSpec((D, 2 * KV), lambda b, t: (0, 0)),
        ],
        out_specs=(pl.BlockSpec((N_PAIRS, M, LANES), lambda b, t: (0, b * nt + t, 0)),
                   pl.BlockSpec((M, KV), newest), pl.BlockSpec((M, KV), newest),
                   pl.BlockSpec((bb, tt, 4 * KV), tile), pl.BlockSpec((bb, tt, 4 * KV), tile)),
        compiler_params=_params("parallel", "arbitrary"),
        name=f"qkv_proj_{l}",
    )(x, mod, g_norm, *tables, w_q, kvmod, g_kv, w_kv)


def _sink_softmax(s, sink):
    m = jnp.max(s, axis=-1, keepdims=True)
    e = jnp.exp(s - m)
    den = jnp.sum(e, axis=-1, keepdims=True) + jnp.exp(sink - m)
    return e * (1.0 / den)


def _attn_kernel(sink_ref, *refs, tq, fused, n_valid=None):
    i = pl.program_id(1)
    G = PAIRS_PER_KV
    ts = WINDOW if fused else tq
    nsub = tq // ts
    R = G * ts
    QW = D_MODEL // N_KV_HEADS
    kp = lax.broadcasted_iota(jnp.int32, (R, KEY_SPAN), 1)
    rblk = lax.broadcasted_iota(jnp.int32, (R, 1), 0) // ts
    nt = (((1,), (1,)), ((), ()))
    sink_cols = []
    for j in range(N_KV_HEADS):
        for parity in range(2):
            col = jnp.full((R, 1), sink_ref[2 * G * j + parity], F32)
            for r in range(1, G):
                col = jnp.where(rblk == r, sink_ref[2 * (G * j + r) + parity], col)
            sink_cols.append(col)
    if fused:
        q_ref, k_prev, k_cur, v_prev, v_cur, x_ref, mod_ref, g_ref, wo_ref, o_ref = refs
        row = lax.broadcasted_iota(jnp.int32, (R, KEY_SPAN), 0)
        cq = (row % ts) // CHUNK
        kb = kp // CHUNK
        in_window = (kb >= cq) & (kb <= cq + WINDOW // CHUNK)
    else:
        q_ref, ck_ref, cv_ref, kd_ref, vd_ref, o_ref = refs

    def finish(rows, mix):
        m = mod_ref[0]
        o_ref[rows, :] = x_ref[rows, :] + _gated_norm(mix, g_ref[0, 1:2, :], m[2:3, :])

    def with_cache(c_ref, new_ref):
        past = _masked_pairs(c_ref[0])
        pad = jnp.zeros((KEY_SPAN - past.shape[0] - tq, past.shape[1]), BF16)
        return jnp.concatenate([past, new_ref[0], pad], axis=0)

    def keys_of(sub):
        rs = slice(sub * ts, (sub + 1) * ts)
        if not fused:
            kw, vw = with_cache(ck_ref, kd_ref), with_cache(cv_ref, vd_ref)
            valid = kp < n_valid
        elif sub == 0:
            kw = jnp.concatenate([k_prev[0], k_cur[0, rs, :]], axis=0)
            vw = jnp.concatenate([v_prev[0], v_cur[0, rs, :]], axis=0)
            valid = in_window & (kp >= jnp.where(i > 0, 0, WINDOW))
        else:
            span = slice(sub * ts - WINDOW, (sub + 1) * ts)
            kw, vw = k_cur[0, span, :], v_cur[0, span, :]
            valid = in_window
        return rs, kw, vw, jnp.where(valid, 0.0, NEG_INF)

    def scores(keys, j):
        rs, kw, _, bias = keys
        qg = q_ref[G * j:G * (j + 1), rs, :].reshape(R, LANES)
        k_lo = kw[:, (2 * j) * LANES:(2 * j + 1) * LANES]
        k_hi = kw[:, (2 * j + 1) * LANES:(2 * j + 2) * LANES]
        return (lax.dot_general(qg, k_lo, nt, preferred_element_type=F32) + bias,
                lax.dot_general(qg, k_hi, nt, preferred_element_type=F32) + bias)

    groups = [(sub, j) for sub in range(nsub) for j in range(N_KV_HEADS)]
    keys = {0: keys_of(0)}
    ahead = scores(keys[0], 0)
    pending = None
    heads, mix_parts = [], []
    for n, (sub, j) in enumerate(groups):
        s0, s1 = ahead
        if n + 1 < len(groups):
            sub_n, j_n = groups[n + 1]
            if sub_n not in keys:
                keys[sub_n] = keys_of(sub_n)
            ahead = scores(keys[sub_n], j_n)
        if pending is not None:
            mix_parts.append(jnp.dot(pending[1], wo_ref[0, :, j * QW:(j + 1) * QW],
                                     preferred_element_type=F32))
        rs, _, vw, _ = keys[sub]
        v_lo = vw[:, (2 * j) * LANES:(2 * j + 1) * LANES]
        v_hi = vw[:, (2 * j + 1) * LANES:(2 * j + 2) * LANES]
        p0 = _sink_softmax(s0, sink_cols[2 * j])
        p1 = _sink_softmax(s1, sink_cols[2 * j + 1])
        o = (jnp.dot(p0.astype(BF16), v_lo, preferred_element_type=F32)
             + jnp.dot(p1.astype(BF16), v_hi, preferred_element_type=F32)).astype(BF16)
        if not fused:
            o_ref[G * j:G * (j + 1), rs, :] = o.reshape(G, ts, LANES)
            continue
        heads.extend(o[r * ts:(r + 1) * ts] for r in range(G))
        if j == N_KV_HEADS - 1:
            if pending is not None:
                finish(pending[0], jnp.concatenate(mix_parts, axis=1))
            pending = (rs, jnp.concatenate(heads, axis=1))
            heads, mix_parts = [], []
    if fused:
        finish(pending[0], jnp.dot(pending[1], wo_ref[0], preferred_element_type=F32))


def _attention(q, cache_k, cache_v, kd, vd, sinks):
    B, T, W = kd.shape
    S = cache_k.shape[1]
    assert S + T <= KEY_SPAN
    kern = functools.partial(_attn_kernel, tq=T, fused=False, n_valid=S + T)
    past = pl.BlockSpec((1, S, cache_k.shape[2]), lambda b, i: (b, 0, 0))
    new = pl.BlockSpec((1, T, W), lambda b, i: (b, 0, 0))
    pair_rows = pl.BlockSpec((N_PAIRS, T, LANES), lambda b, i: (0, b, 0))
    return pl.pallas_call(
        kern,
        out_shape=jax.ShapeDtypeStruct((N_PAIRS, B * T, LANES), BF16),
        grid=(B, 1),
        in_specs=[pl.BlockSpec(memory_space=pltpu.SMEM), pair_rows, past, past, new, new],
        out_specs=pair_rows,
        compiler_params=_params("parallel", "parallel"),
        name="swa_attention",
    )(sinks, q, cache_k, cache_v, kd, vd)


def _window_specs(kd, vd, T, tq):
    W = kd.shape[2]
    assert tq % WINDOW == 0 and kd.shape[1] == T
    nsub = tq // WINDOW
    prev = pl.BlockSpec((1, WINDOW, W), lambda b, i: (b, jnp.maximum(i * nsub - 1, 0), 0))
    cur = pl.BlockSpec((1, tq, W), lambda b, i: (b, i, 0))
    return [prev, cur, prev, cur], (kd, kd, vd, vd)


def _attention_out(q, x, mod, g_norm, w_o, kd, vd, sinks, l, j, *, tq):
    B, T, D = x.shape
    nq = T // tq
    kern = functools.partial(_attn_kernel, tq=tq, fused=True)
    kv_specs, kv_args = _window_specs(kd, vd, T, tq)
    rows = lambda b, i: (b * nq + i, 0)
    out = pl.pallas_call(
        kern,
        out_shape=jax.ShapeDtypeStruct((B * T, D), F32),
        grid=(B, nq),
        in_specs=[
            pl.BlockSpec(memory_space=pltpu.SMEM),
            pl.BlockSpec((N_PAIRS, tq, LANES), lambda b, i: (0, b * nq + i, 0)),
            *kv_specs,
            pl.BlockSpec((tq, D), rows),
            pl.BlockSpec((1, 6, D), lambda b, i: (b, 0, 0)),
            pl.BlockSpec((1, 4, D), lambda b, i: (l, 0, 0)),
            pl.BlockSpec((1, D, D), lambda b, i: (j, 0, 0)),
        ],
        out_specs=pl.BlockSpec((tq, D), rows),
        compiler_params=_params("parallel", "parallel"),
        name=f"attention_out_{l}",
    )(sinks, q, *kv_args, x.reshape(B * T, D), mod, g_norm, w_o)
    return out.reshape(B, T, D)


def _oproj_kernel(o_ref, x_ref, mod_ref, g_ref, w_ref, y_ref, *, bb, tt):
    o = jnp.concatenate([o_ref[p] for p in range(N_PAIRS)], axis=1)
    mix = jnp.dot(o, w_ref[0], preferred_element_type=F32).reshape(bb, tt, D_MODEL)
    m = mod_ref[...]
    y_ref[...] = x_ref[...] + _gated_norm(mix, g_ref[:, 1:2, :], m[:, 2:3, :])


def _oproj_layer(o, x, mod, g_norm, w_o, l, j, *, bb, tt):
    B, T, D = x.shape
    nt = T // tt
    kern = functools.partial(_oproj_kernel, bb=bb, tt=tt)
    return pl.pallas_call(
        kern,
        out_shape=jax.ShapeDtypeStruct((B, T, D), F32),
        grid=(B // bb, nt),
        in_specs=[
            pl.BlockSpec((N_PAIRS, bb * tt, LANES), lambda b, t: (0, b * nt + t, 0)),
            pl.BlockSpec((bb, tt, D), lambda b, t: (b, t, 0)),
            pl.BlockSpec((bb, 6, D), lambda b, t: (b, 0, 0)),
            pl.BlockSpec((1, 4, D), lambda b, t: (l, 0, 0)),
            pl.BlockSpec((1, D, D), lambda b, t: (j, 0, 0)),
        ],
        out_specs=pl.BlockSpec((bb, tt, D), lambda b, t: (b, t, 0)),
        compiler_params=_params("parallel", "parallel"),
        name=f"o_proj_{l}",
    )(o, x, mod, g_norm, w_o)


def _mlp_kernel(x_ref, mod_ref, g_ref, wu_ref, wd_ref, y_ref, *rest, bb, tt, cast):
    f = pl.program_id(2)
    m = mod_ref[...]
    if cast:
        wu_out, wd_out, h_ref = rest
    else:
        h_ref, = rest

    @pl.when(f == 0)
    def _():
        h = _norm_mod(x_ref[...], g_ref[:, 2:3, :], m[:, 3:4, :], m[:, 4:5, :])
        h_ref[...] = h.reshape(bb * tt, D_MODEL).astype(BF16)
        y_ref[...] = jnp.zeros_like(y_ref)

    wu = wu_ref[0]
    if cast:
        wu = wu.astype(BF16)
        wu_out[0] = wu
    u = jnp.dot(h_ref[...], wu, preferred_element_type=F32)
    a = jnp.square(jnp.maximum(u, 0.0)).astype(BF16)
    for n in range(D_MODEL // MLP_OUT_CHUNK):
        cs = slice(n * MLP_OUT_CHUNK, (n + 1) * MLP_OUT_CHUNK)
        wd_n = wd_ref[0, :, cs]
        if cast:
            wd_n = wd_n.astype(BF16)
            wd_out[0, :, cs] = wd_n
        y_ref[:, :, cs] += jnp.dot(a, wd_n, preferred_element_type=F32).reshape(
            bb, tt, MLP_OUT_CHUNK)

    @pl.when(f == pl.num_programs(2) - 1)
    def _():
        y_ref[...] = x_ref[...] + _gated_norm(y_ref[...], g_ref[:, 3:4, :], m[:, 5:6, :])


def _mlp_layer(x, mod, g_norm, w_up, w_down, l, *, bb, tt, tff, cast):
    B, T, D = x.shape
    grid = (B // bb, T // tt, D_FF // tff)
    kern = functools.partial(_mlp_kernel, bb=bb, tt=tt, cast=cast)
    wl = l if cast else 0
    y_shape = jax.ShapeDtypeStruct((B, T, D), F32)
    y_spec = pl.BlockSpec((bb, tt, D), lambda b, t, f: (b, t, 0))
    if cast:
        assert grid[0] * grid[1] == 1
        out_shape = (y_shape, jax.ShapeDtypeStruct((1, D, D_FF), BF16),
                     jax.ShapeDtypeStruct((1, D_FF, D), BF16))
        out_specs = (y_spec, pl.BlockSpec((1, D, tff), lambda b, t, f: (0, 0, f)),
                     pl.BlockSpec((1, tff, D), lambda b, t, f: (0, f, 0)))
    else:
        out_shape, out_specs = y_shape, y_spec
    return pl.pallas_call(
        kern,
        out_shape=out_shape,
        grid=grid,
        in_specs=[
            pl.BlockSpec((bb, tt, D), lambda b, t, f: (b, t, 0)),
            pl.BlockSpec((bb, 6, D), lambda b, t, f: (b, 0, 0)),
            pl.BlockSpec((1, 4, D), lambda b, t, f: (l, 0, 0)),
            pl.BlockSpec((1, D, tff), lambda b, t, f: (wl, 0, f)),
            pl.BlockSpec((1, tff, D), lambda b, t, f: (wl, f, 0)),
        ],
        out_specs=out_specs,
        scratch_shapes=[pltpu.VMEM((bb * tt, D), BF16)],
        compiler_params=_params("parallel", "parallel", "arbitrary"),
        name=f"mlp_{l}",
    )(x, mod, g_norm, w_up, w_down)


def _zero_after(x):
    bits = pltpu.bitcast(x, jnp.uint32)
    r = bits[:, 0:LANES]
    for k in range(1, x.shape[1] // LANES):
        r = r | bits[:, k * LANES:(k + 1) * LANES]
    r8 = r[0:SUBLANES]
    for k in range(1, x.shape[0] // SUBLANES):
        r8 = r8 | r[k * SUBLANES:(k + 1) * SUBLANES]
    z = pltpu.bitcast((r8 >> 16) >> 16, F32)
    return jnp.max(z, axis=(0, 1), keepdims=True)


def _mlp_skew_kernel(xn_ref, xp_ref, modn_ref, modp_ref, g_ref, wu_ref, wd_ref, *rest, nt, rs,
                     cast_next):
    s = pl.program_id(0)
    f = pl.program_id(1)
    if cast_next:
        nu_ref, nd_ref, y_ref, nu_out, nd_out, h0_ref, h1_ref, acc0_ref, acc1_ref = rest
    else:
        y_ref, h0_ref, h1_ref, acc0_ref, acc1_ref = rest
    rows = pl.ds(pl.multiple_of(f * rs, rs), rs)
    g_pre, g_post = g_ref[0, 2:3, :], g_ref[0, 3:4, :]
    h_refs = (h0_ref, h1_ref)
    acc_refs = (acc0_ref, acc1_ref)

    def prologue(h_ref):
        m = modn_ref[0]
        h = _norm_mod(xn_ref[...], g_pre, m[3:4, :], m[4:5, :])
        h_ref[rows, :] = h.astype(BF16)
        return h

    def epilogue(acc_ref):
        mp = modp_ref[0]
        y = xp_ref[...] + _gated_norm(acc_ref[rows, :], g_post, mp[5:6, :])
        y_ref[...] = y
        acc_ref[rows, :] = jnp.zeros((rs, D_MODEL), F32)
        return y

    def matmuls(h_ref, acc_ref, anchors):
        u = jnp.dot(h_ref[...], wu_ref[0], preferred_element_type=F32)
        a = jnp.square(jnp.maximum(u, 0.0)).astype(BF16)
        if cast_next:
            nu_out[...] = nu_ref[...].astype(BF16)
            nd_out[...] = nd_ref[...].astype(BF16)
        for n in range(D_MODEL // MLP_OUT_CHUNK):
            cs = slice(n * MLP_OUT_CHUNK, (n + 1) * MLP_OUT_CHUNK)
            d = jnp.dot(a, wd_ref[0, :, cs], preferred_element_type=F32)
            if anchors.get(n) is not None:
                d = d + anchors[n]
            acc_ref[:, cs] += d

    @pl.when((s == 0) & (f == 0))
    def _():
        acc0_ref[...] = jnp.zeros_like(acc0_ref)
        acc1_ref[...] = jnp.zeros_like(acc1_ref)

    @pl.when(s == 0)
    def _():
        prologue(h0_ref)

    @pl.when(s == 1)
    def _():
        h = prologue(h1_ref)
        matmuls(h0_ref, acc0_ref, {2: _zero_after(h)})

    for c in range(2):
        o = 1 - c

        @pl.when(((s - 1) % 2 == c) & (s >= 2) & (s < nt))
        def _():
            y = epilogue(acc_refs[o])
            h = prologue(h_refs[o])
            matmuls(h_refs[c], acc_refs[c], {0: _zero_after(y), 2: _zero_after(h)})

    last = (nt - 1) % 2

    @pl.when(s == nt)
    def _():
        y = epilogue(acc_refs[1 - last])
        matmuls(h_refs[last], acc_refs[last], {0: _zero_after(y)})

    @pl.when(s == nt + 1)
    def _():
        epilogue(acc_refs[last])


def _mlp_layer_skewed(x, mod, g_norm, w_up, w_down, l, *, tm, tff, next_f32=None):
    B, T, D = x.shape
    nf = D_FF // tff
    rs = tm // nf
    tpb = T // tm
    nt = B * tpb
    assert nt >= 3
    x2 = x.reshape(B * T, D)
    nxt = lambda s: jnp.minimum(s, nt - 1)
    prv = lambda s: jnp.maximum(s - 2, 0)
    wf = lambda s, f: jnp.where(s == 0, 0, jnp.where(s == nt + 1, nf - 1, f))
    cast_next = next_f32 is not None
    kern = functools.partial(_mlp_skew_kernel, nt=nt, rs=rs, cast_next=cast_next)
    in_specs = [
        pl.BlockSpec((rs, D), lambda s, f: (nxt(s) * nf + f, 0)),
        pl.BlockSpec((rs, D), lambda s, f: (prv(s) * nf + f, 0)),
        pl.BlockSpec((1, 6, D), lambda s, f: (nxt(s) // tpb, 0, 0)),
        pl.BlockSpec((1, 6, D), lambda s, f: (prv(s) // tpb, 0, 0)),
        pl.BlockSpec((1, 4, D), lambda s, f: (l, 0, 0)),
        pl.BlockSpec((1, D, tff), lambda s, f: (0, 0, wf(s, f))),
        pl.BlockSpec((1, tff, D), lambda s, f: (0, wf(s, f), 0)),
    ]
    args = [x2, x2, mod, mod, g_norm, w_up, w_down]
    out_specs = [pl.BlockSpec((rs, D), lambda s, f: (jnp.where(s < 2, 0, (s - 2) * nf + f), 0))]
    out_shape = [jax.ShapeDtypeStruct((B * T, D), F32)]
    if cast_next:
        n_slabs = nt * nf
        ru, rd = D // n_slabs, D_FF // n_slabs
        slab = lambda s, f: jnp.clip((s - 1) * nf + f, 0, n_slabs - 1)
        up_spec = lambda: pl.BlockSpec((1, ru, D_FF), lambda s, f: (l + 1, slab(s, f), 0))
        dn_spec = lambda: pl.BlockSpec((1, rd, D), lambda s, f: (l + 1, slab(s, f), 0))
        in_specs += [up_spec(), dn_spec()]
        args += list(next_f32)
        out_specs += [pl.BlockSpec((1, ru, D_FF), lambda s, f: (0, slab(s, f), 0)),
                      pl.BlockSpec((1, rd, D), lambda s, f: (0, slab(s, f), 0))]
        out_shape += [jax.ShapeDtypeStruct((1, D, D_FF), BF16),
                      jax.ShapeDtypeStruct((1, D_FF, D), BF16)]
    outs = pl.pallas_call(
        kern,
        out_shape=tuple(out_shape),
        grid=(nt + 2, nf),
        in_specs=in_specs,
        out_specs=tuple(out_specs),
        scratch_shapes=[pltpu.VMEM((tm, D), BF16), pltpu.VMEM((tm, D), BF16),
                        pltpu.VMEM((tm, D), F32), pltpu.VMEM((tm, D), F32)],
        compiler_params=_params("arbitrary", "arbitrary"),
        name=f"mlp_skewed_{l}",
    )(*args)
    y = outs[0].reshape(B, T, D)
    return (y, outs[1], outs[2]) if cast_next else (y, None, None)


class _Path:
    def __init__(self, x, mods, kvmod, pos0, prefix, kv_past, *, bb, tt, tq):
        self.x, self.mods, self.kvmod, self.pos0 = x, mods, kvmod, pos0
        self.prefix, self.kv_past = prefix, kv_past
        self.bb, self.tt, self.tq = bb, tt, tq
        pos = pos0 + jnp.arange(x.shape[1])
        self.tables = tuple(jnp.tile(tb, (bb, 1)) if bb > 1 else tb for tb in _rope_tables(pos))
        self.new_pool = []
        self.k = self.v = self.kd = self.vd = None

    def mixer(self, l, wts):
        B, T, _ = self.x.shape
        mod, bb, tt = self.mods[l], self.bb, self.tt
        if l < N_A_LAYERS:
            self.x, npool = _pool_layer(self.x, mod, wts["g_norm"], self.prefix, wts["w_pool"],
                                        wts["pool_scale"], l, bb=bb, tt=tt, pos0=self.pos0)
            self.new_pool.append(npool)
            return
        j = l - N_A_LAYERS
        if j == 0:
            q, k, v, self.kd, self.vd = _qkv_layer(
                self.x, mod, wts["g_norm"], self.tables, wts["w_q"], l, j, self.kvmod,
                wts["g_kv"], wts["w_kv"], bb=bb, tt=tt)
            KV = N_KV_HEADS * HEAD_DIM
            self.k, self.v = k.reshape(B, tt, KV), v.reshape(B, tt, KV)
        else:
            q = _qproj_layer(self.x, mod, wts["g_norm"], self.tables, wts["w_q"], l, j,
                             bb=bb, tt=tt)
        if self.kv_past is None:
            self.x = _attention_out(q, self.x, mod, wts["g_norm"], wts["w_o"], self.kd, self.vd,
                                    wts["sinks"][j], l, j, tq=self.tq)
        else:
            past_k, past_v = (c.reshape(B, c.shape[1], -1) for c in self.kv_past)
            o = _attention(q, past_k, past_v, self.kd, self.vd, wts["sinks"][j])
            self.x = _oproj_layer(o, self.x, mod, wts["g_norm"], wts["w_o"], l, j, bb=bb, tt=tt)

def _forward(prompt, sample, wts, *, tm, tff_prompt, tff_cast, tff_sample):
    w_f32 = (wts["w_up"], wts["w_down"])
    w_bf16 = None
    for l in range(DEPTH):
        sample.mixer(l, wts)
        if l == 0:
            sample.x, wu, wd = _mlp_layer(sample.x, sample.mods[l], wts["g_norm"], *w_f32, l,
                                          bb=sample.bb, tt=sample.tt, tff=tff_cast, cast=True)
            w_bf16 = (wu, wd)
        else:
            sample.x = _mlp_layer(sample.x, sample.mods[l], wts["g_norm"], *w_bf16, l,
                                  bb=sample.bb, tt=sample.tt, tff=tff_sample, cast=False)
        prompt.mixer(l, wts)
        prompt.x, wu, wd = _mlp_layer_skewed(
            prompt.x, prompt.mods[l], wts["g_norm"], *w_bf16, l, tm=tm, tff=tff_prompt,
            next_f32=w_f32 if l + 1 < DEPTH else None)
        w_bf16 = (wu, wd)


def _prep_weights(g_norm, w_pool, pool_scale, g_kv, w_kv, w_q, sinks, w_o, w_up, w_down):
    D = D_MODEL
    return {
        "g_norm": g_norm,
        "w_pool": w_pool.astype(BF16),
        "pool_scale": pool_scale.reshape(N_A_LAYERS, 1, D),
        "g_kv": g_kv.reshape(1, 1, D),
        "w_kv": w_kv.astype(BF16),
        "w_q": w_q.astype(BF16),
        "sinks": sinks,
        "w_o": w_o.astype(BF16),
        "w_up": w_up,
        "w_down": w_down,
    }


def kernel(x_prompt, x_sample, c_prompt, c_sample, state_pool, cache_k, cache_v, w_mod, b_mod,
           g_norm, w_pool, pool_scale, w_kv_mod, b_kv_mod, g_kv, w_kv, w_q, sinks, w_o, w_up,
           w_down):
    Bp, Bs = x_prompt.shape[0], x_sample.shape[0]
    T_s = x_sample.shape[1]
    D = D_MODEL

    c_all = jnp.concatenate(
        [c_prompt, c_sample, jnp.zeros((MOD_ROWS - Bp - Bs, D), F32)], axis=0)
    mod_all = _modulation(c_all, w_mod, b_mod.reshape(DEPTH, 1, 6 * D)).reshape(
        DEPTH, MOD_ROWS, 6, D)
    kvmod_all = _modulation(c_all, w_kv_mod.reshape(1, D, 2 * D),
                            b_kv_mod.reshape(1, 1, 2 * D)).reshape(MOD_ROWS, 2, D)
    wts = _prep_weights(g_norm, w_pool, pool_scale, g_kv, w_kv, w_q, sinks, w_o, w_up, w_down)

    prompt = _Path(x_prompt, [mod_all[l, :Bp] for l in range(DEPTH)], kvmod_all[:Bp], 0,
                   jnp.zeros((N_A_LAYERS, Bp, POOL_STATE, D), F32), None,
                   bb=1, tt=512, tq=4 * WINDOW)
    sample = _Path(x_sample, [mod_all[l, Bp:Bp + Bs] for l in range(DEPTH)],
                   kvmod_all[Bp:Bp + Bs], PAST_LEN, state_pool, (cache_k, cache_v),
                   bb=Bs, tt=T_s, tq=T_s)
    _forward(prompt, sample, wts, tm=1024, tff_prompt=1024, tff_cast=512, tff_sample=1024)

    keep = min(WINDOW, x_prompt.shape[1])
    heads = (N_KV_HEADS, HEAD_DIM)
    return (prompt.x, sample.x,
            jnp.concatenate(prompt.new_pool, axis=0), jnp.concatenate(sample.new_pool, axis=0),
            prompt.k[:, -keep:].reshape(Bp, keep, *heads),
            prompt.v[:, -keep:].reshape(Bp, keep, *heads),
            sample.k.reshape(Bs, T_s, *heads), sample.v.reshape(Bs, T_s, *heads))
```
